```python
import numpy as np
import jax
import jax.numpy as jnp
from jax import lax

D_MODEL = 2048
BATCH = 4
SEQ = 2048
DEPTH = 1
DEC_BATCH = 32
DEC_SEQ = 4
PAST_LEN = 8192
PAGE_SIZE = 128

HEAD_DIM = 128
H_A = 8
H_B = 4
C_B = 128
H_M = 4
W_A = H_A * HEAD_DIM
W_B = H_B * C_B
W_M = H_M * HEAD_DIM
MIX_WIDTH = W_A + W_B + W_M
D_FF = 5632
DILATED = ((128, 1), (512, 4), (2048, 16))
WIN_MAX = 2048
BLK = 128
CHUNK = 128
N_MEM = 256
N_BUCKETS = 32
REL_MAX_DIST = WIN_MAX
EPS = 1e-6
SCALE = HEAD_DIM ** -0.5
IN_SPLITS = (W_A, 2 * W_A, 3 * W_A, 3 * W_A + W_B, 3 * W_A + 2 * W_B)
IN_WIDTH = 3 * W_A + 2 * W_B + W_M

kernel_name = 'hybrid_dilated_sgu_memory_decoder_step'


def _rel_bucket(dist):
    dist = np.asarray(dist, np.int64)
    max_exact = N_BUCKETS // 2
    large = max_exact + (np.log(np.maximum(dist, 1) / max_exact) / np.log(REL_MAX_DIST / max_exact)
                         * (N_BUCKETS - max_exact)).astype(np.int64)
    large = np.minimum(large, N_BUCKETS - 1)
    return np.where(dist < max_exact, dist, large).astype(np.int32)


def _rms(t, g):
    t32 = t.astype(jnp.float32)
    y = t32 * lax.rsqrt(jnp.mean(t32 * t32, axis=-1, keepdims=True) + EPS) * g.astype(jnp.float32)
    return y.astype(t.dtype)


def _swiglu(x, wg, wu, wd):
    return (jax.nn.silu(x @ wg) * (x @ wu)) @ wd


def _pre_mix(x, g_ffn1, w1_gate, w1_up, w1_down, g_mix, w_in, g_qa, g_ka, g_sgu, g_qm):
    B, T, _ = x.shape
    h = x + 0.5 * _swiglu(_rms(x, g_ffn1), w1_gate, w1_up, w1_down)
    z = _rms(h, g_mix) @ w_in
    qa, ka, va, u, v, qm = jnp.split(z, IN_SPLITS, axis=-1)
    qa = _rms(qa.reshape(B, T, H_A, HEAD_DIM), g_qa)
    ka = _rms(ka.reshape(B, T, H_A, HEAD_DIM), g_ka)
    va = va.reshape(B, T, H_A, HEAD_DIM)
    u = u.reshape(B, T, H_B, C_B)
    v = _rms(v.reshape(B, T, H_B, C_B), g_sgu)
    qm = _rms(qm.reshape(B, T, H_M, HEAD_DIM), g_qm)
    return h, qa, ka, va, u, v, qm


def _post_mix(h, oa, ob, om, g_mix_out, w_out, g_ffn2, w2_gate, w2_up, w2_down):
    B, T, _ = h.shape
    ga, gb, gm = jnp.split(g_mix_out, (W_A, W_A + W_B))
    cat = jnp.concatenate([_rms(oa.reshape(B, T, W_A), ga),
                           _rms(ob.reshape(B, T, W_B), gb),
                           _rms(om.reshape(B, T, W_M), gm)], axis=-1)
    h = h + cat @ w_out
    return h + 0.5 * _swiglu(_rms(h, g_ffn2), w2_gate, w2_up, w2_down)


def _dilated_prompt(q, k, v, win, dil, rel_bias):
    B, S, H, D = q.shape
    n_steps = win // dil + 1
    span = dil * BLK
    s_pad = -(-S // span) * span
    L = s_pad // dil
    nb = L // BLK

    def to_res(t):
        t = jnp.pad(t.astype(jnp.float32), ((0, 0), (0, s_pad - S), (0, 0), (0, 0)))
        t = t.reshape(B, L, dil, H, D).transpose(0, 2, 1, 3, 4)
        return t.reshape(B, dil, nb, BLK, H, D)

    def with_prev(t):
        prev = jnp.pad(t[:, :, :-1], ((0, 0), (0, 0), (1, 0), (0, 0), (0, 0), (0, 0)))
        return jnp.concatenate([prev, t], axis=3)

    qr = to_res(q)
    kb = with_prev(to_res(k))
    vb = with_prev(to_res(v))
    step = np.arange(BLK)[:, None] + BLK - np.arange(2 * BLK)[None, :]
    band = (step >= 0) & (step < n_steps)
    first = (np.arange(nb)[:, None, None] > 0) | (np.arange(2 * BLK)[None, None, :] >= BLK)
    valid = jnp.asarray(band[None] & first)
    bias = rel_bias.astype(jnp.float32)[_rel_bucket(np.clip(step, 0, n_steps - 1) * dil)]
    bias = jnp.transpose(bias, (2, 0, 1))
    logits = jnp.einsum('brnihd,brnjhd->brnhij', qr, kb) * SCALE + bias
    logits = jnp.where(valid[:, None], logits, -jnp.inf)
    m = jnp.max(logits, axis=-1)
    p = jnp.exp(logits - m[..., None])
    s = jnp.sum(p, axis=-1)
    o = jnp.einsum('brnhij,brnjhd->brnihd', p, vb)

    def from_res(t):
        t = t.reshape((B, dil, L) + t.shape[4:])
        return jnp.swapaxes(t, 1, 2).reshape((B, s_pad) + t.shape[3:])[:, :S]

    return from_res(o), from_res(jnp.swapaxes(m, 3, 4)), from_res(jnp.swapaxes(s, 3, 4))


def _dilated_sample(q, k_all, v_all, n_past, win, dil, rel_bias):
    T = q.shape[1]
    n_steps = win // dil + 1
    dist = np.arange(n_steps) * dil
    idx = n_past + np.arange(T)[:, None] - dist[None, :]
    valid = jnp.asarray(idx >= 0)
    idx = np.maximum(idx, 0)
    kg = k_all[:, idx].astype(jnp.float32)
    vg = v_all[:, idx].astype(jnp.float32)
    bias = rel_bias.astype(jnp.float32)[_rel_bucket(dist)].T[:, None, :]
    logits = jnp.einsum('bthd,btkhd->bhtk', q.astype(jnp.float32), kg) * SCALE + bias
    logits = jnp.where(valid, logits, -jnp.inf)
    m = jnp.max(logits, axis=-1)
    p = jnp.exp(logits - m[..., None])
    s = jnp.sum(p, axis=-1)
    o = jnp.einsum('bhtk,btkhd->bthd', p, vg)
    return o, jnp.swapaxes(m, 1, 2), jnp.swapaxes(s, 1, 2)


def _combine(stats):
    m_all = jnp.stack([m for _, m, _ in stats])
    mx = jnp.max(m_all, axis=0)
    num = sum(jnp.exp(m - mx)[..., None] * o for o, m, _ in stats)
    den = sum(jnp.exp(m - mx) * s for _, m, s in stats)
    return num / den[..., None]


def _sgu_prompt(u, v, w_s, b_s):
    B, S, G, C = v.shape
    w = w_s * jnp.tril(jnp.ones((CHUNK, CHUNK), w_s.dtype))
    vc = v.reshape(B, S // CHUNK, CHUNK, G, C)
    mixed = jnp.einsum('gts,bnsgc->bntgc', w, vc) + b_s.T[None, None, :, :, None]
    return u * mixed.reshape(B, S, G, C)


def _sgu_sample(u, v, w_s, b_s):
    T = v.shape[1]
    w = (w_s * jnp.tril(jnp.ones((CHUNK, CHUNK), w_s.dtype)))[:, :T, :T]
    mixed = jnp.einsum('gts,bsgc->btgc', w, v) + b_s[:, :T].T[None, :, :, None]
    return u * mixed


def _mem_kv(mem, g_mem, w_mem_kv, g_km):
    B, M, _ = mem.shape
    k, v = jnp.split(_rms(mem, g_mem) @ w_mem_kv, 2, axis=-1)
    return _rms(k.reshape(B, M, H_M, HEAD_DIM), g_km), v.reshape(B, M, H_M, HEAD_DIM)


def _mem_attend(q, k, v):
    logits = jnp.einsum('bthd,bmhd->bhtm', q.astype(jnp.float32), k.astype(jnp.float32)) * SCALE
    p = jax.nn.softmax(logits, axis=-1)
    return jnp.einsum('bhtm,bmhd->bthd', p, v.astype(jnp.float32)).astype(q.dtype)


def setup_inputs(seed: int = 0) -> dict:
    key = jax.random.key(seed)
    keys = iter(jax.random.split(key, 40))

    def nrm(shape, scale=1.0):
        return scale * jax.random.normal(next(keys), shape, jnp.float32)

    def gain(shape):
        return 1.0 + 0.05 * nrm(shape)

    w_buf = min(WIN_MAX, PAST_LEN)
    return {
        'x_prompt': nrm((BATCH, SEQ, D_MODEL)),
        'x_sample': nrm((DEC_BATCH, DEC_SEQ, D_MODEL)),
        'cache_win_k': nrm((DEPTH, DEC_BATCH, w_buf, H_A, HEAD_DIM)),
        'cache_win_v': nrm((DEPTH, DEC_BATCH, w_buf, H_A, HEAD_DIM)),
        'cache_mem_k': nrm((DEPTH, DEC_BATCH, N_MEM, H_M, HEAD_DIM)),
        'cache_mem_v': nrm((DEPTH, DEC_BATCH, N_MEM, H_M, HEAD_DIM)),
        'mem_prompt': nrm((BATCH, N_MEM, D_MODEL)),
        'rel_bias': nrm((N_BUCKETS, H_A), 0.5),
        'g_ffn1': gain((DEPTH, D_MODEL)),
        'w1_gate': nrm((DEPTH, D_MODEL, D_FF), D_MODEL ** -0.5),
        'w1_up': nrm((DEPTH, D_MODEL, D_FF), D_MODEL ** -0.5),
        'w1_down': nrm((DEPTH, D_FF, D_MODEL), D_FF ** -0.5),
        'g_mix': gain((DEPTH, D_MODEL)),
        'w_in': nrm((DEPTH, D_MODEL, IN_WIDTH), D_MODEL ** -0.5),
        'g_qa': gain((DEPTH, HEAD_DIM)),
        'g_ka': gain((DEPTH, HEAD_DIM)),
        'g_sgu': gain((DEPTH, H_B, C_B)),
        'w_sgu': nrm((DEPTH, H_B, CHUNK, CHUNK), CHUNK ** -0.5),
        'b_sgu': 1.0 + 0.1 * nrm((DEPTH, H_B, CHUNK)),
        'g_qm': gain((DEPTH, HEAD_DIM)),
        'g_mem': gain((DEPTH, D_MODEL)),
        'w_mem_kv': nrm((DEPTH, D_MODEL, 2 * W_M), D_MODEL ** -0.5),
        'g_km': gain((DEPTH, HEAD_DIM)),
        'g_mix_out': gain((DEPTH, MIX_WIDTH)),
        'w_out': nrm((DEPTH, MIX_WIDTH, D_MODEL), MIX_WIDTH ** -0.5),
        'g_ffn2': gain((DEPTH, D_MODEL)),
        'w2_gate': nrm((DEPTH, D_MODEL, D_FF), D_MODEL ** -0.5),
        'w2_up': nrm((DEPTH, D_MODEL, D_FF), D_MODEL ** -0.5),
        'w2_down': nrm((DEPTH, D_FF, D_MODEL), D_FF ** -0.5),
    }


def reference(x_prompt, x_sample, cache_win_k, cache_win_v, cache_mem_k, cache_mem_v, mem_prompt,
              rel_bias, g_ffn1, w1_gate, w1_up, w1_down, g_mix, w_in, g_qa, g_ka, g_sgu, w_sgu, b_sgu,
              g_qm, g_mem, w_mem_kv, g_km, g_mix_out, w_out, g_ffn2, w2_gate, w2_up, w2_down):
    y_p = x_prompt
    y_s = x_sample
    n_past = cache_win_k.shape[2]
    wk_p, wv_p, mk_p, mv_p, wk_s, wv_s, cv_s = [], [], [], [], [], [], []
    for l in range(DEPTH):
        pre = (g_ffn1[l], w1_gate[l], w1_up[l], w1_down[l], g_mix[l], w_in[l], g_qa[l], g_ka[l], g_sgu[l], g_qm[l])
        post = (g_mix_out[l], w_out[l], g_ffn2[l], w2_gate[l], w2_up[l], w2_down[l])

        h, qa, ka, va, u, v, qm = _pre_mix(y_p, *pre)
        oa = _combine([_dilated_prompt(qa, ka, va, win, dil, rel_bias) for win, dil in DILATED]).astype(h.dtype)
        ob = _sgu_prompt(u, v, w_sgu[l], b_sgu[l])
        mk, mv = _mem_kv(mem_prompt, g_mem[l], w_mem_kv[l], g_km[l])
        om = _mem_attend(qm, mk, mv)
        y_p = _post_mix(h, oa, ob, om, *post)
        keep = min(WIN_MAX, ka.shape[1])
        wk_p.append(ka[:, -keep:])
        wv_p.append(va[:, -keep:])
        mk_p.append(mk)
        mv_p.append(mv)

        h, qa, ka, va, u, v, qm = _pre_mix(y_s, *pre)
        k_all = jnp.concatenate([cache_win_k[l], ka], axis=1)
        v_all = jnp.concatenate([cache_win_v[l], va], axis=1)
        oa = _combine([_dilated_sample(qa, k_all, v_all, n_past, win, dil, rel_bias)
                       for win, dil in DILATED]).astype(h.dtype)
        ob = _sgu_sample(u, v, w_sgu[l], b_sgu[l])
        om = _mem_attend(qm, cache_mem_k[l], cache_mem_v[l])
        y_s = _post_mix(h, oa, ob, om, *post)
        wk_s.append(ka)
        wv_s.append(va)
        cv_s.append(v)

    return (y_p, y_s, jnp.stack(wk_p), jnp.stack(wv_p), jnp.stack(mk_p), jnp.stack(mv_p),
            jnp.stack(wk_s), jnp.stack(wv_s), jnp.stack(cv_s))
```

```python
import functools

import numpy as np
import jax
import jax.numpy as jnp
from jax import lax
from jax.experimental import pallas as pl
from jax.experimental.pallas import tpu as pltpu

D_MODEL = 2048
HEAD_DIM = 128
H_A = 8
H_B = 4
C_B = 128
H_M = 4
W_A = H_A * HEAD_DIM
W_B = H_B * C_B
W_M = H_M * HEAD_DIM
D_FF = 5632
DILATED = ((128, 1), (512, 4), (2048, 16))
BLK = 128
CHUNK = 128
N_MEM = 256
N_BUCKETS = 32
REL_MAX_DIST = 2048
EPS = 1e-6
SCALE = HEAD_DIM ** -0.5
NEG = -1e30

LANES = 128
PROJ_TILE = 4 * HEAD_DIM
VMEM_LIMIT = 56 * 1024 * 1024

F32 = jnp.float32
BF16 = jnp.bfloat16


def _rel_bucket(dist):
    dist = np.asarray(dist, np.int64)
    max_exact = N_BUCKETS // 2
    large = max_exact + (np.log(np.maximum(dist, 1) / max_exact) / np.log(REL_MAX_DIST / max_exact)
                         * (N_BUCKETS - max_exact)).astype(np.int64)
    large = np.minimum(large, N_BUCKETS - 1)
    return np.where(dist < max_exact, dist, large).astype(np.int32)


def _rms(t, g):
    return t * lax.rsqrt(jnp.mean(t * t, axis=-1, keepdims=True) + EPS) * g


def _dot(a, b):
    return jnp.dot(a, b, preferred_element_type=F32)


def _dot_nt(a, b):
    return lax.dot_general(a, b, (((1,), (1,)), ((), ())), preferred_element_type=F32)


def _params(*sem):
    return pltpu.CompilerParams(dimension_semantics=sem, vmem_limit_bytes=VMEM_LIMIT)


def _ffn_body(x_ref, g_ref, wg_ref, wu_ref, wd_ref, o_ref, xn_ref):
    @pl.when(pl.program_id(1) == 0)
    def _():
        x = x_ref[...]
        xn_ref[...] = _rms(x, g_ref[...]).astype(BF16)
        o_ref[...] = x

    xn = xn_ref[...]
    gate = _dot(xn, wg_ref[...].astype(BF16))
    up = _dot(xn, wu_ref[...].astype(BF16))
    act = (gate * jax.nn.sigmoid(gate)) * up * 0.5
    o_ref[...] += _dot(act.astype(BF16), wd_ref[...].astype(BF16))


def _ffn(x, g, wg, wu, wd, *, tm, tf):
    n = x.shape[0]
    return pl.pallas_call(
        _ffn_body,
        out_shape=jax.ShapeDtypeStruct((n, D_MODEL), F32),
        grid=(n // tm, D_FF // tf),
        in_specs=[
            pl.BlockSpec((tm, D_MODEL), lambda i, j: (i, 0)),
            pl.BlockSpec((1, D_MODEL), lambda i, j: (0, 0)),
            pl.BlockSpec((D_MODEL, tf), lambda i, j: (0, j)),
            pl.BlockSpec((D_MODEL, tf), lambda i, j: (0, j)),
            pl.BlockSpec((tf, D_MODEL), lambda i, j: (j, 0)),
        ],
        out_specs=pl.BlockSpec((tm, D_MODEL), lambda i, j: (i, 0)),
        scratch_shapes=[pltpu.VMEM((tm, D_MODEL), BF16)],
        compiler_params=_params("parallel", "arbitrary"),
        name="ffn",
    )(x, g.reshape(1, D_MODEL), wg, wu, wd)


def _proj_body(x_ref, g_ref, w_ref, gain_ref, *rest, seg_tiles, seg_norm):
    n_out = len(seg_tiles)
    out_refs = rest[:n_out]
    xn_ref = rest[n_out]
    j = pl.program_id(1)

    @pl.when(j == 0)
    def _():
        xn_ref[...] = _rms(x_ref[...], g_ref[...]).astype(BF16)

    z = _dot(xn_ref[...], w_ref[...].astype(BF16))
    gain = gain_ref[...]
    lo = 0
    for k in range(n_out):
        hi = lo + seg_tiles[k]

        @pl.when((j >= lo) & (j < hi))
        def _(k=k):
            if seg_norm[k]:
                for hh in range(PROJ_TILE // HEAD_DIM):
                    sl = slice(hh * HEAD_DIM, (hh + 1) * HEAD_DIM)
                    out_refs[k][:, sl] = _rms(z[:, sl], gain[:, sl])
            else:
                out_refs[k][...] = z

        lo = hi


def _proj(x, g, w, gain, seg_tiles, seg_norm, *, tm):
    n, d_in = x.shape
    n_tiles = sum(seg_tiles)
    starts = np.cumsum((0,) + tuple(seg_tiles))[:-1]

    def out_map(lo, nt):
        return lambda i, j: (i, jnp.clip(j - lo, 0, nt - 1))

    return pl.pallas_call(
        functools.partial(_proj_body, seg_tiles=tuple(seg_tiles), seg_norm=tuple(seg_norm)),
        out_shape=[jax.ShapeDtypeStruct((n, nt * PROJ_TILE), F32) for nt in seg_tiles],
        grid=(n // tm, n_tiles),
        in_specs=[
            pl.BlockSpec((tm, d_in), lambda i, j: (i, 0)),
            pl.BlockSpec((1, d_in), lambda i, j: (0, 0)),
            pl.BlockSpec((d_in, PROJ_TILE), lambda i, j: (0, j)),
            pl.BlockSpec((1, PROJ_TILE), lambda i, j: (0, j)),
        ],
        out_specs=[pl.BlockSpec((tm, PROJ_TILE), out_map(int(lo), nt))
                   for lo, nt in zip(starts, seg_tiles)],
        scratch_shapes=[pltpu.VMEM((tm, d_in), BF16)],
        compiler_params=_params("parallel", "arbitrary"),
        name="proj",
    )(x, g.reshape(1, d_in), w, gain.reshape(1, n_tiles * PROJ_TILE))


def _prompt_bias_index():
    step = np.arange(BLK)[:, None] + BLK - np.arange(2 * BLK)[None, :]
    maps = []
    for win, dil in DILATED:
        n_steps = win // dil + 1
        band = (step >= 0) & (step < n_steps)
        bucket = _rel_bucket(np.clip(step, 0, n_steps - 1) * dil)
        maps.append(np.where(band, bucket, -1))
    return np.stack(maps).astype(np.int32)


def _bias_body(idx_ref, tab_ref, o_ref):
    idx = idx_ref[...]
    bias = jnp.full(idx.shape, NEG, F32)
    for bkt in range(N_BUCKETS):
        bias = jnp.where(idx == bkt, tab_ref[:, bkt:bkt + 1], bias)
    o_ref[...] = bias


def _bias_lookup(idx, tab, *, tr, tc):
    n_rows, n_cols = idx.shape
    spec = pl.BlockSpec((tr, tc), lambda i, j: (i, j))
    return pl.pallas_call(
        _bias_body,
        out_shape=jax.ShapeDtypeStruct(idx.shape, F32),
        grid=(n_rows // tr, n_cols // tc),
        in_specs=[spec, pl.BlockSpec((tr, N_BUCKETS), lambda i, j: (i, 0))],
        out_specs=spec,
        compiler_params=_params("parallel", "parallel"),
        name="bias_lookup",
    )(idx, tab)


def _attn_body(bias_s, q_ref, k_ref, v_ref, o_ref, acc_s, m_s, l_s, *, seq):
    def rows(start, dil):
        return pl.ds(start, BLK) if dil == 1 else pl.ds(start, BLK, stride=dil)

    def block(di, dil, cur, prev):
        rc = rows(cur, dil)
        q = q_ref[rc, :].astype(BF16)
        s_c = _dot_nt(q, k_ref[rc, :].astype(BF16)) * SCALE + bias_s[0, di, :, BLK:]
        m = jnp.max(s_c, axis=-1, keepdims=True)
        if prev is not None:
            rp = rows(prev, dil)
            s_p = _dot_nt(q, k_ref[rp, :].astype(BF16)) * SCALE + bias_s[0, di, :, :BLK]
            m = jnp.maximum(m, jnp.max(s_p, axis=-1, keepdims=True))
        p_c = jnp.exp(s_c - m)
        den = jnp.sum(p_c, axis=-1, keepdims=True)
        o = _dot(p_c.astype(BF16), v_ref[rc, :].astype(BF16))
        if prev is not None:
            p_p = jnp.exp(s_p - m)
            den = den + jnp.sum(p_p, axis=-1, keepdims=True)
            o = o + _dot(p_p.astype(BF16), v_ref[rp, :].astype(BF16))
        acc_s[di, rc, :] = o
        m_s[di, rc, :] = jnp.broadcast_to(m, (BLK, HEAD_DIM))
        l_s[di, rc, :] = jnp.broadcast_to(den, (BLK, HEAD_DIM))

    for di, (win, dil) in enumerate(DILATED):
        span = dil * BLK
        nb = seq // span

        def residue(r, carry, di=di, dil=dil, span=span, nb=nb):
            block(di, dil, r, None)
            for n in range(1, nb):
                block(di, dil, r + n * span, r + (n - 1) * span)
            return carry

        if dil == 1:
            block(di, dil, 0, None)

            def step(n, carry, di=di, dil=dil):
                cur = pl.multiple_of(n * BLK, BLK)
                block(di, dil, cur, cur - BLK)
                return carry

            lax.fori_loop(1, nb, step, 0)
        else:
            lax.fori_loop(0, dil, residue, 0)

    rows_per = 256

    def merge(c, carry):
        rc = pl.ds(pl.multiple_of(c * rows_per, rows_per), rows_per)
        ms = [m_s[di, rc, :] for di in range(len(DILATED))]
        mx = jnp.maximum(jnp.maximum(ms[0], ms[1]), ms[2])
        num = 0.0
        den = 0.0
        for di in range(len(DILATED)):
            wgt = jnp.exp(ms[di] - mx)
            num = num + wgt * acc_s[di, rc, :]
            den = den + wgt * l_s[di, rc, :]
        o_ref[rc, :] = num / den
        return carry

    lax.fori_loop(0, seq // rows_per, merge, 0)


def _attn_prompt(zq, zk, zv, rel_bias, *, batch, seq):
    n = batch * seq
    assert all(seq % (dil * BLK) == 0 for _, dil in DILATED)
    idx = _prompt_bias_index()
    n_d = idx.shape[0]
    idx_rows = jnp.asarray(np.tile(idx.reshape(n_d * BLK, 2 * BLK), (H_A, 1)))
    tab_rows = jnp.repeat(rel_bias.T, n_d * BLK, axis=0)
    bias = _bias_lookup(idx_rows, tab_rows, tr=n_d * BLK, tc=2 * BLK)
    bias = bias.reshape(H_A, n_d, BLK, 2 * BLK)
    qkv_spec = pl.BlockSpec((seq, HEAD_DIM), lambda h, b: (b, h))
    stat = pltpu.VMEM((n_d, seq, HEAD_DIM), F32)
    return pl.pallas_call(
        functools.partial(_attn_body, seq=seq),
        out_shape=jax.ShapeDtypeStruct((n, W_A), F32),
        grid=(H_A, batch),
        in_specs=[
            pl.BlockSpec((1, n_d, BLK, 2 * BLK), lambda h, b: (h, 0, 0, 0)),
            qkv_spec, qkv_spec, qkv_spec,
        ],
        out_specs=pl.BlockSpec((seq, HEAD_DIM), lambda h, b: (b, h)),
        scratch_shapes=[stat, stat, stat],
        compiler_params=_params("parallel", "parallel"),
        name="attn_prompt",
    )(bias, zq, zk, zv)


def _sgu_body(u_ref, v_ref, w_ref, bt_ref, o_ref, *, n_chunks):
    tri = (lax.broadcasted_iota(jnp.int32, (CHUNK, CHUNK), 0)
           >= lax.broadcasted_iota(jnp.int32, (CHUNK, CHUNK), 1))
    for g in range(H_B):
        w = jnp.where(tri, w_ref[g], 0.0).astype(BF16)
        bias = bt_ref[:, g:g + 1]
        cols = slice(g * C_B, (g + 1) * C_B)
        for c in range(n_chunks):
            rws = slice(c * CHUNK, (c + 1) * CHUNK)
            mixed = _dot(w, v_ref[rws, cols].astype(BF16)) + bias
            o_ref[rws, cols] = u_ref[rws, cols] * mixed


def _sgu_prompt(zu, zv, w_s, b_s, *, tm):
    n = zu.shape[0]
    spec = pl.BlockSpec((tm, W_B), lambda i: (i, 0))
    return pl.pallas_call(
        functools.partial(_sgu_body, n_chunks=tm // CHUNK),
        out_shape=jax.ShapeDtypeStruct((n, W_B), F32),
        grid=(n // tm,),
        in_specs=[spec, spec,
                  pl.BlockSpec((H_B, CHUNK, CHUNK), lambda i: (0, 0, 0)),
                  pl.BlockSpec((CHUNK, H_B), lambda i: (0, 0))],
        out_specs=spec,
        compiler_params=_params("parallel"),
        name="sgu_prompt",
    )(zu, zv, w_s, b_s.T)


def _mem_body(q_ref, k_ref, v_ref, o_ref):
    for hh in range(H_M):
        sl = slice(hh * HEAD_DIM, (hh + 1) * HEAD_DIM)
        s = _dot_nt(q_ref[0, :, sl].astype(BF16), k_ref[0, :, sl].astype(BF16)) * SCALE
        m = jnp.max(s, axis=-1, keepdims=True)
        p = jnp.exp(s - m)
        den = jnp.sum(p, axis=-1, keepdims=True)
        o_ref[0, :, sl] = _dot(p.astype(BF16), v_ref[0, :, sl].astype(BF16)) / den


def _mem_attend(q, k, v, *, tq):
    batch, t, _ = q.shape
    kv_spec = pl.BlockSpec((1, N_MEM, W_M), lambda b, i: (b, 0, 0))
    q_spec = pl.BlockSpec((1, tq, W_M), lambda b, i: (b, i, 0))
    return pl.pallas_call(
        _mem_body,
        out_shape=jax.ShapeDtypeStruct((batch, t, W_M), F32),
        grid=(batch, t // tq),
        in_specs=[q_spec, kv_spec, kv_spec],
        out_specs=q_spec,
        compiler_params=_params("parallel", "parallel"),
        name="mem_attend",
    )(q, k, v)


def _outproj_body(oa_ref, ob_ref, om_ref, h_ref, g_ref, w_ref, o_ref, cat_ref):
    @pl.when(pl.program_id(1) == 0)
    def _():
        lo = 0
        for src in (oa_ref, ob_ref, om_ref):
            hi = lo + src.shape[1]
            cat_ref[:, lo:hi] = _rms(src[...], g_ref[:, lo:hi]).astype(BF16)
            lo = hi

    o_ref[...] = h_ref[...] + _dot(cat_ref[...], w_ref[...].astype(BF16))


def _outproj(oa, ob, om, h, g, w, *, tm, tn):
    n = h.shape[0]
    mix = W_A + W_B + W_M
    return pl.pallas_call(
        _outproj_body,
        out_shape=jax.ShapeDtypeStruct((n, D_MODEL), F32),
        grid=(n // tm, D_MODEL // tn),
        in_specs=[
            pl.BlockSpec((tm, W_A), lambda i, j: (i, 0)),
            pl.BlockSpec((tm, W_B), lambda i, j: (i, 0)),
            pl.BlockSpec((tm, W_M), lambda i, j: (i, 0)),
            pl.BlockSpec((tm, tn), lambda i, j: (i, j)),
            pl.BlockSpec((1, mix), lambda i, j: (0, 0)),
            pl.BlockSpec((mix, tn), lambda i, j: (0, j)),
        ],
        out_specs=pl.BlockSpec((tm, tn), lambda i, j: (i, j)),
        scratch_shapes=[pltpu.VMEM((tm, mix), BF16)],
        compiler_params=_params("parallel", "arbitrary"),
        name="outproj",
    )(oa, ob, om, h, g.reshape(1, mix), w)


def _sample_bias_index(n_new, n_past):
    (win1, dil1), (tail, dil4), (win16, dil16) = DILATED
    assert dil1 == 1 and win1 == BLK and n_past == win16 and n_past % dil16 == 0
    assert n_past >= tail and n_new <= dil4 and n_new * H_A <= LANES
    n16 = n_past // dil16
    t = np.arange(n_new)[:, None]

    def per_head(index):
        same = np.eye(H_A, dtype=bool)[None, :, None, :]
        full = np.where(same, index[:, None, :, None], -1)
        return full.reshape(n_new * H_A, index.shape[1] * H_A)

    dist = BLK + t - np.arange(BLK)[None, :]
    d1 = np.where(dist <= win1, _rel_bucket(dist), -1)
    dist = tail + t - np.arange(tail)[None, :]
    d4 = np.where((dist % dil4 == 0) & (dist <= tail), _rel_bucket(dist), -1)
    m, tk = np.divmod(np.arange(n16 * n_new)[None, :], n_new)
    dist = n_past + t - dil16 * m - tk
    d16 = np.where(tk == t, _rel_bucket(np.maximum(dist, 0)), -1)
    s = np.arange(LANES // H_A)[None, :]
    dist = t - s
    new = [np.where((s < n_new) & (dist >= 0) & (dist % dil == 0),
                    _rel_bucket(np.maximum(dist, 0)), -1) for _, dil in DILATED]
    pieces = [d1, d4, d16] + new
    return np.concatenate([per_head(p) for p in pieces], axis=1).astype(np.int32)


def _sattn_body(bias_ref, q_ref, kn_ref, vn_ref, kt_ref, k16_ref, vt_ref, v16_ref, o_ref):
    def flat(ref):
        x = ref[0]
        return x.reshape(-1, HEAD_DIM).astype(BF16)

    q = q_ref[0].astype(BF16)
    kt, vt = flat(kt_ref), flat(vt_ref)
    k16, v16 = flat(k16_ref), flat(v16_ref)
    kn, vn = kn_ref[0].astype(BF16), vn_ref[0].astype(BF16)
    n_t, n_16, n_1 = kt.shape[0], k16.shape[0], BLK * H_A

    raw_t = _dot_nt(q, kt)
    raw_n = _dot_nt(q, kn)
    raws = [raw_t[:, n_t - n_1:], raw_t, _dot_nt(q, k16)] + [raw_n] * len(DILATED)
    logits = []
    lo = 0
    for raw in raws:
        hi = lo + raw.shape[1]
        logits.append(raw * SCALE + bias_ref[:, lo:hi])
        lo = hi

    m = functools.reduce(jnp.maximum, [jnp.max(x, axis=-1, keepdims=True) for x in logits])
    probs = [jnp.exp(x - m) for x in logits]
    den = functools.reduce(lambda a, b: a + b, [jnp.sum(p, axis=-1, keepdims=True) for p in probs])
    p_new = functools.reduce(lambda a, b: a + b, probs[3:])
    o = (_dot(probs[0].astype(BF16), vt[n_t - n_1:, :]) + _dot(probs[1].astype(BF16), vt)
         + _dot(probs[2].astype(BF16), v16) + _dot(p_new.astype(BF16), vn))
    o_ref[0] = o / den


def _attn_sample(qa, ka, va, cache_k, cache_v, rel_bias):
    batch, n_new = qa.shape[:2]
    n_past = cache_k.shape[1]
    tail = DILATED[1][0]
    dil16 = DILATED[2][1]
    n16 = n_past // dil16
    n_rows = n_new * H_A
    idx = _sample_bias_index(n_new, n_past)
    bias = _bias_lookup(jnp.asarray(idx), jnp.tile(rel_bias.T, (n_new, 1)),
                        tr=n_rows, tc=3 * LANES)

    def rows(x):
        x = x.reshape(batch, n_rows, HEAD_DIM)
        return jnp.pad(x, ((0, 0), (0, LANES - n_rows), (0, 0)))

    k_res = cache_k.reshape(batch, n16, dil16, H_A, HEAD_DIM)
    v_res = cache_v.reshape(batch, n16, dil16, H_A, HEAD_DIM)
    q_spec = pl.BlockSpec((1, n_rows, HEAD_DIM), lambda b: (b, 0, 0))
    new_spec = pl.BlockSpec((1, LANES, HEAD_DIM), lambda b: (b, 0, 0))
    tail_spec = pl.BlockSpec((1, tail, H_A, HEAD_DIM), lambda b: (b, n_past // tail - 1, 0, 0))
    res_spec = pl.BlockSpec((1, n16, n_new, H_A, HEAD_DIM), lambda b: (b, 0, 0, 0, 0))
    out = pl.pallas_call(
        _sattn_body,
        out_shape=jax.ShapeDtypeStruct((batch, n_rows, HEAD_DIM), F32),
        grid=(batch,),
        in_specs=[
            pl.BlockSpec(bias.shape, lambda b: (0, 0)),
            q_spec, new_spec, new_spec, tail_spec, res_spec, tail_spec, res_spec,
        ],
        out_specs=q_spec,
        compiler_params=_params("parallel"),
        name="attn_sample",
    )(bias, qa.reshape(batch, n_rows, HEAD_DIM), rows(ka), rows(va),
      cache_k, k_res, cache_v, v_res)
    return out.reshape(batch, n_new, W_A)


def _sgus_body(u_ref, v_ref, w_ref, b_ref, o_ref, *, n_new):
    t_idx = lax.broadcasted_iota(jnp.int32, (1, n_new, W_B), 1)
    mixed = 0.0
    for s in range(n_new):
        w_s = jnp.where(t_idx >= s, w_ref[s][None], 0.0).astype(BF16).astype(F32)
        mixed = mixed + w_s * v_ref[:, s:s + 1, :].astype(BF16).astype(F32)
    o_ref[...] = u_ref[...] * (mixed + b_ref[...][None])


def _sgu_sample(u, v, w_s, b_s):
    batch, n_new, _ = u.shape
    w_l = jnp.repeat(jnp.transpose(w_s[:, :n_new, :n_new], (2, 1, 0)), C_B, axis=-1)
    b_l = jnp.repeat(b_s[:, :n_new].T, C_B, axis=-1)
    return pl.pallas_call(
        functools.partial(_sgus_body, n_new=n_new),
        out_shape=jax.ShapeDtypeStruct(u.shape, F32),
        name="sgu_sample",
    )(u, v, w_l, b_l)


def _mixer_inputs(h, g_mix, w_in, g_qa, g_ka, g_sgu, g_qm, *, tm):
    gain = jnp.concatenate([jnp.tile(g_qa, H_A), jnp.tile(g_ka, H_A), jnp.ones((W_A + W_B,), F32),
                            g_sgu.reshape(W_B), jnp.tile(g_qm, H_M)])
    seg_tiles = (W_A // PROJ_TILE, W_A // PROJ_TILE, W_A // PROJ_TILE, 1, 1, 1)
    seg_norm = (True, True, False, False, True, True)
    return _proj(h, g_mix, w_in, gain, seg_tiles, seg_norm, tm=tm)


def kernel(x_prompt, x_sample, cache_win_k, cache_win_v, cache_mem_k, cache_mem_v, mem_prompt, rel_bias, g_ffn1, w1_gate, w1_up, w1_down, g_mix, w_in, g_qa, g_ka, g_sgu, w_sgu, b_sgu, g_qm, g_mem, w_mem_kv, g_km, g_mix_out, w_out, g_ffn2, w2_gate, w2_up, w2_down):
    batch, seq, _ = x_prompt.shape
    dec_batch, n_new, _ = x_sample.shape
    depth = g_ffn1.shape[0]
    n_p = batch * seq
    n_s = dec_batch * n_new
    n_past = cache_win_k.shape[2]

    y_p = x_prompt.reshape(n_p, D_MODEL)
    y_s = x_sample.reshape(n_s, D_MODEL)
    outs = [[] for _ in range(7)]
    for l in range(depth):
        mix_w = (g_mix[l], w_in[l], g_qa[l], g_ka[l], g_sgu[l], g_qm[l])

        h = _ffn(y_p, g_ffn1[l], w1_gate[l], w1_up[l], w1_down[l], tm=1024, tf=256)
        zq, zk, zv, zu, zg, zm = _mixer_inputs(h, *mix_w, tm=512)
        oa = _attn_prompt(zq, zk, zv, rel_bias, batch=batch, seq=seq)
        ob = _sgu_prompt(zu, zg, w_sgu[l], b_sgu[l], tm=1024)
        mem = mem_prompt.reshape(batch * N_MEM, D_MODEL)
        gain_kv = jnp.concatenate([jnp.tile(g_km[l], H_M), jnp.ones((W_M,), F32)])
        mk, mv = _proj(mem, g_mem[l], w_mem_kv[l], gain_kv, (1, 1), (True, False), tm=512)
        om = _mem_attend(zm.reshape(batch, seq, W_M), mk.reshape(batch, N_MEM, W_M),
                         mv.reshape(batch, N_MEM, W_M), tq=1024)
        h = _outproj(oa, ob, om.reshape(n_p, W_M), h, g_mix_out[l], w_out[l], tm=1024, tn=512)
        y_p = _ffn(h, g_ffn2[l], w2_gate[l], w2_up[l], w2_down[l], tm=1024, tf=256)
        keep = min(DILATED[-1][0], seq)
        outs[0].append(zk.reshape(batch, seq, H_A, HEAD_DIM)[:, seq - keep:])
        outs[1].append(zv.reshape(batch, seq, H_A, HEAD_DIM)[:, seq - keep:])
        outs[2].append(mk.reshape(batch, N_MEM, H_M, HEAD_DIM))
        outs[3].append(mv.reshape(batch, N_MEM, H_M, HEAD_DIM))

        h = _ffn(y_s, g_ffn1[l], w1_gate[l], w1_up[l], w1_down[l], tm=n_s, tf=512)
        zq, zk, zv, zu, zg, zm = _mixer_inputs(h, *mix_w, tm=n_s)
        heads = (dec_batch, n_new, H_A, HEAD_DIM)
        oa = _attn_sample(zq.reshape(heads), zk.reshape(heads), zv.reshape(heads),
                          cache_win_k[l], cache_win_v[l], rel_bias)
        ob = _sgu_sample(zu.reshape(dec_batch, n_new, W_B), zg.reshape(dec_batch, n_new, W_B),
                         w_sgu[l], b_sgu[l])
        pad = 16 - n_new
        qm = jnp.pad(zm.reshape(dec_batch, n_new, W_M), ((0, 0), (0, pad), (0, 0)))
        om = _mem_attend(qm, cache_mem_k[l].reshape(dec_batch, N_MEM, W_M),
                         cache_mem_v[l].reshape(dec_batch, N_MEM, W_M), tq=n_new + pad)[:, :n_new]
        h = _outproj(oa.reshape(n_s, W_A), ob.reshape(n_s, W_B), om.reshape(n_s, W_M), h,
                     g_mix_out[l], w_out[l], tm=n_s, tn=512)
        y_s = _ffn(h, g_ffn2[l], w2_gate[l], w2_up[l], w2_down[l], tm=n_s, tf=512)
        outs[4].append(zk.reshape(dec_batch, n_new, H_A, HEAD_DIM))
        outs[5].append(zv.reshape(dec_batch, n_new, H_A, HEAD_DIM))
        outs[6].append(zg.reshape(dec_batch, n_new, H_B, C_B))

    return (y_p.reshape(batch, seq, D_MODEL), y_s.reshape(dec_batch, n_new, D_MODEL),
            *[jnp.stack(o) for o in outs])
```

```python
import functools

import numpy as np
import jax
import jax.numpy as jnp
from jax import lax
from jax.experimental import pallas as pl
from jax.experimental.pallas import tpu as pltpu

D_MODEL = 2048
HEAD_DIM = 128
H_A = 8
H_B = 4
C_B = 128
H_M = 4
W_A = H_A * HEAD_DIM
W_B = H_B * C_B
W_M = H_M * HEAD_DIM
D_FF = 5632
DILATED = ((128, 1), (512, 4), (2048, 16))
BLK = 128
CHUNK = 128
N_MEM = 256
N_BUCKETS = 32
REL_MAX_DIST = 2048
EPS = 1e-6
SCALE = HEAD_DIM ** -0.5
NEG = -1e30

LANES = 128
PROJ_TILE = 4 * HEAD_DIM
VMEM_LIMIT = 56 * 1024 * 1024

F32 = jnp.float32
BF16 = jnp.bfloat16


def _rel_bucket(dist):
    dist = np.asarray(dist, np.int64)
    max_exact = N_BUCKETS // 2
    large = max_exact + (np.log(np.maximum(dist, 1) / max_exact) / np.log(REL_MAX_DIST / max_exact)
                         * (N_BUCKETS - max_exact)).astype(np.int64)
    large = np.minimum(large, N_BUCKETS - 1)
    return np.where(dist < max_exact, dist, large).astype(np.int32)


def _rms(t, g):
    return t * lax.rsqrt(jnp.mean(t * t, axis=-1, keepdims=True) + EPS) * g


def _dot(a, b):
    return jnp.dot(a, b, preferred_element_type=F32)


def _dot_nt(a, b):
    return lax.dot_general(a, b, (((1,), (1,)), ((), ())), preferred_element_type=F32)


def _params(*sem):
    return pltpu.CompilerParams(dimension_semantics=sem, vmem_limit_bytes=VMEM_LIMIT)


def _ffn_body(x_ref, g_ref, wg_ref, wu_ref, wd_ref, o_ref, xn_ref):
    @pl.when(pl.program_id(1) == 0)
    def _():
        x = x_ref[...]
        xn_ref[...] = _rms(x, g_ref[...]).astype(BF16)
        o_ref[...] = x

    xn = xn_ref[...]
    gate = _dot(xn, wg_ref[...].astype(BF16))
    up = _dot(xn, wu_ref[...].astype(BF16))
    act = (gate * jax.nn.sigmoid(gate)) * up * 0.5
    o_ref[...] += _dot(act.astype(BF16), wd_ref[...].astype(BF16))


def _ffn(x, g, wg, wu, wd, *, tm, tf):
    n = x.shape[0]
    return pl.pallas_call(
        _ffn_body,
        out_shape=jax.ShapeDtypeStruct((n, D_MODEL), F32),
        grid=(n // tm, D_FF // tf),
        in_specs=[
            pl.BlockSpec((tm, D_MODEL), lambda i, j: (i, 0)),
            pl.BlockSpec((1, D_MODEL), lambda i, j: (0, 0)),
            pl.BlockSpec((D_MODEL, tf), lambda i, j: (0, j)),
            pl.BlockSpec((D_MODEL, tf), lambda i, j: (0, j)),
            pl.BlockSpec((tf, D_MODEL), lambda i, j: (j, 0)),
        ],
        out_specs=pl.BlockSpec((tm, D_MODEL), lambda i, j: (i, 0)),
        scratch_shapes=[pltpu.VMEM((tm, D_MODEL), BF16)],
        compiler_params=_params("parallel", "arbitrary"),
        name="ffn",
    )(x, g.reshape(1, D_MODEL), wg, wu, wd)


def _proj_body(x_ref, g_ref, w_ref, gain_ref, o_ref, xn_ref, *, norm_tiles):
    j = pl.program_id(1)

    @pl.when(j == 0)
    def _():
        xn_ref[...] = _rms(x_ref[...], g_ref[...]).astype(BF16)

    z = _dot(xn_ref[...], w_ref[...].astype(BF16))
    normed = functools.reduce(jnp.logical_or, [j == t for t in norm_tiles])

    @pl.when(normed)
    def _():
        for hh in range(PROJ_TILE // HEAD_DIM):
            sl = slice(hh * HEAD_DIM, (hh + 1) * HEAD_DIM)
            o_ref[:, sl] = _rms(z[:, sl], gain_ref[:, sl])

    @pl.when(jnp.logical_not(normed))
    def _():
        o_ref[...] = z


def _proj(x, g, w, gain, norm_tiles, *, tm):
    n, d_in = x.shape
    d_out = w.shape[1]
    return pl.pallas_call(
        functools.partial(_proj_body, norm_tiles=tuple(norm_tiles)),
        out_shape=jax.ShapeDtypeStruct((n, d_out), F32),
        grid=(n // tm, d_out // PROJ_TILE),
        in_specs=[
            pl.BlockSpec((tm, d_in), lambda i, j: (i, 0), pipeline_mode=pl.Buffered(1)),
            pl.BlockSpec((1, d_in), lambda i, j: (0, 0)),
            pl.BlockSpec((d_in, PROJ_TILE), lambda i, j: (0, j)),
            pl.BlockSpec((1, PROJ_TILE), lambda i, j: (0, j)),
        ],
        out_specs=pl.BlockSpec((tm, PROJ_TILE), lambda i, j: (i, j)),
        scratch_shapes=[pltpu.VMEM((tm, d_in), BF16)],
        compiler_params=_params("parallel", "arbitrary"),
        name="proj",
    )(x, g.reshape(1, d_in), w, gain.reshape(1, d_out))


def _prompt_bias_index():
    step = np.arange(BLK)[:, None] + BLK - np.arange(2 * BLK)[None, :]
    maps = []
    for win, dil in DILATED:
        n_steps = win // dil + 1
        band = (step >= 0) & (step < n_steps)
        bucket = _rel_bucket(np.clip(step, 0, n_steps - 1) * dil)
        maps.append(np.where(band, bucket, -1))
    return np.stack(maps).astype(np.int32)


def _bias_body(idx_ref, tab_ref, o_ref):
    idx = idx_ref[...]
    bias = jnp.full(idx.shape, NEG, F32)
    for bkt in range(N_BUCKETS):
        bias = jnp.where(idx == bkt, tab_ref[:, bkt:bkt + 1], bias)
    o_ref[...] = bias


def _bias_lookup(idx, tab, *, tr, tc):
    n_rows, n_cols = idx.shape
    spec = pl.BlockSpec((tr, tc), lambda i, j: (i, j))
    return pl.pallas_call(
        _bias_body,
        out_shape=jax.ShapeDtypeStruct(idx.shape, F32),
        grid=(n_rows // tr, n_cols // tc),
        in_specs=[spec, pl.BlockSpec((tr, N_BUCKETS), lambda i, j: (i, 0))],
        out_specs=spec,
        compiler_params=_params("parallel", "parallel"),
        name="bias_lookup",
    )(idx, tab)


def _attn_body(bias_s, q_ref, k_ref, v_ref, o_ref, acc_s, m_s, l_s, *, seq, group_size):
    def rows(start, dil):
        return pl.ds(start, BLK) if dil == 1 else pl.ds(start, BLK, stride=dil)

    def gather(ref, starts, dil):
        return jnp.stack([ref[rows(s, dil), :] for s in starts]).astype(BF16)

    def qk(q, k):
        return jnp.einsum("gid,gjd->gij", q, k, preferred_element_type=F32)

    def pv(p, v):
        return jnp.einsum("gij,gjd->gid", p.astype(BF16), v, preferred_element_type=F32)

    def group(di, dil, curs, prevs):
        q = gather(q_ref, curs, dil)
        s_c = qk(q, gather(k_ref, curs, dil)) * SCALE + bias_s[0, di, :, BLK:]
        if prevs is None:
            m = jnp.max(s_c, axis=-1, keepdims=True)
            p_c = jnp.exp(s_c - m)
            den = jnp.sum(p_c, axis=-1, keepdims=True)
            o = pv(p_c, gather(v_ref, curs, dil))
        else:
            s_p = qk(q, gather(k_ref, prevs, dil)) * SCALE + bias_s[0, di, :, :BLK]
            m = jnp.max(jnp.maximum(s_c, s_p), axis=-1, keepdims=True)
            p_c = jnp.exp(s_c - m)
            p_p = jnp.exp(s_p - m)
            den = jnp.sum(p_c + p_p, axis=-1, keepdims=True)
            o = pv(p_c, gather(v_ref, curs, dil)) + pv(p_p, gather(v_ref, prevs, dil))
        for g, s in enumerate(curs):
            rc = rows(s, dil)
            acc_s[di, rc, :] = o[g]
            m_s[di, rc, :] = jnp.broadcast_to(m[g], (BLK, HEAD_DIM))
            l_s[di, rc, :] = jnp.broadcast_to(den[g], (BLK, HEAD_DIM))

    for di, (win, dil) in enumerate(DILATED):
        span = dil * BLK
        first = [r for r in range(dil)]
        later = [r + n * span for r in range(dil) for n in range(1, seq // span)]
        for lo in range(0, len(first), group_size):
            group(di, dil, first[lo:lo + group_size], None)
        for lo in range(0, len(later), group_size):
            curs = later[lo:lo + group_size]
            group(di, dil, curs, [s - span for s in curs])

    rows_per = 256

    def merge(c, carry):
        rc = pl.ds(pl.multiple_of(c * rows_per, rows_per), rows_per)
        ms = [m_s[di, rc, :] for di in range(len(DILATED))]
        mx = jnp.maximum(jnp.maximum(ms[0], ms[1]), ms[2])
        num = 0.0
        den = 0.0
        for di in range(len(DILATED)):
            wgt = jnp.exp(ms[di] - mx)
            num = num + wgt * acc_s[di, rc, :]
            den = den + wgt * l_s[di, rc, :]
        o_ref[rc, :] = num / den
        return carry

    lax.fori_loop(0, seq // rows_per, merge, 0)


def _attn_prompt(z, rel_bias, *, batch, seq):
    n = batch * seq
    assert all(seq % (dil * BLK) == 0 for _, dil in DILATED)
    idx = _prompt_bias_index()
    n_d = idx.shape[0]
    idx_rows = jnp.asarray(np.tile(idx.reshape(n_d * BLK, 2 * BLK), (H_A, 1)))
    tab_rows = jnp.repeat(rel_bias.T, n_d * BLK, axis=0)
    bias = _bias_lookup(idx_rows, tab_rows, tr=n_d * BLK, tc=2 * BLK)
    bias = bias.reshape(H_A, n_d, BLK, 2 * BLK)
    q_spec, k_spec, v_spec = (pl.BlockSpec((seq, HEAD_DIM), lambda h, b, part=part: (b, part * H_A + h))
                              for part in range(3))
    stat = pltpu.VMEM((n_d, seq, HEAD_DIM), F32)
    return pl.pallas_call(
        functools.partial(_attn_body, seq=seq, group_size=8),
        out_shape=jax.ShapeDtypeStruct((n, W_A), F32),
        grid=(H_A, batch),
        in_specs=[
            pl.BlockSpec((1, n_d, BLK, 2 * BLK), lambda h, b: (h, 0, 0, 0)),
            q_spec, k_spec, v_spec,
        ],
        out_specs=pl.BlockSpec((seq, HEAD_DIM), lambda h, b: (b, h)),
        scratch_shapes=[stat, stat, stat],
        compiler_params=_params("parallel", "parallel"),
        name="attn_prompt",
    )(bias, z, z, z)


def _sgu_body(u_ref, v_ref, w_ref, bt_ref, o_ref, *, n_chunks):
    tri = (lax.broadcasted_iota(jnp.int32, (CHUNK, CHUNK), 0)
           >= lax.broadcasted_iota(jnp.int32, (CHUNK, CHUNK), 1))
    for g in range(H_B):
        w = jnp.where(tri, w_ref[g], 0.0).astype(BF16)
        bias = bt_ref[:, g:g + 1]
        cols = slice(g * C_B, (g + 1) * C_B)
        for c in range(n_chunks):
            rws = slice(c * CHUNK, (c + 1) * CHUNK)
            mixed = _dot(w, v_ref[rws, cols].astype(BF16)) + bias
            o_ref[rws, cols] = u_ref[rws, cols] * mixed


def _sgu_prompt(z, u_blk, v_blk, w_s, b_s, *, tm):
    n = z.shape[0]
    return pl.pallas_call(
        functools.partial(_sgu_body, n_chunks=tm // CHUNK),
        out_shape=jax.ShapeDtypeStruct((n, W_B), F32),
        grid=(n // tm,),
        in_specs=[pl.BlockSpec((tm, W_B), lambda i: (i, u_blk)),
                  pl.BlockSpec((tm, W_B), lambda i: (i, v_blk)),
                  pl.BlockSpec((H_B, CHUNK, CHUNK), lambda i: (0, 0, 0)),
                  pl.BlockSpec((CHUNK, H_B), lambda i: (0, 0))],
        out_specs=pl.BlockSpec((tm, W_B), lambda i: (i, 0)),
        compiler_params=_params("parallel"),
        name="sgu_prompt",
    )(z, z, w_s, b_s.T)


def _mem_body(q_ref, k_ref, v_ref, o_ref):
    for hh in range(H_M):
        sl = slice(hh * HEAD_DIM, (hh + 1) * HEAD_DIM)
        s = _dot_nt(q_ref[0, :, sl].astype(BF16), k_ref[0, :, sl].astype(BF16)) * SCALE
        m = jnp.max(s, axis=-1, keepdims=True)
        p = jnp.exp(s - m)
        den = jnp.sum(p, axis=-1, keepdims=True)
        o_ref[0, :, sl] = _dot(p.astype(BF16), v_ref[0, :, sl].astype(BF16)) / den


def _mem_attend(q, k, v, *, tq, q_blk=0, k_blk=0, v_blk=0):
    batch, t, _ = q.shape
    return pl.pallas_call(
        _mem_body,
        out_shape=jax.ShapeDtypeStruct((batch, t, W_M), F32),
        grid=(batch, t // tq),
        in_specs=[pl.BlockSpec((1, tq, W_M), lambda b, i: (b, i, q_blk)),
                  pl.BlockSpec((1, N_MEM, W_M), lambda b, i: (b, 0, k_blk)),
                  pl.BlockSpec((1, N_MEM, W_M), lambda b, i: (b, 0, v_blk))],
        out_specs=pl.BlockSpec((1, tq, W_M), lambda b, i: (b, i, 0)),
        compiler_params=_params("parallel", "parallel"),
        name="mem_attend",
    )(q, k, v)


def _outproj_body(oa_ref, ob_ref, om_ref, h_ref, g_ref, w_ref, o_ref, cat_ref):
    @pl.when(pl.program_id(1) == 0)
    def _():
        lo = 0
        for src in (oa_ref, ob_ref, om_ref):
            hi = lo + src.shape[1]
            cat_ref[:, lo:hi] = _rms(src[...], g_ref[:, lo:hi]).astype(BF16)
            lo = hi

    o_ref[...] = h_ref[...] + _dot(cat_ref[...], w_ref[...].astype(BF16))


def _outproj(oa, ob, om, h, g, w, *, tm, tn):
    n = h.shape[0]
    mix = W_A + W_B + W_M
    return pl.pallas_call(
        _outproj_body,
        out_shape=jax.ShapeDtypeStruct((n, D_MODEL), F32),
        grid=(n // tm, D_MODEL // tn),
        in_specs=[
            pl.BlockSpec((tm, W_A), lambda i, j: (i, 0)),
            pl.BlockSpec((tm, W_B), lambda i, j: (i, 0)),
            pl.BlockSpec((tm, W_M), lambda i, j: (i, 0)),
            pl.BlockSpec((tm, tn), lambda i, j: (i, j)),
            pl.BlockSpec((1, mix), lambda i, j: (0, 0)),
            pl.BlockSpec((mix, tn), lambda i, j: (0, j)),
        ],
        out_specs=pl.BlockSpec((tm, tn), lambda i, j: (i, j)),
        scratch_shapes=[pltpu.VMEM((tm, mix), BF16)],
        compiler_params=_params("parallel", "arbitrary"),
        name="outproj",
    )(oa, ob, om, h, g.reshape(1, mix), w)


def _sample_bias_index(n_new, n_past):
    (win1, dil1), (tail, dil4), (win16, dil16) = DILATED
    assert dil1 == 1 and win1 == BLK and n_past == win16 and n_past % dil16 == 0
    assert n_past >= tail and n_new <= dil4 and n_new * H_A <= LANES
    n16 = n_past // dil16
    t = np.arange(n_new)[:, None]

    def per_head(index):
        same = np.eye(H_A, dtype=bool)[None, :, None, :]
        full = np.where(same, index[:, None, :, None], -1)
        return full.reshape(n_new * H_A, index.shape[1] * H_A)

    dist = BLK + t - np.arange(BLK)[None, :]
    d1 = np.where(dist <= win1, _rel_bucket(dist), -1)
    dist = tail + t - np.arange(tail)[None, :]
    d4 = np.where((dist % dil4 == 0) & (dist <= tail), _rel_bucket(dist), -1)
    m, tk = np.divmod(np.arange(n16 * n_new)[None, :], n_new)
    dist = n_past + t - dil16 * m - tk
    d16 = np.where(tk == t, _rel_bucket(np.maximum(dist, 0)), -1)
    s = np.arange(LANES // H_A)[None, :]
    dist = t - s
    new = [np.where((s < n_new) & (dist >= 0) & (dist % dil == 0),
                    _rel_bucket(np.maximum(dist, 0)), -1) for _, dil in DILATED]
    pieces = [d1, d4, d16] + new
    return np.concatenate([per_head(p) for p in pieces], axis=1).astype(np.int32)


def _sattn_body(bias_ref, q_ref, kn_ref, vn_ref, kt_ref, k16_ref, vt_ref, v16_ref, o_ref):
    def flat(ref):
        x = ref[0]
        return x.reshape(-1, HEAD_DIM).astype(BF16)

    q = q_ref[0].astype(BF16)
    kt, vt = flat(kt_ref), flat(vt_ref)
    k16, v16 = flat(k16_ref), flat(v16_ref)
    kn, vn = kn_ref[0].astype(BF16), vn_ref[0].astype(BF16)
    n_t, n_16, n_1 = kt.shape[0], k16.shape[0], BLK * H_A

    raw_t = _dot_nt(q, kt)
    raw_n = _dot_nt(q, kn)
    raws = [raw_t[:, n_t - n_1:], raw_t, _dot_nt(q, k16)] + [raw_n] * len(DILATED)
    logits = []
    lo = 0
    for raw in raws:
        hi = lo + raw.shape[1]
        logits.append(raw * SCALE + bias_ref[:, lo:hi])
        lo = hi

    m = functools.reduce(jnp.maximum, [jnp.max(x, axis=-1, keepdims=True) for x in logits])
    probs = [jnp.exp(x - m) for x in logits]
    den = functools.reduce(lambda a, b: a + b, [jnp.sum(p, axis=-1, keepdims=True) for p in probs])
    p_new = functools.reduce(lambda a, b: a + b, probs[3:])
    o = (_dot(probs[0].astype(BF16), vt[n_t - n_1:, :]) + _dot(probs[1].astype(BF16), vt)
         + _dot(probs[2].astype(BF16), v16) + _dot(p_new.astype(BF16), vn))
    o_ref[0] = o / den


def _attn_sample(qa, ka, va, cache_k, cache_v, rel_bias):
    batch, n_new = qa.shape[:2]
    n_past = cache_k.shape[1]
    tail = DILATED[1][0]
    dil16 = DILATED[2][1]
    n16 = n_past // dil16
    n_rows = n_new * H_A
    idx = _sample_bias_index(n_new, n_past)
    bias = _bias_lookup(jnp.asarray(idx), jnp.tile(rel_bias.T, (n_new, 1)),
                        tr=n_rows, tc=3 * LANES)

    def rows(x):
        x = x.reshape(batch, n_rows, HEAD_DIM)
        return jnp.pad(x, ((0, 0), (0, LANES - n_rows), (0, 0)))

    k_res = cache_k.reshape(batch, n16, dil16, H_A, HEAD_DIM)
    v_res = cache_v.reshape(batch, n16, dil16, H_A, HEAD_DIM)
    q_spec = pl.BlockSpec((1, n_rows, HEAD_DIM), lambda b: (b, 0, 0))
    new_spec = pl.BlockSpec((1, LANES, HEAD_DIM), lambda b: (b, 0, 0))
    tail_spec = pl.BlockSpec((1, tail, H_A, HEAD_DIM), lambda b: (b, n_past // tail - 1, 0, 0))
    res_spec = pl.BlockSpec((1, n16, n_new, H_A, HEAD_DIM), lambda b: (b, 0, 0, 0, 0))
    out = pl.pallas_call(
        _sattn_body,
        out_shape=jax.ShapeDtypeStruct((batch, n_rows, HEAD_DIM), F32),
        grid=(batch,),
        in_specs=[
            pl.BlockSpec(bias.shape, lambda b: (0, 0)),
            q_spec, new_spec, new_spec, tail_spec, res_spec, tail_spec, res_spec,
        ],
        out_specs=q_spec,
        compiler_params=_params("parallel"),
        name="attn_sample",
    )(bias, qa.reshape(batch, n_rows, HEAD_DIM), rows(ka), rows(va),
      cache_k, k_res, cache_v, v_res)
    return out.reshape(batch, n_new, W_A)


def _sgus_body(u_ref, v_ref, w_ref, b_ref, o_ref, *, n_new):
    t_idx = lax.broadcasted_iota(jnp.int32, (1, n_new, W_B), 1)
    mixed = 0.0
    for s in range(n_new):
        w_s = jnp.where(t_idx >= s, w_ref[s][None], 0.0).astype(BF16).astype(F32)
        mixed = mixed + w_s * v_ref[:, s:s + 1, :].astype(BF16).astype(F32)
    o_ref[...] = u_ref[...] * (mixed + b_ref[...][None])


def _sgu_sample(u, v, w_s, b_s):
    batch, n_new, _ = u.shape
    w_l = jnp.repeat(jnp.transpose(w_s[:, :n_new, :n_new], (2, 1, 0)), C_B, axis=-1)
    b_l = jnp.repeat(b_s[:, :n_new].T, C_B, axis=-1)
    return pl.pallas_call(
        functools.partial(_sgus_body, n_new=n_new),
        out_shape=jax.ShapeDtypeStruct(u.shape, F32),
        name="sgu_sample",
    )(u, v, w_l, b_l)


COL_KA, COL_VA, COL_U, COL_V, COL_QM = W_A, 2 * W_A, 3 * W_A, 3 * W_A + W_B, 3 * W_A + 2 * W_B


def _mixer_inputs(h, g_mix, w_in, g_qa, g_ka, g_sgu, g_qm, *, tm):
    gain = jnp.concatenate([jnp.tile(g_qa, H_A), jnp.tile(g_ka, H_A), jnp.ones((W_A + W_B,), F32),
                            g_sgu.reshape(W_B), jnp.tile(g_qm, H_M)])
    normed = [c // PROJ_TILE for lo, hi in ((0, COL_VA), (COL_V, COL_QM + W_M))
              for c in range(lo, hi, PROJ_TILE)]
    return _proj(h, g_mix, w_in, gain, normed, tm=tm)


def kernel(x_prompt, x_sample, cache_win_k, cache_win_v, cache_mem_k, cache_mem_v, mem_prompt, rel_bias, g_ffn1, w1_gate, w1_up, w1_down, g_mix, w_in, g_qa, g_ka, g_sgu, w_sgu, b_sgu, g_qm, g_mem, w_mem_kv, g_km, g_mix_out, w_out, g_ffn2, w2_gate, w2_up, w2_down):
    batch, seq, _ = x_prompt.shape
    dec_batch, n_new, _ = x_sample.shape
    depth = g_ffn1.shape[0]
    n_p = batch * seq
    n_s = dec_batch * n_new
    n_past = cache_win_k.shape[2]

    y_p = x_prompt.reshape(n_p, D_MODEL)
    y_s = x_sample.reshape(n_s, D_MODEL)
    outs = [[] for _ in range(7)]
    for l in range(depth):
        mix_w = (g_mix[l], w_in[l], g_qa[l], g_ka[l], g_sgu[l], g_qm[l])

        h = _ffn(y_p, g_ffn1[l], w1_gate[l], w1_up[l], w1_down[l], tm=1024, tf=256)
        z = _mixer_inputs(h, *mix_w, tm=2048)
        oa = _attn_prompt(z, rel_bias, batch=batch, seq=seq)
        ob = _sgu_prompt(z, COL_U // W_B, COL_V // W_B, w_sgu[l], b_sgu[l], tm=1024)
        mem = mem_prompt.reshape(batch * N_MEM, D_MODEL)
        gain_kv = jnp.concatenate([jnp.tile(g_km[l], H_M), jnp.ones((W_M,), F32)])
        mkv = _proj(mem, g_mem[l], w_mem_kv[l], gain_kv, range(W_M // PROJ_TILE), tm=1024)
        mkv3 = mkv.reshape(batch, N_MEM, 2 * W_M)
        om = _mem_attend(z.reshape(batch, seq, -1), mkv3, mkv3, tq=1024,
                         q_blk=COL_QM // W_M, k_blk=0, v_blk=1)
        h = _outproj(oa, ob, om.reshape(n_p, W_M), h, g_mix_out[l], w_out[l], tm=1024, tn=512)
        y_p = _ffn(h, g_ffn2[l], w2_gate[l], w2_up[l], w2_down[l], tm=1024, tf=256)
        keep = min(DILATED[-1][0], seq)
        z4 = z.reshape(batch, seq, -1)[:, seq - keep:]
        outs[0].append(z4[..., COL_KA:COL_VA].reshape(batch, keep, H_A, HEAD_DIM))
        outs[1].append(z4[..., COL_VA:COL_U].reshape(batch, keep, H_A, HEAD_DIM))
        outs[2].append(mkv[:, :W_M].reshape(batch, N_MEM, H_M, HEAD_DIM))
        outs[3].append(mkv[:, W_M:].reshape(batch, N_MEM, H_M, HEAD_DIM))

        h = _ffn(y_s, g_ffn1[l], w1_gate[l], w1_up[l], w1_down[l], tm=n_s, tf=512)
        z = _mixer_inputs(h, *mix_w, tm=n_s)
        zq, zk, zv, zu, zg, zm = (z[:, lo:hi] for lo, hi in zip(
            (0, COL_KA, COL_VA, COL_U, COL_V, COL_QM), (COL_KA, COL_VA, COL_U, COL_V, COL_QM, z.shape[1])))
        heads = (dec_batch, n_new, H_A, HEAD_DIM)
        oa = _attn_sample(zq.reshape(heads), zk.reshape(heads), zv.reshape(heads),
                          cache_win_k[l], cache_win_v[l], rel_bias)
        ob = _sgu_sample(zu.reshape(dec_batch, n_new, W_B), zg.reshape(dec_batch, n_new, W_B),
                         w_sgu[l], b_sgu[l])
        pad = 16 - n_new
        qm = jnp.pad(zm.reshape(dec_batch, n_new, W_M), ((0, 0), (0, pad), (0, 0)))
        om = _mem_attend(qm, cache_mem_k[l].reshape(dec_batch, N_MEM, W_M),
                         cache_mem_v[l].reshape(dec_batch, N_MEM, W_M), tq=n_new + pad)[:, :n_new]
        h = _outproj(oa.reshape(n_s, W_A), ob.reshape(n_s, W_B), om.reshape(n_s, W_M), h,
                     g_mix_out[l], w_out[l], tm=n_s, tn=512)
        y_s = _ffn(h, g_ffn2[l], w2_gate[l], w2_up[l], w2_down[l], tm=n_s, tf=512)
        outs[4].append(zk.reshape(dec_batch, n_new, H_A, HEAD_DIM))
        outs[5].append(zv.reshape(dec_batch, n_new, H_A, HEAD_DIM))
        outs[6].append(zg.reshape(dec_batch, n_new, H_B, C_B))

    return (y_p.reshape(batch, seq, D_MODEL), y_s.reshape(dec_batch, n_new, D_MODEL),
            *[jnp.stack(o) for o in outs])
```

```python
import functools

import numpy as np
import jax
import jax.numpy as jnp
from jax import lax
from jax.experimental import pallas as pl
from jax.experimental.pallas import tpu as pltpu

D_MODEL = 2048
HEAD_DIM = 128
H_A = 8
H_B = 4
C_B = 128
H_M = 4
W_A = H_A * HEAD_DIM
W_B = H_B * C_B
W_M = H_M * HEAD_DIM
D_FF = 5632
DILATED = ((128, 1), (512, 4), (2048, 16))
BLK = 128
CHUNK = 128
N_MEM = 256
N_BUCKETS = 32
REL_MAX_DIST = 2048
EPS = 1e-6
SCALE = HEAD_DIM ** -0.5
NEG = -1e30

LANES = 128
PROJ_TILE = 4 * HEAD_DIM
VMEM_LIMIT = 56 * 1024 * 1024

F32 = jnp.float32
BF16 = jnp.bfloat16


def _rel_bucket(dist):
    dist = np.asarray(dist, np.int64)
    max_exact = N_BUCKETS // 2
    large = max_exact + (np.log(np.maximum(dist, 1) / max_exact) / np.log(REL_MAX_DIST / max_exact)
                         * (N_BUCKETS - max_exact)).astype(np.int64)
    large = np.minimum(large, N_BUCKETS - 1)
    return np.where(dist < max_exact, dist, large).astype(np.int32)


def _rms(t, g):
    return t * lax.rsqrt(jnp.mean(t * t, axis=-1, keepdims=True) + EPS) * g


def _dot(a, b):
    return jnp.dot(a, b, preferred_element_type=F32)


def _dot_nt(a, b):
    return lax.dot_general(a, b, (((1,), (1,)), ((), ())), preferred_element_type=F32)


def _params(*sem):
    return pltpu.CompilerParams(dimension_semantics=sem, vmem_limit_bytes=VMEM_LIMIT)


def _ffn_step(x_ref, g_ref, wg_ref, wu_ref, wd_ref, o_ref, xn_ref):
    @pl.when(pl.program_id(1) == 0)
    def _():
        x = x_ref[...]
        xn_ref[...] = _rms(x, g_ref[...]).astype(BF16)
        o_ref[...] = x

    xn = xn_ref[...]
    gate = _dot(xn, wg_ref[...].astype(BF16))
    up = _dot(xn, wu_ref[...].astype(BF16))
    act = (gate * jax.nn.sigmoid(gate)) * up * 0.5
    o_ref[...] += _dot(act.astype(BF16), wd_ref[...].astype(BF16))


def _ffn_body(x_ref, xs_ref, g_ref, wg_ref, wu_ref, wd_ref, o_ref, os_ref, xn_ref, xsn_ref):
    _ffn_step(x_ref, g_ref, wg_ref, wu_ref, wd_ref, o_ref, xn_ref)

    @pl.when(pl.program_id(0) == 0)
    def _():
        _ffn_step(xs_ref, g_ref, wg_ref, wu_ref, wd_ref, os_ref, xsn_ref)


def _ffn(x, xs, g, wg, wu, wd, *, tm, tf):
    n, ns = x.shape[0], xs.shape[0]
    once = pl.Buffered(1)
    return pl.pallas_call(
        _ffn_body,
        out_shape=[jax.ShapeDtypeStruct((n, D_MODEL), F32), jax.ShapeDtypeStruct((ns, D_MODEL), F32)],
        grid=(n // tm, D_FF // tf),
        in_specs=[
            pl.BlockSpec((tm, D_MODEL), lambda i, j: (i, 0)),
            pl.BlockSpec((ns, D_MODEL), lambda i, j: (0, 0), pipeline_mode=once),
            pl.BlockSpec((1, D_MODEL), lambda i, j: (0, 0), pipeline_mode=once),
            pl.BlockSpec((D_MODEL, tf), lambda i, j: (0, j)),
            pl.BlockSpec((D_MODEL, tf), lambda i, j: (0, j)),
            pl.BlockSpec((tf, D_MODEL), lambda i, j: (j, 0)),
        ],
        out_specs=[pl.BlockSpec((tm, D_MODEL), lambda i, j: (i, 0)),
                   pl.BlockSpec((ns, D_MODEL), lambda i, j: (0, 0))],
        scratch_shapes=[pltpu.VMEM((tm, D_MODEL), BF16), pltpu.VMEM((ns, D_MODEL), BF16)],
        compiler_params=_params("arbitrary", "arbitrary"),
        name="ffn",
    )(x, xs, g.reshape(1, D_MODEL), wg, wu, wd)


def _proj_step(x_ref, g_ref, w_ref, gain_ref, o_ref, xn_ref, norm_tiles):
    j = pl.program_id(1)

    @pl.when(j == 0)
    def _():
        xn_ref[...] = _rms(x_ref[...], g_ref[...]).astype(BF16)

    z = _dot(xn_ref[...], w_ref[...].astype(BF16))
    normed = functools.reduce(jnp.logical_or, [j == t for t in norm_tiles])

    @pl.when(normed)
    def _():
        for hh in range(PROJ_TILE // HEAD_DIM):
            sl = slice(hh * HEAD_DIM, (hh + 1) * HEAD_DIM)
            o_ref[:, sl] = _rms(z[:, sl], gain_ref[:, sl])

    @pl.when(jnp.logical_not(normed))
    def _():
        o_ref[...] = z


def _proj_body(x_ref, g_ref, w_ref, gain_ref, o_ref, xn_ref, *, norm_tiles):
    _proj_step(x_ref, g_ref, w_ref, gain_ref, o_ref, xn_ref, norm_tiles)


def _rider_map(n_i):
    def index_map(i, j):
        last = i == n_i - 1
        return jnp.where(last, 0, 1), jnp.where(last, j, 0)
    return index_map


def _proj_rider_body(x_ref, xs_ref, g_ref, w_ref, gain_ref, o_ref, os_ref, xn_ref, xsn_ref, *,
                     norm_tiles):
    i, j = pl.program_id(0), pl.program_id(1)
    _proj_step(x_ref, g_ref, w_ref, gain_ref, o_ref, xn_ref, norm_tiles)

    @pl.when((i == 0) & (j == 0))
    def _():
        os_ref[...] = jnp.zeros(os_ref.shape, F32)

    @pl.when(i == pl.num_programs(0) - 1)
    def _():
        _proj_step(xs_ref, g_ref, w_ref, gain_ref, os_ref, xsn_ref, norm_tiles)


def _proj(x, xs, g, w, gain, norm_tiles, *, tm):
    n, d_in = x.shape
    d_out = w.shape[1]
    n_i = n // tm
    once = pl.Buffered(1)
    x_spec = pl.BlockSpec((tm, d_in), lambda i, j: (i, 0), pipeline_mode=once)
    w_specs = [pl.BlockSpec((1, d_in), lambda i, j: (0, 0), pipeline_mode=once),
               pl.BlockSpec((d_in, PROJ_TILE), lambda i, j: (0, j)),
               pl.BlockSpec((1, PROJ_TILE), lambda i, j: (0, j))]
    o_spec = pl.BlockSpec((tm, PROJ_TILE), lambda i, j: (i, j))
    w_args = (g.reshape(1, d_in), w, gain.reshape(1, d_out))
    if xs is None:
        return pl.pallas_call(
            functools.partial(_proj_body, norm_tiles=tuple(norm_tiles)),
            out_shape=jax.ShapeDtypeStruct((n, d_out), F32),
            grid=(n_i, d_out // PROJ_TILE),
            in_specs=[x_spec] + w_specs,
            out_specs=o_spec,
            scratch_shapes=[pltpu.VMEM((tm, d_in), BF16)],
            compiler_params=_params("arbitrary", "arbitrary"),
            name="proj",
        )(x, *w_args)
    ns = xs.shape[0]
    z, zs = pl.pallas_call(
        functools.partial(_proj_rider_body, norm_tiles=tuple(norm_tiles)),
        out_shape=[jax.ShapeDtypeStruct((n, d_out), F32), jax.ShapeDtypeStruct((2 * ns, d_out), F32)],
        grid=(n_i, d_out // PROJ_TILE),
        in_specs=[x_spec, pl.BlockSpec((ns, d_in), lambda i, j: (0, 0), pipeline_mode=once)] + w_specs,
        out_specs=[o_spec, pl.BlockSpec((ns, PROJ_TILE), _rider_map(n_i))],
        scratch_shapes=[pltpu.VMEM((tm, d_in), BF16), pltpu.VMEM((ns, d_in), BF16)],
        compiler_params=_params("arbitrary", "arbitrary"),
        name="proj_rider",
    )(x, xs, *w_args)
    return z, zs[:ns]


def _prompt_bias_index():
    step = np.arange(BLK)[:, None] + BLK - np.arange(2 * BLK)[None, :]
    maps = []
    for win, dil in DILATED:
        n_steps = win // dil + 1
        band = (step >= 0) & (step < n_steps)
        bucket = _rel_bucket(np.clip(step, 0, n_steps - 1) * dil)
        maps.append(np.where(band, bucket, -1))
    return np.stack(maps).astype(np.int32)


def _bias_body(idx_ref, tab_ref, o_ref):
    idx = idx_ref[...]
    bias = jnp.full(idx.shape, NEG, F32)
    for bkt in range(N_BUCKETS):
        bias = jnp.where(idx == bkt, tab_ref[:, bkt:bkt + 1], bias)
    o_ref[...] = bias


def _bias_lookup(idx, tab, *, tr, tc):
    n_rows, n_cols = idx.shape
    spec = pl.BlockSpec((tr, tc), lambda i, j: (i, j))
    return pl.pallas_call(
        _bias_body,
        out_shape=jax.ShapeDtypeStruct(idx.shape, F32),
        grid=(n_rows // tr, n_cols // tc),
        in_specs=[spec, pl.BlockSpec((tr, N_BUCKETS), lambda i, j: (i, 0))],
        out_specs=spec,
        compiler_params=_params("parallel", "parallel"),
        name="bias_lookup",
    )(idx, tab)


def _attn_body(bias_s, q_ref, k_ref, v_ref, o_ref, acc_s, m_s, l_s, *, seq, group_size):
    def rows(start, dil):
        return pl.ds(start, BLK) if dil == 1 else pl.ds(start, BLK, stride=dil)

    def gather(ref, starts, dil):
        return jnp.stack([ref[rows(s, dil), :] for s in starts]).astype(BF16)

    def qk(q, k):
        return jnp.einsum("gid,gjd->gij", q, k, preferred_element_type=F32)

    def pv(p, v):
        return jnp.einsum("gij,gjd->gid", p.astype(BF16), v, preferred_element_type=F32)

    def group(di, dil, curs, prevs):
        q = gather(q_ref, curs, dil)
        s_c = qk(q, gather(k_ref, curs, dil)) * SCALE + bias_s[0, di, :, BLK:]
        if prevs is None:
            m = jnp.max(s_c, axis=-1, keepdims=True)
            p_c = jnp.exp(s_c - m)
            den = jnp.sum(p_c, axis=-1, keepdims=True)
            o = pv(p_c, gather(v_ref, curs, dil))
        else:
            s_p = qk(q, gather(k_ref, prevs, dil)) * SCALE + bias_s[0, di, :, :BLK]
            m = jnp.max(jnp.maximum(s_c, s_p), axis=-1, keepdims=True)
            p_c = jnp.exp(s_c - m)
            p_p = jnp.exp(s_p - m)
            den = jnp.sum(p_c + p_p, axis=-1, keepdims=True)
            o = pv(p_c, gather(v_ref, curs, dil)) + pv(p_p, gather(v_ref, prevs, dil))
        for g, s in enumerate(curs):
            rc = rows(s, dil)
            acc_s[di, rc, :] = o[g]
            m_s[di, rc, :] = jnp.broadcast_to(m[g], (BLK, HEAD_DIM))
            l_s[di, rc, :] = jnp.broadcast_to(den[g], (BLK, HEAD_DIM))

    for di, (win, dil) in enumerate(DILATED):
        span = dil * BLK
        first = [r for r in range(dil)]
        later = [r + n * span for r in range(dil) for n in range(1, seq // span)]
        for lo in range(0, len(first), group_size):
            group(di, dil, first[lo:lo + group_size], None)
        for lo in range(0, len(later), group_size):
            curs = later[lo:lo + group_size]
            group(di, dil, curs, [s - span for s in curs])

    rows_per = 256

    def merge(c, carry):
        rc = pl.ds(pl.multiple_of(c * rows_per, rows_per), rows_per)
        ms = [m_s[di, rc, :] for di in range(len(DILATED))]
        mx = jnp.maximum(jnp.maximum(ms[0], ms[1]), ms[2])
        num = 0.0
        den = 0.0
        for di in range(len(DILATED)):
            wgt = jnp.exp(ms[di] - mx)
            num = num + wgt * acc_s[di, rc, :]
            den = den + wgt * l_s[di, rc, :]
        o_ref[rc, :] = num / den
        return carry

    lax.fori_loop(0, seq // rows_per, merge, 0)


def _attn_prompt(z, rel_bias, *, batch, seq):
    n = batch * seq
    assert all(seq % (dil * BLK) == 0 for _, dil in DILATED)
    idx = _prompt_bias_index()
    n_d = idx.shape[0]
    idx_rows = jnp.asarray(np.tile(idx.reshape(n_d * BLK, 2 * BLK), (H_A, 1)))
    tab_rows = jnp.repeat(rel_bias.T, n_d * BLK, axis=0)
    bias = _bias_lookup(idx_rows, tab_rows, tr=n_d * BLK, tc=2 * BLK)
    bias = bias.reshape(H_A, n_d, BLK, 2 * BLK)
    q_spec, k_spec, v_spec = (pl.BlockSpec((seq, HEAD_DIM), lambda h, b, part=part: (b, part * H_A + h))
                              for part in range(3))
    stat = pltpu.VMEM((n_d, seq, HEAD_DIM), F32)
    return pl.pallas_call(
        functools.partial(_attn_body, seq=seq, group_size=8),
        out_shape=jax.ShapeDtypeStruct((n, W_A), F32),
        grid=(H_A, batch),
        in_specs=[
            pl.BlockSpec((1, n_d, BLK, 2 * BLK), lambda h, b: (h, 0, 0, 0)),
            q_spec, k_spec, v_spec,
        ],
        out_specs=pl.BlockSpec((seq, HEAD_DIM), lambda h, b: (b, h)),
        scratch_shapes=[stat, stat, stat],
        compiler_params=_params("parallel", "parallel"),
        name="attn_prompt",
    )(bias, z, z, z)


def _sgu_body(u_ref, v_ref, w_ref, bt_ref, o_ref, *, n_chunks):
    tri = (lax.broadcasted_iota(jnp.int32, (CHUNK, CHUNK), 0)
           >= lax.broadcasted_iota(jnp.int32, (CHUNK, CHUNK), 1))
    for g in range(H_B):
        w = jnp.where(tri, w_ref[g], 0.0).astype(BF16)
        bias = bt_ref[:, g:g + 1]
        cols = slice(g * C_B, (g + 1) * C_B)
        for c in range(n_chunks):
            rws = slice(c * CHUNK, (c + 1) * CHUNK)
            mixed = _dot(w, v_ref[rws, cols].astype(BF16)) + bias
            o_ref[rws, cols] = u_ref[rws, cols] * mixed


def _sgu_prompt(z, u_blk, v_blk, w_s, b_s, *, tm):
    n = z.shape[0]
    return pl.pallas_call(
        functools.partial(_sgu_body, n_chunks=tm // CHUNK),
        out_shape=jax.ShapeDtypeStruct((n, W_B), F32),
        grid=(n // tm,),
        in_specs=[pl.BlockSpec((tm, W_B), lambda i: (i, u_blk)),
                  pl.BlockSpec((tm, W_B), lambda i: (i, v_blk)),
                  pl.BlockSpec((H_B, CHUNK, CHUNK), lambda i: (0, 0, 0)),
                  pl.BlockSpec((CHUNK, H_B), lambda i: (0, 0))],
        out_specs=pl.BlockSpec((tm, W_B), lambda i: (i, 0)),
        compiler_params=_params("parallel"),
        name="sgu_prompt",
    )(z, z, w_s, b_s.T)


def _mem_body(q_ref, k_ref, v_ref, o_ref):
    for hh in range(H_M):
        sl = slice(hh * HEAD_DIM, (hh + 1) * HEAD_DIM)
        s = _dot_nt(q_ref[0, :, sl].astype(BF16), k_ref[0, :, sl].astype(BF16)) * SCALE
        m = jnp.max(s, axis=-1, keepdims=True)
        p = jnp.exp(s - m)
        den = jnp.sum(p, axis=-1, keepdims=True)
        o_ref[0, :, sl] = _dot(p.astype(BF16), v_ref[0, :, sl].astype(BF16)) / den


def _mem_attend(q, k, v, *, tq, q_blk=0, k_blk=0, v_blk=0):
    batch, t, _ = q.shape
    return pl.pallas_call(
        _mem_body,
        out_shape=jax.ShapeDtypeStruct((batch, t, W_M), F32),
        grid=(batch, t // tq),
        in_specs=[pl.BlockSpec((1, tq, W_M), lambda b, i: (b, i, q_blk)),
                  pl.BlockSpec((1, N_MEM, W_M), lambda b, i: (b, 0, k_blk)),
                  pl.BlockSpec((1, N_MEM, W_M), lambda b, i: (b, 0, v_blk))],
        out_specs=pl.BlockSpec((1, tq, W_M), lambda b, i: (b, i, 0)),
        compiler_params=_params("parallel", "parallel"),
        name="mem_attend",
    )(q, k, v)


def _mems_body(q_ref, k_ref, v_ref, o_ref):
    s = _dot_nt(q_ref[0].astype(BF16), k_ref[0].astype(BF16)) * SCALE
    row = lax.broadcasted_iota(jnp.int32, s.shape, 0)
    col = lax.broadcasted_iota(jnp.int32, s.shape, 1)
    s = jnp.where(row % H_M == col % H_M, s, NEG)
    m = jnp.max(s, axis=-1, keepdims=True)
    p = jnp.exp(s - m)
    den = jnp.sum(p, axis=-1, keepdims=True)
    o_ref[0] = _dot(p.astype(BF16), v_ref[0].astype(BF16)) / den


def _mem_attend_sample(q, cache_k, cache_v):
    batch, n_new = q.shape[:2]
    n_rows = n_new * H_M
    q_spec = pl.BlockSpec((1, n_rows, HEAD_DIM), lambda b: (b, 0, 0))
    kv_spec = pl.BlockSpec((1, N_MEM * H_M, HEAD_DIM), lambda b: (b, 0, 0))
    out = pl.pallas_call(
        _mems_body,
        out_shape=jax.ShapeDtypeStruct((batch, n_rows, HEAD_DIM), F32),
        grid=(batch,),
        in_specs=[q_spec, kv_spec, kv_spec],
        out_specs=q_spec,
        compiler_params=_params("parallel"),
        name="mem_attend_sample",
    )(q.reshape(batch, n_rows, HEAD_DIM), cache_k.reshape(batch, N_MEM * H_M, HEAD_DIM),
      cache_v.reshape(batch, N_MEM * H_M, HEAD_DIM))
    return out.reshape(batch, n_new, W_M)


def _outproj_step(oa_ref, ob_ref, om_ref, h_ref, g_ref, w_ref, o_ref, cat_ref):
    @pl.when(pl.program_id(1) == 0)
    def _():
        lo = 0
        for src in (oa_ref, ob_ref, om_ref):
            hi = lo + src.shape[1]
            cat_ref[:, lo:hi] = _rms(src[...], g_ref[:, lo:hi]).astype(BF16)
            lo = hi

    o_ref[...] = h_ref[...] + _dot(cat_ref[...], w_ref[...].astype(BF16))


def _outproj_body(oa_ref, ob_ref, om_ref, h_ref, oas_ref, obs_ref, oms_ref, hs_ref, g_ref, w_ref,
                  o_ref, os_ref, cat_ref, cats_ref):
    i, j = pl.program_id(0), pl.program_id(1)
    _outproj_step(oa_ref, ob_ref, om_ref, h_ref, g_ref, w_ref, o_ref, cat_ref)

    @pl.when((i == 0) & (j == 0))
    def _():
        os_ref[...] = jnp.zeros(os_ref.shape, F32)

    @pl.when(i == pl.num_programs(0) - 1)
    def _():
        _outproj_step(oas_ref, obs_ref, oms_ref, hs_ref, g_ref, w_ref, os_ref, cats_ref)


def _outproj(prompt, sample, g, w, *, tm, tn):
    n, ns = prompt[3].shape[0], sample[3].shape[0]
    mix = W_A + W_B + W_M
    n_i = n // tm
    once = pl.Buffered(1)
    widths = (W_A, W_B, W_M)
    return_shapes = [jax.ShapeDtypeStruct((n, D_MODEL), F32), jax.ShapeDtypeStruct((2 * ns, D_MODEL), F32)]
    out, outs = pl.pallas_call(
        _outproj_body,
        out_shape=return_shapes,
        grid=(n_i, D_MODEL // tn),
        in_specs=(
            [pl.BlockSpec((tm, wd), lambda i, j: (i, 0)) for wd in widths]
            + [pl.BlockSpec((tm, tn), lambda i, j: (i, j))]
            + [pl.BlockSpec((ns, wd), lambda i, j: (0, 0), pipeline_mode=once) for wd in widths]
            + [pl.BlockSpec((ns, tn), lambda i, j: (0, j)),
               pl.BlockSpec((1, mix), lambda i, j: (0, 0), pipeline_mode=once),
               pl.BlockSpec((mix, tn), lambda i, j: (0, j))]),
        out_specs=[pl.BlockSpec((tm, tn), lambda i, j: (i, j)),
                   pl.BlockSpec((ns, tn), _rider_map(n_i))],
        scratch_shapes=[pltpu.VMEM((tm, mix), BF16), pltpu.VMEM((ns, mix), BF16)],
        compiler_params=_params("arbitrary", "arbitrary"),
        name="outproj",
    )(*prompt, *sample, g.reshape(1, mix), w)
    return out, outs[:ns]


def _sample_bias_index(n_new, n_past):
    (win1, dil1), (tail, dil4), (win16, dil16) = DILATED
    assert dil1 == 1 and win1 == BLK and n_past == win16 and n_past % dil16 == 0
    assert n_past >= tail and n_new <= dil4 and n_new * H_A <= LANES
    n16 = n_past // dil16
    t = np.arange(n_new)[:, None]

    def per_head(index):
        same = np.eye(H_A, dtype=bool)[None, :, None, :]
        full = np.where(same, index[:, None, :, None], -1)
        return full.reshape(n_new * H_A, index.shape[1] * H_A)

    dist = BLK + t - np.arange(BLK)[None, :]
    d1 = np.where(dist <= win1, _rel_bucket(dist), -1)
    dist = tail + t - np.arange(tail)[None, :]
    d4 = np.where((dist % dil4 == 0) & (dist <= tail), _rel_bucket(dist), -1)
    m, tk = np.divmod(np.arange(n16 * n_new)[None, :], n_new)
    dist = n_past + t - dil16 * m - tk
    d16 = np.where(tk == t, _rel_bucket(np.maximum(dist, 0)), -1)
    s = np.arange(LANES // H_A)[None, :]
    dist = t - s
    new = [np.where((s < n_new) & (dist >= 0) & (dist % dil == 0),
                    _rel_bucket(np.maximum(dist, 0)), -1) for _, dil in DILATED]
    pieces = [d1, d4, d16] + new
    return np.concatenate([per_head(p) for p in pieces], axis=1).astype(np.int32)


def _sattn_body(bias_ref, q_ref, kn_ref, vn_ref, kt_ref, k16_ref, vt_ref, v16_ref, o_ref):
    def flat(ref):
        x = ref[0]
        return x.reshape(-1, HEAD_DIM).astype(BF16)

    q = q_ref[0].astype(BF16)
    kt, vt = flat(kt_ref), flat(vt_ref)
    k16, v16 = flat(k16_ref), flat(v16_ref)
    kn, vn = kn_ref[0].astype(BF16), vn_ref[0].astype(BF16)
    n_t, n_16, n_1 = kt.shape[0], k16.shape[0], BLK * H_A

    raw_t = _dot_nt(q, kt)
    raw_n = _dot_nt(q, kn)
    raws = [raw_t[:, n_t - n_1:], raw_t, _dot_nt(q, k16)] + [raw_n] * len(DILATED)
    logits = []
    lo = 0
    for raw in raws:
        hi = lo + raw.shape[1]
        logits.append(raw * SCALE + bias_ref[:, lo:hi])
        lo = hi

    m = functools.reduce(jnp.maximum, [jnp.max(x, axis=-1, keepdims=True) for x in logits])
    probs = [jnp.exp(x - m) for x in logits]
    den = functools.reduce(lambda a, b: a + b, [jnp.sum(p, axis=-1, keepdims=True) for p in probs])
    p_new = functools.reduce(lambda a, b: a + b, probs[3:])
    o = (_dot(probs[0].astype(BF16), vt[n_t - n_1:, :]) + _dot(probs[1].astype(BF16), vt)
         + _dot(probs[2].astype(BF16), v16) + _dot(p_new.astype(BF16), vn))
    o_ref[0] = o / den


def _attn_sample(qa, ka, va, cache_k, cache_v, rel_bias):
    batch, n_new = qa.shape[:2]
    n_past = cache_k.shape[1]
    tail = DILATED[1][0]
    dil16 = DILATED[2][1]
    n16 = n_past // dil16
    n_rows = n_new * H_A
    idx = _sample_bias_index(n_new, n_past)
    bias = _bias_lookup(jnp.asarray(idx), jnp.tile(rel_bias.T, (n_new, 1)),
                        tr=n_rows, tc=3 * LANES)

    def rows(x):
        x = x.reshape(batch, n_rows, HEAD_DIM)
        return jnp.pad(x, ((0, 0), (0, LANES - n_rows), (0, 0)))

    k_res = cache_k.reshape(batch, n16, dil16, H_A, HEAD_DIM)
    v_res = cache_v.reshape(batch, n16, dil16, H_A, HEAD_DIM)
    q_spec = pl.BlockSpec((1, n_rows, HEAD_DIM), lambda b: (b, 0, 0))
    new_spec = pl.BlockSpec((1, LANES, HEAD_DIM), lambda b: (b, 0, 0))
    tail_spec = pl.BlockSpec((1, tail, H_A, HEAD_DIM), lambda b: (b, n_past // tail - 1, 0, 0))
    res_spec = pl.BlockSpec((1, n16, n_new, H_A, HEAD_DIM), lambda b: (b, 0, 0, 0, 0))
    out = pl.pallas_call(
        _sattn_body,
        out_shape=jax.ShapeDtypeStruct((batch, n_rows, HEAD_DIM), F32),
        grid=(batch,),
        in_specs=[
            pl.BlockSpec(bias.shape, lambda b: (0, 0)),
            q_spec, new_spec, new_spec, tail_spec, res_spec, tail_spec, res_spec,
        ],
        out_specs=q_spec,
        compiler_params=_params("parallel"),
        name="attn_sample",
    )(bias, qa.reshape(batch, n_rows, HEAD_DIM), rows(ka), rows(va),
      cache_k, k_res, cache_v, v_res)
    return out.reshape(batch, n_new, W_A)


def _sgus_body(u_ref, v_ref, w_ref, b_ref, o_ref, *, n_new):
    t_idx = lax.broadcasted_iota(jnp.int32, (1, n_new, W_B), 1)
    mixed = 0.0
    for s in range(n_new):
        w_s = jnp.where(t_idx >= s, w_ref[s][None], 0.0).astype(BF16).astype(F32)
        mixed = mixed + w_s * v_ref[:, s:s + 1, :].astype(BF16).astype(F32)
    o_ref[...] = u_ref[...] * (mixed + b_ref[...][None])


def _sgu_sample(u, v, w_s, b_s):
    batch, n_new, _ = u.shape
    w_l = jnp.repeat(jnp.transpose(w_s[:, :n_new, :n_new], (2, 1, 0)), C_B, axis=-1)
    b_l = jnp.repeat(b_s[:, :n_new].T, C_B, axis=-1)
    return pl.pallas_call(
        functools.partial(_sgus_body, n_new=n_new),
        out_shape=jax.ShapeDtypeStruct(u.shape, F32),
        name="sgu_sample",
    )(u, v, w_l, b_l)


COL_KA, COL_VA, COL_U, COL_V, COL_QM = W_A, 2 * W_A, 3 * W_A, 3 * W_A + W_B, 3 * W_A + 2 * W_B


def _mixer_inputs(h, hs, g_mix, w_in, g_qa, g_ka, g_sgu, g_qm, *, tm):
    gain = jnp.concatenate([jnp.tile(g_qa, H_A), jnp.tile(g_ka, H_A), jnp.ones((W_A + W_B,), F32),
                            g_sgu.reshape(W_B), jnp.tile(g_qm, H_M)])
    normed = [c // PROJ_TILE for lo, hi in ((0, COL_VA), (COL_V, COL_QM + W_M))
              for c in range(lo, hi, PROJ_TILE)]
    return _proj(h, hs, g_mix, w_in, gain, normed, tm=tm)


def kernel(x_prompt, x_sample, cache_win_k, cache_win_v, cache_mem_k, cache_mem_v, mem_prompt, rel_bias, g_ffn1, w1_gate, w1_up, w1_down, g_mix, w_in, g_qa, g_ka, g_sgu, w_sgu, b_sgu, g_qm, g_mem, w_mem_kv, g_km, g_mix_out, w_out, g_ffn2, w2_gate, w2_up, w2_down):
    batch, seq, _ = x_prompt.shape
    dec_batch, n_new, _ = x_sample.shape
    depth = g_ffn1.shape[0]
    n_p = batch * seq
    n_s = dec_batch * n_new

    y_p = x_prompt.reshape(n_p, D_MODEL)
    y_s = x_sample.reshape(n_s, D_MODEL)
    outs = [[] for _ in range(7)]
    for l in range(depth):
        h_p, h_s = _ffn(y_p, y_s, g_ffn1[l], w1_gate[l], w1_up[l], w1_down[l], tm=1024, tf=256)
        z_p, z_s = _mixer_inputs(h_p, h_s, g_mix[l], w_in[l], g_qa[l], g_ka[l], g_sgu[l], g_qm[l],
                                 tm=2048)

        oa_p = _attn_prompt(z_p, rel_bias, batch=batch, seq=seq)
        ob_p = _sgu_prompt(z_p, COL_U // W_B, COL_V // W_B, w_sgu[l], b_sgu[l], tm=1024)
        mem = mem_prompt.reshape(batch * N_MEM, D_MODEL)
        gain_kv = jnp.concatenate([jnp.tile(g_km[l], H_M), jnp.ones((W_M,), F32)])
        mkv = _proj(mem, None, g_mem[l], w_mem_kv[l], gain_kv, range(W_M // PROJ_TILE), tm=1024)
        mkv3 = mkv.reshape(batch, N_MEM, 2 * W_M)
        om_p = _mem_attend(z_p.reshape(batch, seq, -1), mkv3, mkv3, tq=1024,
                           q_blk=COL_QM // W_M, k_blk=0, v_blk=1)
        keep = min(DILATED[-1][0], seq)
        z4 = z_p.reshape(batch, seq, -1)[:, seq - keep:]
        outs[0].append(z4[..., COL_KA:COL_VA].reshape(batch, keep, H_A, HEAD_DIM))
        outs[1].append(z4[..., COL_VA:COL_U].reshape(batch, keep, H_A, HEAD_DIM))
        outs[2].append(mkv[:, :W_M].reshape(batch, N_MEM, H_M, HEAD_DIM))
        outs[3].append(mkv[:, W_M:].reshape(batch, N_MEM, H_M, HEAD_DIM))

        zq, zk, zv, zu, zg, zm = (z_s[:, lo:hi] for lo, hi in zip(
            (0, COL_KA, COL_VA, COL_U, COL_V, COL_QM), (COL_KA, COL_VA, COL_U, COL_V, COL_QM, z_s.shape[1])))
        heads = (dec_batch, n_new, H_A, HEAD_DIM)
        oa_s = _attn_sample(zq.reshape(heads), zk.reshape(heads), zv.reshape(heads),
                            cache_win_k[l], cache_win_v[l], rel_bias)
        ob_s = _sgu_sample(zu.reshape(dec_batch, n_new, W_B), zg.reshape(dec_batch, n_new, W_B),
                           w_sgu[l], b_sgu[l])
        om_s = _mem_attend_sample(zm.reshape(dec_batch, n_new, H_M, HEAD_DIM),
                                  cache_mem_k[l], cache_mem_v[l])
        outs[4].append(zk.reshape(heads))
        outs[5].append(zv.reshape(heads))
        outs[6].append(zg.reshape(dec_batch, n_new, H_B, C_B))

        h_p, h_s = _outproj((oa_p, ob_p, om_p.reshape(n_p, W_M), h_p),
                            (oa_s.reshape(n_s, W_A), ob_s.reshape(n_s, W_B), om_s.reshape(n_s, W_M), h_s),
                            g_mix_out[l], w_out[l], tm=1024, tn=512)
        y_p, y_s = _ffn(h_p, h_s, g_ffn2[l], w2_gate[l], w2_up[l], w2_down[l], tm=1024, tf=256)

    return (y_p.reshape(batch, seq, D_MODEL), y_s.reshape(dec_batch, n_new, D_MODEL),
            *[jnp.stack(o) for o in outs])
```

```python
import functools

import numpy as np
import jax
import jax.numpy as jnp
from jax import lax
from jax.experimental import pallas as pl
from jax.experimental.pallas import tpu as pltpu

D_MODEL = 2048
HEAD_DIM = 128
H_A = 8
H_B = 4
C_B = 128
H_M = 4
W_A = H_A * HEAD_DIM
W_B = H_B * C_B
W_M = H_M * HEAD_DIM
D_FF = 5632
DILATED = ((128, 1), (512, 4), (2048, 16))
BLK = 128
CHUNK = 128
N_MEM = 256
N_BUCKETS = 32
REL_MAX_DIST = 2048
EPS = 1e-6
SCALE = HEAD_DIM ** -0.5
NEG = -1e30

LANES = 128
BF16_ROWS = 16
PROJ_TILE = 4 * HEAD_DIM
VMEM_LIMIT = 60 * 1024 * 1024

F32 = jnp.float32
BF16 = jnp.bfloat16


def _rel_bucket(dist):
    dist = np.asarray(dist, np.int64)
    max_exact = N_BUCKETS // 2
    large = max_exact + (np.log(np.maximum(dist, 1) / max_exact) / np.log(REL_MAX_DIST / max_exact)
                         * (N_BUCKETS - max_exact)).astype(np.int64)
    large = np.minimum(large, N_BUCKETS - 1)
    return np.where(dist < max_exact, dist, large).astype(np.int32)


def _rms(t, g):
    return t * lax.rsqrt(jnp.mean(t * t, axis=-1, keepdims=True) + EPS) * g


def _dot(a, b):
    return jnp.dot(a, b, preferred_element_type=F32)


def _dot_nt(a, b):
    return lax.dot_general(a, b, (((1,), (1,)), ((), ())), preferred_element_type=F32)


def _params(*sem):
    return pltpu.CompilerParams(dimension_semantics=sem, vmem_limit_bytes=VMEM_LIMIT)


def _tile_rows(groups, n_tiles):
    rows = [x.shape[0] // n_tiles for x in groups]
    assert all(r * n_tiles == x.shape[0] and r % BF16_ROWS == 0 for r, x in zip(rows, groups))
    return rows


def _bounds(refs):
    out, lo = [], 0
    for ref in refs:
        out.append((lo, lo + ref.shape[0]))
        lo += ref.shape[0]
    return out


def _ffn_body(*refs, n_groups):
    x_refs = refs[:n_groups]
    g_ref, wg_ref, wu_ref, wd_ref = refs[n_groups:n_groups + 4]
    o_refs = refs[n_groups + 4:2 * n_groups + 4]
    xn_ref = refs[-1]
    bounds = _bounds(x_refs)

    @pl.when(pl.program_id(1) == 0)
    def _():
        for x_ref, o_ref, (lo, hi) in zip(x_refs, o_refs, bounds):
            x = x_ref[...]
            xn_ref[lo:hi, :] = _rms(x, g_ref[...]).astype(BF16)
            o_ref[...] = x

    xn = xn_ref[...]
    gate = _dot(xn, wg_ref[...].astype(BF16))
    up = _dot(xn, wu_ref[...].astype(BF16))
    act = (gate * jax.nn.sigmoid(gate)) * up * 0.5
    res = _dot(act.astype(BF16), wd_ref[...].astype(BF16))
    for o_ref, (lo, hi) in zip(o_refs, bounds):
        o_ref[...] += res[lo:hi]


def _ffn(groups, g, wg, wu, wd, *, n_tiles, tf):
    rows = _tile_rows(groups, n_tiles)
    once = pl.Buffered(1)
    modes = [once] + [None] * (len(groups) - 1)
    return pl.pallas_call(
        functools.partial(_ffn_body, n_groups=len(groups)),
        out_shape=[jax.ShapeDtypeStruct(x.shape, F32) for x in groups],
        grid=(n_tiles, D_FF // tf),
        in_specs=(
            [pl.BlockSpec((r, D_MODEL), lambda i, j: (i, 0), pipeline_mode=mode)
             for r, mode in zip(rows, modes)]
            + [pl.BlockSpec((1, D_MODEL), lambda i, j: (0, 0), pipeline_mode=once),
               pl.BlockSpec((D_MODEL, tf), lambda i, j: (0, j)),
               pl.BlockSpec((D_MODEL, tf), lambda i, j: (0, j)),
               pl.BlockSpec((tf, D_MODEL), lambda i, j: (j, 0))]),
        out_specs=[pl.BlockSpec((r, D_MODEL), lambda i, j: (i, 0)) for r in rows],
        scratch_shapes=[pltpu.VMEM((sum(rows), D_MODEL), BF16)],
        compiler_params=_params("arbitrary", "arbitrary"),
        name="ffn",
    )(*groups, g.reshape(1, D_MODEL), wg, wu, wd)


def _proj_body(*refs, n_groups, norm_tiles, chunk):
    x_refs = refs[:n_groups]
    g_ref, w_ref, gain_ref = refs[n_groups:n_groups + 3]
    o_refs = refs[n_groups + 3:2 * n_groups + 3]
    xn_ref = refs[-1]
    j = pl.program_id(1)

    @pl.when(j == 0)
    def _():
        for x_ref, (lo, hi) in zip(x_refs, _bounds(x_refs)):
            xn_ref[lo:hi, :] = _rms(x_ref[...], g_ref[...]).astype(BF16)

    w = w_ref[...].astype(BF16)
    gain = gain_ref[...]
    normed = functools.reduce(jnp.logical_or, [j == t for t in norm_tiles])

    def finish(z):
        heads = []
        for hh in range(PROJ_TILE // HEAD_DIM):
            sl = slice(hh * HEAD_DIM, (hh + 1) * HEAD_DIM)
            zz = z[:, sl]
            scale = lax.rsqrt(jnp.mean(zz * zz, axis=-1, keepdims=True) + EPS)
            heads.append(zz * jnp.where(normed, scale, 1.0) * gain[:, sl])
        return jnp.concatenate(heads, axis=1)

    for o_ref, (lo, hi) in zip(o_refs, _bounds(x_refs)):
        for r0 in range(0, hi - lo, chunk):
            r1 = min(r0 + chunk, hi - lo)
            o_ref[r0:r1, :] = finish(_dot(xn_ref[lo + r0:lo + r1, :], w))


def _proj(groups, g, w, gain, norm_tiles, *, n_tiles, chunk):
    d_in, d_out = w.shape
    rows = _tile_rows(groups, n_tiles)
    once = pl.Buffered(1)
    modes = [once] + [None] * (len(groups) - 1)
    return pl.pallas_call(
        functools.partial(_proj_body, n_groups=len(groups), norm_tiles=tuple(norm_tiles), chunk=chunk),
        out_shape=[jax.ShapeDtypeStruct((x.shape[0], d_out), F32) for x in groups],
        grid=(n_tiles, d_out // PROJ_TILE),
        in_specs=(
            [pl.BlockSpec((r, d_in), lambda i, j: (i, 0), pipeline_mode=mode)
             for r, mode in zip(rows, modes)]
            + [pl.BlockSpec((1, d_in), lambda i, j: (0, 0), pipeline_mode=once),
               pl.BlockSpec((d_in, PROJ_TILE), lambda i, j: (0, j)),
               pl.BlockSpec((1, PROJ_TILE), lambda i, j: (0, j))]),
        out_specs=[pl.BlockSpec((r, PROJ_TILE), lambda i, j: (i, j)) for r in rows],
        scratch_shapes=[pltpu.VMEM((sum(rows), d_in), BF16)],
        compiler_params=_params("arbitrary", "arbitrary"),
        name="proj",
    )(*groups, g.reshape(1, d_in), w, gain.reshape(1, d_out))


def _bias_body(idx_ref, tab_ref, o_ref):
    idx = idx_ref[...]
    bias = jnp.full(idx.shape, NEG, F32)
    for bkt in range(N_BUCKETS):
        bias = jnp.where(idx == bkt, tab_ref[:, bkt:bkt + 1], bias)
    o_ref[...] = bias


def _bias_lookup(idx, tab, *, tr, tc):
    n_rows, n_cols = idx.shape
    spec = pl.BlockSpec((tr, tc), lambda i, j: (i, j))
    return pl.pallas_call(
        _bias_body,
        out_shape=jax.ShapeDtypeStruct(idx.shape, F32),
        grid=(n_rows // tr, n_cols // tc),
        in_specs=[spec, pl.BlockSpec((tr, N_BUCKETS), lambda i, j: (i, 0))],
        out_specs=spec,
        compiler_params=_params("parallel", "parallel"),
        name="bias_lookup",
    )(idx, tab)


def _prompt_bias_index():
    step = np.arange(BLK)[:, None] + BLK - np.arange(2 * BLK)[None, :]
    maps = []
    for win, dil in DILATED:
        n_steps = win // dil + 1
        band = (step >= 0) & (step < n_steps)
        bucket = _rel_bucket(np.clip(step, 0, n_steps - 1) * dil)
        maps.append(np.where(band, bucket, -1))
    return np.stack(maps).astype(np.int32)


def _attn_body(bias_s, q_ref, k_ref, v_ref, o_ref, acc_s, m_s, l_s, *, seq, group_size):
    def rows(start, dil):
        return pl.ds(start, BLK) if dil == 1 else pl.ds(start, BLK, stride=dil)

    def gather(ref, starts, dil):
        return jnp.stack([ref[rows(s, dil), :] for s in starts]).astype(BF16)

    def qk(q, k):
        return jnp.einsum("gid,gjd->gij", q, k, preferred_element_type=F32)

    def pv(p, v):
        return jnp.einsum("gij,gjd->gid", p.astype(BF16), v, preferred_element_type=F32)

    def group(di, dil, curs, prevs):
        q = gather(q_ref, curs, dil)
        s_c = qk(q, gather(k_ref, curs, dil)) * SCALE + bias_s[0, di, :, BLK:]
        if prevs is None:
            m = jnp.max(s_c, axis=-1, keepdims=True)
            p_c = jnp.exp(s_c - m)
            den = jnp.sum(p_c, axis=-1, keepdims=True)
            o = pv(p_c, gather(v_ref, curs, dil))
        else:
            s_p = qk(q, gather(k_ref, prevs, dil)) * SCALE + bias_s[0, di, :, :BLK]
            m = jnp.max(jnp.maximum(s_c, s_p), axis=-1, keepdims=True)
            p_c = jnp.exp(s_c - m)
            p_p = jnp.exp(s_p - m)
            den = jnp.sum(p_c + p_p, axis=-1, keepdims=True)
            o = pv(p_c, gather(v_ref, curs, dil)) + pv(p_p, gather(v_ref, prevs, dil))
        for g, s in enumerate(curs):
            rc = rows(s, dil)
            acc_s[di, rc, :] = o[g]
            m_s[di, rc, :] = jnp.broadcast_to(m[g], (BLK, HEAD_DIM))
            l_s[di, rc, :] = jnp.broadcast_to(den[g], (BLK, HEAD_DIM))

    for di, (win, dil) in enumerate(DILATED):
        span = dil * BLK
        first = [r for r in range(dil)]
        later = [r + n * span for r in range(dil) for n in range(1, seq // span)]
        for lo in range(0, len(first), group_size):
            group(di, dil, first[lo:lo + group_size], None)
        for lo in range(0, len(later), group_size):
            curs = later[lo:lo + group_size]
            group(di, dil, curs, [s - span for s in curs])

    rows_per = 256

    def merge(c, carry):
        rc = pl.ds(pl.multiple_of(c * rows_per, rows_per), rows_per)
        ms = [m_s[di, rc, :] for di in range(len(DILATED))]
        mx = jnp.maximum(jnp.maximum(ms[0], ms[1]), ms[2])
        num = 0.0
        den = 0.0
        for di in range(len(DILATED)):
            wgt = jnp.exp(ms[di] - mx)
            num = num + wgt * acc_s[di, rc, :]
            den = den + wgt * l_s[di, rc, :]
        o_ref[rc, :] = num / den
        return carry

    lax.fori_loop(0, seq // rows_per, merge, 0)


def _attn_prompt(z, rel_bias, *, batch, seq):
    n = batch * seq
    assert all(seq % (dil * BLK) == 0 for _, dil in DILATED)
    idx = _prompt_bias_index()
    n_d = idx.shape[0]
    idx_rows = jnp.asarray(np.tile(idx.reshape(n_d * BLK, 2 * BLK), (H_A, 1)))
    tab_rows = jnp.repeat(rel_bias.T, n_d * BLK, axis=0)
    bias = _bias_lookup(idx_rows, tab_rows, tr=n_d * BLK, tc=2 * BLK)
    bias = bias.reshape(H_A, n_d, BLK, 2 * BLK)
    q_spec, k_spec, v_spec = (pl.BlockSpec((seq, HEAD_DIM), lambda h, b, part=part: (b, part * H_A + h))
                              for part in range(3))
    stat = pltpu.VMEM((n_d, seq, HEAD_DIM), F32)
    return pl.pallas_call(
        functools.partial(_attn_body, seq=seq, group_size=8),
        out_shape=jax.ShapeDtypeStruct((n, W_A), F32),
        grid=(H_A, batch),
        in_specs=[
            pl.BlockSpec((1, n_d, BLK, 2 * BLK), lambda h, b: (h, 0, 0, 0)),
            q_spec, k_spec, v_spec,
        ],
        out_specs=pl.BlockSpec((seq, HEAD_DIM), lambda h, b: (b, h)),
        scratch_shapes=[stat, stat, stat],
        compiler_params=_params("parallel", "parallel"),
        name="attn_prompt",
    )(bias, z, z, z)


def _sgu_body(u_ref, v_ref, w_ref, bt_ref, o_ref, *, n_chunks):
    tri = (lax.broadcasted_iota(jnp.int32, (CHUNK, CHUNK), 0)
           >= lax.broadcasted_iota(jnp.int32, (CHUNK, CHUNK), 1))
    for g in range(H_B):
        w = jnp.where(tri, w_ref[g], 0.0).astype(BF16)
        bias = bt_ref[:, g:g + 1]
        cols = slice(g * C_B, (g + 1) * C_B)
        for c in range(n_chunks):
            rws = slice(c * CHUNK, (c + 1) * CHUNK)
            mixed = _dot(w, v_ref[rws, cols].astype(BF16)) + bias
            o_ref[rws, cols] = u_ref[rws, cols] * mixed


def _sgu_prompt(z, u_blk, v_blk, w_s, b_s, *, tm):
    n = z.shape[0]
    return pl.pallas_call(
        functools.partial(_sgu_body, n_chunks=tm // CHUNK),
        out_shape=jax.ShapeDtypeStruct((n, W_B), F32),
        grid=(n // tm,),
        in_specs=[pl.BlockSpec((tm, W_B), lambda i: (i, u_blk)),
                  pl.BlockSpec((tm, W_B), lambda i: (i, v_blk)),
                  pl.BlockSpec((H_B, CHUNK, CHUNK), lambda i: (0, 0, 0)),
                  pl.BlockSpec((CHUNK, H_B), lambda i: (0, 0))],
        out_specs=pl.BlockSpec((tm, W_B), lambda i: (i, 0)),
        compiler_params=_params("parallel"),
        name="sgu_prompt",
    )(z, z, w_s, b_s.T)


def _mem_body(q_ref, k_ref, v_ref, o_ref):
    for hh in range(H_M):
        sl = slice(hh * HEAD_DIM, (hh + 1) * HEAD_DIM)
        s = _dot_nt(q_ref[0, :, sl].astype(BF16), k_ref[0, :, sl].astype(BF16)) * SCALE
        m = jnp.max(s, axis=-1, keepdims=True)
        p = jnp.exp(s - m)
        den = jnp.sum(p, axis=-1, keepdims=True)
        o_ref[0, :, sl] = _dot(p.astype(BF16), v_ref[0, :, sl].astype(BF16)) / den


def _mem_attend(q, k, v, *, tq, q_blk, k_blk, v_blk):
    batch, t, _ = q.shape
    return pl.pallas_call(
        _mem_body,
        out_shape=jax.ShapeDtypeStruct((batch, t, W_M), F32),
        grid=(batch, t // tq),
        in_specs=[pl.BlockSpec((1, tq, W_M), lambda b, i: (b, i, q_blk)),
                  pl.BlockSpec((1, N_MEM, W_M), lambda b, i: (b, 0, k_blk)),
                  pl.BlockSpec((1, N_MEM, W_M), lambda b, i: (b, 0, v_blk))],
        out_specs=pl.BlockSpec((1, tq, W_M), lambda b, i: (b, i, 0)),
        compiler_params=_params("parallel", "parallel"),
        name="mem_attend",
    )(q, k, v)


def _mems_body(q_ref, k_ref, v_ref, o_ref):
    s = _dot_nt(q_ref[0].astype(BF16), k_ref[0].astype(BF16)) * SCALE
    row = lax.broadcasted_iota(jnp.int32, s.shape, 0)
    col = lax.broadcasted_iota(jnp.int32, s.shape, 1)
    s = jnp.where(row % H_M == col % H_M, s, NEG)
    m = jnp.max(s, axis=-1, keepdims=True)
    p = jnp.exp(s - m)
    den = jnp.sum(p, axis=-1, keepdims=True)
    o_ref[0] = _dot(p.astype(BF16), v_ref[0].astype(BF16)) / den


def _mem_attend_sample(q, cache_k, cache_v):
    batch, n_new = q.shape[:2]
    n_rows = n_new * H_M
    q_spec = pl.BlockSpec((1, n_rows, HEAD_DIM), lambda b: (b, 0, 0))
    kv_spec = pl.BlockSpec((1, N_MEM * H_M, HEAD_DIM), lambda b: (b, 0, 0))
    out = pl.pallas_call(
        _mems_body,
        out_shape=jax.ShapeDtypeStruct((batch, n_rows, HEAD_DIM), F32),
        grid=(batch,),
        in_specs=[q_spec, kv_spec, kv_spec],
        out_specs=q_spec,
        compiler_params=_params("parallel"),
        name="mem_attend_sample",
    )(q.reshape(batch, n_rows, HEAD_DIM), cache_k.reshape(batch, N_MEM * H_M, HEAD_DIM),
      cache_v.reshape(batch, N_MEM * H_M, HEAD_DIM))
    return out.reshape(batch, n_new, W_M)


def _outproj_body(*refs, n_groups):
    src_refs = [refs[4 * k:4 * k + 3] for k in range(n_groups)]
    h_refs = [refs[4 * k + 3] for k in range(n_groups)]
    g_ref, w_ref = refs[4 * n_groups:4 * n_groups + 2]
    o_refs = refs[4 * n_groups + 2:5 * n_groups + 2]
    cat_ref = refs[-1]
    bounds = _bounds(h_refs)

    @pl.when(pl.program_id(1) == 0)
    def _():
        for srcs, (r0, r1) in zip(src_refs, bounds):
            lo = 0
            for src in srcs:
                hi = lo + src.shape[1]
                cat_ref[r0:r1, lo:hi] = _rms(src[...], g_ref[:, lo:hi]).astype(BF16)
                lo = hi

    res = _dot(cat_ref[...], w_ref[...].astype(BF16))
    for h_ref, o_ref, (r0, r1) in zip(h_refs, o_refs, bounds):
        o_ref[...] = h_ref[...] + res[r0:r1]


def _outproj(groups, g, w, *, n_tiles, tn):
    mix = W_A + W_B + W_M
    rows = _tile_rows([grp[3] for grp in groups], n_tiles)
    in_specs = []
    for r in rows:
        in_specs += [pl.BlockSpec((r, wd), lambda i, j: (i, 0)) for wd in (W_A, W_B, W_M)]
        in_specs += [pl.BlockSpec((r, tn), lambda i, j: (i, j))]
    in_specs += [pl.BlockSpec((1, mix), lambda i, j: (0, 0), pipeline_mode=pl.Buffered(1)),
                 pl.BlockSpec((mix, tn), lambda i, j: (0, j))]
    return pl.pallas_call(
        functools.partial(_outproj_body, n_groups=len(groups)),
        out_shape=[jax.ShapeDtypeStruct(grp[3].shape, F32) for grp in groups],
        grid=(n_tiles, D_MODEL // tn),
        in_specs=in_specs,
        out_specs=[pl.BlockSpec((r, tn), lambda i, j: (i, j)) for r in rows],
        scratch_shapes=[pltpu.VMEM((sum(rows), mix), BF16)],
        compiler_params=_params("arbitrary", "arbitrary"),
        name="outproj",
    )(*[a for grp in groups for a in grp], g.reshape(1, mix), w)


def _sample_bias_index(n_new, n_past):
    (win1, dil1), (tail, dil4), (win16, dil16) = DILATED
    assert dil1 == 1 and win1 == BLK and n_past == win16 and n_past % dil16 == 0
    assert n_past >= tail and n_new <= dil4 and n_new * H_A <= LANES
    n16 = n_past // dil16
    t = np.arange(n_new)[:, None]

    def per_head(index):
        same = np.eye(H_A, dtype=bool)[None, :, None, :]
        full = np.where(same, index[:, None, :, None], -1)
        return full.reshape(n_new * H_A, index.shape[1] * H_A)

    dist = BLK + t - np.arange(BLK)[None, :]
    d1 = np.where(dist <= win1, _rel_bucket(dist), -1)
    dist = tail + t - np.arange(tail)[None, :]
    d4 = np.where((dist % dil4 == 0) & (dist <= tail), _rel_bucket(dist), -1)
    m, tk = np.divmod(np.arange(n16 * n_new)[None, :], n_new)
    dist = n_past + t - dil16 * m - tk
    d16 = np.where(tk == t, _rel_bucket(np.maximum(dist, 0)), -1)
    s = np.arange(LANES // H_A)[None, :]
    dist = t - s
    new = [np.where((s < n_new) & (dist >= 0) & (dist % dil == 0),
                    _rel_bucket(np.maximum(dist, 0)), -1) for _, dil in DILATED]
    pieces = [d1, d4, d16] + new
    return np.concatenate([per_head(p) for p in pieces], axis=1).astype(np.int32)


def _sattn_body(bias_ref, q_ref, kn_ref, vn_ref, kt_ref, k16_ref, vt_ref, v16_ref, o_ref):
    def flat(ref):
        x = ref[0]
        return x.reshape(-1, HEAD_DIM).astype(BF16)

    q = q_ref[0].astype(BF16)
    kt, vt = flat(kt_ref), flat(vt_ref)
    k16, v16 = flat(k16_ref), flat(v16_ref)
    kn, vn = kn_ref[0].astype(BF16), vn_ref[0].astype(BF16)
    n_t, n_1 = kt.shape[0], BLK * H_A

    raw_t = _dot_nt(q, kt)
    raw_n = _dot_nt(q, kn)
    raws = [raw_t[:, n_t - n_1:], raw_t, _dot_nt(q, k16)] + [raw_n] * len(DILATED)
    logits = []
    lo = 0
    for raw in raws:
        hi = lo + raw.shape[1]
        logits.append(raw * SCALE + bias_ref[:, lo:hi])
        lo = hi

    m = functools.reduce(jnp.maximum, [jnp.max(x, axis=-1, keepdims=True) for x in logits])
    probs = [jnp.exp(x - m) for x in logits]
    den = functools.reduce(lambda a, b: a + b, [jnp.sum(p, axis=-1, keepdims=True) for p in probs])
    p_new = functools.reduce(lambda a, b: a + b, probs[3:])
    o = (_dot(probs[0].astype(BF16), vt[n_t - n_1:, :]) + _dot(probs[1].astype(BF16), vt)
         + _dot(probs[2].astype(BF16), v16) + _dot(p_new.astype(BF16), vn))
    o_ref[0] = o / den


def _attn_sample(qa, ka, va, cache_k, cache_v, rel_bias):
    batch, n_new = qa.shape[:2]
    n_past = cache_k.shape[1]
    tail = DILATED[1][0]
    dil16 = DILATED[2][1]
    n16 = n_past // dil16
    n_rows = n_new * H_A
    idx = _sample_bias_index(n_new, n_past)
    bias = _bias_lookup(jnp.asarray(idx), jnp.tile(rel_bias.T, (n_new, 1)),
                        tr=n_rows, tc=3 * LANES)

    def rows(x):
        x = x.reshape(batch, n_rows, HEAD_DIM)
        return jnp.pad(x, ((0, 0), (0, LANES - n_rows), (0, 0)))

    k_res = cache_k.reshape(batch, n16, dil16, H_A, HEAD_DIM)
    v_res = cache_v.reshape(batch, n16, dil16, H_A, HEAD_DIM)
    q_spec = pl.BlockSpec((1, n_rows, HEAD_DIM), lambda b: (b, 0, 0))
    new_spec = pl.BlockSpec((1, LANES, HEAD_DIM), lambda b: (b, 0, 0))
    tail_spec = pl.BlockSpec((1, tail, H_A, HEAD_DIM), lambda b: (b, n_past // tail - 1, 0, 0))
    res_spec = pl.BlockSpec((1, n16, n_new, H_A, HEAD_DIM), lambda b: (b, 0, 0, 0, 0))
    out = pl.pallas_call(
        _sattn_body,
        out_shape=jax.ShapeDtypeStruct((batch, n_rows, HEAD_DIM), F32),
        grid=(batch,),
        in_specs=[
            pl.BlockSpec(bias.shape, lambda b: (0, 0)),
            q_spec, new_spec, new_spec, tail_spec, res_spec, tail_spec, res_spec,
        ],
        out_specs=q_spec,
        compiler_params=_params("parallel"),
        name="attn_sample",
    )(bias, qa.reshape(batch, n_rows, HEAD_DIM), rows(ka), rows(va),
      cache_k, k_res, cache_v, v_res)
    return out.reshape(batch, n_new, W_A)


def _sgus_body(u_ref, v_ref, w_ref, b_ref, o_ref, *, n_new):
    t_idx = lax.broadcasted_iota(jnp.int32, (1, n_new, W_B), 1)
    mixed = 0.0
    for s in range(n_new):
        w_s = jnp.where(t_idx >= s, w_ref[s][None], 0.0).astype(BF16).astype(F32)
        mixed = mixed + w_s * v_ref[:, s:s + 1, :].astype(BF16).astype(F32)
    o_ref[...] = u_ref[...] * (mixed + b_ref[...][None])


def _sgu_sample(u, v, w_s, b_s):
    batch, n_new, _ = u.shape
    w_l = jnp.repeat(jnp.transpose(w_s[:, :n_new, :n_new], (2, 1, 0)), C_B, axis=-1)
    b_l = jnp.repeat(b_s[:, :n_new].T, C_B, axis=-1)
    return pl.pallas_call(
        functools.partial(_sgus_body, n_new=n_new),
        out_shape=jax.ShapeDtypeStruct(u.shape, F32),
        name="sgu_sample",
    )(u, v, w_l, b_l)


COL_KA, COL_VA, COL_U, COL_V, COL_QM = W_A, 2 * W_A, 3 * W_A, 3 * W_A + W_B, 3 * W_A + 2 * W_B


def _mixer_inputs(groups, g_mix, w_in, g_qa, g_ka, g_sgu, g_qm, *, n_tiles):
    gain = jnp.concatenate([jnp.tile(g_qa, H_A), jnp.tile(g_ka, H_A), jnp.ones((W_A + W_B,), F32),
                            g_sgu.reshape(W_B), jnp.tile(g_qm, H_M)])
    normed = [c // PROJ_TILE for lo, hi in ((0, COL_VA), (COL_V, COL_QM + W_M))
              for c in range(lo, hi, PROJ_TILE)]
    return _proj(groups, g_mix, w_in, gain, normed, n_tiles=n_tiles, chunk=512)


def kernel(x_prompt, x_sample, cache_win_k, cache_win_v, cache_mem_k, cache_mem_v, mem_prompt, rel_bias, g_ffn1, w1_gate, w1_up, w1_down, g_mix, w_in, g_qa, g_ka, g_sgu, w_sgu, b_sgu, g_qm, g_mem, w_mem_kv, g_km, g_mix_out, w_out, g_ffn2, w2_gate, w2_up, w2_down):
    batch, seq, _ = x_prompt.shape
    dec_batch, n_new, _ = x_sample.shape
    depth = g_ffn1.shape[0]
    n_p = batch * seq
    n_s = dec_batch * n_new

    y_p = x_prompt.reshape(n_p, D_MODEL)
    y_s = x_sample.reshape(n_s, D_MODEL)
    outs = [[] for _ in range(7)]
    for l in range(depth):
        h_p, h_s = _ffn([y_p, y_s], g_ffn1[l], w1_gate[l], w1_up[l], w1_down[l], n_tiles=8, tf=512)
        z_p, z_s = _mixer_inputs([h_p, h_s], g_mix[l], w_in[l], g_qa[l], g_ka[l], g_sgu[l], g_qm[l],
                                 n_tiles=4)

        oa_p = _attn_prompt(z_p, rel_bias, batch=batch, seq=seq)
        ob_p = _sgu_prompt(z_p, COL_U // W_B, COL_V // W_B, w_sgu[l], b_sgu[l], tm=1024)
        mem = mem_prompt.reshape(batch * N_MEM, D_MODEL)
        gain_kv = jnp.concatenate([jnp.tile(g_km[l], H_M), jnp.ones((W_M,), F32)])
        mkv, = _proj([mem], g_mem[l], w_mem_kv[l], gain_kv, range(W_M // PROJ_TILE),
                     n_tiles=1, chunk=512)
        mkv3 = mkv.reshape(batch, N_MEM, 2 * W_M)
        om_p = _mem_attend(z_p.reshape(batch, seq, -1), mkv3, mkv3, tq=1024,
                           q_blk=COL_QM // W_M, k_blk=0, v_blk=1)
        keep = min(DILATED[-1][0], seq)
        z4 = z_p.reshape(batch, seq, -1)[:, seq - keep:]
        outs[0].append(z4[..., COL_KA:COL_VA].reshape(batch, keep, H_A, HEAD_DIM))
        outs[1].append(z4[..., COL_VA:COL_U].reshape(batch, keep, H_A, HEAD_DIM))
        outs[2].append(mkv[:, :W_M].reshape(batch, N_MEM, H_M, HEAD_DIM))
        outs[3].append(mkv[:, W_M:].reshape(batch, N_MEM, H_M, HEAD_DIM))

        zq, zk, zv, zu, zg, zm = (z_s[:, lo:hi] for lo, hi in zip(
            (0, COL_KA, COL_VA, COL_U, COL_V, COL_QM), (COL_KA, COL_VA, COL_U, COL_V, COL_QM, z_s.shape[1])))
        heads = (dec_batch, n_new, H_A, HEAD_DIM)
        oa_s = _attn_sample(zq.reshape(heads), zk.reshape(heads), zv.reshape(heads),
                            cache_win_k[l], cache_win_v[l], rel_bias)
        ob_s = _sgu_sample(zu.reshape(dec_batch, n_new, W_B), zg.reshape(dec_batch, n_new, W_B),
                           w_sgu[l], b_sgu[l])
        om_s = _mem_attend_sample(zm.reshape(dec_batch, n_new, H_M, HEAD_DIM),
                                  cache_mem_k[l], cache_mem_v[l])
        outs[4].append(zk.reshape(heads))
        outs[5].append(zv.reshape(heads))
        outs[6].append(zg.reshape(dec_batch, n_new, H_B, C_B))

        h_p, h_s = _outproj([(oa_p, ob_p, om_p.reshape(n_p, W_M), h_p),
                             (oa_s.reshape(n_s, W_A), ob_s.reshape(n_s, W_B), om_s.reshape(n_s, W_M), h_s)],
                            g_mix_out[l], w_out[l], n_tiles=8, tn=512)
        y_p, y_s = _ffn([h_p, h_s], g_ffn2[l], w2_gate[l], w2_up[l], w2_down[l], n_tiles=8, tf=512)

    return (y_p.reshape(batch, seq, D_MODEL), y_s.reshape(dec_batch, n_new, D_MODEL),
            *[jnp.stack(o) for o in outs])
```

```python
import functools

import numpy as np
import jax
import jax.numpy as jnp
from jax import lax
from jax.experimental import pallas as pl
from jax.experimental.pallas import tpu as pltpu

D_MODEL = 2048
HEAD_DIM = 128
H_A = 8
H_B = 4
C_B = 128
H_M = 4
W_A = H_A * HEAD_DIM
W_B = H_B * C_B
W_M = H_M * HEAD_DIM
D_FF = 5632
DILATED = ((128, 1), (512, 4), (2048, 16))
BLK = 128
CHUNK = 128
N_MEM = 256
N_BUCKETS = 32
REL_MAX_DIST = 2048
EPS = 1e-6
SCALE = HEAD_DIM ** -0.5
NEG = -1e30

LANES = 128
BF16_ROWS = 16
PROJ_TILE = 4 * HEAD_DIM
VMEM_LIMIT = 60 * 1024 * 1024

F32 = jnp.float32
BF16 = jnp.bfloat16


def _rel_bucket(dist):
    dist = np.asarray(dist, np.int64)
    max_exact = N_BUCKETS // 2
    large = max_exact + (np.log(np.maximum(dist, 1) / max_exact) / np.log(REL_MAX_DIST / max_exact)
                         * (N_BUCKETS - max_exact)).astype(np.int64)
    large = np.minimum(large, N_BUCKETS - 1)
    return np.where(dist < max_exact, dist, large).astype(np.int32)


def _rms(t, g):
    return t * lax.rsqrt(jnp.mean(t * t, axis=-1, keepdims=True) + EPS) * g


def _dot(a, b):
    return jnp.dot(a, b, preferred_element_type=F32)


def _dot_nt(a, b):
    return lax.dot_general(a, b, (((1,), (1,)), ((), ())), preferred_element_type=F32)


def _params(*sem):
    return pltpu.CompilerParams(dimension_semantics=sem, vmem_limit_bytes=VMEM_LIMIT)


def _tile_rows(groups, n_tiles):
    rows = [x.shape[0] // n_tiles for x in groups]
    assert all(r * n_tiles == x.shape[0] and r % BF16_ROWS == 0 for r, x in zip(rows, groups))
    return rows


def _bounds(refs):
    out, lo = [], 0
    for ref in refs:
        out.append((lo, lo + ref.shape[0]))
        lo += ref.shape[0]
    return out


def _ffn_body(*refs, n_groups):
    x_refs = refs[:n_groups]
    g_ref, wg_ref, wu_ref, wd_ref = refs[n_groups:n_groups + 4]
    o_refs = refs[n_groups + 4:2 * n_groups + 4]
    xn_ref = refs[-1]
    bounds = _bounds(x_refs)

    @pl.when(pl.program_id(1) == 0)
    def _():
        for x_ref, o_ref, (lo, hi) in zip(x_refs, o_refs, bounds):
            x = x_ref[...]
            xn_ref[lo:hi, :] = _rms(x, g_ref[...]).astype(BF16)
            o_ref[...] = x

    xn = xn_ref[...]
    gate = _dot(xn, wg_ref[...].astype(BF16))
    up = _dot(xn, wu_ref[...].astype(BF16))
    act = (gate * jax.nn.sigmoid(gate)) * up * 0.5
    res = _dot(act.astype(BF16), wd_ref[...].astype(BF16))
    for o_ref, (lo, hi) in zip(o_refs, bounds):
        o_ref[...] += res[lo:hi]


def _ffn(groups, g, wg, wu, wd, *, n_tiles, tf):
    rows = _tile_rows(groups, n_tiles)
    return pl.pallas_call(
        functools.partial(_ffn_body, n_groups=len(groups)),
        out_shape=[jax.ShapeDtypeStruct(x.shape, F32) for x in groups],
        grid=(n_tiles, D_FF // tf),
        in_specs=(
            [pl.BlockSpec((r, D_MODEL), lambda i, j: (i, 0)) for r in rows]
            + [pl.BlockSpec((1, D_MODEL), lambda i, j: (0, 0), pipeline_mode=pl.Buffered(1)),
               pl.BlockSpec((D_MODEL, tf), lambda i, j: (0, j)),
               pl.BlockSpec((D_MODEL, tf), lambda i, j: (0, j)),
               pl.BlockSpec((tf, D_MODEL), lambda i, j: (j, 0))]),
        out_specs=[pl.BlockSpec((r, D_MODEL), lambda i, j: (i, 0)) for r in rows],
        scratch_shapes=[pltpu.VMEM((sum(rows), D_MODEL), BF16)],
        compiler_params=_params("arbitrary", "arbitrary"),
        name="ffn",
    )(*groups, g.reshape(1, D_MODEL), wg, wu, wd)


def _proj_body(*refs, n_groups, norm_tiles, chunk):
    x_refs = refs[:n_groups]
    g_ref, w_ref, gain_ref = refs[n_groups:n_groups + 3]
    o_refs = refs[n_groups + 3:2 * n_groups + 3]
    xn_ref = refs[-1]
    j = pl.program_id(1)

    @pl.when(j == 0)
    def _():
        for x_ref, (lo, hi) in zip(x_refs, _bounds(x_refs)):
            xn_ref[lo:hi, :] = _rms(x_ref[...], g_ref[...]).astype(BF16)

    w = w_ref[...].astype(BF16)
    gain = gain_ref[...]
    normed = functools.reduce(jnp.logical_or, [j == t for t in norm_tiles])

    def finish(z):
        heads = []
        for hh in range(PROJ_TILE // HEAD_DIM):
            sl = slice(hh * HEAD_DIM, (hh + 1) * HEAD_DIM)
            zz = z[:, sl]
            scale = lax.rsqrt(jnp.mean(zz * zz, axis=-1, keepdims=True) + EPS)
            heads.append(zz * jnp.where(normed, scale, 1.0) * gain[:, sl])
        return jnp.concatenate(heads, axis=1)

    for o_ref, (lo, hi) in zip(o_refs, _bounds(x_refs)):
        for r0 in range(0, hi - lo, chunk):
            r1 = min(r0 + chunk, hi - lo)
            o_ref[r0:r1, :] = finish(_dot(xn_ref[lo + r0:lo + r1, :], w))


def _proj(groups, g, w, gain, norm_tiles, *, n_tiles, chunk):
    d_in, d_out = w.shape
    rows = _tile_rows(groups, n_tiles)
    once = pl.Buffered(1)
    modes = [once] + [None] * (len(groups) - 1)
    return pl.pallas_call(
        functools.partial(_proj_body, n_groups=len(groups), norm_tiles=tuple(norm_tiles), chunk=chunk),
        out_shape=[jax.ShapeDtypeStruct((x.shape[0], d_out), F32) for x in groups],
        grid=(n_tiles, d_out // PROJ_TILE),
        in_specs=(
            [pl.BlockSpec((r, d_in), lambda i, j: (i, 0), pipeline_mode=mode)
             for r, mode in zip(rows, modes)]
            + [pl.BlockSpec((1, d_in), lambda i, j: (0, 0), pipeline_mode=once),
               pl.BlockSpec((d_in, PROJ_TILE), lambda i, j: (0, j)),
               pl.BlockSpec((1, PROJ_TILE), lambda i, j: (0, j))]),
        out_specs=[pl.BlockSpec((r, PROJ_TILE), lambda i, j: (i, j)) for r in rows],
        scratch_shapes=[pltpu.VMEM((sum(rows), d_in), BF16)],
        compiler_params=_params("arbitrary", "arbitrary"),
        name="proj",
    )(*groups, g.reshape(1, d_in), w, gain.reshape(1, d_out))


def _bias_body(idx_ref, tab_ref, o_ref):
    idx = idx_ref[...]
    tab = tab_ref[0]
    bias = jnp.full(idx.shape, NEG, F32)
    for bkt in range(N_BUCKETS):
        bias = jnp.where(idx == bkt, tab[:, bkt:bkt + 1], bias)
    o_ref[0] = bias


def _bias_lookup(idx, tab, *, n_col_tiles):
    n_rows, n_cols = idx.shape
    n_g, tab_rows, _ = tab.shape
    tc = n_cols // n_col_tiles
    assert tc * n_col_tiles == n_cols and tc % LANES == 0 and tab_rows in (1, n_rows)
    return pl.pallas_call(
        _bias_body,
        out_shape=jax.ShapeDtypeStruct((n_g, n_rows, n_cols), F32),
        grid=(n_g, n_col_tiles),
        in_specs=[pl.BlockSpec((n_rows, tc), lambda g, j: (0, j)),
                  pl.BlockSpec((1, tab_rows, N_BUCKETS), lambda g, j: (g, 0, 0))],
        out_specs=pl.BlockSpec((1, n_rows, tc), lambda g, j: (g, 0, j)),
        compiler_params=_params("parallel", "parallel"),
        name="bias_lookup",
    )(idx, tab)


def _prompt_bias_index():
    step = np.arange(BLK)[:, None] + BLK - np.arange(2 * BLK)[None, :]
    maps = []
    for win, dil in DILATED:
        n_steps = win // dil + 1
        band = (step >= 0) & (step < n_steps)
        bucket = _rel_bucket(np.clip(step, 0, n_steps - 1) * dil)
        maps.append(np.where(band, bucket, -1))
    return np.stack(maps).astype(np.int32)


def _attn_body(bias_s, q_ref, k_ref, v_ref, o_ref, kh_ref, vh_ref, acc_s, m_s, l_s, *, seq, group_size):
    head_rows = pl.ds(pl.program_id(1), seq, stride=H_A)
    kh_ref[0, head_rows, :] = k_ref[...]
    vh_ref[0, head_rows, :] = v_ref[...]

    def rows(start, dil):
        return pl.ds(start, BLK) if dil == 1 else pl.ds(start, BLK, stride=dil)

    def gather(ref, starts, dil):
        return jnp.stack([ref[rows(s, dil), :] for s in starts]).astype(BF16)

    def qk(q, k):
        return jnp.einsum("gid,gjd->gij", q, k, preferred_element_type=F32)

    def pv(p, v):
        return jnp.einsum("gij,gjd->gid", p.astype(BF16), v, preferred_element_type=F32)

    def group(di, dil, curs, prevs):
        q = gather(q_ref, curs, dil)
        s_c = qk(q, gather(k_ref, curs, dil)) * SCALE + bias_s[0, di, :, BLK:]
        if prevs is None:
            m = jnp.max(s_c, axis=-1, keepdims=True)
            p_c = jnp.exp(s_c - m)
            den = jnp.sum(p_c, axis=-1, keepdims=True)
            o = pv(p_c, gather(v_ref, curs, dil))
        else:
            s_p = qk(q, gather(k_ref, prevs, dil)) * SCALE + bias_s[0, di, :, :BLK]
            m = jnp.max(jnp.maximum(s_c, s_p), axis=-1, keepdims=True)
            p_c = jnp.exp(s_c - m)
            p_p = jnp.exp(s_p - m)
            den = jnp.sum(p_c + p_p, axis=-1, keepdims=True)
            o = pv(p_c, gather(v_ref, curs, dil)) + pv(p_p, gather(v_ref, prevs, dil))
        for g, s in enumerate(curs):
            rc = rows(s, dil)
            acc_s[di, rc, :] = o[g]
            m_s[di, rc, :] = jnp.broadcast_to(m[g], (BLK, HEAD_DIM))
            l_s[di, rc, :] = jnp.broadcast_to(den[g], (BLK, HEAD_DIM))

    for di, (win, dil) in enumerate(DILATED):
        span = dil * BLK
        first = [r for r in range(dil)]
        later = [r + n * span for r in range(dil) for n in range(1, seq // span)]
        for lo in range(0, len(first), group_size):
            group(di, dil, first[lo:lo + group_size], None)
        for lo in range(0, len(later), group_size):
            curs = later[lo:lo + group_size]
            group(di, dil, curs, [s - span for s in curs])

    rows_per = 256

    def merge(c, carry):
        rc = pl.ds(pl.multiple_of(c * rows_per, rows_per), rows_per)
        ms = [m_s[di, rc, :] for di in range(len(DILATED))]
        mx = jnp.maximum(jnp.maximum(ms[0], ms[1]), ms[2])
        num = 0.0
        den = 0.0
        for di in range(len(DILATED)):
            wgt = jnp.exp(ms[di] - mx)
            num = num + wgt * acc_s[di, rc, :]
            den = den + wgt * l_s[di, rc, :]
        o_ref[rc, :] = num / den
        return carry

    lax.fori_loop(0, seq // rows_per, merge, 0)


def _attn_prompt(z, rel_bias, *, batch, seq):
    n = batch * seq
    assert all(seq % (dil * BLK) == 0 for _, dil in DILATED)
    idx = _prompt_bias_index()
    n_d = idx.shape[0]
    bias = _bias_lookup(jnp.asarray(idx.reshape(n_d * BLK, 2 * BLK)), rel_bias.T[:, None, :],
                        n_col_tiles=1)
    bias = bias.reshape(H_A, n_d, BLK, 2 * BLK)
    q_spec, k_spec, v_spec = (pl.BlockSpec((seq, HEAD_DIM), lambda b, h, part=part: (b, part * H_A + h))
                              for part in range(3))
    by_head = pl.BlockSpec((1, seq * H_A, HEAD_DIM), lambda b, h: (b, 0, 0))
    by_head_shape = jax.ShapeDtypeStruct((batch, seq * H_A, HEAD_DIM), F32)
    stat = pltpu.VMEM((n_d, seq, HEAD_DIM), F32)
    oa, kh, vh = pl.pallas_call(
        functools.partial(_attn_body, seq=seq, group_size=8),
        out_shape=[jax.ShapeDtypeStruct((n, W_A), F32), by_head_shape, by_head_shape],
        grid=(batch, H_A),
        in_specs=[
            pl.BlockSpec((1, n_d, BLK, 2 * BLK), lambda b, h: (h, 0, 0, 0)),
            q_spec, k_spec, v_spec,
        ],
        out_specs=[pl.BlockSpec((seq, HEAD_DIM), lambda b, h: (b, h)), by_head, by_head],
        scratch_shapes=[stat, stat, stat],
        compiler_params=_params("arbitrary", "arbitrary"),
        name="attn_prompt",
    )(bias, z, z, z)
    heads = (batch, seq, H_A, HEAD_DIM)
    return oa, kh.reshape(heads), vh.reshape(heads)


def _sgu_body(u_ref, v_ref, w_ref, bt_ref, o_ref, *, n_chunks):
    tri = (lax.broadcasted_iota(jnp.int32, (CHUNK, CHUNK), 0)
           >= lax.broadcasted_iota(jnp.int32, (CHUNK, CHUNK), 1))
    for g in range(H_B):
        w = jnp.where(tri, w_ref[g], 0.0).astype(BF16)
        bias = bt_ref[:, g:g + 1]
        cols = slice(g * C_B, (g + 1) * C_B)
        for c in range(n_chunks):
            rws = slice(c * CHUNK, (c + 1) * CHUNK)
            mixed = _dot(w, v_ref[rws, cols].astype(BF16)) + bias
            o_ref[rws, cols] = u_ref[rws, cols] * mixed


def _sgu_prompt(z, u_blk, v_blk, w_s, b_s, *, tm):
    n = z.shape[0]
    return pl.pallas_call(
        functools.partial(_sgu_body, n_chunks=tm // CHUNK),
        out_shape=jax.ShapeDtypeStruct((n, W_B), F32),
        grid=(n // tm,),
        in_specs=[pl.BlockSpec((tm, W_B), lambda i: (i, u_blk)),
                  pl.BlockSpec((tm, W_B), lambda i: (i, v_blk)),
                  pl.BlockSpec((H_B, CHUNK, CHUNK), lambda i: (0, 0, 0)),
                  pl.BlockSpec((CHUNK, H_B), lambda i: (0, 0))],
        out_specs=pl.BlockSpec((tm, W_B), lambda i: (i, 0)),
        compiler_params=_params("parallel"),
        name="sgu_prompt",
    )(z, z, w_s, b_s.T)


def _mem_body(q_ref, k_ref, v_ref, o_ref):
    for hh in range(H_M):
        sl = slice(hh * HEAD_DIM, (hh + 1) * HEAD_DIM)
        s = _dot_nt(q_ref[0, :, sl].astype(BF16), k_ref[0, :, sl].astype(BF16)) * SCALE
        m = jnp.max(s, axis=-1, keepdims=True)
        p = jnp.exp(s - m)
        den = jnp.sum(p, axis=-1, keepdims=True)
        o_ref[0, :, sl] = _dot(p.astype(BF16), v_ref[0, :, sl].astype(BF16)) / den


def _mem_attend(q, k, v, *, tq, q_blk, k_blk, v_blk):
    batch, t, _ = q.shape
    return pl.pallas_call(
        _mem_body,
        out_shape=jax.ShapeDtypeStruct((batch, t, W_M), F32),
        grid=(batch, t // tq),
        in_specs=[pl.BlockSpec((1, tq, W_M), lambda b, i: (b, i, q_blk)),
                  pl.BlockSpec((1, N_MEM, W_M), lambda b, i: (b, 0, k_blk)),
                  pl.BlockSpec((1, N_MEM, W_M), lambda b, i: (b, 0, v_blk))],
        out_specs=pl.BlockSpec((1, tq, W_M), lambda b, i: (b, i, 0)),
        compiler_params=_params("parallel", "parallel"),
        name="mem_attend",
    )(q, k, v)


def _mems_body(q_ref, k_ref, v_ref, o_ref):
    for b in range(q_ref.shape[0]):
        s = _dot_nt(q_ref[b].astype(BF16), k_ref[b].astype(BF16)) * SCALE
        row = lax.broadcasted_iota(jnp.int32, s.shape, 0)
        col = lax.broadcasted_iota(jnp.int32, s.shape, 1)
        s = jnp.where(row % H_M == col % H_M, s, NEG)
        m = jnp.max(s, axis=-1, keepdims=True)
        p = jnp.exp(s - m)
        den = jnp.sum(p, axis=-1, keepdims=True)
        o_ref[b] = _dot(p.astype(BF16), v_ref[b].astype(BF16)) / den


def _mem_attend_sample(q, cache_k, cache_v, *, tb):
    batch, n_new = q.shape[:2]
    n_rows = n_new * H_M
    q_spec = pl.BlockSpec((tb, n_rows, HEAD_DIM), lambda b: (b, 0, 0))
    kv_spec = pl.BlockSpec((tb, N_MEM * H_M, HEAD_DIM), lambda b: (b, 0, 0))
    out = pl.pallas_call(
        _mems_body,
        out_shape=jax.ShapeDtypeStruct((batch, n_rows, HEAD_DIM), F32),
        grid=(batch // tb,),
        in_specs=[q_spec, kv_spec, kv_spec],
        out_specs=q_spec,
        compiler_params=_params("parallel"),
        name="mem_attend_sample",
    )(q.reshape(batch, n_rows, HEAD_DIM), cache_k.reshape(batch, N_MEM * H_M, HEAD_DIM),
      cache_v.reshape(batch, N_MEM * H_M, HEAD_DIM))
    return out.reshape(batch, n_new, W_M)


def _outproj_body(*refs, n_groups):
    src_refs = [refs[4 * k:4 * k + 3] for k in range(n_groups)]
    h_refs = [refs[4 * k + 3] for k in range(n_groups)]
    g_ref, w_ref = refs[4 * n_groups:4 * n_groups + 2]
    o_refs = refs[4 * n_groups + 2:5 * n_groups + 2]
    cat_ref = refs[-1]
    bounds = _bounds(h_refs)

    @pl.when(pl.program_id(1) == 0)
    def _():
        for srcs, (r0, r1) in zip(src_refs, bounds):
            lo = 0
            for src in srcs:
                hi = lo + src.shape[1]
                cat_ref[r0:r1, lo:hi] = _rms(src[...], g_ref[:, lo:hi]).astype(BF16)
                lo = hi

    res = _dot(cat_ref[...], w_ref[...].astype(BF16))
    for h_ref, o_ref, (r0, r1) in zip(h_refs, o_refs, bounds):
        o_ref[...] = h_ref[...] + res[r0:r1]


def _outproj(groups, g, w, *, n_tiles, tn):
    mix = W_A + W_B + W_M
    rows = _tile_rows([grp[3] for grp in groups], n_tiles)
    in_specs = []
    for r in rows:
        in_specs += [pl.BlockSpec((r, wd), lambda i, j: (i, 0)) for wd in (W_A, W_B, W_M)]
        in_specs += [pl.BlockSpec((r, tn), lambda i, j: (i, j))]
    in_specs += [pl.BlockSpec((1, mix), lambda i, j: (0, 0), pipeline_mode=pl.Buffered(1)),
                 pl.BlockSpec((mix, tn), lambda i, j: (0, j))]
    return pl.pallas_call(
        functools.partial(_outproj_body, n_groups=len(groups)),
        out_shape=[jax.ShapeDtypeStruct(grp[3].shape, F32) for grp in groups],
        grid=(n_tiles, D_MODEL // tn),
        in_specs=in_specs,
        out_specs=[pl.BlockSpec((r, tn), lambda i, j: (i, j)) for r in rows],
        scratch_shapes=[pltpu.VMEM((sum(rows), mix), BF16)],
        compiler_params=_params("arbitrary", "arbitrary"),
        name="outproj",
    )(*[a for grp in groups for a in grp], g.reshape(1, mix), w)


def _sample_bias_index(n_new, n_past):
    (win1, dil1), (tail, dil4), (win16, dil16) = DILATED
    assert dil1 == 1 and win1 == BLK and n_past == win16 and n_past % dil16 == 0
    assert n_past >= tail and n_new <= dil4 and n_new * H_A <= LANES
    n16 = n_past // dil16
    t = np.arange(n_new)[:, None]

    def per_head(index):
        same = np.eye(H_A, dtype=bool)[None, :, None, :]
        full = np.where(same, index[:, None, :, None], -1)
        return full.reshape(n_new * H_A, index.shape[1] * H_A)

    dist = BLK + t - np.arange(BLK)[None, :]
    d1 = np.where(dist <= win1, _rel_bucket(dist), -1)
    dist = tail + t - np.arange(tail)[None, :]
    d4 = np.where((dist % dil4 == 0) & (dist <= tail), _rel_bucket(dist), -1)
    m, tk = np.divmod(np.arange(n16 * n_new)[None, :], n_new)
    dist = n_past + t - dil16 * m - tk
    d16 = np.where(tk == t, _rel_bucket(np.maximum(dist, 0)), -1)
    s = np.arange(LANES // H_A)[None, :]
    dist = t - s
    new = [np.where((s < n_new) & (dist >= 0) & (dist % dil == 0),
                    _rel_bucket(np.maximum(dist, 0)), -1) for _, dil in DILATED]
    pieces = [d1, d4, d16] + new
    return np.concatenate([per_head(p) for p in pieces], axis=1).astype(np.int32)


def _sattn_body(bias_ref, q_ref, kn_ref, vn_ref, kt_ref, k16_ref, vt_ref, v16_ref, o_ref):
    def flat(ref):
        x = ref[0]
        return x.reshape(-1, HEAD_DIM).astype(BF16)

    q = q_ref[0].astype(BF16)
    kt, vt = flat(kt_ref), flat(vt_ref)
    k16, v16 = flat(k16_ref), flat(v16_ref)
    kn, vn = kn_ref[0].astype(BF16), vn_ref[0].astype(BF16)
    n_t, n_1 = kt.shape[0], BLK * H_A

    raw_t = _dot_nt(q, kt)
    raw_n = _dot_nt(q, kn)
    raws = [raw_t[:, n_t - n_1:], raw_t, _dot_nt(q, k16)] + [raw_n] * len(DILATED)
    logits = []
    lo = 0
    for raw in raws:
        hi = lo + raw.shape[1]
        logits.append(raw * SCALE + bias_ref[:, lo:hi])
        lo = hi

    m = functools.reduce(jnp.maximum, [jnp.max(x, axis=-1, keepdims=True) for x in logits])
    probs = [jnp.exp(x - m) for x in logits]
    den = functools.reduce(lambda a, b: a + b, [jnp.sum(p, axis=-1, keepdims=True) for p in probs])
    p_new = functools.reduce(lambda a, b: a + b, probs[3:])
    o = (_dot(probs[0].astype(BF16), vt[n_t - n_1:, :]) + _dot(probs[1].astype(BF16), vt)
         + _dot(probs[2].astype(BF16), v16) + _dot(p_new.astype(BF16), vn))
    o_ref[0] = o / den


def _attn_sample(qa, ka, va, cache_k, cache_v, rel_bias):
    batch, n_new = qa.shape[:2]
    n_past = cache_k.shape[1]
    tail = DILATED[1][0]
    dil16 = DILATED[2][1]
    n16 = n_past // dil16
    n_rows = n_new * H_A
    idx = _sample_bias_index(n_new, n_past)
    bias = _bias_lookup(jnp.asarray(idx), jnp.tile(rel_bias.T, (n_new, 1))[None], n_col_tiles=5)[0]

    def rows(x):
        x = x.reshape(batch, n_rows, HEAD_DIM)
        return jnp.pad(x, ((0, 0), (0, LANES - n_rows), (0, 0)))

    k_res = cache_k.reshape(batch, n16, dil16, H_A, HEAD_DIM)
    v_res = cache_v.reshape(batch, n16, dil16, H_A, HEAD_DIM)
    q_spec = pl.BlockSpec((1, n_rows, HEAD_DIM), lambda b: (b, 0, 0))
    new_spec = pl.BlockSpec((1, LANES, HEAD_DIM), lambda b: (b, 0, 0))
    tail_spec = pl.BlockSpec((1, tail, H_A, HEAD_DIM), lambda b: (b, n_past // tail - 1, 0, 0))
    res_spec = pl.BlockSpec((1, n16, n_new, H_A, HEAD_DIM), lambda b: (b, 0, 0, 0, 0))
    out = pl.pallas_call(
        _sattn_body,
        out_shape=jax.ShapeDtypeStruct((batch, n_rows, HEAD_DIM), F32),
        grid=(batch,),
        in_specs=[
            pl.BlockSpec(bias.shape, lambda b: (0, 0)),
            q_spec, new_spec, new_spec, tail_spec, res_spec, tail_spec, res_spec,
        ],
        out_specs=q_spec,
        compiler_params=_params("parallel"),
        name="attn_sample",
    )(bias, qa.reshape(batch, n_rows, HEAD_DIM), rows(ka), rows(va),
      cache_k, k_res, cache_v, v_res)
    return out.reshape(batch, n_new, W_A)


def _sgus_body(u_ref, v_ref, w_ref, b_ref, o_ref, *, n_new):
    t_idx = lax.broadcasted_iota(jnp.int32, (1, n_new, W_B), 1)
    mixed = 0.0
    for s in range(n_new):
        w_s = jnp.where(t_idx >= s, w_ref[s][None], 0.0).astype(BF16).astype(F32)
        mixed = mixed + w_s * v_ref[:, s:s + 1, :].astype(BF16).astype(F32)
    o_ref[...] = u_ref[...] * (mixed + b_ref[...][None])


def _sgu_sample(u, v, w_s, b_s):
    batch, n_new, _ = u.shape
    w_l = jnp.repeat(jnp.transpose(w_s[:, :n_new, :n_new], (2, 1, 0)), C_B, axis=-1)
    b_l = jnp.repeat(b_s[:, :n_new].T, C_B, axis=-1)
    return pl.pallas_call(
        functools.partial(_sgus_body, n_new=n_new),
        out_shape=jax.ShapeDtypeStruct(u.shape, F32),
        name="sgu_sample",
    )(u, v, w_l, b_l)


COL_KA, COL_VA, COL_U, COL_V, COL_QM = W_A, 2 * W_A, 3 * W_A, 3 * W_A + W_B, 3 * W_A + 2 * W_B


def _mixer_inputs(groups, g_mix, w_in, g_qa, g_ka, g_sgu, g_qm, *, n_tiles):
    gain = jnp.concatenate([jnp.tile(g_qa, H_A), jnp.tile(g_ka, H_A), jnp.ones((W_A + W_B,), F32),
                            g_sgu.reshape(W_B), jnp.tile(g_qm, H_M)])
    normed = [c // PROJ_TILE for lo, hi in ((0, COL_VA), (COL_V, COL_QM + W_M))
              for c in range(lo, hi, PROJ_TILE)]
    return _proj(groups, g_mix, w_in, gain, normed, n_tiles=n_tiles, chunk=512)


def kernel(x_prompt, x_sample, cache_win_k, cache_win_v, cache_mem_k, cache_mem_v, mem_prompt, rel_bias, g_ffn1, w1_gate, w1_up, w1_down, g_mix, w_in, g_qa, g_ka, g_sgu, w_sgu, b_sgu, g_qm, g_mem, w_mem_kv, g_km, g_mix_out, w_out, g_ffn2, w2_gate, w2_up, w2_down):
    batch, seq, _ = x_prompt.shape
    dec_batch, n_new, _ = x_sample.shape
    depth = g_ffn1.shape[0]
    n_p = batch * seq
    n_s = dec_batch * n_new

    y_p = x_prompt.reshape(n_p, D_MODEL)
    y_s = x_sample.reshape(n_s, D_MODEL)
    outs = [[] for _ in range(7)]
    for l in range(depth):
        h_p, h_s = _ffn([y_p, y_s], g_ffn1[l], w1_gate[l], w1_up[l], w1_down[l], n_tiles=8, tf=256)
        z_p, z_s = _mixer_inputs([h_p, h_s], g_mix[l], w_in[l], g_qa[l], g_ka[l], g_sgu[l], g_qm[l],
                                 n_tiles=4)

        oa_p, k_p, v_p = _attn_prompt(z_p, rel_bias, batch=batch, seq=seq)
        ob_p = _sgu_prompt(z_p, COL_U // W_B, COL_V // W_B, w_sgu[l], b_sgu[l], tm=1024)
        mem = mem_prompt.reshape(batch * N_MEM, D_MODEL)
        gain_kv = jnp.concatenate([jnp.tile(g_km[l], H_M), jnp.ones((W_M,), F32)])
        mkv, = _proj([mem], g_mem[l], w_mem_kv[l], gain_kv, range(W_M // PROJ_TILE),
                     n_tiles=2, chunk=512)
        mkv3 = mkv.reshape(batch, N_MEM, 2 * W_M)
        om_p = _mem_attend(z_p.reshape(batch, seq, -1), mkv3, mkv3, tq=1024,
                           q_blk=COL_QM // W_M, k_blk=0, v_blk=1)
        keep = min(DILATED[-1][0], seq)
        outs[0].append(k_p[:, seq - keep:])
        outs[1].append(v_p[:, seq - keep:])
        outs[2].append(mkv[:, :W_M].reshape(batch, N_MEM, H_M, HEAD_DIM))
        outs[3].append(mkv[:, W_M:].reshape(batch, N_MEM, H_M, HEAD_DIM))

        zq, zk, zv, zu, zg, zm = (z_s[:, lo:hi] for lo, hi in zip(
            (0, COL_KA, COL_VA, COL_U, COL_V, COL_QM), (COL_KA, COL_VA, COL_U, COL_V, COL_QM, z_s.shape[1])))
        heads = (dec_batch, n_new, H_A, HEAD_DIM)
        oa_s = _attn_sample(zq.reshape(heads), zk.reshape(heads), zv.reshape(heads),
                            cache_win_k[l], cache_win_v[l], rel_bias)
        ob_s = _sgu_sample(zu.reshape(dec_batch, n_new, W_B), zg.reshape(dec_batch, n_new, W_B),
                           w_sgu[l], b_sgu[l])
        om_s = _mem_attend_sample(zm.reshape(dec_batch, n_new, H_M, HEAD_DIM),
                                  cache_mem_k[l], cache_mem_v[l], tb=4)
        outs[4].append(zk.reshape(heads))
        outs[5].append(zv.reshape(heads))
        outs[6].append(zg.reshape(dec_batch, n_new, H_B, C_B))

        h_p, h_s = _outproj([(oa_p, ob_p, om_p.reshape(n_p, W_M), h_p),
                             (oa_s.reshape(n_s, W_A), ob_s.reshape(n_s, W_B), om_s.reshape(n_s, W_M), h_s)],
                            g_mix_out[l], w_out[l], n_tiles=4, tn=256)
        y_p, y_s = _ffn([h_p, h_s], g_ffn2[l], w2_gate[l], w2_up[l], w2_down[l], n_tiles=8, tf=256)

    return (y_p.reshape(batch, seq, D_MODEL), y_s.reshape(dec_batch, n_new, D_MODEL),
            *[jnp.stack(o) for o in outs])
```

```python
import functools

import numpy as np
import jax
import jax.numpy as jnp
from jax import lax
from jax.experimental import pallas as pl
from jax.experimental.pallas import tpu as pltpu

D_MODEL = 2048
HEAD_DIM = 128
H_A = 8
H_B = 4
C_B = 128
H_M = 4
W_A = H_A * HEAD_DIM
W_B = H_B * C_B
W_M = H_M * HEAD_DIM
D_FF = 5632
DILATED = ((128, 1), (512, 4), (2048, 16))
BLK = 128
CHUNK = 128
N_MEM = 256
N_BUCKETS = 32
REL_MAX_DIST = 2048
EPS = 1e-6
SCALE = HEAD_DIM ** -0.5
NEG = -1e30

LANES = 128
BF16_ROWS = 16
PROJ_TILE = 4 * HEAD_DIM
VMEM_LIMIT = 60 * 1024 * 1024

F32 = jnp.float32
BF16 = jnp.bfloat16


def _rel_bucket(dist):
    dist = np.asarray(dist, np.int64)
    max_exact = N_BUCKETS // 2
    large = max_exact + (np.log(np.maximum(dist, 1) / max_exact) / np.log(REL_MAX_DIST / max_exact)
                         * (N_BUCKETS - max_exact)).astype(np.int64)
    large = np.minimum(large, N_BUCKETS - 1)
    return np.where(dist < max_exact, dist, large).astype(np.int32)


def _rms(t, g):
    return t * lax.rsqrt(jnp.mean(t * t, axis=-1, keepdims=True) + EPS) * g


def _dot(a, b):
    return jnp.dot(a, b, preferred_element_type=F32)


def _dot_nt(a, b):
    return lax.dot_general(a, b, (((1,), (1,)), ((), ())), preferred_element_type=F32)


def _params(*sem):
    return pltpu.CompilerParams(dimension_semantics=sem, vmem_limit_bytes=VMEM_LIMIT)


def _tile_rows(groups, n_tiles):
    rows = [x.shape[0] // n_tiles for x in groups]
    assert all(r * n_tiles == x.shape[0] and r % BF16_ROWS == 0 for r, x in zip(rows, groups))
    return rows


def _bounds(refs):
    out, lo = [], 0
    for ref in refs:
        out.append((lo, lo + ref.shape[0]))
        lo += ref.shape[0]
    return out


def _ffn_body(*refs, n_groups, with_next_norm):
    x_refs, refs = refs[:n_groups], refs[n_groups:]
    gn_ref = refs[0] if with_next_norm else None
    g_ref, wg_ref, wu_ref, wd_ref = refs[with_next_norm:with_next_norm + 4]
    refs = refs[with_next_norm + 4:]
    o_refs, on_refs, xn_ref = refs[:n_groups], refs[n_groups:-1], refs[-1]
    bounds = _bounds(x_refs)
    j = pl.program_id(1)

    @pl.when(j == 0)
    def _():
        for x_ref, o_ref, (lo, hi) in zip(x_refs, o_refs, bounds):
            x = x_ref[...]
            xn_ref[lo:hi, :] = _rms(x, g_ref[...]).astype(BF16)
            o_ref[...] = x

    xn = xn_ref[...]
    gate = _dot(xn, wg_ref[...].astype(BF16))
    up = _dot(xn, wu_ref[...].astype(BF16))
    act = (gate * jax.nn.sigmoid(gate)) * up * 0.5
    res = _dot(act.astype(BF16), wd_ref[...].astype(BF16))
    for o_ref, (lo, hi) in zip(o_refs, bounds):
        o_ref[...] += res[lo:hi]

    if with_next_norm:
        @pl.when(j == pl.num_programs(1) - 1)
        def _():
            for o_ref, on_ref in zip(o_refs, on_refs):
                on_ref[...] = _rms(o_ref[...], gn_ref[...]).astype(BF16)


def _ffn(groups, g, wg, wu, wd, *, g_next=None, n_tiles, tf):
    rows = _tile_rows(groups, n_tiles)
    with_next_norm = g_next is not None
    gain_spec = pl.BlockSpec((1, D_MODEL), lambda i, j: (0, 0), pipeline_mode=pl.Buffered(1))
    row_specs = [pl.BlockSpec((r, D_MODEL), lambda i, j: (i, 0)) for r in rows]
    out_shape = [jax.ShapeDtypeStruct(x.shape, F32) for x in groups]
    gains = [g.reshape(1, D_MODEL)]
    if with_next_norm:
        out_shape += [jax.ShapeDtypeStruct(x.shape, BF16) for x in groups]
        gains.insert(0, g_next.reshape(1, D_MODEL))
    return pl.pallas_call(
        functools.partial(_ffn_body, n_groups=len(groups), with_next_norm=with_next_norm),
        out_shape=out_shape,
        grid=(n_tiles, D_FF // tf),
        in_specs=(
            row_specs + [gain_spec] * len(gains)
            + [pl.BlockSpec((D_MODEL, tf), lambda i, j: (0, j)),
               pl.BlockSpec((D_MODEL, tf), lambda i, j: (0, j)),
               pl.BlockSpec((tf, D_MODEL), lambda i, j: (j, 0))]),
        out_specs=row_specs * (2 if with_next_norm else 1),
        scratch_shapes=[pltpu.VMEM((sum(rows), D_MODEL), BF16)],
        compiler_params=_params("arbitrary", "arbitrary"),
        name="ffn",
    )(*groups, *gains, wg, wu, wd)


def _proj_body(*refs, n_groups, pre_normed, norm_tiles, chunk):
    x_refs, refs = refs[:n_groups], refs[n_groups:]
    g_ref = None if pre_normed else refs[0]
    w_ref, gain_ref = refs[1 - pre_normed:3 - pre_normed]
    o_refs, xn_ref = refs[3 - pre_normed:-1], refs[-1]
    bounds = _bounds(x_refs)
    j = pl.program_id(1)

    @pl.when(j == 0)
    def _():
        for x_ref, (lo, hi) in zip(x_refs, bounds):
            x = x_ref[...]
            xn_ref[lo:hi, :] = x if pre_normed else _rms(x, g_ref[...]).astype(BF16)

    w = w_ref[...].astype(BF16)
    gain = gain_ref[...]
    normed = functools.reduce(jnp.logical_or, [j == t for t in norm_tiles])

    def finish(z):
        heads = []
        for hh in range(PROJ_TILE // HEAD_DIM):
            sl = slice(hh * HEAD_DIM, (hh + 1) * HEAD_DIM)
            zz = z[:, sl]
            scale = lax.rsqrt(jnp.mean(zz * zz, axis=-1, keepdims=True) + EPS)
            heads.append(zz * jnp.where(normed, scale, 1.0) * gain[:, sl])
        return jnp.concatenate(heads, axis=1)

    total = bounds[-1][1]
    edges = list(range(0, total, chunk)) + [total]
    if len(edges) > 2 and edges[-1] - edges[-2] < chunk // 2:
        del edges[-2]
    for c0, c1 in zip(edges[:-1], edges[1:]):
        z = finish(_dot(xn_ref[c0:c1, :], w))
        for o_ref, (lo, hi) in zip(o_refs, bounds):
            a, b = max(lo, c0), min(hi, c1)
            if a < b:
                o_ref[a - lo:b - lo, :] = z[a - c0:b - c0, :]


def _proj(groups, g, w, gain, norm_tiles, *, n_tiles, chunk):
    d_in, d_out = w.shape
    rows = _tile_rows(groups, n_tiles)
    pre_normed = g is None
    assert all(x.dtype == (BF16 if pre_normed else F32) for x in groups)
    once = pl.Buffered(1)
    norm_args = [] if pre_normed else [g.reshape(1, d_in)]
    return pl.pallas_call(
        functools.partial(_proj_body, n_groups=len(groups), pre_normed=pre_normed,
                          norm_tiles=tuple(norm_tiles), chunk=chunk),
        out_shape=[jax.ShapeDtypeStruct((x.shape[0], d_out), F32) for x in groups],
        grid=(n_tiles, d_out // PROJ_TILE),
        in_specs=(
            [pl.BlockSpec((r, d_in), lambda i, j: (i, 0)) for r in rows]
            + [pl.BlockSpec((1, d_in), lambda i, j: (0, 0), pipeline_mode=once)] * len(norm_args)
            + [pl.BlockSpec((d_in, PROJ_TILE), lambda i, j: (0, j)),
               pl.BlockSpec((1, PROJ_TILE), lambda i, j: (0, j))]),
        out_specs=[pl.BlockSpec((r, PROJ_TILE), lambda i, j: (i, j)) for r in rows],
        scratch_shapes=[pltpu.VMEM((sum(rows), d_in), BF16)],
        compiler_params=_params("arbitrary", "arbitrary"),
        name="proj",
    )(*groups, *norm_args, w, gain.reshape(1, d_out))


def _bias_body(idx_ref, tab_ref, o_ref):
    idx = idx_ref[...]
    tab = tab_ref[0]
    bias = jnp.full(idx.shape, NEG, F32)
    for bkt in range(N_BUCKETS):
        bias = jnp.where(idx == bkt, tab[:, bkt:bkt + 1], bias)
    o_ref[0] = bias


def _bias_lookup(idx, tab, *, n_col_tiles):
    n_rows, n_cols = idx.shape
    n_g, tab_rows, _ = tab.shape
    tc = n_cols // n_col_tiles
    assert tc * n_col_tiles == n_cols and tc % LANES == 0 and tab_rows in (1, n_rows)
    return pl.pallas_call(
        _bias_body,
        out_shape=jax.ShapeDtypeStruct((n_g, n_rows, n_cols), F32),
        grid=(n_g, n_col_tiles),
        in_specs=[pl.BlockSpec((n_rows, tc), lambda g, j: (0, j)),
                  pl.BlockSpec((1, tab_rows, N_BUCKETS), lambda g, j: (g, 0, 0))],
        out_specs=pl.BlockSpec((1, n_rows, tc), lambda g, j: (g, 0, j)),
        compiler_params=_params("parallel", "parallel"),
        name="bias_lookup",
    )(idx, tab)


def _prompt_bias_index():
    step = np.arange(BLK)[:, None] + BLK - np.arange(2 * BLK)[None, :]
    maps = []
    for win, dil in DILATED:
        n_steps = win // dil + 1
        band = (step >= 0) & (step < n_steps)
        bucket = _rel_bucket(np.clip(step, 0, n_steps - 1) * dil)
        maps.append(np.where(band, bucket, -1))
    return np.stack(maps).astype(np.int32)


def _attn_body(bias_s, q_ref, k_ref, v_ref, o_ref, kh_ref, vh_ref, acc_s, m_s, l_s, *, seq, group_size):
    head_rows = pl.ds(pl.program_id(1), seq, stride=H_A)
    kh_ref[0, head_rows, :] = k_ref[...]
    vh_ref[0, head_rows, :] = v_ref[...]

    def rows(start, dil):
        return pl.ds(start, BLK) if dil == 1 else pl.ds(start, BLK, stride=dil)

    def gather(ref, starts, dil):
        return jnp.stack([ref[rows(s, dil), :] for s in starts]).astype(BF16)

    def qk(q, k):
        return jnp.einsum("gid,gjd->gij", q, k, preferred_element_type=F32)

    def pv(p, v):
        return jnp.einsum("gij,gjd->gid", p.astype(BF16), v, preferred_element_type=F32)

    def group(di, dil, curs, prevs):
        q = gather(q_ref, curs, dil)
        s_c = qk(q, gather(k_ref, curs, dil)) * SCALE + bias_s[0, di, :, BLK:]
        if prevs is None:
            m = jnp.max(s_c, axis=-1, keepdims=True)
            p_c = jnp.exp(s_c - m)
            den = jnp.sum(p_c, axis=-1, keepdims=True)
            o = pv(p_c, gather(v_ref, curs, dil))
        else:
            s_p = qk(q, gather(k_ref, prevs, dil)) * SCALE + bias_s[0, di, :, :BLK]
            m = jnp.max(jnp.maximum(s_c, s_p), axis=-1, keepdims=True)
            p_c = jnp.exp(s_c - m)
            p_p = jnp.exp(s_p - m)
            den = jnp.sum(p_c + p_p, axis=-1, keepdims=True)
            o = pv(p_c, gather(v_ref, curs, dil)) + pv(p_p, gather(v_ref, prevs, dil))
        for g, s in enumerate(curs):
            rc = rows(s, dil)
            acc_s[di, rc, :] = o[g]
            m_s[di, rc, :] = jnp.broadcast_to(m[g], (BLK, HEAD_DIM))
            l_s[di, rc, :] = jnp.broadcast_to(den[g], (BLK, HEAD_DIM))

    for di, (win, dil) in enumerate(DILATED):
        span = dil * BLK
        first = [r for r in range(dil)]
        later = [r + n * span for r in range(dil) for n in range(1, seq // span)]
        for lo in range(0, len(first), group_size):
            group(di, dil, first[lo:lo + group_size], None)
        for lo in range(0, len(later), group_size):
            curs = later[lo:lo + group_size]
            group(di, dil, curs, [s - span for s in curs])

    rows_per = 256

    def merge(c, carry):
        rc = pl.ds(pl.multiple_of(c * rows_per, rows_per), rows_per)
        ms = [m_s[di, rc, :] for di in range(len(DILATED))]
        mx = jnp.maximum(jnp.maximum(ms[0], ms[1]), ms[2])
        num = 0.0
        den = 0.0
        for di in range(len(DILATED)):
            wgt = jnp.exp(ms[di] - mx)
            num = num + wgt * acc_s[di, rc, :]
            den = den + wgt * l_s[di, rc, :]
        o_ref[rc, :] = num / den
        return carry

    lax.fori_loop(0, seq // rows_per, merge, 0)


def _attn_prompt(z, rel_bias, *, batch, seq):
    n = batch * seq
    assert all(seq % (dil * BLK) == 0 for _, dil in DILATED)
    idx = _prompt_bias_index()
    n_d = idx.shape[0]
    bias = _bias_lookup(jnp.asarray(idx.reshape(n_d * BLK, 2 * BLK)), rel_bias.T[:, None, :],
                        n_col_tiles=1)
    bias = bias.reshape(H_A, n_d, BLK, 2 * BLK)
    q_spec, k_spec, v_spec = (pl.BlockSpec((seq, HEAD_DIM), lambda b, h, part=part: (b, part * H_A + h))
                              for part in range(3))
    by_head = pl.BlockSpec((1, seq * H_A, HEAD_DIM), lambda b, h: (b, 0, 0))
    by_head_shape = jax.ShapeDtypeStruct((batch, seq * H_A, HEAD_DIM), F32)
    stat = pltpu.VMEM((n_d, seq, HEAD_DIM), F32)
    oa, kh, vh = pl.pallas_call(
        functools.partial(_attn_body, seq=seq, group_size=8),
        out_shape=[jax.ShapeDtypeStruct((n, W_A), F32), by_head_shape, by_head_shape],
        grid=(batch, H_A),
        in_specs=[
            pl.BlockSpec((1, n_d, BLK, 2 * BLK), lambda b, h: (h, 0, 0, 0)),
            q_spec, k_spec, v_spec,
        ],
        out_specs=[pl.BlockSpec((seq, HEAD_DIM), lambda b, h: (b, h)), by_head, by_head],
        scratch_shapes=[stat, stat, stat],
        compiler_params=_params("arbitrary", "arbitrary"),
        name="attn_prompt",
    )(bias, z, z, z)
    heads = (batch, seq, H_A, HEAD_DIM)
    return oa, kh.reshape(heads), vh.reshape(heads)


def _sgu_body(u_ref, v_ref, w_ref, bt_ref, o_ref, *, n_chunks):
    tri = (lax.broadcasted_iota(jnp.int32, (CHUNK, CHUNK), 0)
           >= lax.broadcasted_iota(jnp.int32, (CHUNK, CHUNK), 1))
    for g in range(H_B):
        w = jnp.where(tri, w_ref[g], 0.0).astype(BF16)
        bias = bt_ref[:, g:g + 1]
        cols = slice(g * C_B, (g + 1) * C_B)
        for c in range(n_chunks):
            rws = slice(c * CHUNK, (c + 1) * CHUNK)
            mixed = _dot(w, v_ref[rws, cols].astype(BF16)) + bias
            o_ref[rws, cols] = u_ref[rws, cols] * mixed


def _sgu_prompt(z, u_blk, v_blk, w_s, b_s, *, tm):
    n = z.shape[0]
    return pl.pallas_call(
        functools.partial(_sgu_body, n_chunks=tm // CHUNK),
        out_shape=jax.ShapeDtypeStruct((n, W_B), F32),
        grid=(n // tm,),
        in_specs=[pl.BlockSpec((tm, W_B), lambda i: (i, u_blk)),
                  pl.BlockSpec((tm, W_B), lambda i: (i, v_blk)),
                  pl.BlockSpec((H_B, CHUNK, CHUNK), lambda i: (0, 0, 0)),
                  pl.BlockSpec((CHUNK, H_B), lambda i: (0, 0))],
        out_specs=pl.BlockSpec((tm, W_B), lambda i: (i, 0)),
        compiler_params=_params("parallel"),
        name="sgu_prompt",
    )(z, z, w_s, b_s.T)


def _mem_body(q_ref, k_ref, v_ref, o_ref):
    for hh in range(H_M):
        sl = slice(hh * HEAD_DIM, (hh + 1) * HEAD_DIM)
        s = _dot_nt(q_ref[0, :, sl].astype(BF16), k_ref[0, :, sl].astype(BF16)) * SCALE
        m = jnp.max(s, axis=-1, keepdims=True)
        p = jnp.exp(s - m)
        den = jnp.sum(p, axis=-1, keepdims=True)
        o_ref[0, :, sl] = _dot(p.astype(BF16), v_ref[0, :, sl].astype(BF16)) / den


def _mem_attend(q, k, v, *, tq, q_blk, k_blk, v_blk):
    batch, t, _ = q.shape
    return pl.pallas_call(
        _mem_body,
        out_shape=jax.ShapeDtypeStruct((batch, t, W_M), F32),
        grid=(batch, t // tq),
        in_specs=[pl.BlockSpec((1, tq, W_M), lambda b, i: (b, i, q_blk)),
                  pl.BlockSpec((1, N_MEM, W_M), lambda b, i: (b, 0, k_blk)),
                  pl.BlockSpec((1, N_MEM, W_M), lambda b, i: (b, 0, v_blk))],
        out_specs=pl.BlockSpec((1, tq, W_M), lambda b, i: (b, i, 0)),
        compiler_params=_params("parallel", "parallel"),
        name="mem_attend",
    )(q, k, v)


def _mems_body(q_ref, k_ref, v_ref, o_ref):
    for b in range(q_ref.shape[0]):
        s = _dot_nt(q_ref[b].astype(BF16), k_ref[b].astype(BF16)) * SCALE
        row = lax.broadcasted_iota(jnp.int32, s.shape, 0)
        col = lax.broadcasted_iota(jnp.int32, s.shape, 1)
        s = jnp.where(row % H_M == col % H_M, s, NEG)
        m = jnp.max(s, axis=-1, keepdims=True)
        p = jnp.exp(s - m)
        den = jnp.sum(p, axis=-1, keepdims=True)
        o_ref[b] = _dot(p.astype(BF16), v_ref[b].astype(BF16)) / den


def _mem_attend_sample(q, cache_k, cache_v, *, tb):
    batch, n_new = q.shape[:2]
    n_rows = n_new * H_M
    q_spec = pl.BlockSpec((tb, n_rows, HEAD_DIM), lambda b: (b, 0, 0))
    kv_spec = pl.BlockSpec((tb, N_MEM * H_M, HEAD_DIM), lambda b: (b, 0, 0))
    out = pl.pallas_call(
        _mems_body,
        out_shape=jax.ShapeDtypeStruct((batch, n_rows, HEAD_DIM), F32),
        grid=(batch // tb,),
        in_specs=[q_spec, kv_spec, kv_spec],
        out_specs=q_spec,
        compiler_params=_params("parallel"),
        name="mem_attend_sample",
    )(q.reshape(batch, n_rows, HEAD_DIM), cache_k.reshape(batch, N_MEM * H_M, HEAD_DIM),
      cache_v.reshape(batch, N_MEM * H_M, HEAD_DIM))
    return out.reshape(batch, n_new, W_M)


def _outproj_body(*refs, n_groups):
    src_refs = [refs[4 * k:4 * k + 3] for k in range(n_groups)]
    h_refs = [refs[4 * k + 3] for k in range(n_groups)]
    g_ref, w_ref = refs[4 * n_groups:4 * n_groups + 2]
    o_refs = refs[4 * n_groups + 2:5 * n_groups + 2]
    cat_ref = refs[-1]
    bounds = _bounds(h_refs)

    @pl.when(pl.program_id(1) == 0)
    def _():
        for srcs, (r0, r1) in zip(src_refs, bounds):
            lo = 0
            for src in srcs:
                hi = lo + src.shape[1]
                cat_ref[r0:r1, lo:hi] = _rms(src[...], g_ref[:, lo:hi]).astype(BF16)
                lo = hi

    res = _dot(cat_ref[...], w_ref[...].astype(BF16))
    for h_ref, o_ref, (r0, r1) in zip(h_refs, o_refs, bounds):
        o_ref[...] = h_ref[...] + res[r0:r1]


def _outproj(groups, g, w, *, n_tiles, tn):
    mix = W_A + W_B + W_M
    rows = _tile_rows([grp[3] for grp in groups], n_tiles)
    in_specs = []
    for r in rows:
        in_specs += [pl.BlockSpec((r, wd), lambda i, j: (i, 0)) for wd in (W_A, W_B, W_M)]
        in_specs += [pl.BlockSpec((r, tn), lambda i, j: (i, j))]
    in_specs += [pl.BlockSpec((1, mix), lambda i, j: (0, 0), pipeline_mode=pl.Buffered(1)),
                 pl.BlockSpec((mix, tn), lambda i, j: (0, j))]
    return pl.pallas_call(
        functools.partial(_outproj_body, n_groups=len(groups)),
        out_shape=[jax.ShapeDtypeStruct(grp[3].shape, F32) for grp in groups],
        grid=(n_tiles, D_MODEL // tn),
        in_specs=in_specs,
        out_specs=[pl.BlockSpec((r, tn), lambda i, j: (i, j)) for r in rows],
        scratch_shapes=[pltpu.VMEM((sum(rows), mix), BF16)],
        compiler_params=_params("arbitrary", "arbitrary"),
        name="outproj",
    )(*[a for grp in groups for a in grp], g.reshape(1, mix), w)


def _sample_bias_index(n_new, n_past):
    (win1, dil1), (tail, dil4), (win16, dil16) = DILATED
    assert dil1 == 1 and win1 == BLK and n_past == win16 and n_past % dil16 == 0
    assert n_past >= tail and n_new <= dil4 and n_new * H_A <= LANES
    n16 = n_past // dil16
    t = np.arange(n_new)[:, None]

    def per_head(index):
        same = np.eye(H_A, dtype=bool)[None, :, None, :]
        full = np.where(same, index[:, None, :, None], -1)
        return full.reshape(n_new * H_A, index.shape[1] * H_A)

    dist = BLK + t - np.arange(BLK)[None, :]
    d1 = np.where(dist <= win1, _rel_bucket(dist), -1)
    dist = tail + t - np.arange(tail)[None, :]
    d4 = np.where((dist % dil4 == 0) & (dist <= tail), _rel_bucket(dist), -1)
    m, tk = np.divmod(np.arange(n16 * n_new)[None, :], n_new)
    dist = n_past + t - dil16 * m - tk
    d16 = np.where(tk == t, _rel_bucket(np.maximum(dist, 0)), -1)
    s = np.arange(LANES // H_A)[None, :]
    dist = t - s
    new = [np.where((s < n_new) & (dist >= 0) & (dist % dil == 0),
                    _rel_bucket(np.maximum(dist, 0)), -1) for _, dil in DILATED]
    pieces = [d1, d4, d16] + new
    return np.concatenate([per_head(p) for p in pieces], axis=1).astype(np.int32)


def _sattn_body(bias_ref, q_ref, kn_ref, vn_ref, kt_ref, k16_ref, vt_ref, v16_ref, o_ref):
    def flat(ref):
        x = ref[0]
        return x.reshape(-1, HEAD_DIM).astype(BF16)

    q = q_ref[0].astype(BF16)
    kt, vt = flat(kt_ref), flat(vt_ref)
    k16, v16 = flat(k16_ref), flat(v16_ref)
    kn, vn = kn_ref[0].astype(BF16), vn_ref[0].astype(BF16)
    n_t, n_1 = kt.shape[0], BLK * H_A

    raw_t = _dot_nt(q, kt)
    raw_n = _dot_nt(q, kn)
    raws = [raw_t[:, n_t - n_1:], raw_t, _dot_nt(q, k16)] + [raw_n] * len(DILATED)
    logits = []
    lo = 0
    for raw in raws:
        hi = lo + raw.shape[1]
        logits.append(raw * SCALE + bias_ref[:, lo:hi])
        lo = hi

    m = functools.reduce(jnp.maximum, [jnp.max(x, axis=-1, keepdims=True) for x in logits])
    probs = [jnp.exp(x - m) for x in logits]
    den = functools.reduce(lambda a, b: a + b, [jnp.sum(p, axis=-1, keepdims=True) for p in probs])
    p_new = functools.reduce(lambda a, b: a + b, probs[3:])
    o = (_dot(probs[0].astype(BF16), vt[n_t - n_1:, :]) + _dot(probs[1].astype(BF16), vt)
         + _dot(probs[2].astype(BF16), v16) + _dot(p_new.astype(BF16), vn))
    o_ref[0] = o / den


def _attn_sample(qa, ka, va, cache_k, cache_v, rel_bias):
    batch, n_new = qa.shape[:2]
    n_past = cache_k.shape[1]
    tail = DILATED[1][0]
    dil16 = DILATED[2][1]
    n16 = n_past // dil16
    n_rows = n_new * H_A
    idx = _sample_bias_index(n_new, n_past)
    bias = _bias_lookup(jnp.asarray(idx), jnp.tile(rel_bias.T, (n_new, 1))[None], n_col_tiles=5)[0]

    def rows(x):
        x = x.reshape(batch, n_rows, HEAD_DIM)
        return jnp.pad(x, ((0, 0), (0, LANES - n_rows), (0, 0)))

    k_res = cache_k.reshape(batch, n16, dil16, H_A, HEAD_DIM)
    v_res = cache_v.reshape(batch, n16, dil16, H_A, HEAD_DIM)
    q_spec = pl.BlockSpec((1, n_rows, HEAD_DIM), lambda b: (b, 0, 0))
    new_spec = pl.BlockSpec((1, LANES, HEAD_DIM), lambda b: (b, 0, 0))
    tail_spec = pl.BlockSpec((1, tail, H_A, HEAD_DIM), lambda b: (b, n_past // tail - 1, 0, 0))
    res_spec = pl.BlockSpec((1, n16, n_new, H_A, HEAD_DIM), lambda b: (b, 0, 0, 0, 0))
    out = pl.pallas_call(
        _sattn_body,
        out_shape=jax.ShapeDtypeStruct((batch, n_rows, HEAD_DIM), F32),
        grid=(batch,),
        in_specs=[
            pl.BlockSpec(bias.shape, lambda b: (0, 0)),
            q_spec, new_spec, new_spec, tail_spec, res_spec, tail_spec, res_spec,
        ],
        out_specs=q_spec,
        compiler_params=_params("parallel"),
        name="attn_sample",
    )(bias, qa.reshape(batch, n_rows, HEAD_DIM), rows(ka), rows(va),
      cache_k, k_res, cache_v, v_res)
    return out.reshape(batch, n_new, W_A)


def _sgus_body(u_ref, v_ref, w_ref, b_ref, o_ref, *, n_new):
    t_idx = lax.broadcasted_iota(jnp.int32, (1, n_new, W_B), 1)
    mixed = 0.0
    for s in range(n_new):
        w_s = jnp.where(t_idx >= s, w_ref[s][None], 0.0).astype(BF16).astype(F32)
        mixed = mixed + w_s * v_ref[:, s:s + 1, :].astype(BF16).astype(F32)
    o_ref[...] = u_ref[...] * (mixed + b_ref[...][None])


def _sgu_sample(u, v, w_s, b_s):
    batch, n_new, _ = u.shape
    w_l = jnp.repeat(jnp.transpose(w_s[:, :n_new, :n_new], (2, 1, 0)), C_B, axis=-1)
    b_l = jnp.repeat(b_s[:, :n_new].T, C_B, axis=-1)
    return pl.pallas_call(
        functools.partial(_sgus_body, n_new=n_new),
        out_shape=jax.ShapeDtypeStruct(u.shape, F32),
        name="sgu_sample",
    )(u, v, w_l, b_l)


COL_KA, COL_VA, COL_U, COL_V, COL_QM = W_A, 2 * W_A, 3 * W_A, 3 * W_A + W_B, 3 * W_A + 2 * W_B


def _mixer_inputs(normed_groups, w_in, g_qa, g_ka, g_sgu, g_qm, *, n_tiles):
    gain = jnp.concatenate([jnp.tile(g_qa, H_A), jnp.tile(g_ka, H_A), jnp.ones((W_A + W_B,), F32),
                            g_sgu.reshape(W_B), jnp.tile(g_qm, H_M)])
    normed = [c // PROJ_TILE for lo, hi in ((0, COL_VA), (COL_V, COL_QM + W_M))
              for c in range(lo, hi, PROJ_TILE)]
    return _proj(normed_groups, None, w_in, gain, normed, n_tiles=n_tiles, chunk=512)


def kernel(x_prompt, x_sample, cache_win_k, cache_win_v, cache_mem_k, cache_mem_v, mem_prompt, rel_bias, g_ffn1, w1_gate, w1_up, w1_down, g_mix, w_in, g_qa, g_ka, g_sgu, w_sgu, b_sgu, g_qm, g_mem, w_mem_kv, g_km, g_mix_out, w_out, g_ffn2, w2_gate, w2_up, w2_down):
    batch, seq, _ = x_prompt.shape
    dec_batch, n_new, _ = x_sample.shape
    depth = g_ffn1.shape[0]
    n_p = batch * seq
    n_s = dec_batch * n_new

    y_p = x_prompt.reshape(n_p, D_MODEL)
    y_s = x_sample.reshape(n_s, D_MODEL)
    outs = [[] for _ in range(7)]
    for l in range(depth):
        h_p, h_s, hn_p, hn_s = _ffn([y_p, y_s], g_ffn1[l], w1_gate[l], w1_up[l], w1_down[l],
                                    g_next=g_mix[l], n_tiles=8, tf=256)
        z_p, z_s = _mixer_inputs([hn_p, hn_s], w_in[l], g_qa[l], g_ka[l], g_sgu[l], g_qm[l], n_tiles=4)

        oa_p, k_p, v_p = _attn_prompt(z_p, rel_bias, batch=batch, seq=seq)
        ob_p = _sgu_prompt(z_p, COL_U // W_B, COL_V // W_B, w_sgu[l], b_sgu[l], tm=1024)
        mem = mem_prompt.reshape(batch * N_MEM, D_MODEL)
        gain_kv = jnp.concatenate([jnp.tile(g_km[l], H_M), jnp.ones((W_M,), F32)])
        mkv, = _proj([mem], g_mem[l], w_mem_kv[l], gain_kv, range(W_M // PROJ_TILE),
                     n_tiles=2, chunk=512)
        mkv3 = mkv.reshape(batch, N_MEM, 2 * W_M)
        om_p = _mem_attend(z_p.reshape(batch, seq, -1), mkv3, mkv3, tq=1024,
                           q_blk=COL_QM // W_M, k_blk=0, v_blk=1)
        keep = min(DILATED[-1][0], seq)
        outs[0].append(k_p[:, seq - keep:])
        outs[1].append(v_p[:, seq - keep:])
        outs[2].append(mkv[:, :W_M].reshape(batch, N_MEM, H_M, HEAD_DIM))
        outs[3].append(mkv[:, W_M:].reshape(batch, N_MEM, H_M, HEAD_DIM))

        zq, zk, zv, zu, zg, zm = (z_s[:, lo:hi] for lo, hi in zip(
            (0, COL_KA, COL_VA, COL_U, COL_V, COL_QM), (COL_KA, COL_VA, COL_U, COL_V, COL_QM, z_s.shape[1])))
        heads = (dec_batch, n_new, H_A, HEAD_DIM)
        oa_s = _attn_sample(zq.reshape(heads), zk.reshape(heads), zv.reshape(heads),
                            cache_win_k[l], cache_win_v[l], rel_bias)
        ob_s = _sgu_sample(zu.reshape(dec_batch, n_new, W_B), zg.reshape(dec_batch, n_new, W_B),
                           w_sgu[l], b_sgu[l])
        om_s = _mem_attend_sample(zm.reshape(dec_batch, n_new, H_M, HEAD_DIM),
                                  cache_mem_k[l], cache_mem_v[l], tb=4)
        outs[4].append(zk.reshape(heads))
        outs[5].append(zv.reshape(heads))
        outs[6].append(zg.reshape(dec_batch, n_new, H_B, C_B))

        h_p, h_s = _outproj([(oa_p, ob_p, om_p.reshape(n_p, W_M), h_p),
                             (oa_s.reshape(n_s, W_A), ob_s.reshape(n_s, W_B), om_s.reshape(n_s, W_M), h_s)],
                            g_mix_out[l], w_out[l], n_tiles=4, tn=256)
        y_p, y_s = _ffn([h_p, h_s], g_ffn2[l], w2_gate[l], w2_up[l], w2_down[l], n_tiles=8, tf=256)

    return (y_p.reshape(batch, seq, D_MODEL), y_s.reshape(dec_batch, n_new, D_MODEL),
            *[jnp.stack(o) for o in outs])
```

```python
import functools

import numpy as np
import jax
import jax.numpy as jnp
from jax import lax
from jax.experimental import pallas as pl
from jax.experimental.pallas import tpu as pltpu

D_MODEL = 2048
HEAD_DIM = 128
H_A = 8
H_B = 4
C_B = 128
H_M = 4
W_A = H_A * HEAD_DIM
W_B = H_B * C_B
W_M = H_M * HEAD_DIM
D_FF = 5632
DILATED = ((128, 1), (512, 4), (2048, 16))
BLK = 128
CHUNK = 128
N_MEM = 256
N_BUCKETS = 32
REL_MAX_DIST = 2048
EPS = 1e-6
SCALE = HEAD_DIM ** -0.5
NEG = -1e30

LANES = 128
BF16_ROWS = 16
PROJ_TILE = 4 * HEAD_DIM
VMEM_LIMIT = 60 * 1024 * 1024

F32 = jnp.float32
BF16 = jnp.bfloat16


def _rel_bucket(dist):
    dist = np.asarray(dist, np.int64)
    max_exact = N_BUCKETS // 2
    large = max_exact + (np.log(np.maximum(dist, 1) / max_exact) / np.log(REL_MAX_DIST / max_exact)
                         * (N_BUCKETS - max_exact)).astype(np.int64)
    large = np.minimum(large, N_BUCKETS - 1)
    return np.where(dist < max_exact, dist, large).astype(np.int32)


def _rms(t, g):
    return t * lax.rsqrt(jnp.mean(t * t, axis=-1, keepdims=True) + EPS) * g


def _dot(a, b):
    return jnp.dot(a, b, preferred_element_type=F32)


def _dot_nt(a, b):
    return lax.dot_general(a, b, (((1,), (1,)), ((), ())), preferred_element_type=F32)


def _params(*sem):
    return pltpu.CompilerParams(dimension_semantics=sem, vmem_limit_bytes=VMEM_LIMIT)


def _tile_rows(groups, n_tiles):
    rows = [x.shape[0] // n_tiles for x in groups]
    assert all(r * n_tiles == x.shape[0] and r % BF16_ROWS == 0 for r, x in zip(rows, groups))
    return rows


def _bounds(refs):
    out, lo = [], 0
    for ref in refs:
        out.append((lo, lo + ref.shape[0]))
        lo += ref.shape[0]
    return out


def _ffn_body(*refs, n_groups, with_next_norm):
    x_refs, refs = refs[:n_groups], refs[n_groups:]
    gn_ref = refs[0] if with_next_norm else None
    g_ref, wg_ref, wu_ref, wd_ref = refs[with_next_norm:with_next_norm + 4]
    refs = refs[with_next_norm + 4:]
    o_refs, on_refs, xn_ref = refs[:n_groups], refs[n_groups:-1], refs[-1]
    bounds = _bounds(x_refs)
    j = pl.program_id(1)

    @pl.when(j == 0)
    def _():
        for x_ref, o_ref, (lo, hi) in zip(x_refs, o_refs, bounds):
            x = x_ref[...]
            xn_ref[lo:hi, :] = _rms(x, g_ref[...]).astype(BF16)
            o_ref[...] = x

    xn = xn_ref[...]
    gate = _dot(xn, wg_ref[...].astype(BF16))
    up = _dot(xn, wu_ref[...].astype(BF16))
    act = (gate * jax.nn.sigmoid(gate)) * up * 0.5
    res = _dot(act.astype(BF16), wd_ref[...].astype(BF16))
    for o_ref, (lo, hi) in zip(o_refs, bounds):
        o_ref[...] += res[lo:hi]

    if with_next_norm:
        @pl.when(j == pl.num_programs(1) - 1)
        def _():
            for o_ref, on_ref in zip(o_refs, on_refs):
                on_ref[...] = _rms(o_ref[...], gn_ref[...]).astype(BF16)


def _ffn(groups, g, wg, wu, wd, *, g_next=None, n_tiles, tf):
    rows = _tile_rows(groups, n_tiles)
    with_next_norm = g_next is not None
    gain_spec = pl.BlockSpec((1, D_MODEL), lambda i, j: (0, 0), pipeline_mode=pl.Buffered(1))
    row_specs = [pl.BlockSpec((r, D_MODEL), lambda i, j: (i, 0)) for r in rows]
    out_shape = [jax.ShapeDtypeStruct(x.shape, F32) for x in groups]
    gains = [g.reshape(1, D_MODEL)]
    if with_next_norm:
        out_shape += [jax.ShapeDtypeStruct(x.shape, BF16) for x in groups]
        gains.insert(0, g_next.reshape(1, D_MODEL))
    return pl.pallas_call(
        functools.partial(_ffn_body, n_groups=len(groups), with_next_norm=with_next_norm),
        out_shape=out_shape,
        grid=(n_tiles, D_FF // tf),
        in_specs=(
            row_specs + [gain_spec] * len(gains)
            + [pl.BlockSpec((D_MODEL, tf), lambda i, j: (0, j)),
               pl.BlockSpec((D_MODEL, tf), lambda i, j: (0, j)),
               pl.BlockSpec((tf, D_MODEL), lambda i, j: (j, 0))]),
        out_specs=row_specs * (2 if with_next_norm else 1),
        scratch_shapes=[pltpu.VMEM((sum(rows), D_MODEL), BF16)],
        compiler_params=_params("arbitrary", "arbitrary"),
        name="ffn",
    )(*groups, *gains, wg, wu, wd)


def _proj_body(*refs, n_groups, pre_normed, norm_tiles, chunk):
    x_refs, refs = refs[:n_groups], refs[n_groups:]
    g_ref = None if pre_normed else refs[0]
    w_ref, gain_ref = refs[1 - pre_normed:3 - pre_normed]
    o_refs, xn_ref = refs[3 - pre_normed:-1], refs[-1]
    bounds = _bounds(x_refs)
    j = pl.program_id(1)

    @pl.when(j == 0)
    def _():
        for x_ref, (lo, hi) in zip(x_refs, bounds):
            x = x_ref[...]
            xn_ref[lo:hi, :] = x if pre_normed else _rms(x, g_ref[...]).astype(BF16)

    w = w_ref[...].astype(BF16)
    gain = gain_ref[...]
    normed = functools.reduce(jnp.logical_or, [j == t for t in norm_tiles])

    def finish(z):
        heads = []
        for hh in range(PROJ_TILE // HEAD_DIM):
            sl = slice(hh * HEAD_DIM, (hh + 1) * HEAD_DIM)
            zz = z[:, sl]
            scale = lax.rsqrt(jnp.mean(zz * zz, axis=-1, keepdims=True) + EPS)
            heads.append(zz * jnp.where(normed, scale, 1.0) * gain[:, sl])
        return jnp.concatenate(heads, axis=1)

    total = bounds[-1][1]
    edges = list(range(0, total, chunk)) + [total]
    if len(edges) > 2 and edges[-1] - edges[-2] < chunk // 2:
        del edges[-2]
    for c0, c1 in zip(edges[:-1], edges[1:]):
        z = finish(_dot(xn_ref[c0:c1, :], w))
        for o_ref, (lo, hi) in zip(o_refs, bounds):
            a, b = max(lo, c0), min(hi, c1)
            if a < b:
                o_ref[a - lo:b - lo, :] = z[a - c0:b - c0, :]


def _proj(groups, g, w, gain, norm_tiles, *, n_tiles, chunk):
    d_in, d_out = w.shape
    rows = _tile_rows(groups, n_tiles)
    pre_normed = g is None
    assert all(x.dtype == (BF16 if pre_normed else F32) for x in groups)
    once = pl.Buffered(1)
    norm_args = [] if pre_normed else [g.reshape(1, d_in)]
    return pl.pallas_call(
        functools.partial(_proj_body, n_groups=len(groups), pre_normed=pre_normed,
                          norm_tiles=tuple(norm_tiles), chunk=chunk),
        out_shape=[jax.ShapeDtypeStruct((x.shape[0], d_out), F32) for x in groups],
        grid=(n_tiles, d_out // PROJ_TILE),
        in_specs=(
            [pl.BlockSpec((r, d_in), lambda i, j: (i, 0)) for r in rows]
            + [pl.BlockSpec((1, d_in), lambda i, j: (0, 0), pipeline_mode=once)] * len(norm_args)
            + [pl.BlockSpec((d_in, PROJ_TILE), lambda i, j: (0, j)),
               pl.BlockSpec((1, PROJ_TILE), lambda i, j: (0, j))]),
        out_specs=[pl.BlockSpec((r, PROJ_TILE), lambda i, j: (i, j)) for r in rows],
        scratch_shapes=[pltpu.VMEM((sum(rows), d_in), BF16)],
        compiler_params=_params("arbitrary", "arbitrary"),
        name="proj",
    )(*groups, *norm_args, w, gain.reshape(1, d_out))


def _bias_body(idx_ref, tab_ref, o_ref):
    idx = idx_ref[...]
    tab = tab_ref[0]
    bias = jnp.full(idx.shape, NEG, F32)
    for bkt in range(N_BUCKETS):
        bias = jnp.where(idx == bkt, tab[:, bkt:bkt + 1], bias)
    o_ref[0] = bias


def _bias_lookup(idx, tab, *, n_col_tiles):
    n_rows, n_cols = idx.shape
    n_g, tab_rows, _ = tab.shape
    tc = n_cols // n_col_tiles
    assert tc * n_col_tiles == n_cols and tc % LANES == 0 and tab_rows in (1, n_rows)
    return pl.pallas_call(
        _bias_body,
        out_shape=jax.ShapeDtypeStruct((n_g, n_rows, n_cols), F32),
        grid=(n_g, n_col_tiles),
        in_specs=[pl.BlockSpec((n_rows, tc), lambda g, j: (0, j)),
                  pl.BlockSpec((1, tab_rows, N_BUCKETS), lambda g, j: (g, 0, 0))],
        out_specs=pl.BlockSpec((1, n_rows, tc), lambda g, j: (g, 0, j)),
        compiler_params=_params("parallel", "parallel"),
        name="bias_lookup",
    )(idx, tab)


SLABS = DILATED[1][1]
PIECE = BLK // SLABS


def _prompt_bias_index():
    step = np.arange(BLK)[:, None] + BLK - np.arange(2 * BLK)[None, :]
    perm = (SLABS * np.arange(PIECE)[None, :] + np.arange(SLABS)[:, None]).reshape(BLK)
    maps = []
    for win, dil in DILATED:
        n_steps = win // dil + 1
        band = (step >= 0) & (step < n_steps)
        bucket = _rel_bucket(np.clip(step, 0, n_steps - 1) * dil)
        idx = np.where(band, bucket, -1)
        if dil == 1:
            idx = idx[np.ix_(perm, np.concatenate([perm, BLK + perm]))]
        maps.append(idx)
    return np.stack(maps).astype(np.int32)


def _block_pieces(dil, r, n):
    if dil == 1:
        return [(s, slice(n * PIECE, (n + 1) * PIECE), PIECE) for s in range(SLABS)]
    if dil == SLABS:
        return [(r, slice(n * BLK, (n + 1) * BLK), BLK)]
    sub = dil // SLABS
    s, t = r % SLABS, r // SLABS
    return [(s, pl.ds(t + sub * BLK * n, BLK, stride=sub), BLK)]


def _attn_body(*refs, seq, group_size):
    bias_s, k_ref, v_ref = refs[:3]
    q_s, k_s, v_s = (refs[3 + SLABS * a:3 + SLABS * (a + 1)] for a in range(3))
    o_ref, kh_ref, vh_ref, acc_s, m_s, l_s = refs[3 + 3 * SLABS:]

    def copy_by_head(c, n_chunks):
        n_rows = seq // n_chunks
        src = pl.ds(c * n_rows, n_rows)
        dst = pl.ds(pl.program_id(1) + H_A * c * n_rows, n_rows, stride=H_A)
        kh_ref[0, dst, :] = k_ref[src, :]
        vh_ref[0, dst, :] = v_ref[src, :]

    def load(slabs, pieces):
        parts = [slabs[s][idx, :] for s, idx, _ in pieces]
        return parts[0] if len(parts) == 1 else jnp.concatenate(parts, axis=0)

    def gather(slabs, blocks):
        return jnp.stack([load(slabs, pieces) for pieces in blocks]).astype(BF16)

    def qk(q, k):
        return jnp.einsum("gid,gjd->gij", q, k, preferred_element_type=F32)

    def pv(p, v):
        return jnp.einsum("gij,gjd->gid", p.astype(BF16), v, preferred_element_type=F32)

    def group(di, curs, prevs):
        q = gather(q_s, curs)
        s_c = qk(q, gather(k_s, curs)) * SCALE + bias_s[0, di, :, BLK:]
        if prevs is None:
            m = jnp.max(s_c, axis=-1, keepdims=True)
            p_c = jnp.exp(s_c - m)
            den = jnp.sum(p_c, axis=-1, keepdims=True)
            o = pv(p_c, gather(v_s, curs))
        else:
            s_p = qk(q, gather(k_s, prevs)) * SCALE + bias_s[0, di, :, :BLK]
            m = jnp.max(jnp.maximum(s_c, s_p), axis=-1, keepdims=True)
            p_c = jnp.exp(s_c - m)
            p_p = jnp.exp(s_p - m)
            den = jnp.sum(p_c + p_p, axis=-1, keepdims=True)
            o = pv(p_c, gather(v_s, curs)) + pv(p_p, gather(v_s, prevs))
        for g, pieces in enumerate(curs):
            m_g = jnp.broadcast_to(m[g], (BLK, HEAD_DIM))
            l_g = jnp.broadcast_to(den[g], (BLK, HEAD_DIM))
            lo = 0
            for s, idx, n_rows in pieces:
                acc_s[di, s, idx, :] = o[g, lo:lo + n_rows]
                m_s[di, s, idx, :] = m_g[lo:lo + n_rows]
                l_s[di, s, idx, :] = l_g[lo:lo + n_rows]
                lo += n_rows

    groups = []
    for di, (win, dil) in enumerate(DILATED):
        n_blocks = seq // (dil * BLK)
        first = [_block_pieces(dil, r, 0) for r in range(dil)]
        later = [(r, n) for r in range(dil) for n in range(1, n_blocks)]
        for lo in range(0, len(first), group_size):
            groups.append((di, first[lo:lo + group_size], None))
        for lo in range(0, len(later), group_size):
            part = later[lo:lo + group_size]
            groups.append((di, [_block_pieces(dil, r, n) for r, n in part],
                           [_block_pieces(dil, r, n - 1) for r, n in part]))
    n_chunks = 1 << (len(groups).bit_length() - 1)
    for c, args in enumerate(groups):
        group(*args)
        if c < n_chunks:
            copy_by_head(c, n_chunks)

    rows_per = 256
    per_slab = seq // SLABS // rows_per

    def merge(c, carry):
        s = c // per_slab
        r0 = pl.multiple_of((c % per_slab) * rows_per, rows_per)
        rc = pl.ds(r0, rows_per)
        ms = [m_s[di, s, rc, :] for di in range(len(DILATED))]
        mx = jnp.maximum(jnp.maximum(ms[0], ms[1]), ms[2])
        num = 0.0
        den = 0.0
        for di in range(len(DILATED)):
            wgt = jnp.exp(ms[di] - mx)
            num = num + wgt * acc_s[di, s, rc, :]
            den = den + wgt * l_s[di, s, rc, :]
        o_ref[pl.ds(SLABS * r0 + s, rows_per, stride=SLABS), :] = num / den
        return carry

    lax.fori_loop(0, SLABS * per_slab, merge, 0)


def _attn_prompt(z, rel_bias, *, batch, seq):
    n = batch * seq
    assert all(seq % (dil * BLK) == 0 for _, dil in DILATED)
    idx = _prompt_bias_index()
    n_d = idx.shape[0]
    bias = _bias_lookup(jnp.asarray(idx.reshape(n_d * BLK, 2 * BLK)), rel_bias.T[:, None, :],
                        n_col_tiles=1)
    bias = bias.reshape(H_A, n_d, BLK, 2 * BLK)
    rows = seq // SLABS
    z_cols = z.shape[1] // HEAD_DIM
    z_slabs = z.reshape(batch, rows, SLABS * z.shape[1])
    in_specs = [pl.BlockSpec((1, n_d, BLK, 2 * BLK), lambda b, h: (h, 0, 0, 0))]
    in_specs += [pl.BlockSpec((seq, HEAD_DIM), lambda b, h, part=part: (b, part * H_A + h))
                 for part in (1, 2)]
    in_specs += [pl.BlockSpec((None, rows, HEAD_DIM),
                              lambda b, h, s=s, part=part: (b, 0, s * z_cols + part * H_A + h))
                 for part in range(3) for s in range(SLABS)]
    by_head = pl.BlockSpec((1, seq * H_A, HEAD_DIM), lambda b, h: (b, 0, 0))
    by_head_shape = jax.ShapeDtypeStruct((batch, seq * H_A, HEAD_DIM), F32)
    stat = pltpu.VMEM((n_d, SLABS, rows, HEAD_DIM), F32)
    oa, kh, vh = pl.pallas_call(
        functools.partial(_attn_body, seq=seq, group_size=8),
        out_shape=[jax.ShapeDtypeStruct((n, W_A), F32), by_head_shape, by_head_shape],
        grid=(batch, H_A),
        in_specs=in_specs,
        out_specs=[pl.BlockSpec((seq, HEAD_DIM), lambda b, h: (b, h)), by_head, by_head],
        scratch_shapes=[stat, stat, stat],
        compiler_params=_params("arbitrary", "arbitrary"),
        name="attn_prompt",
    )(bias, z, z, *([z_slabs] * (3 * SLABS)))
    heads = (batch, seq, H_A, HEAD_DIM)
    return oa, kh.reshape(heads), vh.reshape(heads)


def _sgu_body(u_ref, v_ref, w_ref, bt_ref, o_ref, *, n_chunks):
    tri = (lax.broadcasted_iota(jnp.int32, (CHUNK, CHUNK), 0)
           >= lax.broadcasted_iota(jnp.int32, (CHUNK, CHUNK), 1))
    for g in range(H_B):
        w = jnp.where(tri, w_ref[g], 0.0).astype(BF16)
        bias = bt_ref[:, g:g + 1]
        cols = slice(g * C_B, (g + 1) * C_B)
        for c in range(n_chunks):
            rws = slice(c * CHUNK, (c + 1) * CHUNK)
            mixed = _dot(w, v_ref[rws, cols].astype(BF16)) + bias
            o_ref[rws, cols] = u_ref[rws, cols] * mixed


def _sgu_prompt(z, u_blk, v_blk, w_s, b_s, *, tm):
    n = z.shape[0]
    return pl.pallas_call(
        functools.partial(_sgu_body, n_chunks=tm // CHUNK),
        out_shape=jax.ShapeDtypeStruct((n, W_B), F32),
        grid=(n // tm,),
        in_specs=[pl.BlockSpec((tm, W_B), lambda i: (i, u_blk)),
                  pl.BlockSpec((tm, W_B), lambda i: (i, v_blk)),
                  pl.BlockSpec((H_B, CHUNK, CHUNK), lambda i: (0, 0, 0)),
                  pl.BlockSpec((CHUNK, H_B), lambda i: (0, 0))],
        out_specs=pl.BlockSpec((tm, W_B), lambda i: (i, 0)),
        compiler_params=_params("parallel"),
        name="sgu_prompt",
    )(z, z, w_s, b_s.T)


def _mem_body(q_ref, k_ref, v_ref, o_ref):
    for hh in range(H_M):
        sl = slice(hh * HEAD_DIM, (hh + 1) * HEAD_DIM)
        s = _dot_nt(q_ref[0, :, sl].astype(BF16), k_ref[0, :, sl].astype(BF16)) * SCALE
        m = jnp.max(s, axis=-1, keepdims=True)
        p = jnp.exp(s - m)
        den = jnp.sum(p, axis=-1, keepdims=True)
        o_ref[0, :, sl] = _dot(p.astype(BF16), v_ref[0, :, sl].astype(BF16)) / den


def _mem_attend(q, k, v, *, tq, q_blk, k_blk, v_blk):
    batch, t, _ = q.shape
    return pl.pallas_call(
        _mem_body,
        out_shape=jax.ShapeDtypeStruct((batch, t, W_M), F32),
        grid=(batch, t // tq),
        in_specs=[pl.BlockSpec((1, tq, W_M), lambda b, i: (b, i, q_blk)),
                  pl.BlockSpec((1, N_MEM, W_M), lambda b, i: (b, 0, k_blk)),
                  pl.BlockSpec((1, N_MEM, W_M), lambda b, i: (b, 0, v_blk))],
        out_specs=pl.BlockSpec((1, tq, W_M), lambda b, i: (b, i, 0)),
        compiler_params=_params("parallel", "parallel"),
        name="mem_attend",
    )(q, k, v)


def _mems_body(q_ref, k_ref, v_ref, o_ref):
    for b in range(q_ref.shape[0]):
        s = _dot_nt(q_ref[b].astype(BF16), k_ref[b].astype(BF16)) * SCALE
        row = lax.broadcasted_iota(jnp.int32, s.shape, 0)
        col = lax.broadcasted_iota(jnp.int32, s.shape, 1)
        s = jnp.where(row % H_M == col % H_M, s, NEG)
        m = jnp.max(s, axis=-1, keepdims=True)
        p = jnp.exp(s - m)
        den = jnp.sum(p, axis=-1, keepdims=True)
        o_ref[b] = _dot(p.astype(BF16), v_ref[b].astype(BF16)) / den


def _mem_attend_sample(q, cache_k, cache_v, *, tb):
    batch, n_new = q.shape[:2]
    n_rows = n_new * H_M
    q_spec = pl.BlockSpec((tb, n_rows, HEAD_DIM), lambda b: (b, 0, 0))
    kv_spec = pl.BlockSpec((tb, N_MEM * H_M, HEAD_DIM), lambda b: (b, 0, 0))
    out = pl.pallas_call(
        _mems_body,
        out_shape=jax.ShapeDtypeStruct((batch, n_rows, HEAD_DIM), F32),
        grid=(batch // tb,),
        in_specs=[q_spec, kv_spec, kv_spec],
        out_specs=q_spec,
        compiler_params=_params("parallel"),
        name="mem_attend_sample",
    )(q.reshape(batch, n_rows, HEAD_DIM), cache_k.reshape(batch, N_MEM * H_M, HEAD_DIM),
      cache_v.reshape(batch, N_MEM * H_M, HEAD_DIM))
    return out.reshape(batch, n_new, W_M)


def _outproj_body(*refs, n_groups):
    src_refs = [refs[4 * k:4 * k + 3] for k in range(n_groups)]
    h_refs = [refs[4 * k + 3] for k in range(n_groups)]
    g_ref, w_ref = refs[4 * n_groups:4 * n_groups + 2]
    o_refs = refs[4 * n_groups + 2:5 * n_groups + 2]
    cat_ref = refs[-1]
    bounds = _bounds(h_refs)

    @pl.when(pl.program_id(1) == 0)
    def _():
        for srcs, (r0, r1) in zip(src_refs, bounds):
            lo = 0
            for src in srcs:
                hi = lo + src.shape[1]
                cat_ref[r0:r1, lo:hi] = _rms(src[...], g_ref[:, lo:hi]).astype(BF16)
                lo = hi

    res = _dot(cat_ref[...], w_ref[...].astype(BF16))
    for h_ref, o_ref, (r0, r1) in zip(h_refs, o_refs, bounds):
        o_ref[...] = h_ref[...] + res[r0:r1]


def _outproj(groups, g, w, *, n_tiles, tn):
    mix = W_A + W_B + W_M
    rows = _tile_rows([grp[3] for grp in groups], n_tiles)
    in_specs = []
    for r in rows:
        in_specs += [pl.BlockSpec((r, wd), lambda i, j: (i, 0)) for wd in (W_A, W_B, W_M)]
        in_specs += [pl.BlockSpec((r, tn), lambda i, j: (i, j))]
    in_specs += [pl.BlockSpec((1, mix), lambda i, j: (0, 0), pipeline_mode=pl.Buffered(1)),
                 pl.BlockSpec((mix, tn), lambda i, j: (0, j))]
    return pl.pallas_call(
        functools.partial(_outproj_body, n_groups=len(groups)),
        out_shape=[jax.ShapeDtypeStruct(grp[3].shape, F32) for grp in groups],
        grid=(n_tiles, D_MODEL // tn),
        in_specs=in_specs,
        out_specs=[pl.BlockSpec((r, tn), lambda i, j: (i, j)) for r in rows],
        scratch_shapes=[pltpu.VMEM((sum(rows), mix), BF16)],
        compiler_params=_params("arbitrary", "arbitrary"),
        name="outproj",
    )(*[a for grp in groups for a in grp], g.reshape(1, mix), w)


def _sample_bias_index(n_new, n_past):
    (win1, dil1), (tail, dil4), (win16, dil16) = DILATED
    assert dil1 == 1 and win1 == BLK and n_past == win16 and n_past % dil16 == 0
    assert n_past >= tail and n_new <= dil4 and n_new * H_A <= LANES
    n16 = n_past // dil16
    t = np.arange(n_new)[:, None]

    def per_head(index):
        same = np.eye(H_A, dtype=bool)[None, :, None, :]
        full = np.where(same, index[:, None, :, None], -1)
        return full.reshape(n_new * H_A, index.shape[1] * H_A)

    dist = BLK + t - np.arange(BLK)[None, :]
    d1 = np.where(dist <= win1, _rel_bucket(dist), -1)
    dist = tail + t - np.arange(tail)[None, :]
    d4 = np.where((dist % dil4 == 0) & (dist <= tail), _rel_bucket(dist), -1)
    m, tk = np.divmod(np.arange(n16 * n_new)[None, :], n_new)
    dist = n_past + t - dil16 * m - tk
    d16 = np.where(tk == t, _rel_bucket(np.maximum(dist, 0)), -1)
    s = np.arange(LANES // H_A)[None, :]
    dist = t - s
    new = [np.where((s < n_new) & (dist >= 0) & (dist % dil == 0),
                    _rel_bucket(np.maximum(dist, 0)), -1) for _, dil in DILATED]
    pieces = [d1, d4, d16] + new
    return np.concatenate([per_head(p) for p in pieces], axis=1).astype(np.int32)


def _sattn_body(bias_ref, q_ref, kn_ref, vn_ref, kt_ref, k16_ref, vt_ref, v16_ref, o_ref):
    def flat(ref):
        x = ref[0]
        return x.reshape(-1, HEAD_DIM).astype(BF16)

    q = q_ref[0].astype(BF16)
    kt, vt = flat(kt_ref), flat(vt_ref)
    k16, v16 = flat(k16_ref), flat(v16_ref)
    kn, vn = kn_ref[0].astype(BF16), vn_ref[0].astype(BF16)
    n_t, n_1 = kt.shape[0], BLK * H_A

    raw_t = _dot_nt(q, kt)
    raw_n = _dot_nt(q, kn)
    raws = [raw_t[:, n_t - n_1:], raw_t, _dot_nt(q, k16)] + [raw_n] * len(DILATED)
    logits = []
    lo = 0
    for raw in raws:
        hi = lo + raw.shape[1]
        logits.append(raw * SCALE + bias_ref[:, lo:hi])
        lo = hi

    m = functools.reduce(jnp.maximum, [jnp.max(x, axis=-1, keepdims=True) for x in logits])
    probs = [jnp.exp(x - m) for x in logits]
    den = functools.reduce(lambda a, b: a + b, [jnp.sum(p, axis=-1, keepdims=True) for p in probs])
    p_new = functools.reduce(lambda a, b: a + b, probs[3:])
    o = (_dot(probs[0].astype(BF16), vt[n_t - n_1:, :]) + _dot(probs[1].astype(BF16), vt)
         + _dot(probs[2].astype(BF16), v16) + _dot(p_new.astype(BF16), vn))
    o_ref[0] = o / den


def _attn_sample(qa, ka, va, cache_k, cache_v, rel_bias):
    batch, n_new = qa.shape[:2]
    n_past = cache_k.shape[1]
    tail = DILATED[1][0]
    dil16 = DILATED[2][1]
    n16 = n_past // dil16
    n_rows = n_new * H_A
    idx = _sample_bias_index(n_new, n_past)
    bias = _bias_lookup(jnp.asarray(idx), jnp.tile(rel_bias.T, (n_new, 1))[None], n_col_tiles=5)[0]

    def rows(x):
        x = x.reshape(batch, n_rows, HEAD_DIM)
        return jnp.pad(x, ((0, 0), (0, LANES - n_rows), (0, 0)))

    k_res = cache_k.reshape(batch, n16, dil16, H_A, HEAD_DIM)
    v_res = cache_v.reshape(batch, n16, dil16, H_A, HEAD_DIM)
    q_spec = pl.BlockSpec((1, n_rows, HEAD_DIM), lambda b: (b, 0, 0))
    new_spec = pl.BlockSpec((1, LANES, HEAD_DIM), lambda b: (b, 0, 0))
    tail_spec = pl.BlockSpec((1, tail, H_A, HEAD_DIM), lambda b: (b, n_past // tail - 1, 0, 0))
    res_spec = pl.BlockSpec((1, n16, n_new, H_A, HEAD_DIM), lambda b: (b, 0, 0, 0, 0))
    out = pl.pallas_call(
        _sattn_body,
        out_shape=jax.ShapeDtypeStruct((batch, n_rows, HEAD_DIM), F32),
        grid=(batch,),
        in_specs=[
            pl.BlockSpec(bias.shape, lambda b: (0, 0)),
            q_spec, new_spec, new_spec, tail_spec, res_spec, tail_spec, res_spec,
        ],
        out_specs=q_spec,
        compiler_params=_params("parallel"),
        name="attn_sample",
    )(bias, qa.reshape(batch, n_rows, HEAD_DIM), rows(ka), rows(va),
      cache_k, k_res, cache_v, v_res)
    return out.reshape(batch, n_new, W_A)


def _sgus_body(u_ref, v_ref, w_ref, b_ref, o_ref, *, n_new):
    t_idx = lax.broadcasted_iota(jnp.int32, (1, n_new, W_B), 1)
    mixed = 0.0
    for s in range(n_new):
        w_s = jnp.where(t_idx >= s, w_ref[s][None], 0.0).astype(BF16).astype(F32)
        mixed = mixed + w_s * v_ref[:, s:s + 1, :].astype(BF16).astype(F32)
    o_ref[...] = u_ref[...] * (mixed + b_ref[...][None])


def _sgu_sample(u, v, w_s, b_s):
    batch, n_new, _ = u.shape
    w_l = jnp.repeat(jnp.transpose(w_s[:, :n_new, :n_new], (2, 1, 0)), C_B, axis=-1)
    b_l = jnp.repeat(b_s[:, :n_new].T, C_B, axis=-1)
    return pl.pallas_call(
        functools.partial(_sgus_body, n_new=n_new),
        out_shape=jax.ShapeDtypeStruct(u.shape, F32),
        name="sgu_sample",
    )(u, v, w_l, b_l)


COL_KA, COL_VA, COL_U, COL_V, COL_QM = W_A, 2 * W_A, 3 * W_A, 3 * W_A + W_B, 3 * W_A + 2 * W_B


def _mixer_inputs(normed_groups, w_in, g_qa, g_ka, g_sgu, g_qm, *, n_tiles):
    gain = jnp.concatenate([jnp.tile(g_qa, H_A), jnp.tile(g_ka, H_A), jnp.ones((W_A + W_B,), F32),
                            g_sgu.reshape(W_B), jnp.tile(g_qm, H_M)])
    normed = [c // PROJ_TILE for lo, hi in ((0, COL_VA), (COL_V, COL_QM + W_M))
              for c in range(lo, hi, PROJ_TILE)]
    return _proj(normed_groups, None, w_in, gain, normed, n_tiles=n_tiles, chunk=512)


def kernel(x_prompt, x_sample, cache_win_k, cache_win_v, cache_mem_k, cache_mem_v, mem_prompt, rel_bias, g_ffn1, w1_gate, w1_up, w1_down, g_mix, w_in, g_qa, g_ka, g_sgu, w_sgu, b_sgu, g_qm, g_mem, w_mem_kv, g_km, g_mix_out, w_out, g_ffn2, w2_gate, w2_up, w2_down):
    batch, seq, _ = x_prompt.shape
    dec_batch, n_new, _ = x_sample.shape
    depth = g_ffn1.shape[0]
    n_p = batch * seq
    n_s = dec_batch * n_new

    y_p = x_prompt.reshape(n_p, D_MODEL)
    y_s = x_sample.reshape(n_s, D_MODEL)
    outs = [[] for _ in range(7)]
    for l in range(depth):
        h_p, h_s, hn_p, hn_s = _ffn([y_p, y_s], g_ffn1[l], w1_gate[l], w1_up[l], w1_down[l],
                                    g_next=g_mix[l], n_tiles=8, tf=256)
        z_p, z_s = _mixer_inputs([hn_p, hn_s], w_in[l], g_qa[l], g_ka[l], g_sgu[l], g_qm[l], n_tiles=4)

        oa_p, k_p, v_p = _attn_prompt(z_p, rel_bias, batch=batch, seq=seq)
        ob_p = _sgu_prompt(z_p, COL_U // W_B, COL_V // W_B, w_sgu[l], b_sgu[l], tm=1024)
        mem = mem_prompt.reshape(batch * N_MEM, D_MODEL)
        gain_kv = jnp.concatenate([jnp.tile(g_km[l], H_M), jnp.ones((W_M,), F32)])
        mkv, = _proj([mem], g_mem[l], w_mem_kv[l], gain_kv, range(W_M // PROJ_TILE),
                     n_tiles=2, chunk=512)
        mkv3 = mkv.reshape(batch, N_MEM, 2 * W_M)
        om_p = _mem_attend(z_p.reshape(batch, seq, -1), mkv3, mkv3, tq=1024,
                           q_blk=COL_QM // W_M, k_blk=0, v_blk=1)
        keep = min(DILATED[-1][0], seq)
        outs[0].append(k_p[:, seq - keep:])
        outs[1].append(v_p[:, seq - keep:])
        outs[2].append(mkv[:, :W_M].reshape(batch, N_MEM, H_M, HEAD_DIM))
        outs[3].append(mkv[:, W_M:].reshape(batch, N_MEM, H_M, HEAD_DIM))

        zq, zk, zv, zu, zg, zm = (z_s[:, lo:hi] for lo, hi in zip(
            (0, COL_KA, COL_VA, COL_U, COL_V, COL_QM), (COL_KA, COL_VA, COL_U, COL_V, COL_QM, z_s.shape[1])))
        heads = (dec_batch, n_new, H_A, HEAD_DIM)
        oa_s = _attn_sample(zq.reshape(heads), zk.reshape(heads), zv.reshape(heads),
                            cache_win_k[l], cache_win_v[l], rel_bias)
        ob_s = _sgu_sample(zu.reshape(dec_batch, n_new, W_B), zg.reshape(dec_batch, n_new, W_B),
                           w_sgu[l], b_sgu[l])
        om_s = _mem_attend_sample(zm.reshape(dec_batch, n_new, H_M, HEAD_DIM),
                                  cache_mem_k[l], cache_mem_v[l], tb=4)
        outs[4].append(zk.reshape(heads))
        outs[5].append(zv.reshape(heads))
        outs[6].append(zg.reshape(dec_batch, n_new, H_B, C_B))

        h_p, h_s = _outproj([(oa_p, ob_p, om_p.reshape(n_p, W_M), h_p),
                             (oa_s.reshape(n_s, W_A), ob_s.reshape(n_s, W_B), om_s.reshape(n_s, W_M), h_s)],
                            g_mix_out[l], w_out[l], n_tiles=4, tn=256)
        y_p, y_s = _ffn([h_p, h_s], g_ffn2[l], w2_gate[l], w2_up[l], w2_down[l], n_tiles=8, tf=256)

    return (y_p.reshape(batch, seq, D_MODEL), y_s.reshape(dec_batch, n_new, D_MODEL),
            *[jnp.stack(o) for o in outs])
```

```python
import functools

import numpy as np
import jax
import jax.numpy as jnp
from jax import lax
from jax.experimental import pallas as pl
from jax.experimental.pallas import tpu as pltpu

D_MODEL = 2048
HEAD_DIM = 128
H_A = 8
H_B = 4
C_B = 128
H_M = 4
W_A = H_A * HEAD_DIM
W_B = H_B * C_B
W_M = H_M * HEAD_DIM
D_FF = 5632
DILATED = ((128, 1), (512, 4), (2048, 16))
BLK = 128
CHUNK = 128
N_MEM = 256
N_BUCKETS = 32
REL_MAX_DIST = 2048
EPS = 1e-6
SCALE = HEAD_DIM ** -0.5
NEG = -1e30

LANES = 128
BF16_ROWS = 16
PROJ_TILE = 4 * HEAD_DIM
VMEM_LIMIT = 60 * 1024 * 1024

F32 = jnp.float32
BF16 = jnp.bfloat16


def _rel_bucket(dist):
    dist = np.asarray(dist, np.int64)
    max_exact = N_BUCKETS // 2
    large = max_exact + (np.log(np.maximum(dist, 1) / max_exact) / np.log(REL_MAX_DIST / max_exact)
                         * (N_BUCKETS - max_exact)).astype(np.int64)
    large = np.minimum(large, N_BUCKETS - 1)
    return np.where(dist < max_exact, dist, large).astype(np.int32)


def _rms(t, g):
    return t * lax.rsqrt(jnp.mean(t * t, axis=-1, keepdims=True) + EPS) * g


def _dot(a, b):
    return jnp.dot(a, b, preferred_element_type=F32)


def _dot_nt(a, b):
    return lax.dot_general(a, b, (((1,), (1,)), ((), ())), preferred_element_type=F32)


def _params(*sem):
    return pltpu.CompilerParams(dimension_semantics=sem, vmem_limit_bytes=VMEM_LIMIT)


def _tile_rows(groups, n_tiles):
    rows = [x.shape[0] // n_tiles for x in groups]
    assert all(r * n_tiles == x.shape[0] and r % BF16_ROWS == 0 for r, x in zip(rows, groups))
    return rows


def _bounds(refs):
    out, lo = [], 0
    for ref in refs:
        out.append((lo, lo + ref.shape[0]))
        lo += ref.shape[0]
    return out


def _ffn_body(*refs, n_groups, with_next_norm):
    x_refs, refs = refs[:n_groups], refs[n_groups:]
    gn_ref = refs[0] if with_next_norm else None
    g_ref, wg_ref, wu_ref, wd_ref = refs[with_next_norm:with_next_norm + 4]
    refs = refs[with_next_norm + 4:]
    o_refs, on_refs, xn_ref = refs[:n_groups], refs[n_groups:-1], refs[-1]
    bounds = _bounds(x_refs)
    j = pl.program_id(1)

    @pl.when(j == 0)
    def _():
        for x_ref, o_ref, (lo, hi) in zip(x_refs, o_refs, bounds):
            x = x_ref[...]
            xn_ref[lo:hi, :] = _rms(x, g_ref[...]).astype(BF16)
            o_ref[...] = x

    xn = xn_ref[...]
    gate = _dot(xn, wg_ref[...].astype(BF16))
    up = _dot(xn, wu_ref[...].astype(BF16))
    act = (gate * jax.nn.sigmoid(gate)) * up * 0.5
    res = _dot(act.astype(BF16), wd_ref[...].astype(BF16))
    for o_ref, (lo, hi) in zip(o_refs, bounds):
        o_ref[...] += res[lo:hi]

    if with_next_norm:
        @pl.when(j == pl.num_programs(1) - 1)
        def _():
            for o_ref, on_ref in zip(o_refs, on_refs):
                on_ref[...] = _rms(o_ref[...], gn_ref[...]).astype(BF16)


def _ffn(groups, g, wg, wu, wd, *, g_next=None, n_tiles, tf):
    rows = _tile_rows(groups, n_tiles)
    with_next_norm = g_next is not None
    gain_spec = pl.BlockSpec((1, D_MODEL), lambda i, j: (0, 0), pipeline_mode=pl.Buffered(1))
    row_specs = [pl.BlockSpec((r, D_MODEL), lambda i, j: (i, 0)) for r in rows]
    out_shape = [jax.ShapeDtypeStruct(x.shape, F32) for x in groups]
    gains = [g.reshape(1, D_MODEL)]
    if with_next_norm:
        out_shape += [jax.ShapeDtypeStruct(x.shape, BF16) for x in groups]
        gains.insert(0, g_next.reshape(1, D_MODEL))
    return pl.pallas_call(
        functools.partial(_ffn_body, n_groups=len(groups), with_next_norm=with_next_norm),
        out_shape=out_shape,
        grid=(n_tiles, D_FF // tf),
        in_specs=(
            row_specs + [gain_spec] * len(gains)
            + [pl.BlockSpec((D_MODEL, tf), lambda i, j: (0, j)),
               pl.BlockSpec((D_MODEL, tf), lambda i, j: (0, j)),
               pl.BlockSpec((tf, D_MODEL), lambda i, j: (j, 0))]),
        out_specs=row_specs * (2 if with_next_norm else 1),
        scratch_shapes=[pltpu.VMEM((sum(rows), D_MODEL), BF16)],
        compiler_params=_params("arbitrary", "arbitrary"),
        name="ffn",
    )(*groups, *gains, wg, wu, wd)


def _proj_body(*refs, n_groups, pre_normed, norm_tiles, chunk):
    x_refs, refs = refs[:n_groups], refs[n_groups:]
    g_ref = None if pre_normed else refs[0]
    w_ref, gain_ref = refs[1 - pre_normed:3 - pre_normed]
    o_refs, xn_ref = refs[3 - pre_normed:-1], refs[-1]
    bounds = _bounds(x_refs)
    j = pl.program_id(1)

    @pl.when(j == 0)
    def _():
        for x_ref, (lo, hi) in zip(x_refs, bounds):
            x = x_ref[...]
            xn_ref[lo:hi, :] = x if pre_normed else _rms(x, g_ref[...]).astype(BF16)

    w = w_ref[...].astype(BF16)
    gain = gain_ref[...]
    normed = functools.reduce(jnp.logical_or, [j == t for t in norm_tiles])

    def finish(z):
        heads = []
        for hh in range(PROJ_TILE // HEAD_DIM):
            sl = slice(hh * HEAD_DIM, (hh + 1) * HEAD_DIM)
            zz = z[:, sl]
            scale = lax.rsqrt(jnp.mean(zz * zz, axis=-1, keepdims=True) + EPS)
            heads.append(zz * jnp.where(normed, scale, 1.0) * gain[:, sl])
        return jnp.concatenate(heads, axis=1)

    total = bounds[-1][1]
    edges = list(range(0, total, chunk)) + [total]
    if len(edges) > 2 and edges[-1] - edges[-2] < chunk // 2:
        del edges[-2]
    for c0, c1 in zip(edges[:-1], edges[1:]):
        z = finish(_dot(xn_ref[c0:c1, :], w))
        for o_ref, (lo, hi) in zip(o_refs, bounds):
            a, b = max(lo, c0), min(hi, c1)
            if a < b:
                o_ref[a - lo:b - lo, :] = z[a - c0:b - c0, :]


def _proj(groups, g, w, gain, norm_tiles, *, n_tiles, chunk):
    d_in, d_out = w.shape
    rows = _tile_rows(groups, n_tiles)
    pre_normed = g is None
    assert all(x.dtype == (BF16 if pre_normed else F32) for x in groups)
    once = pl.Buffered(1)
    norm_args = [] if pre_normed else [g.reshape(1, d_in)]
    return pl.pallas_call(
        functools.partial(_proj_body, n_groups=len(groups), pre_normed=pre_normed,
                          norm_tiles=tuple(norm_tiles), chunk=chunk),
        out_shape=[jax.ShapeDtypeStruct((x.shape[0], d_out), F32) for x in groups],
        grid=(n_tiles, d_out // PROJ_TILE),
        in_specs=(
            [pl.BlockSpec((r, d_in), lambda i, j: (i, 0)) for r in rows]
            + [pl.BlockSpec((1, d_in), lambda i, j: (0, 0), pipeline_mode=once)] * len(norm_args)
            + [pl.BlockSpec((d_in, PROJ_TILE), lambda i, j: (0, j)),
               pl.BlockSpec((1, PROJ_TILE), lambda i, j: (0, j))]),
        out_specs=[pl.BlockSpec((r, PROJ_TILE), lambda i, j: (i, j)) for r in rows],
        scratch_shapes=[pltpu.VMEM((sum(rows), d_in), BF16)],
        compiler_params=_params("arbitrary", "arbitrary"),
        name="proj",
    )(*groups, *norm_args, w, gain.reshape(1, d_out))


def _bias_body(idx_ref, tab_ref, o_ref):
    idx = idx_ref[...]
    tab = tab_ref[0]
    bias = jnp.full(idx.shape, NEG, F32)
    for bkt in range(N_BUCKETS):
        bias = jnp.where(idx == bkt, tab[:, bkt:bkt + 1], bias)
    o_ref[0] = bias


def _bias_lookup(idx, tab, *, n_col_tiles):
    n_rows, n_cols = idx.shape
    n_g, tab_rows, _ = tab.shape
    tc = n_cols // n_col_tiles
    assert tc * n_col_tiles == n_cols and tc % LANES == 0 and tab_rows in (1, n_rows)
    return pl.pallas_call(
        _bias_body,
        out_shape=jax.ShapeDtypeStruct((n_g, n_rows, n_cols), F32),
        grid=(n_g, n_col_tiles),
        in_specs=[pl.BlockSpec((n_rows, tc), lambda g, j: (0, j)),
                  pl.BlockSpec((1, tab_rows, N_BUCKETS), lambda g, j: (g, 0, 0))],
        out_specs=pl.BlockSpec((1, n_rows, tc), lambda g, j: (g, 0, j)),
        compiler_params=_params("parallel", "parallel"),
        name="bias_lookup",
    )(idx, tab)


SLABS = DILATED[1][1]
PIECE = BLK // SLABS


def _prompt_bias_index():
    step = np.arange(BLK)[:, None] + BLK - np.arange(2 * BLK)[None, :]
    perm = (SLABS * np.arange(PIECE)[None, :] + np.arange(SLABS)[:, None]).reshape(BLK)
    maps = []
    for win, dil in DILATED:
        n_steps = win // dil + 1
        band = (step >= 0) & (step < n_steps)
        bucket = _rel_bucket(np.clip(step, 0, n_steps - 1) * dil)
        idx = np.where(band, bucket, -1)
        if dil == 1:
            idx = idx[np.ix_(perm, np.concatenate([perm, BLK + perm]))]
        maps.append(idx)
    return np.stack(maps).astype(np.int32)


def _block_pieces(dil, r, n):
    if dil == 1:
        return [(s, slice(n * PIECE, (n + 1) * PIECE), PIECE) for s in range(SLABS)]
    if dil == SLABS:
        return [(r, slice(n * BLK, (n + 1) * BLK), BLK)]
    sub = dil // SLABS
    s, t = r % SLABS, r // SLABS
    return [(s, pl.ds(t + sub * BLK * n, BLK, stride=sub), BLK)]


def _attn_body(bias_s, q_ref, k_ref, v_ref, o_ref, kh_ref, vh_ref, slab_s, acc_s, m_s, l_s, *,
               seq, group_size):
    for part, ref in enumerate((q_ref, k_ref, v_ref)):
        for s in range(SLABS):
            slab_s[part, s] = ref[pl.ds(s, seq // SLABS, stride=SLABS), :]
    q_s, k_s, v_s = ([slab_s.at[part, s] for s in range(SLABS)] for part in range(3))

    def copy_by_head(c, n_chunks):
        n_rows = seq // n_chunks
        src = pl.ds(c * n_rows, n_rows)
        dst = pl.ds(pl.program_id(1) + H_A * c * n_rows, n_rows, stride=H_A)
        kh_ref[0, dst, :] = k_ref[src, :]
        vh_ref[0, dst, :] = v_ref[src, :]

    def load(slabs, pieces):
        parts = [slabs[s][idx, :] for s, idx, _ in pieces]
        return parts[0] if len(parts) == 1 else jnp.concatenate(parts, axis=0)

    def gather(slabs, blocks):
        return jnp.stack([load(slabs, pieces) for pieces in blocks]).astype(BF16)

    def qk(q, k):
        return jnp.einsum("gid,gjd->gij", q, k, preferred_element_type=F32)

    def pv(p, v):
        return jnp.einsum("gij,gjd->gid", p.astype(BF16), v, preferred_element_type=F32)

    def group(di, curs, prevs):
        q = gather(q_s, curs)
        s_c = qk(q, gather(k_s, curs)) * SCALE + bias_s[0, di, :, BLK:]
        if prevs is None:
            m = jnp.max(s_c, axis=-1, keepdims=True)
            p_c = jnp.exp(s_c - m)
            den = jnp.sum(p_c, axis=-1, keepdims=True)
            o = pv(p_c, gather(v_s, curs))
        else:
            s_p = qk(q, gather(k_s, prevs)) * SCALE + bias_s[0, di, :, :BLK]
            m = jnp.max(jnp.maximum(s_c, s_p), axis=-1, keepdims=True)
            p_c = jnp.exp(s_c - m)
            p_p = jnp.exp(s_p - m)
            den = jnp.sum(p_c + p_p, axis=-1, keepdims=True)
            o = pv(p_c, gather(v_s, curs)) + pv(p_p, gather(v_s, prevs))
        for g, pieces in enumerate(curs):
            m_g = jnp.broadcast_to(m[g], (BLK, HEAD_DIM))
            l_g = jnp.broadcast_to(den[g], (BLK, HEAD_DIM))
            lo = 0
            for s, idx, n_rows in pieces:
                acc_s[di, s, idx, :] = o[g, lo:lo + n_rows]
                m_s[di, s, idx, :] = m_g[lo:lo + n_rows]
                l_s[di, s, idx, :] = l_g[lo:lo + n_rows]
                lo += n_rows

    groups = []
    for di, (win, dil) in enumerate(DILATED):
        n_blocks = seq // (dil * BLK)
        first = [_block_pieces(dil, r, 0) for r in range(dil)]
        later = [(r, n) for r in range(dil) for n in range(1, n_blocks)]
        for lo in range(0, len(first), group_size):
            groups.append((di, first[lo:lo + group_size], None))
        for lo in range(0, len(later), group_size):
            part = later[lo:lo + group_size]
            groups.append((di, [_block_pieces(dil, r, n) for r, n in part],
                           [_block_pieces(dil, r, n - 1) for r, n in part]))
    n_chunks = 1 << (len(groups).bit_length() - 1)
    for c, args in enumerate(groups):
        group(*args)
        if c < n_chunks:
            copy_by_head(c, n_chunks)

    rows_per = 256
    per_slab = seq // SLABS // rows_per

    def merge(c, carry):
        s = c // per_slab
        r0 = pl.multiple_of((c % per_slab) * rows_per, rows_per)
        rc = pl.ds(r0, rows_per)
        ms = [m_s[di, s, rc, :] for di in range(len(DILATED))]
        mx = jnp.maximum(jnp.maximum(ms[0], ms[1]), ms[2])
        num = 0.0
        den = 0.0
        for di in range(len(DILATED)):
            wgt = jnp.exp(ms[di] - mx)
            num = num + wgt * acc_s[di, s, rc, :]
            den = den + wgt * l_s[di, s, rc, :]
        o_ref[pl.ds(SLABS * r0 + s, rows_per, stride=SLABS), :] = num / den
        return carry

    lax.fori_loop(0, SLABS * per_slab, merge, 0)


def _attn_prompt(z, rel_bias, *, batch, seq):
    n = batch * seq
    assert all(seq % (dil * BLK) == 0 for _, dil in DILATED)
    idx = _prompt_bias_index()
    n_d = idx.shape[0]
    bias = _bias_lookup(jnp.asarray(idx.reshape(n_d * BLK, 2 * BLK)), rel_bias.T[:, None, :],
                        n_col_tiles=1)
    bias = bias.reshape(H_A, n_d, BLK, 2 * BLK)
    q_spec, k_spec, v_spec = (pl.BlockSpec((seq, HEAD_DIM), lambda b, h, part=part: (b, part * H_A + h))
                              for part in range(3))
    by_head = pl.BlockSpec((1, seq * H_A, HEAD_DIM), lambda b, h: (b, 0, 0))
    by_head_shape = jax.ShapeDtypeStruct((batch, seq * H_A, HEAD_DIM), F32)
    stat = pltpu.VMEM((n_d, SLABS, seq // SLABS, HEAD_DIM), F32)
    oa, kh, vh = pl.pallas_call(
        functools.partial(_attn_body, seq=seq, group_size=8),
        out_shape=[jax.ShapeDtypeStruct((n, W_A), F32), by_head_shape, by_head_shape],
        grid=(batch, H_A),
        in_specs=[
            pl.BlockSpec((1, n_d, BLK, 2 * BLK), lambda b, h: (h, 0, 0, 0)),
            q_spec, k_spec, v_spec,
        ],
        out_specs=[pl.BlockSpec((seq, HEAD_DIM), lambda b, h: (b, h)), by_head, by_head],
        scratch_shapes=[stat, stat, stat, stat],
        compiler_params=_params("arbitrary", "arbitrary"),
        name="attn_prompt",
    )(bias, z, z, z)
    heads = (batch, seq, H_A, HEAD_DIM)
    return oa, kh.reshape(heads), vh.reshape(heads)


def _sgu_body(u_ref, v_ref, w_ref, bt_ref, o_ref, *, n_chunks):
    tri = (lax.broadcasted_iota(jnp.int32, (CHUNK, CHUNK), 0)
           >= lax.broadcasted_iota(jnp.int32, (CHUNK, CHUNK), 1))
    for g in range(H_B):
        w = jnp.where(tri, w_ref[g], 0.0).astype(BF16)
        bias = bt_ref[:, g:g + 1]
        cols = slice(g * C_B, (g + 1) * C_B)
        for c in range(n_chunks):
            rws = slice(c * CHUNK, (c + 1) * CHUNK)
            mixed = _dot(w, v_ref[rws, cols].astype(BF16)) + bias
            o_ref[rws, cols] = u_ref[rws, cols] * mixed


def _sgu_prompt(z, u_blk, v_blk, w_s, b_s, *, tm):
    n = z.shape[0]
    return pl.pallas_call(
        functools.partial(_sgu_body, n_chunks=tm // CHUNK),
        out_shape=jax.ShapeDtypeStruct((n, W_B), F32),
        grid=(n // tm,),
        in_specs=[pl.BlockSpec((tm, W_B), lambda i: (i, u_blk)),
                  pl.BlockSpec((tm, W_B), lambda i: (i, v_blk)),
                  pl.BlockSpec((H_B, CHUNK, CHUNK), lambda i: (0, 0, 0)),
                  pl.BlockSpec((CHUNK, H_B), lambda i: (0, 0))],
        out_specs=pl.BlockSpec((tm, W_B), lambda i: (i, 0)),
        compiler_params=_params("parallel"),
        name="sgu_prompt",
    )(z, z, w_s, b_s.T)


def _mem_body(q_ref, k_ref, v_ref, o_ref):
    for hh in range(H_M):
        sl = slice(hh * HEAD_DIM, (hh + 1) * HEAD_DIM)
        s = _dot_nt(q_ref[0, :, sl].astype(BF16), k_ref[0, :, sl].astype(BF16)) * SCALE
        m = jnp.max(s, axis=-1, keepdims=True)
        p = jnp.exp(s - m)
        den = jnp.sum(p, axis=-1, keepdims=True)
        o_ref[0, :, sl] = _dot(p.astype(BF16), v_ref[0, :, sl].astype(BF16)) / den


def _mem_attend(q, k, v, *, tq, q_blk, k_blk, v_blk):
    batch, t, _ = q.shape
    return pl.pallas_call(
        _mem_body,
        out_shape=jax.ShapeDtypeStruct((batch, t, W_M), F32),
        grid=(batch, t // tq),
        in_specs=[pl.BlockSpec((1, tq, W_M), lambda b, i: (b, i, q_blk)),
                  pl.BlockSpec((1, N_MEM, W_M), lambda b, i: (b, 0, k_blk)),
                  pl.BlockSpec((1, N_MEM, W_M), lambda b, i: (b, 0, v_blk))],
        out_specs=pl.BlockSpec((1, tq, W_M), lambda b, i: (b, i, 0)),
        compiler_params=_params("parallel", "parallel"),
        name="mem_attend",
    )(q, k, v)


def _mems_body(q_ref, k_ref, v_ref, o_ref):
    for b in range(q_ref.shape[0]):
        s = _dot_nt(q_ref[b].astype(BF16), k_ref[b].astype(BF16)) * SCALE
        row = lax.broadcasted_iota(jnp.int32, s.shape, 0)
        col = lax.broadcasted_iota(jnp.int32, s.shape, 1)
        s = jnp.where(row % H_M == col % H_M, s, NEG)
        m = jnp.max(s, axis=-1, keepdims=True)
        p = jnp.exp(s - m)
        den = jnp.sum(p, axis=-1, keepdims=True)
        o_ref[b] = _dot(p.astype(BF16), v_ref[b].astype(BF16)) / den


def _mem_attend_sample(q, cache_k, cache_v, *, tb):
    batch, n_new = q.shape[:2]
    n_rows = n_new * H_M
    q_spec = pl.BlockSpec((tb, n_rows, HEAD_DIM), lambda b: (b, 0, 0))
    kv_spec = pl.BlockSpec((tb, N_MEM * H_M, HEAD_DIM), lambda b: (b, 0, 0))
    out = pl.pallas_call(
        _mems_body,
        out_shape=jax.ShapeDtypeStruct((batch, n_rows, HEAD_DIM), F32),
        grid=(batch // tb,),
        in_specs=[q_spec, kv_spec, kv_spec],
        out_specs=q_spec,
        compiler_params=_params("parallel"),
        name="mem_attend_sample",
    )(q.reshape(batch, n_rows, HEAD_DIM), cache_k.reshape(batch, N_MEM * H_M, HEAD_DIM),
      cache_v.reshape(batch, N_MEM * H_M, HEAD_DIM))
    return out.reshape(batch, n_new, W_M)


def _outproj_body(*refs, n_groups):
    src_refs = [refs[4 * k:4 * k + 3] for k in range(n_groups)]
    h_refs = [refs[4 * k + 3] for k in range(n_groups)]
    g_ref, w_ref = refs[4 * n_groups:4 * n_groups + 2]
    o_refs = refs[4 * n_groups + 2:5 * n_groups + 2]
    cat_ref = refs[-1]
    bounds = _bounds(h_refs)

    @pl.when(pl.program_id(1) == 0)
    def _():
        for srcs, (r0, r1) in zip(src_refs, bounds):
            lo = 0
            for src in srcs:
                hi = lo + src.shape[1]
                cat_ref[r0:r1, lo:hi] = _rms(src[...], g_ref[:, lo:hi]).astype(BF16)
                lo = hi

    res = _dot(cat_ref[...], w_ref[...].astype(BF16))
    for h_ref, o_ref, (r0, r1) in zip(h_refs, o_refs, bounds):
        o_ref[...] = h_ref[...] + res[r0:r1]


def _outproj(groups, g, w, *, n_tiles, tn):
    mix = W_A + W_B + W_M
    rows = _tile_rows([grp[3] for grp in groups], n_tiles)
    in_specs = []
    for r in rows:
        in_specs += [pl.BlockSpec((r, wd), lambda i, j: (i, 0)) for wd in (W_A, W_B, W_M)]
        in_specs += [pl.BlockSpec((r, tn), lambda i, j: (i, j))]
    in_specs += [pl.BlockSpec((1, mix), lambda i, j: (0, 0), pipeline_mode=pl.Buffered(1)),
                 pl.BlockSpec((mix, tn), lambda i, j: (0, j))]
    return pl.pallas_call(
        functools.partial(_outproj_body, n_groups=len(groups)),
        out_shape=[jax.ShapeDtypeStruct(grp[3].shape, F32) for grp in groups],
        grid=(n_tiles, D_MODEL // tn),
        in_specs=in_specs,
        out_specs=[pl.BlockSpec((r, tn), lambda i, j: (i, j)) for r in rows],
        scratch_shapes=[pltpu.VMEM((sum(rows), mix), BF16)],
        compiler_params=_params("arbitrary", "arbitrary"),
        name="outproj",
    )(*[a for grp in groups for a in grp], g.reshape(1, mix), w)


def _sample_bias_index(n_new, n_past):
    (win1, dil1), (tail, dil4), (win16, dil16) = DILATED
    assert dil1 == 1 and win1 == BLK and n_past == win16 and n_past % dil16 == 0
    assert n_past >= tail and n_new <= dil4 and n_new * H_A <= LANES
    n16 = n_past // dil16
    t = np.arange(n_new)[:, None]

    def per_head(index):
        same = np.eye(H_A, dtype=bool)[None, :, None, :]
        full = np.where(same, index[:, None, :, None], -1)
        return full.reshape(n_new * H_A, index.shape[1] * H_A)

    dist = BLK + t - np.arange(BLK)[None, :]
    d1 = np.where(dist <= win1, _rel_bucket(dist), -1)
    dist = tail + t - np.arange(tail)[None, :]
    d4 = np.where((dist % dil4 == 0) & (dist <= tail), _rel_bucket(dist), -1)
    m, tk = np.divmod(np.arange(n16 * n_new)[None, :], n_new)
    dist = n_past + t - dil16 * m - tk
    d16 = np.where(tk == t, _rel_bucket(np.maximum(dist, 0)), -1)
    s = np.arange(LANES // H_A)[None, :]
    dist = t - s
    new = [np.where((s < n_new) & (dist >= 0) & (dist % dil == 0),
                    _rel_bucket(np.maximum(dist, 0)), -1) for _, dil in DILATED]
    pieces = [d1, d4, d16] + new
    return np.concatenate([per_head(p) for p in pieces], axis=1).astype(np.int32)


def _sattn_body(bias_ref, q_ref, kn_ref, vn_ref, kt_ref, k16_ref, vt_ref, v16_ref, o_ref):
    def flat(ref):
        x = ref[0]
        return x.reshape(-1, HEAD_DIM).astype(BF16)

    q = q_ref[0].astype(BF16)
    kt, vt = flat(kt_ref), flat(vt_ref)
    k16, v16 = flat(k16_ref), flat(v16_ref)
    kn, vn = kn_ref[0].astype(BF16), vn_ref[0].astype(BF16)
    n_t, n_1 = kt.shape[0], BLK * H_A

    raw_t = _dot_nt(q, kt)
    raw_n = _dot_nt(q, kn)
    raws = [raw_t[:, n_t - n_1:], raw_t, _dot_nt(q, k16)] + [raw_n] * len(DILATED)
    logits = []
    lo = 0
    for raw in raws:
        hi = lo + raw.shape[1]
        logits.append(raw * SCALE + bias_ref[:, lo:hi])
        lo = hi

    m = functools.reduce(jnp.maximum, [jnp.max(x, axis=-1, keepdims=True) for x in logits])
    probs = [jnp.exp(x - m) for x in logits]
    den = functools.reduce(lambda a, b: a + b, [jnp.sum(p, axis=-1, keepdims=True) for p in probs])
    p_new = functools.reduce(lambda a, b: a + b, probs[3:])
    o = (_dot(probs[0].astype(BF16), vt[n_t - n_1:, :]) + _dot(probs[1].astype(BF16), vt)
         + _dot(probs[2].astype(BF16), v16) + _dot(p_new.astype(BF16), vn))
    o_ref[0] = o / den


def _attn_sample(qa, ka, va, cache_k, cache_v, rel_bias):
    batch, n_new = qa.shape[:2]
    n_past = cache_k.shape[1]
    tail = DILATED[1][0]
    dil16 = DILATED[2][1]
    n16 = n_past // dil16
    n_rows = n_new * H_A
    idx = _sample_bias_index(n_new, n_past)
    bias = _bias_lookup(jnp.asarray(idx), jnp.tile(rel_bias.T, (n_new, 1))[None], n_col_tiles=5)[0]

    def rows(x):
        x = x.reshape(batch, n_rows, HEAD_DIM)
        return jnp.pad(x, ((0, 0), (0, LANES - n_rows), (0, 0)))

    k_res = cache_k.reshape(batch, n16, dil16, H_A, HEAD_DIM)
    v_res = cache_v.reshape(batch, n16, dil16, H_A, HEAD_DIM)
    q_spec = pl.BlockSpec((1, n_rows, HEAD_DIM), lambda b: (b, 0, 0))
    new_spec = pl.BlockSpec((1, LANES, HEAD_DIM), lambda b: (b, 0, 0))
    tail_spec = pl.BlockSpec((1, tail, H_A, HEAD_DIM), lambda b: (b, n_past // tail - 1, 0, 0))
    res_spec = pl.BlockSpec((1, n16, n_new, H_A, HEAD_DIM), lambda b: (b, 0, 0, 0, 0))
    out = pl.pallas_call(
        _sattn_body,
        out_shape=jax.ShapeDtypeStruct((batch, n_rows, HEAD_DIM), F32),
        grid=(batch,),
        in_specs=[
            pl.BlockSpec(bias.shape, lambda b: (0, 0)),
            q_spec, new_spec, new_spec, tail_spec, res_spec, tail_spec, res_spec,
        ],
        out_specs=q_spec,
        compiler_params=_params("parallel"),
        name="attn_sample",
    )(bias, qa.reshape(batch, n_rows, HEAD_DIM), rows(ka), rows(va),
      cache_k, k_res, cache_v, v_res)
    return out.reshape(batch, n_new, W_A)


def _sgus_body(u_ref, v_ref, w_ref, b_ref, o_ref, *, n_new):
    t_idx = lax.broadcasted_iota(jnp.int32, (1, n_new, W_B), 1)
    mixed = 0.0
    for s in range(n_new):
        w_s = jnp.where(t_idx >= s, w_ref[s][None], 0.0).astype(BF16).astype(F32)
        mixed = mixed + w_s * v_ref[:, s:s + 1, :].astype(BF16).astype(F32)
    o_ref[...] = u_ref[...] * (mixed + b_ref[...][None])


def _sgu_sample(u, v, w_s, b_s):
    batch, n_new, _ = u.shape
    w_l = jnp.repeat(jnp.transpose(w_s[:, :n_new, :n_new], (2, 1, 0)), C_B, axis=-1)
    b_l = jnp.repeat(b_s[:, :n_new].T, C_B, axis=-1)
    return pl.pallas_call(
        functools.partial(_sgus_body, n_new=n_new),
        out_shape=jax.ShapeDtypeStruct(u.shape, F32),
        name="sgu_sample",
    )(u, v, w_l, b_l)


COL_KA, COL_VA, COL_U, COL_V, COL_QM = W_A, 2 * W_A, 3 * W_A, 3 * W_A + W_B, 3 * W_A + 2 * W_B


def _mixer_inputs(normed_groups, w_in, g_qa, g_ka, g_sgu, g_qm, *, n_tiles):
    gain = jnp.concatenate([jnp.tile(g_qa, H_A), jnp.tile(g_ka, H_A), jnp.ones((W_A + W_B,), F32),
                            g_sgu.reshape(W_B), jnp.tile(g_qm, H_M)])
    normed = [c // PROJ_TILE for lo, hi in ((0, COL_VA), (COL_V, COL_QM + W_M))
              for c in range(lo, hi, PROJ_TILE)]
    return _proj(normed_groups, None, w_in, gain, normed, n_tiles=n_tiles, chunk=512)


def kernel(x_prompt, x_sample, cache_win_k, cache_win_v, cache_mem_k, cache_mem_v, mem_prompt, rel_bias, g_ffn1, w1_gate, w1_up, w1_down, g_mix, w_in, g_qa, g_ka, g_sgu, w_sgu, b_sgu, g_qm, g_mem, w_mem_kv, g_km, g_mix_out, w_out, g_ffn2, w2_gate, w2_up, w2_down):
    batch, seq, _ = x_prompt.shape
    dec_batch, n_new, _ = x_sample.shape
    depth = g_ffn1.shape[0]
    n_p = batch * seq
    n_s = dec_batch * n_new

    y_p = x_prompt.reshape(n_p, D_MODEL)
    y_s = x_sample.reshape(n_s, D_MODEL)
    outs = [[] for _ in range(7)]
    for l in range(depth):
        h_p, h_s, hn_p, hn_s = _ffn([y_p, y_s], g_ffn1[l], w1_gate[l], w1_up[l], w1_down[l],
                                    g_next=g_mix[l], n_tiles=8, tf=256)
        z_p, z_s = _mixer_inputs([hn_p, hn_s], w_in[l], g_qa[l], g_ka[l], g_sgu[l], g_qm[l], n_tiles=4)

        oa_p, k_p, v_p = _attn_prompt(z_p, rel_bias, batch=batch, seq=seq)
        ob_p = _sgu_prompt(z_p, COL_U // W_B, COL_V // W_B, w_sgu[l], b_sgu[l], tm=1024)
        mem = mem_prompt.reshape(batch * N_MEM, D_MODEL)
        gain_kv = jnp.concatenate([jnp.tile(g_km[l], H_M), jnp.ones((W_M,), F32)])
        mkv, = _proj([mem], g_mem[l], w_mem_kv[l], gain_kv, range(W_M // PROJ_TILE),
                     n_tiles=2, chunk=512)
        mkv3 = mkv.reshape(batch, N_MEM, 2 * W_M)
        om_p = _mem_attend(z_p.reshape(batch, seq, -1), mkv3, mkv3, tq=1024,
                           q_blk=COL_QM // W_M, k_blk=0, v_blk=1)
        keep = min(DILATED[-1][0], seq)
        outs[0].append(k_p[:, seq - keep:])
        outs[1].append(v_p[:, seq - keep:])
        outs[2].append(mkv[:, :W_M].reshape(batch, N_MEM, H_M, HEAD_DIM))
        outs[3].append(mkv[:, W_M:].reshape(batch, N_MEM, H_M, HEAD_DIM))

        zq, zk, zv, zu, zg, zm = (z_s[:, lo:hi] for lo, hi in zip(
            (0, COL_KA, COL_VA, COL_U, COL_V, COL_QM), (COL_KA, COL_VA, COL_U, COL_V, COL_QM, z_s.shape[1])))
        heads = (dec_batch, n_new, H_A, HEAD_DIM)
        oa_s = _attn_sample(zq.reshape(heads), zk.reshape(heads), zv.reshape(heads),
                            cache_win_k[l], cache_win_v[l], rel_bias)
        ob_s = _sgu_sample(zu.reshape(dec_batch, n_new, W_B), zg.reshape(dec_batch, n_new, W_B),
                           w_sgu[l], b_sgu[l])
        om_s = _mem_attend_sample(zm.reshape(dec_batch, n_new, H_M, HEAD_DIM),
                                  cache_mem_k[l], cache_mem_v[l], tb=4)
        outs[4].append(zk.reshape(heads))
        outs[5].append(zv.reshape(heads))
        outs[6].append(zg.reshape(dec_batch, n_new, H_B, C_B))

        h_p, h_s = _outproj([(oa_p, ob_p, om_p.reshape(n_p, W_M), h_p),
                             (oa_s.reshape(n_s, W_A), ob_s.reshape(n_s, W_B), om_s.reshape(n_s, W_M), h_s)],
                            g_mix_out[l], w_out[l], n_tiles=4, tn=256)
        y_p, y_s = _ffn([h_p, h_s], g_ffn2[l], w2_gate[l], w2_up[l], w2_down[l], n_tiles=8, tf=256)

    return (y_p.reshape(batch, seq, D_MODEL), y_s.reshape(dec_batch, n_new, D_MODEL),
            *[jnp.stack(o) for o in outs])
```

```python
import functools

import numpy as np
import jax
import jax.numpy as jnp
from jax import lax
from jax.experimental import pallas as pl
from jax.experimental.pallas import tpu as pltpu

D_MODEL = 2048
HEAD_DIM = 128
H_A = 8
H_B = 4
C_B = 128
H_M = 4
W_A = H_A * HEAD_DIM
W_B = H_B * C_B
W_M = H_M * HEAD_DIM
D_FF = 5632
DILATED = ((128, 1), (512, 4), (2048, 16))
BLK = 128
CHUNK = 128
N_MEM = 256
N_BUCKETS = 32
REL_MAX_DIST = 2048
EPS = 1e-6
SCALE = HEAD_DIM ** -0.5
NEG = -1e30

LANES = 128
BF16_ROWS = 16
PROJ_TILE = 4 * HEAD_DIM
VMEM_LIMIT = 60 * 1024 * 1024

F32 = jnp.float32
BF16 = jnp.bfloat16


def _rel_bucket(dist):
    dist = np.asarray(dist, np.int64)
    max_exact = N_BUCKETS // 2
    large = max_exact + (np.log(np.maximum(dist, 1) / max_exact) / np.log(REL_MAX_DIST / max_exact)
                         * (N_BUCKETS - max_exact)).astype(np.int64)
    large = np.minimum(large, N_BUCKETS - 1)
    return np.where(dist < max_exact, dist, large).astype(np.int32)


def _rms(t, g):
    return t * lax.rsqrt(jnp.mean(t * t, axis=-1, keepdims=True) + EPS) * g


def _dot(a, b):
    return jnp.dot(a, b, preferred_element_type=F32)


def _dot_nt(a, b):
    return lax.dot_general(a, b, (((1,), (1,)), ((), ())), preferred_element_type=F32)


def _params(*sem):
    return pltpu.CompilerParams(dimension_semantics=sem, vmem_limit_bytes=VMEM_LIMIT)


def _tile_rows(groups, n_tiles):
    rows = [x.shape[0] // n_tiles for x in groups]
    assert all(r * n_tiles == x.shape[0] and r % BF16_ROWS == 0 for r, x in zip(rows, groups))
    return rows


def _bounds(refs):
    out, lo = [], 0
    for ref in refs:
        out.append((lo, lo + ref.shape[0]))
        lo += ref.shape[0]
    return out


def _ffn_body(*refs, n_groups, with_next_norm):
    x_refs, refs = refs[:n_groups], refs[n_groups:]
    gn_ref = refs[0] if with_next_norm else None
    g_ref, wg_ref, wu_ref, wd_ref = refs[with_next_norm:with_next_norm + 4]
    refs = refs[with_next_norm + 4:]
    o_refs, on_refs, xn_ref = refs[:n_groups], refs[n_groups:-1], refs[-1]
    bounds = _bounds(x_refs)
    j = pl.program_id(1)

    @pl.when(j == 0)
    def _():
        for x_ref, o_ref, (lo, hi) in zip(x_refs, o_refs, bounds):
            x = x_ref[...]
            xn_ref[lo:hi, :] = _rms(x, g_ref[...]).astype(BF16)
            o_ref[...] = x

    xn = xn_ref[...]
    gate = _dot(xn, wg_ref[...].astype(BF16))
    up = _dot(xn, wu_ref[...].astype(BF16))
    act = (gate * jax.nn.sigmoid(gate)) * up * 0.5
    res = _dot(act.astype(BF16), wd_ref[...].astype(BF16))
    for o_ref, (lo, hi) in zip(o_refs, bounds):
        o_ref[...] += res[lo:hi]

    if with_next_norm:
        @pl.when(j == pl.num_programs(1) - 1)
        def _():
            for o_ref, on_ref in zip(o_refs, on_refs):
                on_ref[...] = _rms(o_ref[...], gn_ref[...]).astype(BF16)


def _ffn(groups, g, wg, wu, wd, *, g_next=None, n_tiles, tf):
    rows = _tile_rows(groups, n_tiles)
    with_next_norm = g_next is not None
    gain_spec = pl.BlockSpec((1, D_MODEL), lambda i, j: (0, 0), pipeline_mode=pl.Buffered(1))
    row_specs = [pl.BlockSpec((r, D_MODEL), lambda i, j: (i, 0)) for r in rows]
    out_shape = [jax.ShapeDtypeStruct(x.shape, F32) for x in groups]
    gains = [g.reshape(1, D_MODEL)]
    if with_next_norm:
        out_shape += [jax.ShapeDtypeStruct(x.shape, BF16) for x in groups]
        gains.insert(0, g_next.reshape(1, D_MODEL))
    return pl.pallas_call(
        functools.partial(_ffn_body, n_groups=len(groups), with_next_norm=with_next_norm),
        out_shape=out_shape,
        grid=(n_tiles, D_FF // tf),
        in_specs=(
            row_specs + [gain_spec] * len(gains)
            + [pl.BlockSpec((D_MODEL, tf), lambda i, j: (0, j)),
               pl.BlockSpec((D_MODEL, tf), lambda i, j: (0, j)),
               pl.BlockSpec((tf, D_MODEL), lambda i, j: (j, 0))]),
        out_specs=row_specs * (2 if with_next_norm else 1),
        scratch_shapes=[pltpu.VMEM((sum(rows), D_MODEL), BF16)],
        compiler_params=_params("arbitrary", "arbitrary"),
        name="ffn",
    )(*groups, *gains, wg, wu, wd)


def _proj_body(*refs, n_groups, pre_normed, norm_tiles, chunk):
    x_refs, refs = refs[:n_groups], refs[n_groups:]
    g_ref = None if pre_normed else refs[0]
    w_ref, gain_ref = refs[1 - pre_normed:3 - pre_normed]
    o_refs, xn_ref = refs[3 - pre_normed:-1], refs[-1]
    bounds = _bounds(x_refs)
    j = pl.program_id(1)

    @pl.when(j == 0)
    def _():
        for x_ref, (lo, hi) in zip(x_refs, bounds):
            x = x_ref[...]
            xn_ref[lo:hi, :] = x if pre_normed else _rms(x, g_ref[...]).astype(BF16)

    w = w_ref[...].astype(BF16)
    gain = gain_ref[...]
    normed = functools.reduce(jnp.logical_or, [j == t for t in norm_tiles])

    def finish(z):
        heads = []
        for hh in range(PROJ_TILE // HEAD_DIM):
            sl = slice(hh * HEAD_DIM, (hh + 1) * HEAD_DIM)
            zz = z[:, sl]
            scale = lax.rsqrt(jnp.mean(zz * zz, axis=-1, keepdims=True) + EPS)
            heads.append(zz * jnp.where(normed, scale, 1.0) * gain[:, sl])
        return jnp.concatenate(heads, axis=1)

    total = bounds[-1][1]
    edges = list(range(0, total, chunk)) + [total]
    if len(edges) > 2 and edges[-1] - edges[-2] < chunk // 2:
        del edges[-2]
    for c0, c1 in zip(edges[:-1], edges[1:]):
        z = finish(_dot(xn_ref[c0:c1, :], w))
        for o_ref, (lo, hi) in zip(o_refs, bounds):
            a, b = max(lo, c0), min(hi, c1)
            if a < b:
                o_ref[a - lo:b - lo, :] = z[a - c0:b - c0, :]


def _proj(groups, g, w, gain, norm_tiles, *, n_tiles, chunk):
    d_in, d_out = w.shape
    rows = _tile_rows(groups, n_tiles)
    pre_normed = g is None
    assert all(x.dtype == (BF16 if pre_normed else F32) for x in groups)
    once = pl.Buffered(1)
    norm_args = [] if pre_normed else [g.reshape(1, d_in)]
    return pl.pallas_call(
        functools.partial(_proj_body, n_groups=len(groups), pre_normed=pre_normed,
                          norm_tiles=tuple(norm_tiles), chunk=chunk),
        out_shape=[jax.ShapeDtypeStruct((x.shape[0], d_out), F32) for x in groups],
        grid=(n_tiles, d_out // PROJ_TILE),
        in_specs=(
            [pl.BlockSpec((r, d_in), lambda i, j: (i, 0)) for r in rows]
            + [pl.BlockSpec((1, d_in), lambda i, j: (0, 0), pipeline_mode=once)] * len(norm_args)
            + [pl.BlockSpec((d_in, PROJ_TILE), lambda i, j: (0, j)),
               pl.BlockSpec((1, PROJ_TILE), lambda i, j: (0, j))]),
        out_specs=[pl.BlockSpec((r, PROJ_TILE), lambda i, j: (i, j)) for r in rows],
        scratch_shapes=[pltpu.VMEM((sum(rows), d_in), BF16)],
        compiler_params=_params("arbitrary", "arbitrary"),
        name="proj",
    )(*groups, *norm_args, w, gain.reshape(1, d_out))


def _bias_body(idx_ref, tab_ref, o_ref):
    idx = idx_ref[...]
    tab = tab_ref[0]
    bias = jnp.full(idx.shape, NEG, F32)
    for bkt in range(N_BUCKETS):
        bias = jnp.where(idx == bkt, tab[:, bkt:bkt + 1], bias)
    o_ref[0] = bias


def _bias_lookup(idx, tab, *, n_col_tiles):
    n_rows, n_cols = idx.shape
    n_g, tab_rows, _ = tab.shape
    tc = n_cols // n_col_tiles
    assert tc * n_col_tiles == n_cols and tc % LANES == 0 and tab_rows in (1, n_rows)
    return pl.pallas_call(
        _bias_body,
        out_shape=jax.ShapeDtypeStruct((n_g, n_rows, n_cols), F32),
        grid=(n_g, n_col_tiles),
        in_specs=[pl.BlockSpec((n_rows, tc), lambda g, j: (0, j)),
                  pl.BlockSpec((1, tab_rows, N_BUCKETS), lambda g, j: (g, 0, 0))],
        out_specs=pl.BlockSpec((1, n_rows, tc), lambda g, j: (g, 0, j)),
        compiler_params=_params("parallel", "parallel"),
        name="bias_lookup",
    )(idx, tab)


SLABS = DILATED[1][1]
PIECE = BLK // SLABS


def _prompt_bias_index():
    step = np.arange(BLK)[:, None] + BLK - np.arange(2 * BLK)[None, :]
    perm = (SLABS * np.arange(PIECE)[None, :] + np.arange(SLABS)[:, None]).reshape(BLK)
    maps = []
    for win, dil in DILATED:
        n_steps = win // dil + 1
        band = (step >= 0) & (step < n_steps)
        bucket = _rel_bucket(np.clip(step, 0, n_steps - 1) * dil)
        idx = np.where(band, bucket, -1)
        if dil == 1:
            idx = idx[np.ix_(perm, np.concatenate([perm, BLK + perm]))]
        maps.append(idx)
    return np.stack(maps).astype(np.int32)


def _block_pieces(dil, r, n):
    if dil == 1:
        return [(s, slice(n * PIECE, (n + 1) * PIECE), PIECE) for s in range(SLABS)]
    if dil == SLABS:
        return [(r, slice(n * BLK, (n + 1) * BLK), BLK)]
    sub = dil // SLABS
    s, t = r % SLABS, r // SLABS
    return [(s, pl.ds(t + sub * BLK * n, BLK, stride=sub), BLK)]


def _attn_body(bias_s, q_ref, k_ref, v_ref, o_ref, kh_ref, vh_ref, slab_s, acc_s, m_s, l_s, *,
               seq, group_size):
    for part, ref in enumerate((q_ref, k_ref, v_ref)):
        for s in range(SLABS):
            slab_s[part, s] = ref[pl.ds(s, seq // SLABS, stride=SLABS), :]
    q_s, k_s, v_s = ([slab_s.at[part, s] for s in range(SLABS)] for part in range(3))

    def copy_by_head(c, n_chunks):
        n_rows = seq // n_chunks
        src = pl.ds(c * n_rows, n_rows)
        dst = pl.ds(pl.program_id(1) + H_A * c * n_rows, n_rows, stride=H_A)
        kh_ref[0, dst, :] = k_ref[src, :]
        vh_ref[0, dst, :] = v_ref[src, :]

    def load(slabs, pieces):
        parts = [slabs[s][idx, :] for s, idx, _ in pieces]
        return parts[0] if len(parts) == 1 else jnp.concatenate(parts, axis=0)

    def gather(slabs, blocks):
        return jnp.stack([load(slabs, pieces) for pieces in blocks]).astype(BF16)

    def qk(q, k):
        return jnp.einsum("gid,gjd->gij", q, k, preferred_element_type=F32)

    def pv(p, v):
        return jnp.einsum("gij,gjd->gid", p.astype(BF16), v, preferred_element_type=F32)

    def bias_of(dis, cols):
        return jnp.stack([bias_s[0, di, :, cols] for di in dis])

    def group(dis, curs, prevs):
        q = gather(q_s, curs)
        s_c = qk(q, gather(k_s, curs)) * SCALE + bias_of(dis, slice(BLK, None))
        if prevs is None:
            m = jnp.max(s_c, axis=-1, keepdims=True)
            p_c = jnp.exp(s_c - m)
            den = jnp.sum(p_c, axis=-1, keepdims=True)
            o = pv(p_c, gather(v_s, curs))
        else:
            s_p = qk(q, gather(k_s, prevs)) * SCALE + bias_of(dis, slice(0, BLK))
            m = jnp.max(jnp.maximum(s_c, s_p), axis=-1, keepdims=True)
            p_c = jnp.exp(s_c - m)
            p_p = jnp.exp(s_p - m)
            den = jnp.sum(p_c + p_p, axis=-1, keepdims=True)
            o = pv(p_c, gather(v_s, curs)) + pv(p_p, gather(v_s, prevs))
        for g, (di, pieces) in enumerate(zip(dis, curs)):
            m_g = jnp.broadcast_to(m[g], (BLK, HEAD_DIM))
            l_g = jnp.broadcast_to(den[g], (BLK, HEAD_DIM))
            lo = 0
            for s, idx, n_rows in pieces:
                acc_s[di, s, idx, :] = o[g, lo:lo + n_rows]
                m_s[di, s, idx, :] = m_g[lo:lo + n_rows]
                l_s[di, s, idx, :] = l_g[lo:lo + n_rows]
                lo += n_rows

    first, later = [], []
    for di, (win, dil) in enumerate(DILATED):
        for r in range(dil):
            first.append((di, dil, r, 0))
            later += [(di, dil, r, n) for n in range(1, seq // (dil * BLK))]
    groups = []
    for blocks, has_prev in ((first, False), (later, True)):
        for lo in range(0, len(blocks), group_size):
            part = blocks[lo:lo + group_size]
            groups.append(([di for di, _, _, _ in part],
                           [_block_pieces(dil, r, n) for _, dil, r, n in part],
                           [_block_pieces(dil, r, n - 1) for _, dil, r, n in part] if has_prev else None))
    n_chunks = 1 << (len(groups).bit_length() - 1)
    for c, args in enumerate(groups):
        group(*args)
        if c < n_chunks:
            copy_by_head(c, n_chunks)

    rows_per = 256
    per_slab = seq // SLABS // rows_per

    def merge(c, carry):
        s = c // per_slab
        r0 = pl.multiple_of((c % per_slab) * rows_per, rows_per)
        rc = pl.ds(r0, rows_per)
        ms = [m_s[di, s, rc, :] for di in range(len(DILATED))]
        mx = jnp.maximum(jnp.maximum(ms[0], ms[1]), ms[2])
        num = 0.0
        den = 0.0
        for di in range(len(DILATED)):
            wgt = jnp.exp(ms[di] - mx)
            num = num + wgt * acc_s[di, s, rc, :]
            den = den + wgt * l_s[di, s, rc, :]
        o_ref[pl.ds(SLABS * r0 + s, rows_per, stride=SLABS), :] = num / den
        return carry

    lax.fori_loop(0, SLABS * per_slab, merge, 0)


def _attn_prompt(z, rel_bias, *, batch, seq):
    n = batch * seq
    assert all(seq % (dil * BLK) == 0 for _, dil in DILATED)
    idx = _prompt_bias_index()
    n_d = idx.shape[0]
    bias = _bias_lookup(jnp.asarray(idx.reshape(n_d * BLK, 2 * BLK)), rel_bias.T[:, None, :],
                        n_col_tiles=1)
    bias = bias.reshape(H_A, n_d, BLK, 2 * BLK)
    q_spec, k_spec, v_spec = (pl.BlockSpec((seq, HEAD_DIM), lambda b, h, part=part: (b, part * H_A + h))
                              for part in range(3))
    by_head = pl.BlockSpec((1, seq * H_A, HEAD_DIM), lambda b, h: (b, 0, 0))
    by_head_shape = jax.ShapeDtypeStruct((batch, seq * H_A, HEAD_DIM), F32)
    stat = pltpu.VMEM((n_d, SLABS, seq // SLABS, HEAD_DIM), F32)
    oa, kh, vh = pl.pallas_call(
        functools.partial(_attn_body, seq=seq, group_size=32),
        out_shape=[jax.ShapeDtypeStruct((n, W_A), F32), by_head_shape, by_head_shape],
        grid=(batch, H_A),
        in_specs=[
            pl.BlockSpec((1, n_d, BLK, 2 * BLK), lambda b, h: (h, 0, 0, 0)),
            q_spec, k_spec, v_spec,
        ],
        out_specs=[pl.BlockSpec((seq, HEAD_DIM), lambda b, h: (b, h)), by_head, by_head],
        scratch_shapes=[stat, stat, stat, stat],
        compiler_params=_params("arbitrary", "arbitrary"),
        name="attn_prompt",
    )(bias, z, z, z)
    heads = (batch, seq, H_A, HEAD_DIM)
    return oa, kh.reshape(heads), vh.reshape(heads)


def _sgu_body(u_ref, v_ref, w_ref, bt_ref, o_ref, *, n_chunks):
    tri = (lax.broadcasted_iota(jnp.int32, (CHUNK, CHUNK), 0)
           >= lax.broadcasted_iota(jnp.int32, (CHUNK, CHUNK), 1))
    for g in range(H_B):
        w = jnp.where(tri, w_ref[g], 0.0).astype(BF16)
        bias = bt_ref[:, g:g + 1]
        cols = slice(g * C_B, (g + 1) * C_B)
        for c in range(n_chunks):
            rws = slice(c * CHUNK, (c + 1) * CHUNK)
            mixed = _dot(w, v_ref[rws, cols].astype(BF16)) + bias
            o_ref[rws, cols] = u_ref[rws, cols] * mixed


def _sgu_prompt(z, u_blk, v_blk, w_s, b_s, *, tm):
    n = z.shape[0]
    return pl.pallas_call(
        functools.partial(_sgu_body, n_chunks=tm // CHUNK),
        out_shape=jax.ShapeDtypeStruct((n, W_B), F32),
        grid=(n // tm,),
        in_specs=[pl.BlockSpec((tm, W_B), lambda i: (i, u_blk)),
                  pl.BlockSpec((tm, W_B), lambda i: (i, v_blk)),
                  pl.BlockSpec((H_B, CHUNK, CHUNK), lambda i: (0, 0, 0)),
                  pl.BlockSpec((CHUNK, H_B), lambda i: (0, 0))],
        out_specs=pl.BlockSpec((tm, W_B), lambda i: (i, 0)),
        compiler_params=_params("parallel"),
        name="sgu_prompt",
    )(z, z, w_s, b_s.T)


def _mem_body(q_ref, k_ref, v_ref, o_ref):
    for hh in range(H_M):
        sl = slice(hh * HEAD_DIM, (hh + 1) * HEAD_DIM)
        s = _dot_nt(q_ref[0, :, sl].astype(BF16), k_ref[0, :, sl].astype(BF16)) * SCALE
        m = jnp.max(s, axis=-1, keepdims=True)
        p = jnp.exp(s - m)
        den = jnp.sum(p, axis=-1, keepdims=True)
        o_ref[0, :, sl] = _dot(p.astype(BF16), v_ref[0, :, sl].astype(BF16)) / den


def _mem_attend(q, k, v, *, tq, q_blk, k_blk, v_blk):
    batch, t, _ = q.shape
    return pl.pallas_call(
        _mem_body,
        out_shape=jax.ShapeDtypeStruct((batch, t, W_M), F32),
        grid=(batch, t // tq),
        in_specs=[pl.BlockSpec((1, tq, W_M), lambda b, i: (b, i, q_blk)),
                  pl.BlockSpec((1, N_MEM, W_M), lambda b, i: (b, 0, k_blk)),
                  pl.BlockSpec((1, N_MEM, W_M), lambda b, i: (b, 0, v_blk))],
        out_specs=pl.BlockSpec((1, tq, W_M), lambda b, i: (b, i, 0)),
        compiler_params=_params("parallel", "parallel"),
        name="mem_attend",
    )(q, k, v)


def _mems_body(q_ref, k_ref, v_ref, o_ref):
    for b in range(q_ref.shape[0]):
        s = _dot_nt(q_ref[b].astype(BF16), k_ref[b].astype(BF16)) * SCALE
        row = lax.broadcasted_iota(jnp.int32, s.shape, 0)
        col = lax.broadcasted_iota(jnp.int32, s.shape, 1)
        s = jnp.where(row % H_M == col % H_M, s, NEG)
        m = jnp.max(s, axis=-1, keepdims=True)
        p = jnp.exp(s - m)
        den = jnp.sum(p, axis=-1, keepdims=True)
        o_ref[b] = _dot(p.astype(BF16), v_ref[b].astype(BF16)) / den


def _mem_attend_sample(q, cache_k, cache_v, *, tb):
    batch, n_new = q.shape[:2]
    n_rows = n_new * H_M
    q_spec = pl.BlockSpec((tb, n_rows, HEAD_DIM), lambda b: (b, 0, 0))
    kv_spec = pl.BlockSpec((tb, N_MEM * H_M, HEAD_DIM), lambda b: (b, 0, 0))
    out = pl.pallas_call(
        _mems_body,
        out_shape=jax.ShapeDtypeStruct((batch, n_rows, HEAD_DIM), F32),
        grid=(batch // tb,),
        in_specs=[q_spec, kv_spec, kv_spec],
        out_specs=q_spec,
        compiler_params=_params("parallel"),
        name="mem_attend_sample",
    )(q.reshape(batch, n_rows, HEAD_DIM), cache_k.reshape(batch, N_MEM * H_M, HEAD_DIM),
      cache_v.reshape(batch, N_MEM * H_M, HEAD_DIM))
    return out.reshape(batch, n_new, W_M)


def _outproj_body(*refs, n_groups):
    src_refs = [refs[4 * k:4 * k + 3] for k in range(n_groups)]
    h_refs = [refs[4 * k + 3] for k in range(n_groups)]
    g_ref, w_ref = refs[4 * n_groups:4 * n_groups + 2]
    o_refs = refs[4 * n_groups + 2:5 * n_groups + 2]
    cat_ref = refs[-1]
    bounds = _bounds(h_refs)

    @pl.when(pl.program_id(1) == 0)
    def _():
        for srcs, (r0, r1) in zip(src_refs, bounds):
            lo = 0
            for src in srcs:
                hi = lo + src.shape[1]
                cat_ref[r0:r1, lo:hi] = _rms(src[...], g_ref[:, lo:hi]).astype(BF16)
                lo = hi

    res = _dot(cat_ref[...], w_ref[...].astype(BF16))
    for h_ref, o_ref, (r0, r1) in zip(h_refs, o_refs, bounds):
        o_ref[...] = h_ref[...] + res[r0:r1]


def _outproj(groups, g, w, *, n_tiles, tn):
    mix = W_A + W_B + W_M
    rows = _tile_rows([grp[3] for grp in groups], n_tiles)
    in_specs = []
    for r in rows:
        in_specs += [pl.BlockSpec((r, wd), lambda i, j: (i, 0)) for wd in (W_A, W_B, W_M)]
        in_specs += [pl.BlockSpec((r, tn), lambda i, j: (i, j))]
    in_specs += [pl.BlockSpec((1, mix), lambda i, j: (0, 0), pipeline_mode=pl.Buffered(1)),
                 pl.BlockSpec((mix, tn), lambda i, j: (0, j))]
    return pl.pallas_call(
        functools.partial(_outproj_body, n_groups=len(groups)),
        out_shape=[jax.ShapeDtypeStruct(grp[3].shape, F32) for grp in groups],
        grid=(n_tiles, D_MODEL // tn),
        in_specs=in_specs,
        out_specs=[pl.BlockSpec((r, tn), lambda i, j: (i, j)) for r in rows],
        scratch_shapes=[pltpu.VMEM((sum(rows), mix), BF16)],
        compiler_params=_params("arbitrary", "arbitrary"),
        name="outproj",
    )(*[a for grp in groups for a in grp], g.reshape(1, mix), w)


def _sample_bias_index(n_new, n_past):
    (win1, dil1), (tail, dil4), (win16, dil16) = DILATED
    assert dil1 == 1 and win1 == BLK and n_past == win16 and n_past % dil16 == 0
    assert n_past >= tail and n_new <= dil4 and n_new * H_A <= LANES
    n16 = n_past // dil16
    t = np.arange(n_new)[:, None]

    def per_head(index):
        same = np.eye(H_A, dtype=bool)[None, :, None, :]
        full = np.where(same, index[:, None, :, None], -1)
        return full.reshape(n_new * H_A, index.shape[1] * H_A)

    dist = BLK + t - np.arange(BLK)[None, :]
    d1 = np.where(dist <= win1, _rel_bucket(dist), -1)
    dist = tail + t - np.arange(tail)[None, :]
    d4 = np.where((dist % dil4 == 0) & (dist <= tail), _rel_bucket(dist), -1)
    m, tk = np.divmod(np.arange(n16 * n_new)[None, :], n_new)
    dist = n_past + t - dil16 * m - tk
    d16 = np.where(tk == t, _rel_bucket(np.maximum(dist, 0)), -1)
    s = np.arange(LANES // H_A)[None, :]
    dist = t - s
    new = [np.where((s < n_new) & (dist >= 0) & (dist % dil == 0),
                    _rel_bucket(np.maximum(dist, 0)), -1) for _, dil in DILATED]
    pieces = [d1, d4, d16] + new
    return np.concatenate([per_head(p) for p in pieces], axis=1).astype(np.int32)


def _sattn_body(bias_ref, q_ref, kn_ref, vn_ref, kt_ref, k16_ref, vt_ref, v16_ref, o_ref):
    def flat(ref):
        x = ref[0]
        return x.reshape(-1, HEAD_DIM).astype(BF16)

    q = q_ref[0].astype(BF16)
    kt, vt = flat(kt_ref), flat(vt_ref)
    k16, v16 = flat(k16_ref), flat(v16_ref)
    kn, vn = kn_ref[0].astype(BF16), vn_ref[0].astype(BF16)
    n_t, n_1 = kt.shape[0], BLK * H_A

    raw_t = _dot_nt(q, kt)
    raw_n = _dot_nt(q, kn)
    raws = [raw_t[:, n_t - n_1:], raw_t, _dot_nt(q, k16)] + [raw_n] * len(DILATED)
    logits = []
    lo = 0
    for raw in raws:
        hi = lo + raw.shape[1]
        logits.append(raw * SCALE + bias_ref[:, lo:hi])
        lo = hi

    m = functools.reduce(jnp.maximum, [jnp.max(x, axis=-1, keepdims=True) for x in logits])
    probs = [jnp.exp(x - m) for x in logits]
    den = functools.reduce(lambda a, b: a + b, [jnp.sum(p, axis=-1, keepdims=True) for p in probs])
    p_new = functools.reduce(lambda a, b: a + b, probs[3:])
    o = (_dot(probs[0].astype(BF16), vt[n_t - n_1:, :]) + _dot(probs[1].astype(BF16), vt)
         + _dot(probs[2].astype(BF16), v16) + _dot(p_new.astype(BF16), vn))
    o_ref[0] = o / den


def _attn_sample(qa, ka, va, cache_k, cache_v, rel_bias):
    batch, n_new = qa.shape[:2]
    n_past = cache_k.shape[1]
    tail = DILATED[1][0]
    dil16 = DILATED[2][1]
    n16 = n_past // dil16
    n_rows = n_new * H_A
    idx = _sample_bias_index(n_new, n_past)
    bias = _bias_lookup(jnp.asarray(idx), jnp.tile(rel_bias.T, (n_new, 1))[None], n_col_tiles=5)[0]

    def rows(x):
        x = x.reshape(batch, n_rows, HEAD_DIM)
        return jnp.pad(x, ((0, 0), (0, LANES - n_rows), (0, 0)))

    k_res = cache_k.reshape(batch, n16, dil16, H_A, HEAD_DIM)
    v_res = cache_v.reshape(batch, n16, dil16, H_A, HEAD_DIM)
    q_spec = pl.BlockSpec((1, n_rows, HEAD_DIM), lambda b: (b, 0, 0))
    new_spec = pl.BlockSpec((1, LANES, HEAD_DIM), lambda b: (b, 0, 0))
    tail_spec = pl.BlockSpec((1, tail, H_A, HEAD_DIM), lambda b: (b, n_past // tail - 1, 0, 0))
    res_spec = pl.BlockSpec((1, n16, n_new, H_A, HEAD_DIM), lambda b: (b, 0, 0, 0, 0))
    out = pl.pallas_call(
        _sattn_body,
        out_shape=jax.ShapeDtypeStruct((batch, n_rows, HEAD_DIM), F32),
        grid=(batch,),
        in_specs=[
            pl.BlockSpec(bias.shape, lambda b: (0, 0)),
            q_spec, new_spec, new_spec, tail_spec, res_spec, tail_spec, res_spec,
        ],
        out_specs=q_spec,
        compiler_params=_params("parallel"),
        name="attn_sample",
    )(bias, qa.reshape(batch, n_rows, HEAD_DIM), rows(ka), rows(va),
      cache_k, k_res, cache_v, v_res)
    return out.reshape(batch, n_new, W_A)


def _sgus_body(u_ref, v_ref, w_ref, b_ref, o_ref, *, n_new):
    t_idx = lax.broadcasted_iota(jnp.int32, (1, n_new, W_B), 1)
    mixed = 0.0
    for s in range(n_new):
        w_s = jnp.where(t_idx >= s, w_ref[s][None], 0.0).astype(BF16).astype(F32)
        mixed = mixed + w_s * v_ref[:, s:s + 1, :].astype(BF16).astype(F32)
    o_ref[...] = u_ref[...] * (mixed + b_ref[...][None])


def _sgu_sample(u, v, w_s, b_s):
    batch, n_new, _ = u.shape
    w_l = jnp.repeat(jnp.transpose(w_s[:, :n_new, :n_new], (2, 1, 0)), C_B, axis=-1)
    b_l = jnp.repeat(b_s[:, :n_new].T, C_B, axis=-1)
    return pl.pallas_call(
        functools.partial(_sgus_body, n_new=n_new),
        out_shape=jax.ShapeDtypeStruct(u.shape, F32),
        name="sgu_sample",
    )(u, v, w_l, b_l)


COL_KA, COL_VA, COL_U, COL_V, COL_QM = W_A, 2 * W_A, 3 * W_A, 3 * W_A + W_B, 3 * W_A + 2 * W_B


def _mixer_inputs(normed_groups, w_in, g_qa, g_ka, g_sgu, g_qm, *, n_tiles):
    gain = jnp.concatenate([jnp.tile(g_qa, H_A), jnp.tile(g_ka, H_A), jnp.ones((W_A + W_B,), F32),
                            g_sgu.reshape(W_B), jnp.tile(g_qm, H_M)])
    normed = [c // PROJ_TILE for lo, hi in ((0, COL_VA), (COL_V, COL_QM + W_M))
              for c in range(lo, hi, PROJ_TILE)]
    return _proj(normed_groups, None, w_in, gain, normed, n_tiles=n_tiles, chunk=512)


def kernel(x_prompt, x_sample, cache_win_k, cache_win_v, cache_mem_k, cache_mem_v, mem_prompt, rel_bias, g_ffn1, w1_gate, w1_up, w1_down, g_mix, w_in, g_qa, g_ka, g_sgu, w_sgu, b_sgu, g_qm, g_mem, w_mem_kv, g_km, g_mix_out, w_out, g_ffn2, w2_gate, w2_up, w2_down):
    batch, seq, _ = x_prompt.shape
    dec_batch, n_new, _ = x_sample.shape
    depth = g_ffn1.shape[0]
    n_p = batch * seq
    n_s = dec_batch * n_new

    y_p = x_prompt.reshape(n_p, D_MODEL)
    y_s = x_sample.reshape(n_s, D_MODEL)
    outs = [[] for _ in range(7)]
    for l in range(depth):
        h_p, h_s, hn_p, hn_s = _ffn([y_p, y_s], g_ffn1[l], w1_gate[l], w1_up[l], w1_down[l],
                                    g_next=g_mix[l], n_tiles=8, tf=256)
        z_p, z_s = _mixer_inputs([hn_p, hn_s], w_in[l], g_qa[l], g_ka[l], g_sgu[l], g_qm[l], n_tiles=4)

        oa_p, k_p, v_p = _attn_prompt(z_p, rel_bias, batch=batch, seq=seq)
        ob_p = _sgu_prompt(z_p, COL_U // W_B, COL_V // W_B, w_sgu[l], b_sgu[l], tm=1024)
        mem = mem_prompt.reshape(batch * N_MEM, D_MODEL)
        gain_kv = jnp.concatenate([jnp.tile(g_km[l], H_M), jnp.ones((W_M,), F32)])
        mkv, = _proj([mem], g_mem[l], w_mem_kv[l], gain_kv, range(W_M // PROJ_TILE),
                     n_tiles=2, chunk=512)
        mkv3 = mkv.reshape(batch, N_MEM, 2 * W_M)
        om_p = _mem_attend(z_p.reshape(batch, seq, -1), mkv3, mkv3, tq=1024,
                           q_blk=COL_QM // W_M, k_blk=0, v_blk=1)
        keep = min(DILATED[-1][0], seq)
        outs[0].append(k_p[:, seq - keep:])
        outs[1].append(v_p[:, seq - keep:])
        outs[2].append(mkv[:, :W_M].reshape(batch, N_MEM, H_M, HEAD_DIM))
        outs[3].append(mkv[:, W_M:].reshape(batch, N_MEM, H_M, HEAD_DIM))

        zq, zk, zv, zu, zg, zm = (z_s[:, lo:hi] for lo, hi in zip(
            (0, COL_KA, COL_VA, COL_U, COL_V, COL_QM), (COL_KA, COL_VA, COL_U, COL_V, COL_QM, z_s.shape[1])))
        heads = (dec_batch, n_new, H_A, HEAD_DIM)
        oa_s = _attn_sample(zq.reshape(heads), zk.reshape(heads), zv.reshape(heads),
                            cache_win_k[l], cache_win_v[l], rel_bias)
        ob_s = _sgu_sample(zu.reshape(dec_batch, n_new, W_B), zg.reshape(dec_batch, n_new, W_B),
                           w_sgu[l], b_sgu[l])
        om_s = _mem_attend_sample(zm.reshape(dec_batch, n_new, H_M, HEAD_DIM),
                                  cache_mem_k[l], cache_mem_v[l], tb=4)
        outs[4].append(zk.reshape(heads))
        outs[5].append(zv.reshape(heads))
        outs[6].append(zg.reshape(dec_batch, n_new, H_B, C_B))

        h_p, h_s = _outproj([(oa_p, ob_p, om_p.reshape(n_p, W_M), h_p),
                             (oa_s.reshape(n_s, W_A), ob_s.reshape(n_s, W_B), om_s.reshape(n_s, W_M), h_s)],
                            g_mix_out[l], w_out[l], n_tiles=4, tn=256)
        y_p, y_s = _ffn([h_p, h_s], g_ffn2[l], w2_gate[l], w2_up[l], w2_down[l], n_tiles=8, tf=256)

    return (y_p.reshape(batch, seq, D_MODEL), y_s.reshape(dec_batch, n_new, D_MODEL),
            *[jnp.stack(o) for o in outs])
```

```python
import functools

import numpy as np
import jax
import jax.numpy as jnp
from jax import lax
from jax.experimental import pallas as pl
from jax.experimental.pallas import tpu as pltpu

D_MODEL = 2048
HEAD_DIM = 128
H_A = 8
H_B = 4
C_B = 128
H_M = 4
W_A = H_A * HEAD_DIM
W_B = H_B * C_B
W_M = H_M * HEAD_DIM
D_FF = 5632
DILATED = ((128, 1), (512, 4), (2048, 16))
BLK = 128
CHUNK = 128
N_MEM = 256
N_BUCKETS = 32
REL_MAX_DIST = 2048
EPS = 1e-6
SCALE = HEAD_DIM ** -0.5
NEG = -1e30

LANES = 128
BF16_ROWS = 16
PROJ_TILE = 4 * HEAD_DIM
VMEM_LIMIT = 60 * 1024 * 1024

F32 = jnp.float32
BF16 = jnp.bfloat16


def _rel_bucket(dist):
    dist = np.asarray(dist, np.int64)
    max_exact = N_BUCKETS // 2
    large = max_exact + (np.log(np.maximum(dist, 1) / max_exact) / np.log(REL_MAX_DIST / max_exact)
                         * (N_BUCKETS - max_exact)).astype(np.int64)
    large = np.minimum(large, N_BUCKETS - 1)
    return np.where(dist < max_exact, dist, large).astype(np.int32)


def _rms(t, g):
    return t * lax.rsqrt(jnp.mean(t * t, axis=-1, keepdims=True) + EPS) * g


def _dot(a, b):
    return jnp.dot(a, b, preferred_element_type=F32)


def _dot_nt(a, b):
    return lax.dot_general(a, b, (((1,), (1,)), ((), ())), preferred_element_type=F32)


def _params(*sem):
    return pltpu.CompilerParams(dimension_semantics=sem, vmem_limit_bytes=VMEM_LIMIT)


def _tile_rows(groups, n_tiles):
    rows = [x.shape[0] // n_tiles for x in groups]
    assert all(r * n_tiles == x.shape[0] and r % BF16_ROWS == 0 for r, x in zip(rows, groups))
    return rows


def _bounds(refs):
    out, lo = [], 0
    for ref in refs:
        out.append((lo, lo + ref.shape[0]))
        lo += ref.shape[0]
    return out


def _ffn_body(*refs, n_groups, with_next_norm):
    x_refs, refs = refs[:n_groups], refs[n_groups:]
    gn_ref = refs[0] if with_next_norm else None
    g_ref, wg_ref, wu_ref, wd_ref = refs[with_next_norm:with_next_norm + 4]
    refs = refs[with_next_norm + 4:]
    o_refs, on_refs, xn_ref = refs[:n_groups], refs[n_groups:-1], refs[-1]
    bounds = _bounds(x_refs)
    j = pl.program_id(1)

    @pl.when(j == 0)
    def _():
        for x_ref, o_ref, (lo, hi) in zip(x_refs, o_refs, bounds):
            x = x_ref[...]
            xn_ref[lo:hi, :] = _rms(x, g_ref[...]).astype(BF16)
            o_ref[...] = x

    xn = xn_ref[...]
    gate = _dot(xn, wg_ref[...].astype(BF16))
    up = _dot(xn, wu_ref[...].astype(BF16))
    act = (gate * jax.nn.sigmoid(gate)) * up * 0.5
    res = _dot(act.astype(BF16), wd_ref[...].astype(BF16))
    for o_ref, (lo, hi) in zip(o_refs, bounds):
        o_ref[...] += res[lo:hi]

    if with_next_norm:
        @pl.when(j == pl.num_programs(1) - 1)
        def _():
            for o_ref, on_ref in zip(o_refs, on_refs):
                on_ref[...] = _rms(o_ref[...], gn_ref[...]).astype(BF16)


def _ffn(groups, g, wg, wu, wd, *, g_next=None, n_tiles, tf):
    rows = _tile_rows(groups, n_tiles)
    with_next_norm = g_next is not None
    gain_spec = pl.BlockSpec((1, D_MODEL), lambda i, j: (0, 0), pipeline_mode=pl.Buffered(1))
    row_specs = [pl.BlockSpec((r, D_MODEL), lambda i, j: (i, 0)) for r in rows]
    out_shape = [jax.ShapeDtypeStruct(x.shape, F32) for x in groups]
    gains = [g.reshape(1, D_MODEL)]
    if with_next_norm:
        out_shape += [jax.ShapeDtypeStruct(x.shape, BF16) for x in groups]
        gains.insert(0, g_next.reshape(1, D_MODEL))
    return pl.pallas_call(
        functools.partial(_ffn_body, n_groups=len(groups), with_next_norm=with_next_norm),
        out_shape=out_shape,
        grid=(n_tiles, D_FF // tf),
        in_specs=(
            row_specs + [gain_spec] * len(gains)
            + [pl.BlockSpec((D_MODEL, tf), lambda i, j: (0, j)),
               pl.BlockSpec((D_MODEL, tf), lambda i, j: (0, j)),
               pl.BlockSpec((tf, D_MODEL), lambda i, j: (j, 0))]),
        out_specs=row_specs * (2 if with_next_norm else 1),
        scratch_shapes=[pltpu.VMEM((sum(rows), D_MODEL), BF16)],
        compiler_params=_params("arbitrary", "arbitrary"),
        name="ffn",
    )(*groups, *gains, wg, wu, wd)


def _proj_body(*refs, n_groups, pre_normed, norm_tiles, chunk):
    x_refs, refs = refs[:n_groups], refs[n_groups:]
    g_ref = None if pre_normed else refs[0]
    w_ref, gain_ref = refs[1 - pre_normed:3 - pre_normed]
    o_refs, xn_ref = refs[3 - pre_normed:-1], refs[-1]
    bounds = _bounds(x_refs)
    j = pl.program_id(1)

    @pl.when(j == 0)
    def _():
        for x_ref, (lo, hi) in zip(x_refs, bounds):
            x = x_ref[...]
            xn_ref[lo:hi, :] = x if pre_normed else _rms(x, g_ref[...]).astype(BF16)

    w = w_ref[...].astype(BF16)
    gain = gain_ref[...]
    normed = functools.reduce(jnp.logical_or, [j == t for t in norm_tiles])

    def finish(z):
        heads = []
        for hh in range(PROJ_TILE // HEAD_DIM):
            sl = slice(hh * HEAD_DIM, (hh + 1) * HEAD_DIM)
            zz = z[:, sl]
            scale = lax.rsqrt(jnp.mean(zz * zz, axis=-1, keepdims=True) + EPS)
            heads.append(zz * jnp.where(normed, scale, 1.0) * gain[:, sl])
        return jnp.concatenate(heads, axis=1)

    total = bounds[-1][1]
    edges = list(range(0, total, chunk)) + [total]
    if len(edges) > 2 and edges[-1] - edges[-2] < chunk // 2:
        del edges[-2]
    for c0, c1 in zip(edges[:-1], edges[1:]):
        z = finish(_dot(xn_ref[c0:c1, :], w))
        for o_ref, (lo, hi) in zip(o_refs, bounds):
            a, b = max(lo, c0), min(hi, c1)
            if a < b:
                o_ref[a - lo:b - lo, :] = z[a - c0:b - c0, :]


def _proj(groups, g, w, gain, norm_tiles, *, n_tiles, chunk):
    d_in, d_out = w.shape
    rows = _tile_rows(groups, n_tiles)
    pre_normed = g is None
    assert all(x.dtype == (BF16 if pre_normed else F32) for x in groups)
    once = pl.Buffered(1)
    norm_args = [] if pre_normed else [g.reshape(1, d_in)]
    return pl.pallas_call(
        functools.partial(_proj_body, n_groups=len(groups), pre_normed=pre_normed,
                          norm_tiles=tuple(norm_tiles), chunk=chunk),
        out_shape=[jax.ShapeDtypeStruct((x.shape[0], d_out), F32) for x in groups],
        grid=(n_tiles, d_out // PROJ_TILE),
        in_specs=(
            [pl.BlockSpec((r, d_in), lambda i, j: (i, 0)) for r in rows]
            + [pl.BlockSpec((1, d_in), lambda i, j: (0, 0), pipeline_mode=once)] * len(norm_args)
            + [pl.BlockSpec((d_in, PROJ_TILE), lambda i, j: (0, j)),
               pl.BlockSpec((1, PROJ_TILE), lambda i, j: (0, j))]),
        out_specs=[pl.BlockSpec((r, PROJ_TILE), lambda i, j: (i, j)) for r in rows],
        scratch_shapes=[pltpu.VMEM((sum(rows), d_in), BF16)],
        compiler_params=_params("arbitrary", "arbitrary"),
        name="proj",
    )(*groups, *norm_args, w, gain.reshape(1, d_out))


def _bias_body(idx_ref, tab_ref, o_ref):
    idx = idx_ref[...]
    tab = tab_ref[0]
    bias = jnp.full(idx.shape, NEG, F32)
    for bkt in range(N_BUCKETS):
        bias = jnp.where(idx == bkt, tab[:, bkt:bkt + 1], bias)
    o_ref[0] = bias


def _bias_lookup(idx, tab, *, n_col_tiles):
    n_rows, n_cols = idx.shape
    n_g, tab_rows, _ = tab.shape
    tc = n_cols // n_col_tiles
    assert tc * n_col_tiles == n_cols and tc % LANES == 0 and tab_rows in (1, n_rows)
    return pl.pallas_call(
        _bias_body,
        out_shape=jax.ShapeDtypeStruct((n_g, n_rows, n_cols), F32),
        grid=(n_g, n_col_tiles),
        in_specs=[pl.BlockSpec((n_rows, tc), lambda g, j: (0, j)),
                  pl.BlockSpec((1, tab_rows, N_BUCKETS), lambda g, j: (g, 0, 0))],
        out_specs=pl.BlockSpec((1, n_rows, tc), lambda g, j: (g, 0, j)),
        compiler_params=_params("parallel", "parallel"),
        name="bias_lookup",
    )(idx, tab)


SLABS = DILATED[1][1]
PIECE = BLK // SLABS


def _prompt_bias_index():
    step = np.arange(BLK)[:, None] + BLK - np.arange(2 * BLK)[None, :]
    perm = (SLABS * np.arange(PIECE)[None, :] + np.arange(SLABS)[:, None]).reshape(BLK)
    maps = []
    for win, dil in DILATED:
        n_steps = win // dil + 1
        band = (step >= 0) & (step < n_steps)
        bucket = _rel_bucket(np.clip(step, 0, n_steps - 1) * dil)
        idx = np.where(band, bucket, -1)
        if dil == 1:
            idx = idx[np.ix_(perm, np.concatenate([perm, BLK + perm]))]
        maps.append(idx)
    return np.stack(maps).astype(np.int32)


def _block_pieces(dil, r, n):
    if dil == 1:
        return [(s, slice(n * PIECE, (n + 1) * PIECE), PIECE) for s in range(SLABS)]
    if dil == SLABS:
        return [(r, slice(n * BLK, (n + 1) * BLK), BLK)]
    sub = dil // SLABS
    s, t = r % SLABS, r // SLABS
    return [(s, pl.ds(t + sub * BLK * n, BLK, stride=sub), BLK)]


def _attn_body(bias_s, q_ref, k_ref, v_ref, o_ref, kh_ref, vh_ref, slab_s, acc_s, m_s, l_s, *,
               seq, group_size):
    for part, ref in enumerate((q_ref, k_ref, v_ref)):
        for s in range(SLABS):
            slab_s[part, s] = ref[pl.ds(s, seq // SLABS, stride=SLABS), :]
    q_s, k_s, v_s = ([slab_s.at[part, s] for s in range(SLABS)] for part in range(3))

    def copy_by_head(c, n_chunks):
        n_rows = seq // n_chunks
        src = pl.ds(c * n_rows, n_rows)
        dst = pl.ds(pl.program_id(1) + H_A * c * n_rows, n_rows, stride=H_A)
        kh_ref[0, dst, :] = k_ref[src, :]
        vh_ref[0, dst, :] = v_ref[src, :]

    def load(slabs, pieces):
        parts = [slabs[s][idx, :] for s, idx, _ in pieces]
        return parts[0] if len(parts) == 1 else jnp.concatenate(parts, axis=0)

    def gather(slabs, blocks):
        return jnp.stack([load(slabs, pieces) for pieces in blocks]).astype(BF16)

    def qk(q, k):
        return jnp.einsum("gid,gjd->gij", q, k, preferred_element_type=F32)

    def pv(p, v):
        return jnp.einsum("gij,gjd->gid", p.astype(BF16), v, preferred_element_type=F32)

    def bias_of(dis, cols):
        return jnp.stack([bias_s[0, di, :, cols] for di in dis])

    def group(dis, curs, prevs):
        q = gather(q_s, curs)
        s_c = qk(q, gather(k_s, curs)) * SCALE + bias_of(dis, slice(BLK, None))
        if prevs is None:
            m = jnp.max(s_c, axis=-1, keepdims=True)
            p_c = jnp.exp(s_c - m)
            den = jnp.sum(p_c, axis=-1, keepdims=True)
            o = pv(p_c, gather(v_s, curs))
        else:
            s_p = qk(q, gather(k_s, prevs)) * SCALE + bias_of(dis, slice(0, BLK))
            m = jnp.max(jnp.maximum(s_c, s_p), axis=-1, keepdims=True)
            p_c = jnp.exp(s_c - m)
            p_p = jnp.exp(s_p - m)
            den = jnp.sum(p_c + p_p, axis=-1, keepdims=True)
            o = pv(p_c, gather(v_s, curs)) + pv(p_p, gather(v_s, prevs))
        for g, (di, pieces) in enumerate(zip(dis, curs)):
            m_g = jnp.broadcast_to(m[g], (BLK, HEAD_DIM))
            l_g = jnp.broadcast_to(den[g], (BLK, HEAD_DIM))
            lo = 0
            for s, idx, n_rows in pieces:
                acc_s[di, s, idx, :] = o[g, lo:lo + n_rows]
                m_s[di, s, idx, :] = m_g[lo:lo + n_rows]
                l_s[di, s, idx, :] = l_g[lo:lo + n_rows]
                lo += n_rows

    first, later = [], []
    for di, (win, dil) in enumerate(DILATED):
        for r in range(dil):
            first.append((di, dil, r, 0))
            later += [(di, dil, r, n) for n in range(1, seq // (dil * BLK))]
    groups = []
    for blocks, has_prev in ((first, False), (later, True)):
        for lo in range(0, len(blocks), group_size):
            part = blocks[lo:lo + group_size]
            groups.append(([di for di, _, _, _ in part],
                           [_block_pieces(dil, r, n) for _, dil, r, n in part],
                           [_block_pieces(dil, r, n - 1) for _, dil, r, n in part] if has_prev else None))
    n_chunks = 1 << (len(groups).bit_length() - 1)
    for c, args in enumerate(groups):
        group(*args)
        if c < n_chunks:
            copy_by_head(c, n_chunks)

    rows_per = 256
    per_slab = seq // SLABS // rows_per

    def merge(c, carry):
        s = c // per_slab
        r0 = pl.multiple_of((c % per_slab) * rows_per, rows_per)
        rc = pl.ds(r0, rows_per)
        ms = [m_s[di, s, rc, :] for di in range(len(DILATED))]
        mx = jnp.maximum(jnp.maximum(ms[0], ms[1]), ms[2])
        num = 0.0
        den = 0.0
        for di in range(len(DILATED)):
            wgt = jnp.exp(ms[di] - mx)
            num = num + wgt * acc_s[di, s, rc, :]
            den = den + wgt * l_s[di, s, rc, :]
        o_ref[pl.ds(SLABS * r0 + s, rows_per, stride=SLABS), :] = num / den
        return carry

    lax.fori_loop(0, SLABS * per_slab, merge, 0)


def _sgu_body(u_ref, v_ref, w_ref, bt_ref, o_ref, *, n_chunks):
    tri = (lax.broadcasted_iota(jnp.int32, (CHUNK, CHUNK), 0)
           >= lax.broadcasted_iota(jnp.int32, (CHUNK, CHUNK), 1))
    for g in range(H_B):
        w = jnp.where(tri, w_ref[g], 0.0).astype(BF16)
        bias = bt_ref[:, g:g + 1]
        cols = slice(g * C_B, (g + 1) * C_B)
        for c in range(n_chunks):
            rws = slice(c * CHUNK, (c + 1) * CHUNK)
            mixed = _dot(w, v_ref[rws, cols].astype(BF16)) + bias
            o_ref[rws, cols] = u_ref[rws, cols] * mixed


def _sgu_prompt(z, u_blk, v_blk, w_s, b_s, *, tm):
    n = z.shape[0]
    return pl.pallas_call(
        functools.partial(_sgu_body, n_chunks=tm // CHUNK),
        out_shape=jax.ShapeDtypeStruct((n, W_B), F32),
        grid=(n // tm,),
        in_specs=[pl.BlockSpec((tm, W_B), lambda i: (i, u_blk)),
                  pl.BlockSpec((tm, W_B), lambda i: (i, v_blk)),
                  pl.BlockSpec((H_B, CHUNK, CHUNK), lambda i: (0, 0, 0)),
                  pl.BlockSpec((CHUNK, H_B), lambda i: (0, 0))],
        out_specs=pl.BlockSpec((tm, W_B), lambda i: (i, 0)),
        compiler_params=_params("parallel"),
        name="sgu_prompt",
    )(z, z, w_s, b_s.T)


def _mem_body(q_ref, k_ref, v_ref, o_ref):
    for hh in range(H_M):
        sl = slice(hh * HEAD_DIM, (hh + 1) * HEAD_DIM)
        s = _dot_nt(q_ref[0, :, sl].astype(BF16), k_ref[0, :, sl].astype(BF16)) * SCALE
        m = jnp.max(s, axis=-1, keepdims=True)
        p = jnp.exp(s - m)
        den = jnp.sum(p, axis=-1, keepdims=True)
        o_ref[0, :, sl] = _dot(p.astype(BF16), v_ref[0, :, sl].astype(BF16)) / den


def _mem_attend(q, k, v, *, tq, q_blk, k_blk, v_blk):
    batch, t, _ = q.shape
    return pl.pallas_call(
        _mem_body,
        out_shape=jax.ShapeDtypeStruct((batch, t, W_M), F32),
        grid=(batch, t // tq),
        in_specs=[pl.BlockSpec((1, tq, W_M), lambda b, i: (b, i, q_blk)),
                  pl.BlockSpec((1, N_MEM, W_M), lambda b, i: (b, 0, k_blk)),
                  pl.BlockSpec((1, N_MEM, W_M), lambda b, i: (b, 0, v_blk))],
        out_specs=pl.BlockSpec((1, tq, W_M), lambda b, i: (b, i, 0)),
        compiler_params=_params("parallel", "parallel"),
        name="mem_attend",
    )(q, k, v)


def _mems_body(q_ref, k_ref, v_ref, o_ref):
    for b in range(q_ref.shape[0]):
        s = _dot_nt(q_ref[b].astype(BF16), k_ref[b].astype(BF16)) * SCALE
        row = lax.broadcasted_iota(jnp.int32, s.shape, 0)
        col = lax.broadcasted_iota(jnp.int32, s.shape, 1)
        s = jnp.where(row % H_M == col % H_M, s, NEG)
        m = jnp.max(s, axis=-1, keepdims=True)
        p = jnp.exp(s - m)
        den = jnp.sum(p, axis=-1, keepdims=True)
        o_ref[b] = _dot(p.astype(BF16), v_ref[b].astype(BF16)) / den


def _mem_attend_sample(q, cache_k, cache_v, *, tb):
    batch, n_new = q.shape[:2]
    n_rows = n_new * H_M
    q_spec = pl.BlockSpec((tb, n_rows, HEAD_DIM), lambda b: (b, 0, 0))
    kv_spec = pl.BlockSpec((tb, N_MEM * H_M, HEAD_DIM), lambda b: (b, 0, 0))
    out = pl.pallas_call(
        _mems_body,
        out_shape=jax.ShapeDtypeStruct((batch, n_rows, HEAD_DIM), F32),
        grid=(batch // tb,),
        in_specs=[q_spec, kv_spec, kv_spec],
        out_specs=q_spec,
        compiler_params=_params("parallel"),
        name="mem_attend_sample",
    )(q.reshape(batch, n_rows, HEAD_DIM), cache_k.reshape(batch, N_MEM * H_M, HEAD_DIM),
      cache_v.reshape(batch, N_MEM * H_M, HEAD_DIM))
    return out.reshape(batch, n_new, W_M)


def _outproj_body(*refs, n_groups):
    src_refs = [refs[4 * k:4 * k + 3] for k in range(n_groups)]
    h_refs = [refs[4 * k + 3] for k in range(n_groups)]
    g_ref, w_ref = refs[4 * n_groups:4 * n_groups + 2]
    o_refs = refs[4 * n_groups + 2:5 * n_groups + 2]
    cat_ref = refs[-1]
    bounds = _bounds(h_refs)

    @pl.when(pl.program_id(1) == 0)
    def _():
        for srcs, (r0, r1) in zip(src_refs, bounds):
            lo = 0
            for src in srcs:
                hi = lo + src.shape[1]
                cat_ref[r0:r1, lo:hi] = _rms(src[...], g_ref[:, lo:hi]).astype(BF16)
                lo = hi

    res = _dot(cat_ref[...], w_ref[...].astype(BF16))
    for h_ref, o_ref, (r0, r1) in zip(h_refs, o_refs, bounds):
        o_ref[...] = h_ref[...] + res[r0:r1]


def _outproj(groups, g, w, *, n_tiles, tn):
    mix = W_A + W_B + W_M
    rows = _tile_rows([grp[3] for grp in groups], n_tiles)
    in_specs = []
    for r in rows:
        in_specs += [pl.BlockSpec((r, wd), lambda i, j: (i, 0)) for wd in (W_A, W_B, W_M)]
        in_specs += [pl.BlockSpec((r, tn), lambda i, j: (i, j))]
    in_specs += [pl.BlockSpec((1, mix), lambda i, j: (0, 0), pipeline_mode=pl.Buffered(1)),
                 pl.BlockSpec((mix, tn), lambda i, j: (0, j))]
    return pl.pallas_call(
        functools.partial(_outproj_body, n_groups=len(groups)),
        out_shape=[jax.ShapeDtypeStruct(grp[3].shape, F32) for grp in groups],
        grid=(n_tiles, D_MODEL // tn),
        in_specs=in_specs,
        out_specs=[pl.BlockSpec((r, tn), lambda i, j: (i, j)) for r in rows],
        scratch_shapes=[pltpu.VMEM((sum(rows), mix), BF16)],
        compiler_params=_params("arbitrary", "arbitrary"),
        name="outproj",
    )(*[a for grp in groups for a in grp], g.reshape(1, mix), w)


def _sample_bias_index(n_new, n_past):
    (win1, dil1), (tail, dil4), (win16, dil16) = DILATED
    assert dil1 == 1 and win1 == BLK and n_past == win16 and n_past % dil16 == 0
    assert n_past >= tail and n_new <= dil4 and n_new * H_A <= LANES
    n16 = n_past // dil16
    t = np.arange(n_new)[:, None]

    def per_head(index):
        same = np.eye(H_A, dtype=bool)[None, :, None, :]
        full = np.where(same, index[:, None, :, None], -1)
        return full.reshape(n_new * H_A, index.shape[1] * H_A)

    dist = BLK + t - np.arange(BLK)[None, :]
    d1 = np.where(dist <= win1, _rel_bucket(dist), -1)
    dist = tail + t - np.arange(tail)[None, :]
    d4 = np.where((dist % dil4 == 0) & (dist <= tail), _rel_bucket(dist), -1)
    m, tk = np.divmod(np.arange(n16 * n_new)[None, :], n_new)
    dist = n_past + t - dil16 * m - tk
    d16 = np.where(tk == t, _rel_bucket(np.maximum(dist, 0)), -1)
    s = np.arange(LANES // H_A)[None, :]
    dist = t - s
    new = [np.where((s < n_new) & (dist >= 0) & (dist % dil == 0),
                    _rel_bucket(np.maximum(dist, 0)), -1) for _, dil in DILATED]
    pieces = [d1, d4, d16] + new
    return np.concatenate([per_head(p) for p in pieces], axis=1).astype(np.int32)


def _sattn_body(bias_ref, q_ref, kn_ref, vn_ref, kt_ref, k16_ref, vt_ref, v16_ref, o_ref):
    def flat(ref):
        x = ref[0]
        return x.reshape(-1, HEAD_DIM).astype(BF16)

    q = q_ref[0].astype(BF16)
    kt, vt = flat(kt_ref), flat(vt_ref)
    k16, v16 = flat(k16_ref), flat(v16_ref)
    kn, vn = kn_ref[0].astype(BF16), vn_ref[0].astype(BF16)
    n_t, n_1 = kt.shape[0], BLK * H_A

    raw_t = _dot_nt(q, kt)
    raw_n = _dot_nt(q, kn)
    raws = [raw_t[:, n_t - n_1:], raw_t, _dot_nt(q, k16)] + [raw_n] * len(DILATED)
    logits = []
    lo = 0
    for raw in raws:
        hi = lo + raw.shape[1]
        logits.append(raw * SCALE + bias_ref[:, lo:hi])
        lo = hi

    m = functools.reduce(jnp.maximum, [jnp.max(x, axis=-1, keepdims=True) for x in logits])
    probs = [jnp.exp(x - m) for x in logits]
    den = functools.reduce(lambda a, b: a + b, [jnp.sum(p, axis=-1, keepdims=True) for p in probs])
    p_new = functools.reduce(lambda a, b: a + b, probs[3:])
    o = (_dot(probs[0].astype(BF16), vt[n_t - n_1:, :]) + _dot(probs[1].astype(BF16), vt)
         + _dot(probs[2].astype(BF16), v16) + _dot(p_new.astype(BF16), vn))
    o_ref[0] = o / den


N_PROMPT_IN, N_SAMPLE_IN, N_PROMPT_OUT = 4, 8, 3


def _attention_body(*refs, seq, group_size):
    prompt_in, refs = refs[:N_PROMPT_IN], refs[N_PROMPT_IN:]
    sample_in, refs = refs[:N_SAMPLE_IN], refs[N_SAMPLE_IN:]
    prompt_out, sample_out, scratch = refs[:N_PROMPT_OUT], refs[N_PROMPT_OUT], refs[N_PROMPT_OUT + 1:]
    _sattn_body(*sample_in, sample_out)
    _attn_body(*prompt_in, *prompt_out, *scratch, seq=seq, group_size=group_size)


def _attention(z, qa, ka, va, cache_k, cache_v, rel_bias, *, batch, seq):
    n = batch * seq
    dec_batch, n_new = qa.shape[:2]
    n_past = cache_k.shape[1]
    assert all(seq % (dil * BLK) == 0 for _, dil in DILATED) and dec_batch == batch * H_A

    idx = _prompt_bias_index()
    n_d = idx.shape[0]
    bias_p = _bias_lookup(jnp.asarray(idx.reshape(n_d * BLK, 2 * BLK)), rel_bias.T[:, None, :],
                          n_col_tiles=1)
    bias_p = bias_p.reshape(H_A, n_d, BLK, 2 * BLK)
    prompt_in_specs = [pl.BlockSpec((1, n_d, BLK, 2 * BLK), lambda b, h: (h, 0, 0, 0))]
    prompt_in_specs += [pl.BlockSpec((seq, HEAD_DIM), lambda b, h, part=part: (b, part * H_A + h))
                        for part in range(3)]
    by_head = pl.BlockSpec((1, seq * H_A, HEAD_DIM), lambda b, h: (b, 0, 0), pipeline_mode=pl.Buffered(1))
    by_head_shape = jax.ShapeDtypeStruct((batch, seq * H_A, HEAD_DIM), F32)
    stat = pltpu.VMEM((n_d, SLABS, seq // SLABS, HEAD_DIM), F32)

    tail = DILATED[1][0]
    dil16 = DILATED[2][1]
    n16 = n_past // dil16
    n_rows = n_new * H_A
    bias_s = _bias_lookup(jnp.asarray(_sample_bias_index(n_new, n_past)),
                          jnp.tile(rel_bias.T, (n_new, 1))[None], n_col_tiles=5)[0]

    def rows(x):
        x = x.reshape(dec_batch, n_rows, HEAD_DIM)
        return jnp.pad(x, ((0, 0), (0, LANES - n_rows), (0, 0)))

    def sample(b, h):
        return b * H_A + h

    k_res = cache_k.reshape(dec_batch, n16, dil16, H_A, HEAD_DIM)
    v_res = cache_v.reshape(dec_batch, n16, dil16, H_A, HEAD_DIM)
    q_spec = pl.BlockSpec((1, n_rows, HEAD_DIM), lambda b, h: (sample(b, h), 0, 0))
    new_spec = pl.BlockSpec((1, LANES, HEAD_DIM), lambda b, h: (sample(b, h), 0, 0))
    tail_spec = pl.BlockSpec((1, tail, H_A, HEAD_DIM),
                             lambda b, h: (sample(b, h), n_past // tail - 1, 0, 0))
    res_spec = pl.BlockSpec((1, n16, n_new, H_A, HEAD_DIM), lambda b, h: (sample(b, h), 0, 0, 0, 0))
    sample_in_specs = [pl.BlockSpec(bias_s.shape, lambda b, h: (0, 0), pipeline_mode=pl.Buffered(1)),
                       q_spec, new_spec, new_spec, tail_spec, res_spec, tail_spec, res_spec]
    assert len(prompt_in_specs) == N_PROMPT_IN and len(sample_in_specs) == N_SAMPLE_IN

    oa_p, kh, vh, oa_s = pl.pallas_call(
        functools.partial(_attention_body, seq=seq, group_size=32),
        out_shape=[jax.ShapeDtypeStruct((n, W_A), F32), by_head_shape, by_head_shape,
                   jax.ShapeDtypeStruct((dec_batch, n_rows, HEAD_DIM), F32)],
        grid=(batch, H_A),
        in_specs=prompt_in_specs + sample_in_specs,
        out_specs=[pl.BlockSpec((seq, HEAD_DIM), lambda b, h: (b, h)), by_head, by_head, q_spec],
        scratch_shapes=[stat, stat, stat, stat],
        compiler_params=_params("arbitrary", "arbitrary"),
        name="attention",
    )(bias_p, z, z, z,
      bias_s, qa.reshape(dec_batch, n_rows, HEAD_DIM), rows(ka), rows(va), cache_k, k_res, cache_v, v_res)
    heads = (batch, seq, H_A, HEAD_DIM)
    return oa_p, kh.reshape(heads), vh.reshape(heads), oa_s.reshape(dec_batch, n_new, W_A)


def _sgus_body(u_ref, v_ref, w_ref, b_ref, o_ref, *, n_new):
    t_idx = lax.broadcasted_iota(jnp.int32, (1, n_new, W_B), 1)
    mixed = 0.0
    for s in range(n_new):
        w_s = jnp.where(t_idx >= s, w_ref[s][None], 0.0).astype(BF16).astype(F32)
        mixed = mixed + w_s * v_ref[:, s:s + 1, :].astype(BF16).astype(F32)
    o_ref[...] = u_ref[...] * (mixed + b_ref[...][None])


def _sgu_sample(u, v, w_s, b_s):
    batch, n_new, _ = u.shape
    w_l = jnp.repeat(jnp.transpose(w_s[:, :n_new, :n_new], (2, 1, 0)), C_B, axis=-1)
    b_l = jnp.repeat(b_s[:, :n_new].T, C_B, axis=-1)
    return pl.pallas_call(
        functools.partial(_sgus_body, n_new=n_new),
        out_shape=jax.ShapeDtypeStruct(u.shape, F32),
        name="sgu_sample",
    )(u, v, w_l, b_l)


COL_KA, COL_VA, COL_U, COL_V, COL_QM = W_A, 2 * W_A, 3 * W_A, 3 * W_A + W_B, 3 * W_A + 2 * W_B


def _mixer_inputs(normed_groups, w_in, g_qa, g_ka, g_sgu, g_qm, *, n_tiles):
    gain = jnp.concatenate([jnp.tile(g_qa, H_A), jnp.tile(g_ka, H_A), jnp.ones((W_A + W_B,), F32),
                            g_sgu.reshape(W_B), jnp.tile(g_qm, H_M)])
    normed = [c // PROJ_TILE for lo, hi in ((0, COL_VA), (COL_V, COL_QM + W_M))
              for c in range(lo, hi, PROJ_TILE)]
    return _proj(normed_groups, None, w_in, gain, normed, n_tiles=n_tiles, chunk=512)


def kernel(x_prompt, x_sample, cache_win_k, cache_win_v, cache_mem_k, cache_mem_v, mem_prompt, rel_bias, g_ffn1, w1_gate, w1_up, w1_down, g_mix, w_in, g_qa, g_ka, g_sgu, w_sgu, b_sgu, g_qm, g_mem, w_mem_kv, g_km, g_mix_out, w_out, g_ffn2, w2_gate, w2_up, w2_down):
    batch, seq, _ = x_prompt.shape
    dec_batch, n_new, _ = x_sample.shape
    depth = g_ffn1.shape[0]
    n_p = batch * seq
    n_s = dec_batch * n_new

    y_p = x_prompt.reshape(n_p, D_MODEL)
    y_s = x_sample.reshape(n_s, D_MODEL)
    outs = [[] for _ in range(7)]
    for l in range(depth):
        h_p, h_s, hn_p, hn_s = _ffn([y_p, y_s], g_ffn1[l], w1_gate[l], w1_up[l], w1_down[l],
                                    g_next=g_mix[l], n_tiles=8, tf=256)
        z_p, z_s = _mixer_inputs([hn_p, hn_s], w_in[l], g_qa[l], g_ka[l], g_sgu[l], g_qm[l], n_tiles=4)

        zq, zk, zv, zu, zg, zm = (z_s[:, lo:hi] for lo, hi in zip(
            (0, COL_KA, COL_VA, COL_U, COL_V, COL_QM), (COL_KA, COL_VA, COL_U, COL_V, COL_QM, z_s.shape[1])))
        heads = (dec_batch, n_new, H_A, HEAD_DIM)
        oa_p, k_p, v_p, oa_s = _attention(z_p, zq.reshape(heads), zk.reshape(heads), zv.reshape(heads),
                                          cache_win_k[l], cache_win_v[l], rel_bias, batch=batch, seq=seq)

        ob_p = _sgu_prompt(z_p, COL_U // W_B, COL_V // W_B, w_sgu[l], b_sgu[l], tm=1024)
        mem = mem_prompt.reshape(batch * N_MEM, D_MODEL)
        gain_kv = jnp.concatenate([jnp.tile(g_km[l], H_M), jnp.ones((W_M,), F32)])
        mkv, = _proj([mem], g_mem[l], w_mem_kv[l], gain_kv, range(W_M // PROJ_TILE),
                     n_tiles=2, chunk=512)
        mkv3 = mkv.reshape(batch, N_MEM, 2 * W_M)
        om_p = _mem_attend(z_p.reshape(batch, seq, -1), mkv3, mkv3, tq=1024,
                           q_blk=COL_QM // W_M, k_blk=0, v_blk=1)
        keep = min(DILATED[-1][0], seq)
        outs[0].append(k_p[:, seq - keep:])
        outs[1].append(v_p[:, seq - keep:])
        outs[2].append(mkv[:, :W_M].reshape(batch, N_MEM, H_M, HEAD_DIM))
        outs[3].append(mkv[:, W_M:].reshape(batch, N_MEM, H_M, HEAD_DIM))

        ob_s = _sgu_sample(zu.reshape(dec_batch, n_new, W_B), zg.reshape(dec_batch, n_new, W_B),
                           w_sgu[l], b_sgu[l])
        om_s = _mem_attend_sample(zm.reshape(dec_batch, n_new, H_M, HEAD_DIM),
                                  cache_mem_k[l], cache_mem_v[l], tb=4)
        outs[4].append(zk.reshape(heads))
        outs[5].append(zv.reshape(heads))
        outs[6].append(zg.reshape(dec_batch, n_new, H_B, C_B))

        h_p, h_s = _outproj([(oa_p, ob_p, om_p.reshape(n_p, W_M), h_p),
                             (oa_s.reshape(n_s, W_A), ob_s.reshape(n_s, W_B), om_s.reshape(n_s, W_M), h_s)],
                            g_mix_out[l], w_out[l], n_tiles=4, tn=256)
        y_p, y_s = _ffn([h_p, h_s], g_ffn2[l], w2_gate[l], w2_up[l], w2_down[l], n_tiles=8, tf=256)

    return (y_p.reshape(batch, seq, D_MODEL), y_s.reshape(dec_batch, n_new, D_MODEL),
            *[jnp.stack(o) for o in outs])
```

```python
import functools

import numpy as np
import jax
import jax.numpy as jnp
from jax import lax
from jax.experimental import pallas as pl
from jax.experimental.pallas import tpu as pltpu

D_MODEL = 2048
HEAD_DIM = 128
H_A = 8
H_B = 4
C_B = 128
H_M = 4
W_A = H_A * HEAD_DIM
W_B = H_B * C_B
W_M = H_M * HEAD_DIM
D_FF = 5632
DILATED = ((128, 1), (512, 4), (2048, 16))
BLK = 128
CHUNK = 128
N_MEM = 256
N_BUCKETS = 32
REL_MAX_DIST = 2048
EPS = 1e-6
SCALE = HEAD_DIM ** -0.5
NEG = -1e30

LANES = 128
BF16_ROWS = 16
PROJ_TILE = 4 * HEAD_DIM
VMEM_LIMIT = 60 * 1024 * 1024

F32 = jnp.float32
BF16 = jnp.bfloat16


def _rel_bucket(dist):
    dist = np.asarray(dist, np.int64)
    max_exact = N_BUCKETS // 2
    large = max_exact + (np.log(np.maximum(dist, 1) / max_exact) / np.log(REL_MAX_DIST / max_exact)
                         * (N_BUCKETS - max_exact)).astype(np.int64)
    large = np.minimum(large, N_BUCKETS - 1)
    return np.where(dist < max_exact, dist, large).astype(np.int32)


def _rms(t, g):
    return t * lax.rsqrt(jnp.mean(t * t, axis=-1, keepdims=True) + EPS) * g


def _dot(a, b):
    return jnp.dot(a, b, preferred_element_type=F32)


def _dot_nt(a, b):
    return lax.dot_general(a, b, (((1,), (1,)), ((), ())), preferred_element_type=F32)


def _params(*sem):
    return pltpu.CompilerParams(dimension_semantics=sem, vmem_limit_bytes=VMEM_LIMIT)


def _tile_rows(groups, n_tiles):
    rows = [x.shape[0] // n_tiles for x in groups]
    assert all(r * n_tiles == x.shape[0] and r % BF16_ROWS == 0 for r, x in zip(rows, groups))
    return rows


def _bounds(refs):
    out, lo = [], 0
    for ref in refs:
        out.append((lo, lo + ref.shape[0]))
        lo += ref.shape[0]
    return out


def _ffn_body(*refs, n_groups, with_next_norm):
    x_refs, refs = refs[:n_groups], refs[n_groups:]
    gn_ref = refs[0] if with_next_norm else None
    g_ref, wg_ref, wu_ref, wd_ref = refs[with_next_norm:with_next_norm + 4]
    refs = refs[with_next_norm + 4:]
    o_refs, on_refs, xn_ref = refs[:n_groups], refs[n_groups:-1], refs[-1]
    bounds = _bounds(x_refs)
    j = pl.program_id(1)

    @pl.when(j == 0)
    def _():
        for x_ref, o_ref, (lo, hi) in zip(x_refs, o_refs, bounds):
            x = x_ref[...]
            xn_ref[lo:hi, :] = _rms(x, g_ref[...]).astype(BF16)
            o_ref[...] = x

    xn = xn_ref[...]
    gate = _dot(xn, wg_ref[...].astype(BF16))
    up = _dot(xn, wu_ref[...].astype(BF16))
    act = (gate * jax.nn.sigmoid(gate)) * up * 0.5
    res = _dot(act.astype(BF16), wd_ref[...].astype(BF16))
    for o_ref, (lo, hi) in zip(o_refs, bounds):
        o_ref[...] += res[lo:hi]

    if with_next_norm:
        @pl.when(j == pl.num_programs(1) - 1)
        def _():
            for o_ref, on_ref in zip(o_refs, on_refs):
                on_ref[...] = _rms(o_ref[...], gn_ref[...]).astype(BF16)


def _ffn(groups, g, wg, wu, wd, *, g_next=None, n_tiles, tf):
    rows = _tile_rows(groups, n_tiles)
    with_next_norm = g_next is not None
    gain_spec = pl.BlockSpec((1, D_MODEL), lambda i, j: (0, 0), pipeline_mode=pl.Buffered(1))
    row_specs = [pl.BlockSpec((r, D_MODEL), lambda i, j: (i, 0)) for r in rows]
    out_shape = [jax.ShapeDtypeStruct(x.shape, F32) for x in groups]
    gains = [g.reshape(1, D_MODEL)]
    if with_next_norm:
        out_shape += [jax.ShapeDtypeStruct(x.shape, BF16) for x in groups]
        gains.insert(0, g_next.reshape(1, D_MODEL))
    return pl.pallas_call(
        functools.partial(_ffn_body, n_groups=len(groups), with_next_norm=with_next_norm),
        out_shape=out_shape,
        grid=(n_tiles, D_FF // tf),
        in_specs=(
            row_specs + [gain_spec] * len(gains)
            + [pl.BlockSpec((D_MODEL, tf), lambda i, j: (0, j)),
               pl.BlockSpec((D_MODEL, tf), lambda i, j: (0, j)),
               pl.BlockSpec((tf, D_MODEL), lambda i, j: (j, 0))]),
        out_specs=row_specs * (2 if with_next_norm else 1),
        scratch_shapes=[pltpu.VMEM((sum(rows), D_MODEL), BF16)],
        compiler_params=_params("arbitrary", "arbitrary"),
        name="ffn",
    )(*groups, *gains, wg, wu, wd)


def _proj_body(*refs, n_groups, pre_normed, norm_tiles, chunk):
    x_refs, refs = refs[:n_groups], refs[n_groups:]
    g_ref = None if pre_normed else refs[0]
    w_ref, gain_ref = refs[1 - pre_normed:3 - pre_normed]
    o_refs, xn_ref = refs[3 - pre_normed:-1], refs[-1]
    bounds = _bounds(x_refs)
    j = pl.program_id(1)

    @pl.when(j == 0)
    def _():
        for x_ref, (lo, hi) in zip(x_refs, bounds):
            x = x_ref[...]
            xn_ref[lo:hi, :] = x if pre_normed else _rms(x, g_ref[...]).astype(BF16)

    w = w_ref[...].astype(BF16)
    gain = gain_ref[...]
    normed = functools.reduce(jnp.logical_or, [j == t for t in norm_tiles])

    def finish(z):
        heads = []
        for hh in range(PROJ_TILE // HEAD_DIM):
            sl = slice(hh * HEAD_DIM, (hh + 1) * HEAD_DIM)
            zz = z[:, sl]
            scale = lax.rsqrt(jnp.mean(zz * zz, axis=-1, keepdims=True) + EPS)
            heads.append(zz * jnp.where(normed, scale, 1.0) * gain[:, sl])
        return jnp.concatenate(heads, axis=1)

    total = bounds[-1][1]
    edges = list(range(0, total, chunk)) + [total]
    if len(edges) > 2 and edges[-1] - edges[-2] < chunk // 2:
        del edges[-2]
    for c0, c1 in zip(edges[:-1], edges[1:]):
        z = finish(_dot(xn_ref[c0:c1, :], w))
        for o_ref, (lo, hi) in zip(o_refs, bounds):
            a, b = max(lo, c0), min(hi, c1)
            if a < b:
                o_ref[a - lo:b - lo, :] = z[a - c0:b - c0, :]


def _proj(groups, g, w, gain, norm_tiles, *, n_tiles, chunk):
    d_in, d_out = w.shape
    rows = _tile_rows(groups, n_tiles)
    pre_normed = g is None
    assert all(x.dtype == (BF16 if pre_normed else F32) for x in groups)
    once = pl.Buffered(1)
    norm_args = [] if pre_normed else [g.reshape(1, d_in)]
    return pl.pallas_call(
        functools.partial(_proj_body, n_groups=len(groups), pre_normed=pre_normed,
                          norm_tiles=tuple(norm_tiles), chunk=chunk),
        out_shape=[jax.ShapeDtypeStruct((x.shape[0], d_out), F32) for x in groups],
        grid=(n_tiles, d_out // PROJ_TILE),
        in_specs=(
            [pl.BlockSpec((r, d_in), lambda i, j: (i, 0)) for r in rows]
            + [pl.BlockSpec((1, d_in), lambda i, j: (0, 0), pipeline_mode=once)] * len(norm_args)
            + [pl.BlockSpec((d_in, PROJ_TILE), lambda i, j: (0, j)),
               pl.BlockSpec((1, PROJ_TILE), lambda i, j: (0, j))]),
        out_specs=[pl.BlockSpec((r, PROJ_TILE), lambda i, j: (i, j)) for r in rows],
        scratch_shapes=[pltpu.VMEM((sum(rows), d_in), BF16)],
        compiler_params=_params("arbitrary", "arbitrary"),
        name="proj",
    )(*groups, *norm_args, w, gain.reshape(1, d_out))


def _bias_body(idx_ref, tab_ref, o_ref):
    idx = idx_ref[...]
    tab = tab_ref[0]
    bias = jnp.full(idx.shape, NEG, F32)
    for bkt in range(N_BUCKETS):
        bias = jnp.where(idx == bkt, tab[:, bkt:bkt + 1], bias)
    o_ref[0] = bias


def _bias_lookup(idx, tab, *, n_col_tiles):
    n_rows, n_cols = idx.shape
    n_g, tab_rows, _ = tab.shape
    tc = n_cols // n_col_tiles
    assert tc * n_col_tiles == n_cols and tc % LANES == 0 and tab_rows in (1, n_rows)
    return pl.pallas_call(
        _bias_body,
        out_shape=jax.ShapeDtypeStruct((n_g, n_rows, n_cols), F32),
        grid=(n_g, n_col_tiles),
        in_specs=[pl.BlockSpec((n_rows, tc), lambda g, j: (0, j)),
                  pl.BlockSpec((1, tab_rows, N_BUCKETS), lambda g, j: (g, 0, 0))],
        out_specs=pl.BlockSpec((1, n_rows, tc), lambda g, j: (g, 0, j)),
        compiler_params=_params("parallel", "parallel"),
        name="bias_lookup",
    )(idx, tab)


SLABS = DILATED[1][1]
PIECE = BLK // SLABS


def _prompt_bias_index():
    step = np.arange(BLK)[:, None] + BLK - np.arange(2 * BLK)[None, :]
    perm = (SLABS * np.arange(PIECE)[None, :] + np.arange(SLABS)[:, None]).reshape(BLK)
    maps = []
    for win, dil in DILATED:
        n_steps = win // dil + 1
        band = (step >= 0) & (step < n_steps)
        bucket = _rel_bucket(np.clip(step, 0, n_steps - 1) * dil)
        idx = np.where(band, bucket, -1)
        if dil == 1:
            idx = idx[np.ix_(perm, np.concatenate([perm, BLK + perm]))]
        maps.append(idx)
    return np.stack(maps).astype(np.int32)


def _block_pieces(dil, r, n):
    if dil == 1:
        return [(s, slice(n * PIECE, (n + 1) * PIECE), PIECE) for s in range(SLABS)]
    if dil == SLABS:
        return [(r, slice(n * BLK, (n + 1) * BLK), BLK)]
    sub = dil // SLABS
    s, t = r % SLABS, r // SLABS
    return [(s, pl.ds(t + sub * BLK * n, BLK, stride=sub), BLK)]


def _attn_body(bias_s, q_ref, k_ref, v_ref, o_ref, kh_ref, vh_ref, slab_s, acc_s, m_s, l_s, *,
               seq, group_size):
    for part, ref in enumerate((q_ref, k_ref, v_ref)):
        for s in range(SLABS):
            slab_s[part, s] = ref[pl.ds(s, seq // SLABS, stride=SLABS), :]
    q_s, k_s, v_s = ([slab_s.at[part, s] for s in range(SLABS)] for part in range(3))

    def copy_by_head(c, n_chunks):
        n_rows = seq // n_chunks
        src = pl.ds(c * n_rows, n_rows)
        dst = pl.ds(pl.program_id(1) + H_A * c * n_rows, n_rows, stride=H_A)
        kh_ref[0, dst, :] = k_ref[src, :]
        vh_ref[0, dst, :] = v_ref[src, :]

    def load(slabs, pieces):
        parts = [slabs[s][idx, :] for s, idx, _ in pieces]
        return parts[0] if len(parts) == 1 else jnp.concatenate(parts, axis=0)

    def gather(slabs, blocks):
        return jnp.stack([load(slabs, pieces) for pieces in blocks]).astype(BF16)

    def qk(q, k):
        return jnp.einsum("gid,gjd->gij", q, k, preferred_element_type=F32)

    def pv(p, v):
        return jnp.einsum("gij,gjd->gid", p.astype(BF16), v, preferred_element_type=F32)

    def bias_of(dis, cols):
        return jnp.stack([bias_s[0, di, :, cols] for di in dis])

    def group(dis, curs, prevs):
        q = gather(q_s, curs)
        s_c = qk(q, gather(k_s, curs)) * SCALE + bias_of(dis, slice(BLK, None))
        if prevs is None:
            m = jnp.max(s_c, axis=-1, keepdims=True)
            p_c = jnp.exp(s_c - m)
            den = jnp.sum(p_c, axis=-1, keepdims=True)
            o = pv(p_c, gather(v_s, curs))
        else:
            s_p = qk(q, gather(k_s, prevs)) * SCALE + bias_of(dis, slice(0, BLK))
            m = jnp.max(jnp.maximum(s_c, s_p), axis=-1, keepdims=True)
            p_c = jnp.exp(s_c - m)
            p_p = jnp.exp(s_p - m)
            den = jnp.sum(p_c + p_p, axis=-1, keepdims=True)
            o = pv(p_c, gather(v_s, curs)) + pv(p_p, gather(v_s, prevs))
        for g, (di, pieces) in enumerate(zip(dis, curs)):
            m_g = jnp.broadcast_to(m[g], (BLK, HEAD_DIM))
            l_g = jnp.broadcast_to(den[g], (BLK, HEAD_DIM))
            lo = 0
            for s, idx, n_rows in pieces:
                acc_s[di, s, idx, :] = o[g, lo:lo + n_rows]
                m_s[di, s, idx, :] = m_g[lo:lo + n_rows]
                l_s[di, s, idx, :] = l_g[lo:lo + n_rows]
                lo += n_rows

    first, later = [], []
    for di, (win, dil) in enumerate(DILATED):
        for r in range(dil):
            first.append((di, dil, r, 0))
            later += [(di, dil, r, n) for n in range(1, seq // (dil * BLK))]
    groups = []
    for blocks, has_prev in ((first, False), (later, True)):
        for lo in range(0, len(blocks), group_size):
            part = blocks[lo:lo + group_size]
            groups.append(([di for di, _, _, _ in part],
                           [_block_pieces(dil, r, n) for _, dil, r, n in part],
                           [_block_pieces(dil, r, n - 1) for _, dil, r, n in part] if has_prev else None))
    n_chunks = 1 << (len(groups).bit_length() - 1)
    for c, args in enumerate(groups):
        group(*args)
        if c < n_chunks:
            copy_by_head(c, n_chunks)

    rows_per = 256
    per_slab = seq // SLABS // rows_per

    def merge(c, carry):
        s = c // per_slab
        r0 = pl.multiple_of((c % per_slab) * rows_per, rows_per)
        rc = pl.ds(r0, rows_per)
        ms = [m_s[di, s, rc, :] for di in range(len(DILATED))]
        mx = jnp.maximum(jnp.maximum(ms[0], ms[1]), ms[2])
        num = 0.0
        den = 0.0
        for di in range(len(DILATED)):
            wgt = jnp.exp(ms[di] - mx)
            num = num + wgt * acc_s[di, s, rc, :]
            den = den + wgt * l_s[di, s, rc, :]
        o_ref[pl.ds(SLABS * r0 + s, rows_per, stride=SLABS), :] = num / den
        return carry

    lax.fori_loop(0, SLABS * per_slab, merge, 0)


def _sgu_body(u_ref, v_ref, w_ref, bt_ref, o_ref, *, n_chunks):
    tri = (lax.broadcasted_iota(jnp.int32, (CHUNK, CHUNK), 0)
           >= lax.broadcasted_iota(jnp.int32, (CHUNK, CHUNK), 1))
    for g in range(H_B):
        w = jnp.where(tri, w_ref[g], 0.0).astype(BF16)
        bias = bt_ref[:, g:g + 1]
        cols = slice(g * C_B, (g + 1) * C_B)
        for c in range(n_chunks):
            rws = slice(c * CHUNK, (c + 1) * CHUNK)
            mixed = _dot(w, v_ref[rws, cols].astype(BF16)) + bias
            o_ref[rws, cols] = u_ref[rws, cols] * mixed


def _mem_body(q_ref, k_ref, v_ref, o_ref):
    for hh in range(H_M):
        sl = slice(hh * HEAD_DIM, (hh + 1) * HEAD_DIM)
        s = _dot_nt(q_ref[0, :, sl].astype(BF16), k_ref[0, :, sl].astype(BF16)) * SCALE
        m = jnp.max(s, axis=-1, keepdims=True)
        p = jnp.exp(s - m)
        den = jnp.sum(p, axis=-1, keepdims=True)
        o_ref[0, :, sl] = _dot(p.astype(BF16), v_ref[0, :, sl].astype(BF16)) / den


def _mems_body(q_ref, k_ref, v_ref, o_ref):
    for b in range(q_ref.shape[0]):
        s = _dot_nt(q_ref[b].astype(BF16), k_ref[b].astype(BF16)) * SCALE
        row = lax.broadcasted_iota(jnp.int32, s.shape, 0)
        col = lax.broadcasted_iota(jnp.int32, s.shape, 1)
        s = jnp.where(row % H_M == col % H_M, s, NEG)
        m = jnp.max(s, axis=-1, keepdims=True)
        p = jnp.exp(s - m)
        den = jnp.sum(p, axis=-1, keepdims=True)
        o_ref[b] = _dot(p.astype(BF16), v_ref[b].astype(BF16)) / den


def _outproj_body(*refs, n_groups):
    src_refs = [refs[4 * k:4 * k + 3] for k in range(n_groups)]
    h_refs = [refs[4 * k + 3] for k in range(n_groups)]
    g_ref, w_ref = refs[4 * n_groups:4 * n_groups + 2]
    o_refs = refs[4 * n_groups + 2:5 * n_groups + 2]
    cat_ref = refs[-1]
    bounds = _bounds(h_refs)

    @pl.when(pl.program_id(1) == 0)
    def _():
        for srcs, (r0, r1) in zip(src_refs, bounds):
            lo = 0
            for src in srcs:
                hi = lo + src.shape[1]
                cat_ref[r0:r1, lo:hi] = _rms(src[...], g_ref[:, lo:hi]).astype(BF16)
                lo = hi

    res = _dot(cat_ref[...], w_ref[...].astype(BF16))
    for h_ref, o_ref, (r0, r1) in zip(h_refs, o_refs, bounds):
        o_ref[...] = h_ref[...] + res[r0:r1]


def _outproj(groups, g, w, *, n_tiles, tn):
    mix = W_A + W_B + W_M
    rows = _tile_rows([grp[3] for grp in groups], n_tiles)
    in_specs = []
    for r in rows:
        in_specs += [pl.BlockSpec((r, wd), lambda i, j: (i, 0)) for wd in (W_A, W_B, W_M)]
        in_specs += [pl.BlockSpec((r, tn), lambda i, j: (i, j))]
    in_specs += [pl.BlockSpec((1, mix), lambda i, j: (0, 0), pipeline_mode=pl.Buffered(1)),
                 pl.BlockSpec((mix, tn), lambda i, j: (0, j))]
    return pl.pallas_call(
        functools.partial(_outproj_body, n_groups=len(groups)),
        out_shape=[jax.ShapeDtypeStruct(grp[3].shape, F32) for grp in groups],
        grid=(n_tiles, D_MODEL // tn),
        in_specs=in_specs,
        out_specs=[pl.BlockSpec((r, tn), lambda i, j: (i, j)) for r in rows],
        scratch_shapes=[pltpu.VMEM((sum(rows), mix), BF16)],
        compiler_params=_params("arbitrary", "arbitrary"),
        name="outproj",
    )(*[a for grp in groups for a in grp], g.reshape(1, mix), w)


def _sample_bias_index(n_new, n_past):
    (win1, dil1), (tail, dil4), (win16, dil16) = DILATED
    assert dil1 == 1 and win1 == BLK and n_past == win16 and n_past % dil16 == 0
    assert n_past >= tail and n_new <= dil4 and n_new * H_A <= LANES
    n16 = n_past // dil16
    t = np.arange(n_new)[:, None]

    def per_head(index):
        same = np.eye(H_A, dtype=bool)[None, :, None, :]
        full = np.where(same, index[:, None, :, None], -1)
        return full.reshape(n_new * H_A, index.shape[1] * H_A)

    dist = BLK + t - np.arange(BLK)[None, :]
    d1 = np.where(dist <= win1, _rel_bucket(dist), -1)
    dist = tail + t - np.arange(tail)[None, :]
    d4 = np.where((dist % dil4 == 0) & (dist <= tail), _rel_bucket(dist), -1)
    m, tk = np.divmod(np.arange(n16 * n_new)[None, :], n_new)
    dist = n_past + t - dil16 * m - tk
    d16 = np.where(tk == t, _rel_bucket(np.maximum(dist, 0)), -1)
    s = np.arange(LANES // H_A)[None, :]
    dist = t - s
    new = [np.where((s < n_new) & (dist >= 0) & (dist % dil == 0),
                    _rel_bucket(np.maximum(dist, 0)), -1) for _, dil in DILATED]
    pieces = [d1, d4, d16] + new
    return np.concatenate([per_head(p) for p in pieces], axis=1).astype(np.int32)


def _sattn_body(bias_ref, q_ref, kn_ref, vn_ref, kt_ref, k16_ref, vt_ref, v16_ref, o_ref):
    def flat(ref):
        x = ref[0]
        return x.reshape(-1, HEAD_DIM).astype(BF16)

    q = q_ref[0].astype(BF16)
    kt, vt = flat(kt_ref), flat(vt_ref)
    k16, v16 = flat(k16_ref), flat(v16_ref)
    kn, vn = kn_ref[0].astype(BF16), vn_ref[0].astype(BF16)
    n_t, n_1 = kt.shape[0], BLK * H_A

    raw_t = _dot_nt(q, kt)
    raw_n = _dot_nt(q, kn)
    raws = [raw_t[:, n_t - n_1:], raw_t, _dot_nt(q, k16)] + [raw_n] * len(DILATED)
    logits = []
    lo = 0
    for raw in raws:
        hi = lo + raw.shape[1]
        logits.append(raw * SCALE + bias_ref[:, lo:hi])
        lo = hi

    m = functools.reduce(jnp.maximum, [jnp.max(x, axis=-1, keepdims=True) for x in logits])
    probs = [jnp.exp(x - m) for x in logits]
    den = functools.reduce(lambda a, b: a + b, [jnp.sum(p, axis=-1, keepdims=True) for p in probs])
    p_new = functools.reduce(lambda a, b: a + b, probs[3:])
    o = (_dot(probs[0].astype(BF16), vt[n_t - n_1:, :]) + _dot(probs[1].astype(BF16), vt)
         + _dot(probs[2].astype(BF16), v16) + _dot(p_new.astype(BF16), vn))
    o_ref[0] = o / den


N_PROMPT_IN, N_SAMPLE_IN, N_PROMPT_OUT = 4, 8, 3


def _attention_body(*refs, seq, group_size):
    prompt_in, refs = refs[:N_PROMPT_IN], refs[N_PROMPT_IN:]
    sample_in, refs = refs[:N_SAMPLE_IN], refs[N_SAMPLE_IN:]
    prompt_out, sample_out, scratch = refs[:N_PROMPT_OUT], refs[N_PROMPT_OUT], refs[N_PROMPT_OUT + 1:]
    _sattn_body(*sample_in, sample_out)
    _attn_body(*prompt_in, *prompt_out, *scratch, seq=seq, group_size=group_size)


def _attention(z, qa, ka, va, cache_k, cache_v, rel_bias, *, batch, seq):
    n = batch * seq
    dec_batch, n_new = qa.shape[:2]
    n_past = cache_k.shape[1]
    assert all(seq % (dil * BLK) == 0 for _, dil in DILATED) and dec_batch == batch * H_A

    idx = _prompt_bias_index()
    n_d = idx.shape[0]
    bias_p = _bias_lookup(jnp.asarray(idx.reshape(n_d * BLK, 2 * BLK)), rel_bias.T[:, None, :],
                          n_col_tiles=1)
    bias_p = bias_p.reshape(H_A, n_d, BLK, 2 * BLK)
    prompt_in_specs = [pl.BlockSpec((1, n_d, BLK, 2 * BLK), lambda b, h: (h, 0, 0, 0))]
    prompt_in_specs += [pl.BlockSpec((seq, HEAD_DIM), lambda b, h, part=part: (b, part * H_A + h))
                        for part in range(3)]
    by_head = pl.BlockSpec((1, seq * H_A, HEAD_DIM), lambda b, h: (b, 0, 0), pipeline_mode=pl.Buffered(1))
    by_head_shape = jax.ShapeDtypeStruct((batch, seq * H_A, HEAD_DIM), F32)
    stat = pltpu.VMEM((n_d, SLABS, seq // SLABS, HEAD_DIM), F32)

    tail = DILATED[1][0]
    dil16 = DILATED[2][1]
    n16 = n_past // dil16
    n_rows = n_new * H_A
    bias_s = _bias_lookup(jnp.asarray(_sample_bias_index(n_new, n_past)),
                          jnp.tile(rel_bias.T, (n_new, 1))[None], n_col_tiles=5)[0]

    def rows(x):
        x = x.reshape(dec_batch, n_rows, HEAD_DIM)
        return jnp.pad(x, ((0, 0), (0, LANES - n_rows), (0, 0)))

    def sample(b, h):
        return b * H_A + h

    k_res = cache_k.reshape(dec_batch, n16, dil16, H_A, HEAD_DIM)
    v_res = cache_v.reshape(dec_batch, n16, dil16, H_A, HEAD_DIM)
    q_spec = pl.BlockSpec((1, n_rows, HEAD_DIM), lambda b, h: (sample(b, h), 0, 0))
    new_spec = pl.BlockSpec((1, LANES, HEAD_DIM), lambda b, h: (sample(b, h), 0, 0))
    tail_spec = pl.BlockSpec((1, tail, H_A, HEAD_DIM),
                             lambda b, h: (sample(b, h), n_past // tail - 1, 0, 0))
    res_spec = pl.BlockSpec((1, n16, n_new, H_A, HEAD_DIM), lambda b, h: (sample(b, h), 0, 0, 0, 0))
    sample_in_specs = [pl.BlockSpec(bias_s.shape, lambda b, h: (0, 0), pipeline_mode=pl.Buffered(1)),
                       q_spec, new_spec, new_spec, tail_spec, res_spec, tail_spec, res_spec]
    assert len(prompt_in_specs) == N_PROMPT_IN and len(sample_in_specs) == N_SAMPLE_IN

    oa_p, kh, vh, oa_s = pl.pallas_call(
        functools.partial(_attention_body, seq=seq, group_size=32),
        out_shape=[jax.ShapeDtypeStruct((n, W_A), F32), by_head_shape, by_head_shape,
                   jax.ShapeDtypeStruct((dec_batch, n_rows, HEAD_DIM), F32)],
        grid=(batch, H_A),
        in_specs=prompt_in_specs + sample_in_specs,
        out_specs=[pl.BlockSpec((seq, HEAD_DIM), lambda b, h: (b, h)), by_head, by_head, q_spec],
        scratch_shapes=[stat, stat, stat, stat],
        compiler_params=_params("arbitrary", "arbitrary"),
        name="attention",
    )(bias_p, z, z, z,
      bias_s, qa.reshape(dec_batch, n_rows, HEAD_DIM), rows(ka), rows(va), cache_k, k_res, cache_v, v_res)
    heads = (batch, seq, H_A, HEAD_DIM)
    return oa_p, kh.reshape(heads), vh.reshape(heads), oa_s.reshape(dec_batch, n_new, W_A)


def _sgus_body(u_ref, v_ref, w_ref, b_ref, o_ref, *, n_new):
    t_idx = lax.broadcasted_iota(jnp.int32, (1, n_new, W_B), 1)
    mixed = 0.0
    for s in range(n_new):
        w_s = jnp.where(t_idx >= s, w_ref[s][None], 0.0).astype(BF16).astype(F32)
        mixed = mixed + w_s * v_ref[:, s:s + 1, :].astype(BF16).astype(F32)
    o_ref[...] = u_ref[...] * (mixed + b_ref[...][None])


def _side_body(u_ref, v_ref, w_ref, bt_ref, qm_ref, mk_ref, mv_ref,
               us_ref, vs_ref, wl_ref, bl_ref, qs_ref, ck_ref, cv_ref,
               ob_ref, om_ref, obs_ref, oms_ref, *, n_chunks, n_new):
    _mems_body(qs_ref, ck_ref, cv_ref, oms_ref)
    _sgus_body(us_ref, vs_ref, wl_ref, bl_ref, obs_ref, n_new=n_new)
    _mem_body(qm_ref, mk_ref, mv_ref, om_ref)
    _sgu_body(u_ref, v_ref, w_ref, bt_ref, ob_ref, n_chunks=n_chunks)


def _side_mixers(z, mkv, u_s, v_s, qm_s, cache_k, cache_v, w_s, b_s, *, batch, seq, n_steps):
    dec_batch, n_new, _ = u_s.shape
    tm = batch * seq // n_steps
    tb = dec_batch // n_steps
    per_batch = seq // tm
    assert tm * per_batch == seq and tb * n_steps == dec_batch and tm % CHUNK == 0
    n_rows = n_new * H_M
    w_l = jnp.repeat(jnp.transpose(w_s[:, :n_new, :n_new], (2, 1, 0)), C_B, axis=-1)
    b_l = jnp.repeat(b_s[:, :n_new].T, C_B, axis=-1)
    z3 = z.reshape(batch, seq, -1)
    mkv3 = mkv.reshape(batch, N_MEM, 2 * W_M)
    once = pl.Buffered(1)
    sgu_s_spec = pl.BlockSpec((tb, n_new, W_B), lambda i: (i, 0, 0))
    qs_spec = pl.BlockSpec((tb, n_rows, HEAD_DIM), lambda i: (i, 0, 0))
    cache_spec = pl.BlockSpec((tb, N_MEM * H_M, HEAD_DIM), lambda i: (i, 0, 0))
    om_spec = pl.BlockSpec((1, tm, W_M), lambda i: (i // per_batch, i % per_batch, 0))
    ob, om, ob_s, om_s = pl.pallas_call(
        functools.partial(_side_body, n_chunks=tm // CHUNK, n_new=n_new),
        out_shape=[jax.ShapeDtypeStruct((batch * seq, W_B), F32),
                   jax.ShapeDtypeStruct((batch, seq, W_M), F32),
                   jax.ShapeDtypeStruct(u_s.shape, F32),
                   jax.ShapeDtypeStruct((dec_batch, n_rows, HEAD_DIM), F32)],
        grid=(n_steps,),
        in_specs=[
            pl.BlockSpec((tm, W_B), lambda i: (i, COL_U // W_B)),
            pl.BlockSpec((tm, W_B), lambda i: (i, COL_V // W_B)),
            pl.BlockSpec((H_B, CHUNK, CHUNK), lambda i: (0, 0, 0), pipeline_mode=once),
            pl.BlockSpec((CHUNK, H_B), lambda i: (0, 0), pipeline_mode=once),
            pl.BlockSpec((1, tm, W_M), lambda i: (i // per_batch, i % per_batch, COL_QM // W_M)),
            pl.BlockSpec((1, N_MEM, W_M), lambda i: (i // per_batch, 0, 0)),
            pl.BlockSpec((1, N_MEM, W_M), lambda i: (i // per_batch, 0, 1)),
            sgu_s_spec, sgu_s_spec,
            pl.BlockSpec(w_l.shape, lambda i: (0, 0, 0), pipeline_mode=once),
            pl.BlockSpec(b_l.shape, lambda i: (0, 0), pipeline_mode=once),
            qs_spec, cache_spec, cache_spec,
        ],
        out_specs=[pl.BlockSpec((tm, W_B), lambda i: (i, 0)), om_spec, sgu_s_spec, qs_spec],
        compiler_params=_params("parallel"),
        name="side_mixers",
    )(z, z, w_s, b_s.T, z3, mkv3, mkv3, u_s, v_s, w_l, b_l,
      qm_s.reshape(dec_batch, n_rows, HEAD_DIM),
      cache_k.reshape(dec_batch, N_MEM * H_M, HEAD_DIM), cache_v.reshape(dec_batch, N_MEM * H_M, HEAD_DIM))
    return ob, om.reshape(batch * seq, W_M), ob_s, om_s.reshape(dec_batch, n_new, W_M)


COL_KA, COL_VA, COL_U, COL_V, COL_QM = W_A, 2 * W_A, 3 * W_A, 3 * W_A + W_B, 3 * W_A + 2 * W_B


def _mixer_inputs(normed_groups, w_in, g_qa, g_ka, g_sgu, g_qm, *, n_tiles):
    gain = jnp.concatenate([jnp.tile(g_qa, H_A), jnp.tile(g_ka, H_A), jnp.ones((W_A + W_B,), F32),
                            g_sgu.reshape(W_B), jnp.tile(g_qm, H_M)])
    normed = [c // PROJ_TILE for lo, hi in ((0, COL_VA), (COL_V, COL_QM + W_M))
              for c in range(lo, hi, PROJ_TILE)]
    return _proj(normed_groups, None, w_in, gain, normed, n_tiles=n_tiles, chunk=512)


def kernel(x_prompt, x_sample, cache_win_k, cache_win_v, cache_mem_k, cache_mem_v, mem_prompt, rel_bias, g_ffn1, w1_gate, w1_up, w1_down, g_mix, w_in, g_qa, g_ka, g_sgu, w_sgu, b_sgu, g_qm, g_mem, w_mem_kv, g_km, g_mix_out, w_out, g_ffn2, w2_gate, w2_up, w2_down):
    batch, seq, _ = x_prompt.shape
    dec_batch, n_new, _ = x_sample.shape
    depth = g_ffn1.shape[0]
    n_p = batch * seq
    n_s = dec_batch * n_new

    y_p = x_prompt.reshape(n_p, D_MODEL)
    y_s = x_sample.reshape(n_s, D_MODEL)
    outs = [[] for _ in range(7)]
    for l in range(depth):
        h_p, h_s, hn_p, hn_s = _ffn([y_p, y_s], g_ffn1[l], w1_gate[l], w1_up[l], w1_down[l],
                                    g_next=g_mix[l], n_tiles=8, tf=256)
        z_p, z_s = _mixer_inputs([hn_p, hn_s], w_in[l], g_qa[l], g_ka[l], g_sgu[l], g_qm[l], n_tiles=4)

        zq, zk, zv, zu, zg, zm = (z_s[:, lo:hi] for lo, hi in zip(
            (0, COL_KA, COL_VA, COL_U, COL_V, COL_QM), (COL_KA, COL_VA, COL_U, COL_V, COL_QM, z_s.shape[1])))
        heads = (dec_batch, n_new, H_A, HEAD_DIM)
        oa_p, k_p, v_p, oa_s = _attention(z_p, zq.reshape(heads), zk.reshape(heads), zv.reshape(heads),
                                          cache_win_k[l], cache_win_v[l], rel_bias, batch=batch, seq=seq)

        mem = mem_prompt.reshape(batch * N_MEM, D_MODEL)
        gain_kv = jnp.concatenate([jnp.tile(g_km[l], H_M), jnp.ones((W_M,), F32)])
        mkv, = _proj([mem], g_mem[l], w_mem_kv[l], gain_kv, range(W_M // PROJ_TILE),
                     n_tiles=2, chunk=512)
        ob_p, om_p, ob_s, om_s = _side_mixers(
            z_p, mkv, zu.reshape(dec_batch, n_new, W_B), zg.reshape(dec_batch, n_new, W_B),
            zm.reshape(dec_batch, n_new, H_M, HEAD_DIM), cache_mem_k[l], cache_mem_v[l],
            w_sgu[l], b_sgu[l], batch=batch, seq=seq, n_steps=8)
        keep = min(DILATED[-1][0], seq)
        outs[0].append(k_p[:, seq - keep:])
        outs[1].append(v_p[:, seq - keep:])
        outs[2].append(mkv[:, :W_M].reshape(batch, N_MEM, H_M, HEAD_DIM))
        outs[3].append(mkv[:, W_M:].reshape(batch, N_MEM, H_M, HEAD_DIM))
        outs[4].append(zk.reshape(heads))
        outs[5].append(zv.reshape(heads))
        outs[6].append(zg.reshape(dec_batch, n_new, H_B, C_B))

        h_p, h_s = _outproj([(oa_p, ob_p, om_p, h_p),
                             (oa_s.reshape(n_s, W_A), ob_s.reshape(n_s, W_B), om_s.reshape(n_s, W_M), h_s)],
                            g_mix_out[l], w_out[l], n_tiles=4, tn=256)
        y_p, y_s = _ffn([h_p, h_s], g_ffn2[l], w2_gate[l], w2_up[l], w2_down[l], n_tiles=8, tf=256)

    return (y_p.reshape(batch, seq, D_MODEL), y_s.reshape(dec_batch, n_new, D_MODEL),
            *[jnp.stack(o) for o in outs])
```

```python
import functools

import numpy as np
import jax
import jax.numpy as jnp
from jax import lax
from jax.experimental import pallas as pl
from jax.experimental.pallas import tpu as pltpu

D_MODEL = 2048
HEAD_DIM = 128
H_A = 8
H_B = 4
C_B = 128
H_M = 4
W_A = H_A * HEAD_DIM
W_B = H_B * C_B
W_M = H_M * HEAD_DIM
D_FF = 5632
DILATED = ((128, 1), (512, 4), (2048, 16))
BLK = 128
CHUNK = 128
N_MEM = 256
N_BUCKETS = 32
REL_MAX_DIST = 2048
EPS = 1e-6
SCALE = HEAD_DIM ** -0.5
NEG = -1e30

LANES = 128
BF16_ROWS = 16
PROJ_TILE = 4 * HEAD_DIM
VMEM_LIMIT = 60 * 1024 * 1024

F32 = jnp.float32
BF16 = jnp.bfloat16


def _rel_bucket(dist):
    dist = np.asarray(dist, np.int64)
    max_exact = N_BUCKETS // 2
    large = max_exact + (np.log(np.maximum(dist, 1) / max_exact) / np.log(REL_MAX_DIST / max_exact)
                         * (N_BUCKETS - max_exact)).astype(np.int64)
    large = np.minimum(large, N_BUCKETS - 1)
    return np.where(dist < max_exact, dist, large).astype(np.int32)


def _rms(t, g):
    return t * lax.rsqrt(jnp.mean(t * t, axis=-1, keepdims=True) + EPS) * g


def _dot(a, b):
    return jnp.dot(a, b, preferred_element_type=F32)


def _dot_nt(a, b):
    return lax.dot_general(a, b, (((1,), (1,)), ((), ())), preferred_element_type=F32)


def _params(*sem):
    return pltpu.CompilerParams(dimension_semantics=sem, vmem_limit_bytes=VMEM_LIMIT)


def _tile_rows(groups, n_tiles):
    rows = [x.shape[0] // n_tiles for x in groups]
    assert all(r * n_tiles == x.shape[0] and r % BF16_ROWS == 0 for r, x in zip(rows, groups))
    return rows


def _bounds(refs):
    out, lo = [], 0
    for ref in refs:
        out.append((lo, lo + ref.shape[0]))
        lo += ref.shape[0]
    return out


def _chunk_edges(total, chunk):
    edges = list(range(0, total, chunk)) + [total]
    if len(edges) > 2 and edges[-1] - edges[-2] < chunk // 2:
        del edges[-2]
    return list(zip(edges[:-1], edges[1:]))


def _ffn_body(*refs, n_groups, with_next_norm):
    x_refs, refs = refs[:n_groups], refs[n_groups:]
    gn_ref = refs[0] if with_next_norm else None
    g_ref, wg_ref, wu_ref, wd_ref = refs[with_next_norm:with_next_norm + 4]
    refs = refs[with_next_norm + 4:]
    o_refs, on_refs, xn_ref = refs[:n_groups], refs[n_groups:-1], refs[-1]
    bounds = _bounds(x_refs)
    j = pl.program_id(1)

    @pl.when(j == 0)
    def _():
        for x_ref, o_ref, (lo, hi) in zip(x_refs, o_refs, bounds):
            x = x_ref[...]
            xn_ref[lo:hi, :] = _rms(x, g_ref[...]).astype(BF16)
            o_ref[...] = x

    xn = xn_ref[...]
    gate = _dot(xn, wg_ref[...].astype(BF16))
    up = _dot(xn, wu_ref[...].astype(BF16))
    act = (gate * jax.nn.sigmoid(gate)) * up * 0.5
    res = _dot(act.astype(BF16), wd_ref[...].astype(BF16))
    for o_ref, (lo, hi) in zip(o_refs, bounds):
        o_ref[...] += res[lo:hi]

    if with_next_norm:
        @pl.when(j == pl.num_programs(1) - 1)
        def _():
            for o_ref, on_ref in zip(o_refs, on_refs):
                on_ref[...] = _rms(o_ref[...], gn_ref[...]).astype(BF16)


def _ffn(groups, g, wg, wu, wd, *, g_next=None, n_tiles, tf):
    rows = _tile_rows(groups, n_tiles)
    with_next_norm = g_next is not None
    gain_spec = pl.BlockSpec((1, D_MODEL), lambda i, j: (0, 0), pipeline_mode=pl.Buffered(1))
    row_specs = [pl.BlockSpec((r, D_MODEL), lambda i, j: (i, 0)) for r in rows]
    out_shape = [jax.ShapeDtypeStruct(x.shape, F32) for x in groups]
    gains = [g.reshape(1, D_MODEL)]
    if with_next_norm:
        out_shape += [jax.ShapeDtypeStruct(x.shape, BF16) for x in groups]
        gains.insert(0, g_next.reshape(1, D_MODEL))
    return pl.pallas_call(
        functools.partial(_ffn_body, n_groups=len(groups), with_next_norm=with_next_norm),
        out_shape=out_shape,
        grid=(n_tiles, D_FF // tf),
        in_specs=(
            row_specs + [gain_spec] * len(gains)
            + [pl.BlockSpec((D_MODEL, tf), lambda i, j: (0, j)),
               pl.BlockSpec((D_MODEL, tf), lambda i, j: (0, j)),
               pl.BlockSpec((tf, D_MODEL), lambda i, j: (j, 0))]),
        out_specs=row_specs * (2 if with_next_norm else 1),
        scratch_shapes=[pltpu.VMEM((sum(rows), D_MODEL), BF16)],
        compiler_params=_params("arbitrary", "arbitrary"),
        name="ffn",
    )(*groups, *gains, wg, wu, wd)


def _proj_body(*refs, n_groups, pre_normed, norm_tiles, chunk):
    x_refs, refs = refs[:n_groups], refs[n_groups:]
    g_ref = None if pre_normed else refs[0]
    w_ref, gain_ref = refs[1 - pre_normed:3 - pre_normed]
    o_refs, xn_ref = refs[3 - pre_normed:-1], refs[-1]
    bounds = _bounds(x_refs)
    j = pl.program_id(1)

    @pl.when(j == 0)
    def _():
        for x_ref, (lo, hi) in zip(x_refs, bounds):
            x = x_ref[...]
            xn_ref[lo:hi, :] = x if pre_normed else _rms(x, g_ref[...]).astype(BF16)

    w = w_ref[...].astype(BF16)
    gain = gain_ref[...]
    normed = functools.reduce(jnp.logical_or, [j == t for t in norm_tiles])

    def finish(z):
        heads = []
        for hh in range(PROJ_TILE // HEAD_DIM):
            sl = slice(hh * HEAD_DIM, (hh + 1) * HEAD_DIM)
            zz = z[:, sl]
            scale = lax.rsqrt(jnp.mean(zz * zz, axis=-1, keepdims=True) + EPS)
            heads.append(zz * jnp.where(normed, scale, 1.0) * gain[:, sl])
        return jnp.concatenate(heads, axis=1)

    for c0, c1 in _chunk_edges(bounds[-1][1], chunk):
        z = finish(_dot(xn_ref[c0:c1, :], w))
        for o_ref, (lo, hi) in zip(o_refs, bounds):
            a, b = max(lo, c0), min(hi, c1)
            if a < b:
                o_ref[a - lo:b - lo, :] = z[a - c0:b - c0, :]


def _proj(groups, g, w, gain, norm_tiles, *, n_tiles, chunk):
    d_in, d_out = w.shape
    rows = _tile_rows(groups, n_tiles)
    pre_normed = g is None
    assert all(x.dtype == (BF16 if pre_normed else F32) for x in groups)
    once = pl.Buffered(1)
    norm_args = [] if pre_normed else [g.reshape(1, d_in)]
    return pl.pallas_call(
        functools.partial(_proj_body, n_groups=len(groups), pre_normed=pre_normed,
                          norm_tiles=tuple(norm_tiles), chunk=chunk),
        out_shape=[jax.ShapeDtypeStruct((x.shape[0], d_out), F32) for x in groups],
        grid=(n_tiles, d_out // PROJ_TILE),
        in_specs=(
            [pl.BlockSpec((r, d_in), lambda i, j: (i, 0)) for r in rows]
            + [pl.BlockSpec((1, d_in), lambda i, j: (0, 0), pipeline_mode=once)] * len(norm_args)
            + [pl.BlockSpec((d_in, PROJ_TILE), lambda i, j: (0, j)),
               pl.BlockSpec((1, PROJ_TILE), lambda i, j: (0, j))]),
        out_specs=[pl.BlockSpec((r, PROJ_TILE), lambda i, j: (i, j)) for r in rows],
        scratch_shapes=[pltpu.VMEM((sum(rows), d_in), BF16)],
        compiler_params=_params("arbitrary", "arbitrary"),
        name="proj",
    )(*groups, *norm_args, w, gain.reshape(1, d_out))


def _bias_body(idx_ref, tab_ref, o_ref):
    idx = idx_ref[...]
    tab = tab_ref[0]
    bias = jnp.full(idx.shape, NEG, F32)
    for bkt in range(N_BUCKETS):
        bias = jnp.where(idx == bkt, tab[:, bkt:bkt + 1], bias)
    o_ref[0] = bias


def _bias_lookup(idx, tab, *, n_col_tiles):
    n_rows, n_cols = idx.shape
    n_g, tab_rows, _ = tab.shape
    tc = n_cols // n_col_tiles
    assert tc * n_col_tiles == n_cols and tc % LANES == 0 and tab_rows in (1, n_rows)
    return pl.pallas_call(
        _bias_body,
        out_shape=jax.ShapeDtypeStruct((n_g, n_rows, n_cols), F32),
        grid=(n_g, n_col_tiles),
        in_specs=[pl.BlockSpec((n_rows, tc), lambda g, j: (0, j)),
                  pl.BlockSpec((1, tab_rows, N_BUCKETS), lambda g, j: (g, 0, 0))],
        out_specs=pl.BlockSpec((1, n_rows, tc), lambda g, j: (g, 0, j)),
        compiler_params=_params("parallel", "parallel"),
        name="bias_lookup",
    )(idx, tab)


SLABS = DILATED[1][1]
PIECE = BLK // SLABS


def _prompt_bias_index():
    step = np.arange(BLK)[:, None] + BLK - np.arange(2 * BLK)[None, :]
    perm = (SLABS * np.arange(PIECE)[None, :] + np.arange(SLABS)[:, None]).reshape(BLK)
    maps = []
    for win, dil in DILATED:
        n_steps = win // dil + 1
        band = (step >= 0) & (step < n_steps)
        bucket = _rel_bucket(np.clip(step, 0, n_steps - 1) * dil)
        idx = np.where(band, bucket, -1)
        if dil == 1:
            idx = idx[np.ix_(perm, np.concatenate([perm, BLK + perm]))]
        maps.append(idx)
    return np.stack(maps).astype(np.int32)


def _block_pieces(dil, r, n):
    if dil == 1:
        return [(s, slice(n * PIECE, (n + 1) * PIECE), PIECE) for s in range(SLABS)]
    if dil == SLABS:
        return [(r, slice(n * BLK, (n + 1) * BLK), BLK)]
    sub = dil // SLABS
    s, t = r % SLABS, r // SLABS
    return [(s, pl.ds(t + sub * BLK * n, BLK, stride=sub), BLK)]


def _attn_body(bias_s, q_ref, k_ref, v_ref, o_ref, kh_ref, vh_ref, slab_s, acc_s, m_s, l_s, *,
               seq, group_size):
    for part, ref in enumerate((q_ref, k_ref, v_ref)):
        for s in range(SLABS):
            slab_s[part, s] = ref[pl.ds(s, seq // SLABS, stride=SLABS), :]
    q_s, k_s, v_s = ([slab_s.at[part, s] for s in range(SLABS)] for part in range(3))

    def copy_by_head(c, n_chunks):
        n_rows = seq // n_chunks
        src = pl.ds(c * n_rows, n_rows)
        dst = pl.ds(pl.program_id(1) + H_A * c * n_rows, n_rows, stride=H_A)
        kh_ref[0, dst, :] = k_ref[src, :]
        vh_ref[0, dst, :] = v_ref[src, :]

    def load(slabs, pieces):
        parts = [slabs[s][idx, :] for s, idx, _ in pieces]
        return parts[0] if len(parts) == 1 else jnp.concatenate(parts, axis=0)

    def gather(slabs, blocks):
        return jnp.stack([load(slabs, pieces) for pieces in blocks]).astype(BF16)

    def qk(q, k):
        return jnp.einsum("gid,gjd->gij", q, k, preferred_element_type=F32)

    def pv(p, v):
        return jnp.einsum("gij,gjd->gid", p.astype(BF16), v, preferred_element_type=F32)

    def bias_of(dis, cols):
        return jnp.stack([bias_s[0, di, :, cols] for di in dis])

    def group(dis, curs, prevs):
        q = gather(q_s, curs)
        s_c = qk(q, gather(k_s, curs)) * SCALE + bias_of(dis, slice(BLK, None))
        if prevs is None:
            m = jnp.max(s_c, axis=-1, keepdims=True)
            p_c = jnp.exp(s_c - m)
            den = jnp.sum(p_c, axis=-1, keepdims=True)
            o = pv(p_c, gather(v_s, curs))
        else:
            s_p = qk(q, gather(k_s, prevs)) * SCALE + bias_of(dis, slice(0, BLK))
            m = jnp.max(jnp.maximum(s_c, s_p), axis=-1, keepdims=True)
            p_c = jnp.exp(s_c - m)
            p_p = jnp.exp(s_p - m)
            den = jnp.sum(p_c + p_p, axis=-1, keepdims=True)
            o = pv(p_c, gather(v_s, curs)) + pv(p_p, gather(v_s, prevs))
        for g, (di, pieces) in enumerate(zip(dis, curs)):
            m_g = jnp.broadcast_to(m[g], (BLK, HEAD_DIM))
            l_g = jnp.broadcast_to(den[g], (BLK, HEAD_DIM))
            lo = 0
            for s, idx, n_rows in pieces:
                acc_s[di, s, idx, :] = o[g, lo:lo + n_rows]
                m_s[di, s, idx, :] = m_g[lo:lo + n_rows]
                l_s[di, s, idx, :] = l_g[lo:lo + n_rows]
                lo += n_rows

    first, later = [], []
    for di, (win, dil) in enumerate(DILATED):
        for r in range(dil):
            first.append((di, dil, r, 0))
            later += [(di, dil, r, n) for n in range(1, seq // (dil * BLK))]
    groups = []
    for blocks, has_prev in ((first, False), (later, True)):
        for lo in range(0, len(blocks), group_size):
            part = blocks[lo:lo + group_size]
            groups.append(([di for di, _, _, _ in part],
                           [_block_pieces(dil, r, n) for _, dil, r, n in part],
                           [_block_pieces(dil, r, n - 1) for _, dil, r, n in part] if has_prev else None))
    n_chunks = 1 << (len(groups).bit_length() - 1)
    for c, args in enumerate(groups):
        group(*args)
        if c < n_chunks:
            copy_by_head(c, n_chunks)

    rows_per = 256
    per_slab = seq // SLABS // rows_per

    def merge(c, carry):
        s = c // per_slab
        r0 = pl.multiple_of((c % per_slab) * rows_per, rows_per)
        rc = pl.ds(r0, rows_per)
        ms = [m_s[di, s, rc, :] for di in range(len(DILATED))]
        mx = jnp.maximum(jnp.maximum(ms[0], ms[1]), ms[2])
        num = 0.0
        den = 0.0
        for di in range(len(DILATED)):
            wgt = jnp.exp(ms[di] - mx)
            num = num + wgt * acc_s[di, s, rc, :]
            den = den + wgt * l_s[di, s, rc, :]
        o_ref[pl.ds(SLABS * r0 + s, rows_per, stride=SLABS), :] = num / den
        return carry

    lax.fori_loop(0, SLABS * per_slab, merge, 0)


def _sgu_body(u_ref, v_ref, w_ref, bt_ref, o_ref, *, n_chunks):
    tri = (lax.broadcasted_iota(jnp.int32, (CHUNK, CHUNK), 0)
           >= lax.broadcasted_iota(jnp.int32, (CHUNK, CHUNK), 1))
    for g in range(H_B):
        w = jnp.where(tri, w_ref[g], 0.0).astype(BF16)
        bias = bt_ref[:, g:g + 1]
        cols = slice(g * C_B, (g + 1) * C_B)
        for c in range(n_chunks):
            rws = slice(c * CHUNK, (c + 1) * CHUNK)
            mixed = _dot(w, v_ref[rws, cols].astype(BF16)) + bias
            o_ref[rws, cols] = u_ref[rws, cols] * mixed


def _mem_body(q_ref, k_ref, v_ref, o_ref):
    for hh in range(H_M):
        sl = slice(hh * HEAD_DIM, (hh + 1) * HEAD_DIM)
        s = _dot_nt(q_ref[0, :, sl].astype(BF16), k_ref[0, :, sl].astype(BF16)) * SCALE
        m = jnp.max(s, axis=-1, keepdims=True)
        p = jnp.exp(s - m)
        den = jnp.sum(p, axis=-1, keepdims=True)
        o_ref[:, sl] = _dot(p.astype(BF16), v_ref[0, :, sl].astype(BF16)) / den


def _mems_body(q_ref, k_ref, v_ref, o_ref):
    for b in range(q_ref.shape[0]):
        s = _dot_nt(q_ref[b].astype(BF16), k_ref[b].astype(BF16)) * SCALE
        row = lax.broadcasted_iota(jnp.int32, s.shape, 0)
        col = lax.broadcasted_iota(jnp.int32, s.shape, 1)
        s = jnp.where(row % H_M == col % H_M, s, NEG)
        m = jnp.max(s, axis=-1, keepdims=True)
        p = jnp.exp(s - m)
        den = jnp.sum(p, axis=-1, keepdims=True)
        o_ref[b] = _dot(p.astype(BF16), v_ref[b].astype(BF16)) / den


def _outproj_body(*refs, n_groups):
    src_refs = [refs[4 * k:4 * k + 3] for k in range(n_groups)]
    h_refs = [refs[4 * k + 3] for k in range(n_groups)]
    g_ref, w_ref = refs[4 * n_groups:4 * n_groups + 2]
    o_refs = refs[4 * n_groups + 2:5 * n_groups + 2]
    cat_ref = refs[-1]
    bounds = _bounds(h_refs)

    @pl.when(pl.program_id(1) == 0)
    def _():
        for srcs, (r0, r1) in zip(src_refs, bounds):
            lo = 0
            for src in srcs:
                hi = lo + src.shape[1]
                x = src[...]
                cat_ref[r0:r1, lo:hi] = x if x.dtype == BF16 else _rms(x, g_ref[:, lo:hi]).astype(BF16)
                lo = hi

    res = _dot(cat_ref[...], w_ref[...].astype(BF16))
    for h_ref, o_ref, (r0, r1) in zip(h_refs, o_refs, bounds):
        o_ref[...] = h_ref[...] + res[r0:r1]


def _outproj(groups, g, w, *, n_tiles, tn):
    mix = W_A + W_B + W_M
    rows = _tile_rows([grp[3] for grp in groups], n_tiles)
    in_specs = []
    for r in rows:
        in_specs += [pl.BlockSpec((r, wd), lambda i, j: (i, 0)) for wd in (W_A, W_B, W_M)]
        in_specs += [pl.BlockSpec((r, tn), lambda i, j: (i, j))]
    in_specs += [pl.BlockSpec((1, mix), lambda i, j: (0, 0), pipeline_mode=pl.Buffered(1)),
                 pl.BlockSpec((mix, tn), lambda i, j: (0, j))]
    return pl.pallas_call(
        functools.partial(_outproj_body, n_groups=len(groups)),
        out_shape=[jax.ShapeDtypeStruct(grp[3].shape, F32) for grp in groups],
        grid=(n_tiles, D_MODEL // tn),
        in_specs=in_specs,
        out_specs=[pl.BlockSpec((r, tn), lambda i, j: (i, j)) for r in rows],
        scratch_shapes=[pltpu.VMEM((sum(rows), mix), BF16)],
        compiler_params=_params("arbitrary", "arbitrary"),
        name="outproj",
    )(*[a for grp in groups for a in grp], g.reshape(1, mix), w)


def _sample_bias_index(n_new, n_past):
    (win1, dil1), (tail, dil4), (win16, dil16) = DILATED
    assert dil1 == 1 and win1 == BLK and n_past == win16 and n_past % dil16 == 0
    assert n_past >= tail and n_new <= dil4 and n_new * H_A <= LANES
    n16 = n_past // dil16
    t = np.arange(n_new)[:, None]

    def per_head(index):
        same = np.eye(H_A, dtype=bool)[None, :, None, :]
        full = np.where(same, index[:, None, :, None], -1)
        return full.reshape(n_new * H_A, index.shape[1] * H_A)

    dist = BLK + t - np.arange(BLK)[None, :]
    d1 = np.where(dist <= win1, _rel_bucket(dist), -1)
    dist = tail + t - np.arange(tail)[None, :]
    d4 = np.where((dist % dil4 == 0) & (dist <= tail), _rel_bucket(dist), -1)
    m, tk = np.divmod(np.arange(n16 * n_new)[None, :], n_new)
    dist = n_past + t - dil16 * m - tk
    d16 = np.where(tk == t, _rel_bucket(np.maximum(dist, 0)), -1)
    s = np.arange(LANES // H_A)[None, :]
    dist = t - s
    new = [np.where((s < n_new) & (dist >= 0) & (dist % dil == 0),
                    _rel_bucket(np.maximum(dist, 0)), -1) for _, dil in DILATED]
    pieces = [d1, d4, d16] + new
    return np.concatenate([per_head(p) for p in pieces], axis=1).astype(np.int32)


def _sattn_body(bias_ref, q_ref, kn_ref, vn_ref, kt_ref, k16_ref, vt_ref, v16_ref, o_ref):
    def flat(ref):
        x = ref[0]
        return x.reshape(-1, HEAD_DIM).astype(BF16)

    q = q_ref[0].astype(BF16)
    kt, vt = flat(kt_ref), flat(vt_ref)
    k16, v16 = flat(k16_ref), flat(v16_ref)
    kn, vn = kn_ref[0].astype(BF16), vn_ref[0].astype(BF16)
    n_t, n_1 = kt.shape[0], BLK * H_A

    raw_t = _dot_nt(q, kt)
    raw_n = _dot_nt(q, kn)
    raws = [raw_t[:, n_t - n_1:], raw_t, _dot_nt(q, k16)] + [raw_n] * len(DILATED)
    logits = []
    lo = 0
    for raw in raws:
        hi = lo + raw.shape[1]
        logits.append(raw * SCALE + bias_ref[:, lo:hi])
        lo = hi

    m = functools.reduce(jnp.maximum, [jnp.max(x, axis=-1, keepdims=True) for x in logits])
    probs = [jnp.exp(x - m) for x in logits]
    den = functools.reduce(lambda a, b: a + b, [jnp.sum(p, axis=-1, keepdims=True) for p in probs])
    p_new = functools.reduce(lambda a, b: a + b, probs[3:])
    o = (_dot(probs[0].astype(BF16), vt[n_t - n_1:, :]) + _dot(probs[1].astype(BF16), vt)
         + _dot(probs[2].astype(BF16), v16) + _dot(p_new.astype(BF16), vn))
    o_ref[0] = o / den


N_PROMPT_IN, N_SAMPLE_IN, N_PROMPT_OUT = 4, 8, 3


def _attention_body(*refs, seq, group_size):
    prompt_in, refs = refs[:N_PROMPT_IN], refs[N_PROMPT_IN:]
    sample_in, refs = refs[:N_SAMPLE_IN], refs[N_SAMPLE_IN:]
    prompt_out, sample_out, scratch = refs[:N_PROMPT_OUT], refs[N_PROMPT_OUT], refs[N_PROMPT_OUT + 1:]
    _sattn_body(*sample_in, sample_out)
    _attn_body(*prompt_in, *prompt_out, *scratch, seq=seq, group_size=group_size)


def _attention(z, qa, ka, va, cache_k, cache_v, rel_bias, *, batch, seq):
    n = batch * seq
    dec_batch, n_new = qa.shape[:2]
    n_past = cache_k.shape[1]
    assert all(seq % (dil * BLK) == 0 for _, dil in DILATED) and dec_batch == batch * H_A

    idx = _prompt_bias_index()
    n_d = idx.shape[0]
    bias_p = _bias_lookup(jnp.asarray(idx.reshape(n_d * BLK, 2 * BLK)), rel_bias.T[:, None, :],
                          n_col_tiles=1)
    bias_p = bias_p.reshape(H_A, n_d, BLK, 2 * BLK)
    prompt_in_specs = [pl.BlockSpec((1, n_d, BLK, 2 * BLK), lambda b, h: (h, 0, 0, 0))]
    prompt_in_specs += [pl.BlockSpec((seq, HEAD_DIM), lambda b, h, part=part: (b, part * H_A + h))
                        for part in range(3)]
    by_head = pl.BlockSpec((1, seq * H_A, HEAD_DIM), lambda b, h: (b, 0, 0), pipeline_mode=pl.Buffered(1))
    by_head_shape = jax.ShapeDtypeStruct((batch, seq * H_A, HEAD_DIM), F32)
    stat = pltpu.VMEM((n_d, SLABS, seq // SLABS, HEAD_DIM), F32)

    tail = DILATED[1][0]
    dil16 = DILATED[2][1]
    n16 = n_past // dil16
    n_rows = n_new * H_A
    bias_s = _bias_lookup(jnp.asarray(_sample_bias_index(n_new, n_past)),
                          jnp.tile(rel_bias.T, (n_new, 1))[None], n_col_tiles=5)[0]

    def rows(x):
        x = x.reshape(dec_batch, n_rows, HEAD_DIM)
        return jnp.pad(x, ((0, 0), (0, LANES - n_rows), (0, 0)))

    def sample(b, h):
        return b * H_A + h

    k_res = cache_k.reshape(dec_batch, n16, dil16, H_A, HEAD_DIM)
    v_res = cache_v.reshape(dec_batch, n16, dil16, H_A, HEAD_DIM)
    q_spec = pl.BlockSpec((1, n_rows, HEAD_DIM), lambda b, h: (sample(b, h), 0, 0))
    new_spec = pl.BlockSpec((1, LANES, HEAD_DIM), lambda b, h: (sample(b, h), 0, 0))
    tail_spec = pl.BlockSpec((1, tail, H_A, HEAD_DIM),
                             lambda b, h: (sample(b, h), n_past // tail - 1, 0, 0))
    res_spec = pl.BlockSpec((1, n16, n_new, H_A, HEAD_DIM), lambda b, h: (sample(b, h), 0, 0, 0, 0))
    sample_in_specs = [pl.BlockSpec(bias_s.shape, lambda b, h: (0, 0), pipeline_mode=pl.Buffered(1)),
                       q_spec, new_spec, new_spec, tail_spec, res_spec, tail_spec, res_spec]
    assert len(prompt_in_specs) == N_PROMPT_IN and len(sample_in_specs) == N_SAMPLE_IN

    oa_p, kh, vh, oa_s = pl.pallas_call(
        functools.partial(_attention_body, seq=seq, group_size=32),
        out_shape=[jax.ShapeDtypeStruct((n, W_A), F32), by_head_shape, by_head_shape,
                   jax.ShapeDtypeStruct((dec_batch, n_rows, HEAD_DIM), F32)],
        grid=(batch, H_A),
        in_specs=prompt_in_specs + sample_in_specs,
        out_specs=[pl.BlockSpec((seq, HEAD_DIM), lambda b, h: (b, h)), by_head, by_head, q_spec],
        scratch_shapes=[stat, stat, stat, stat],
        compiler_params=_params("arbitrary", "arbitrary"),
        name="attention",
    )(bias_p, z, z, z,
      bias_s, qa.reshape(dec_batch, n_rows, HEAD_DIM), rows(ka), rows(va), cache_k, k_res, cache_v, v_res)
    heads = (batch, seq, H_A, HEAD_DIM)
    return oa_p, kh.reshape(heads), vh.reshape(heads), oa_s.reshape(dec_batch, n_new, W_A)


def _sgus_body(u_ref, v_ref, w_ref, b_ref, o_ref, *, n_new):
    t_idx = lax.broadcasted_iota(jnp.int32, (1, n_new, W_B), 1)
    mixed = 0.0
    for s in range(n_new):
        w_s = jnp.where(t_idx >= s, w_ref[s][None], 0.0).astype(BF16).astype(F32)
        mixed = mixed + w_s * v_ref[:, s:s + 1, :].astype(BF16).astype(F32)
    o_ref[...] = u_ref[...] * (mixed + b_ref[...][None])


def _side_body(u_ref, v_ref, w_ref, bt_ref, gb_ref, qm_ref, mk_ref, mv_ref, gm_ref,
               us_ref, vs_ref, wl_ref, bl_ref, qs_ref, ck_ref, cv_ref,
               obn_ref, omn_ref, obs_ref, oms_ref, ob_s, om_s, *, n_chunks, n_new):
    _mems_body(qs_ref, ck_ref, cv_ref, oms_ref)
    _sgus_body(us_ref, vs_ref, wl_ref, bl_ref, obs_ref, n_new=n_new)
    _mem_body(qm_ref, mk_ref, mv_ref, om_s)
    omn_ref[0] = _rms(om_s[...], gm_ref[...]).astype(BF16)
    _sgu_body(u_ref, v_ref, w_ref, bt_ref, ob_s, n_chunks=n_chunks)
    obn_ref[...] = _rms(ob_s[...], gb_ref[...]).astype(BF16)


def _side_mixers(z, mkv, u_s, v_s, qm_s, cache_k, cache_v, w_s, b_s, g_b, g_m, *, batch, seq, n_steps):
    dec_batch, n_new, _ = u_s.shape
    tm = batch * seq // n_steps
    tb = dec_batch // n_steps
    per_batch = seq // tm
    assert tm * per_batch == seq and tb * n_steps == dec_batch and tm % CHUNK == 0
    n_rows = n_new * H_M
    w_l = jnp.repeat(jnp.transpose(w_s[:, :n_new, :n_new], (2, 1, 0)), C_B, axis=-1)
    b_l = jnp.repeat(b_s[:, :n_new].T, C_B, axis=-1)
    z3 = z.reshape(batch, seq, -1)
    mkv3 = mkv.reshape(batch, N_MEM, 2 * W_M)
    once = pl.Buffered(1)
    sgu_s_spec = pl.BlockSpec((tb, n_new, W_B), lambda i: (i, 0, 0))
    qs_spec = pl.BlockSpec((tb, n_rows, HEAD_DIM), lambda i: (i, 0, 0))
    cache_spec = pl.BlockSpec((tb, N_MEM * H_M, HEAD_DIM), lambda i: (i, 0, 0))
    om_spec = pl.BlockSpec((1, tm, W_M), lambda i: (i // per_batch, i % per_batch, 0))
    gain_b = pl.BlockSpec((1, W_B), lambda i: (0, 0), pipeline_mode=once)
    gain_m = pl.BlockSpec((1, W_M), lambda i: (0, 0), pipeline_mode=once)
    obn, omn, ob_s, om_s = pl.pallas_call(
        functools.partial(_side_body, n_chunks=tm // CHUNK, n_new=n_new),
        out_shape=[jax.ShapeDtypeStruct((batch * seq, W_B), BF16),
                   jax.ShapeDtypeStruct((batch, seq, W_M), BF16),
                   jax.ShapeDtypeStruct(u_s.shape, F32),
                   jax.ShapeDtypeStruct((dec_batch, n_rows, HEAD_DIM), F32)],
        grid=(n_steps,),
        in_specs=[
            pl.BlockSpec((tm, W_B), lambda i: (i, COL_U // W_B)),
            pl.BlockSpec((tm, W_B), lambda i: (i, COL_V // W_B)),
            pl.BlockSpec((H_B, CHUNK, CHUNK), lambda i: (0, 0, 0), pipeline_mode=once),
            pl.BlockSpec((CHUNK, H_B), lambda i: (0, 0), pipeline_mode=once),
            gain_b,
            pl.BlockSpec((1, tm, W_M), lambda i: (i // per_batch, i % per_batch, COL_QM // W_M)),
            pl.BlockSpec((1, N_MEM, W_M), lambda i: (i // per_batch, 0, 0)),
            pl.BlockSpec((1, N_MEM, W_M), lambda i: (i // per_batch, 0, 1)),
            gain_m,
            sgu_s_spec, sgu_s_spec,
            pl.BlockSpec(w_l.shape, lambda i: (0, 0, 0), pipeline_mode=once),
            pl.BlockSpec(b_l.shape, lambda i: (0, 0), pipeline_mode=once),
            qs_spec, cache_spec, cache_spec,
        ],
        out_specs=[pl.BlockSpec((tm, W_B), lambda i: (i, 0)), om_spec, sgu_s_spec, qs_spec],
        scratch_shapes=[pltpu.VMEM((tm, W_B), F32), pltpu.VMEM((tm, W_M), F32)],
        compiler_params=_params("parallel"),
        name="side_mixers",
    )(z, z, w_s, b_s.T, g_b.reshape(1, W_B), z3, mkv3, mkv3, g_m.reshape(1, W_M), u_s, v_s, w_l, b_l,
      qm_s.reshape(dec_batch, n_rows, HEAD_DIM),
      cache_k.reshape(dec_batch, N_MEM * H_M, HEAD_DIM), cache_v.reshape(dec_batch, N_MEM * H_M, HEAD_DIM))
    return obn, omn.reshape(batch * seq, W_M), ob_s, om_s.reshape(dec_batch, n_new, W_M)


COL_KA, COL_VA, COL_U, COL_V, COL_QM = W_A, 2 * W_A, 3 * W_A, 3 * W_A + W_B, 3 * W_A + 2 * W_B


def _mixer_inputs(normed_groups, w_in, g_qa, g_ka, g_sgu, g_qm, *, n_tiles):
    gain = jnp.concatenate([jnp.tile(g_qa, H_A), jnp.tile(g_ka, H_A), jnp.ones((W_A + W_B,), F32),
                            g_sgu.reshape(W_B), jnp.tile(g_qm, H_M)])
    normed = [c // PROJ_TILE for lo, hi in ((0, COL_VA), (COL_V, COL_QM + W_M))
              for c in range(lo, hi, PROJ_TILE)]
    return _proj(normed_groups, None, w_in, gain, normed, n_tiles=n_tiles, chunk=256)


def kernel(x_prompt, x_sample, cache_win_k, cache_win_v, cache_mem_k, cache_mem_v, mem_prompt, rel_bias, g_ffn1, w1_gate, w1_up, w1_down, g_mix, w_in, g_qa, g_ka, g_sgu, w_sgu, b_sgu, g_qm, g_mem, w_mem_kv, g_km, g_mix_out, w_out, g_ffn2, w2_gate, w2_up, w2_down):
    batch, seq, _ = x_prompt.shape
    dec_batch, n_new, _ = x_sample.shape
    depth = g_ffn1.shape[0]
    n_p = batch * seq
    n_s = dec_batch * n_new

    y_p = x_prompt.reshape(n_p, D_MODEL)
    y_s = x_sample.reshape(n_s, D_MODEL)
    outs = [[] for _ in range(7)]
    for l in range(depth):
        h_p, h_s, hn_p, hn_s = _ffn([y_p, y_s], g_ffn1[l], w1_gate[l], w1_up[l], w1_down[l],
                                    g_next=g_mix[l], n_tiles=8, tf=256)
        z_p, z_s = _mixer_inputs([hn_p, hn_s], w_in[l], g_qa[l], g_ka[l], g_sgu[l], g_qm[l], n_tiles=4)

        zq, zk, zv, zu, zg, zm = (z_s[:, lo:hi] for lo, hi in zip(
            (0, COL_KA, COL_VA, COL_U, COL_V, COL_QM), (COL_KA, COL_VA, COL_U, COL_V, COL_QM, z_s.shape[1])))
        heads = (dec_batch, n_new, H_A, HEAD_DIM)
        oa_p, k_p, v_p, oa_s = _attention(z_p, zq.reshape(heads), zk.reshape(heads), zv.reshape(heads),
                                          cache_win_k[l], cache_win_v[l], rel_bias, batch=batch, seq=seq)

        mem = mem_prompt.reshape(batch * N_MEM, D_MODEL)
        gain_kv = jnp.concatenate([jnp.tile(g_km[l], H_M), jnp.ones((W_M,), F32)])
        mkv, = _proj([mem], g_mem[l], w_mem_kv[l], gain_kv, range(W_M // PROJ_TILE),
                     n_tiles=2, chunk=512)
        ob_p, om_p, ob_s, om_s = _side_mixers(
            z_p, mkv, zu.reshape(dec_batch, n_new, W_B), zg.reshape(dec_batch, n_new, W_B),
            zm.reshape(dec_batch, n_new, H_M, HEAD_DIM), cache_mem_k[l], cache_mem_v[l],
            w_sgu[l], b_sgu[l], g_mix_out[l][W_A:W_A + W_B], g_mix_out[l][W_A + W_B:],
            batch=batch, seq=seq, n_steps=8)
        keep = min(DILATED[-1][0], seq)
        outs[0].append(k_p[:, seq - keep:])
        outs[1].append(v_p[:, seq - keep:])
        outs[2].append(mkv[:, :W_M].reshape(batch, N_MEM, H_M, HEAD_DIM))
        outs[3].append(mkv[:, W_M:].reshape(batch, N_MEM, H_M, HEAD_DIM))
        outs[4].append(zk.reshape(heads))
        outs[5].append(zv.reshape(heads))
        outs[6].append(zg.reshape(dec_batch, n_new, H_B, C_B))

        h_p, h_s = _outproj([(oa_p, ob_p, om_p, h_p),
                             (oa_s.reshape(n_s, W_A), ob_s.reshape(n_s, W_B), om_s.reshape(n_s, W_M), h_s)],
                            g_mix_out[l], w_out[l], n_tiles=4, tn=256)
        y_p, y_s = _ffn([h_p, h_s], g_ffn2[l], w2_gate[l], w2_up[l], w2_down[l], n_tiles=8, tf=256)

    return (y_p.reshape(batch, seq, D_MODEL), y_s.reshape(dec_batch, n_new, D_MODEL),
            *[jnp.stack(o) for o in outs])
```

```python
import functools

import numpy as np
import jax
import jax.numpy as jnp
from jax import lax
from jax.experimental import pallas as pl
from jax.experimental.pallas import tpu as pltpu

D_MODEL = 2048
HEAD_DIM = 128
H_A = 8
H_B = 4
C_B = 128
H_M = 4
W_A = H_A * HEAD_DIM
W_B = H_B * C_B
W_M = H_M * HEAD_DIM
D_FF = 5632
DILATED = ((128, 1), (512, 4), (2048, 16))
BLK = 128
CHUNK = 128
N_MEM = 256
N_BUCKETS = 32
REL_MAX_DIST = 2048
EPS = 1e-6
SCALE = HEAD_DIM ** -0.5
NEG = -1e30

LANES = 128
BF16_ROWS = 16
PROJ_TILE = 4 * HEAD_DIM
VMEM_LIMIT = 60 * 1024 * 1024

F32 = jnp.float32
BF16 = jnp.bfloat16


def _rel_bucket(dist):
    dist = np.asarray(dist, np.int64)
    max_exact = N_BUCKETS // 2
    large = max_exact + (np.log(np.maximum(dist, 1) / max_exact) / np.log(REL_MAX_DIST / max_exact)
                         * (N_BUCKETS - max_exact)).astype(np.int64)
    large = np.minimum(large, N_BUCKETS - 1)
    return np.where(dist < max_exact, dist, large).astype(np.int32)


def _rms(t, g):
    return t * lax.rsqrt(jnp.mean(t * t, axis=-1, keepdims=True) + EPS) * g


def _dot(a, b):
    return jnp.dot(a, b, preferred_element_type=F32)


def _dot_nt(a, b):
    return lax.dot_general(a, b, (((1,), (1,)), ((), ())), preferred_element_type=F32)


def _params(*sem):
    return pltpu.CompilerParams(dimension_semantics=sem, vmem_limit_bytes=VMEM_LIMIT)


def _tile_rows(groups, n_tiles):
    rows = [x.shape[0] // n_tiles for x in groups]
    assert all(r * n_tiles == x.shape[0] and r % BF16_ROWS == 0 for r, x in zip(rows, groups))
    return rows


def _bounds(refs):
    out, lo = [], 0
    for ref in refs:
        out.append((lo, lo + ref.shape[0]))
        lo += ref.shape[0]
    return out


def _chunk_edges(total, chunk):
    edges = list(range(0, total, chunk)) + [total]
    if len(edges) > 2 and edges[-1] - edges[-2] < chunk // 2:
        del edges[-2]
    return list(zip(edges[:-1], edges[1:]))


def _ffn_body(*refs, n_groups, with_next_norm):
    x_refs, refs = refs[:n_groups], refs[n_groups:]
    gn_ref = refs[0] if with_next_norm else None
    g_ref, wg_ref, wu_ref, wd_ref = refs[with_next_norm:with_next_norm + 4]
    refs = refs[with_next_norm + 4:]
    o_refs, on_refs, xn_ref = refs[:n_groups], refs[n_groups:-1], refs[-1]
    bounds = _bounds(x_refs)
    j = pl.program_id(1)

    @pl.when(j == 0)
    def _():
        for x_ref, o_ref, (lo, hi) in zip(x_refs, o_refs, bounds):
            x = x_ref[...]
            xn_ref[lo:hi, :] = _rms(x, g_ref[...]).astype(BF16)
            o_ref[...] = x

    xn = xn_ref[...]
    gate = _dot(xn, wg_ref[...].astype(BF16))
    up = _dot(xn, wu_ref[...].astype(BF16))
    act = (gate * jax.nn.sigmoid(gate)) * up * 0.5
    res = _dot(act.astype(BF16), wd_ref[...].astype(BF16))
    for o_ref, (lo, hi) in zip(o_refs, bounds):
        o_ref[...] += res[lo:hi]

    if with_next_norm:
        @pl.when(j == pl.num_programs(1) - 1)
        def _():
            for o_ref, on_ref in zip(o_refs, on_refs):
                on_ref[...] = _rms(o_ref[...], gn_ref[...]).astype(BF16)


def _ffn(groups, g, wg, wu, wd, *, g_next=None, n_tiles, tf):
    rows = _tile_rows(groups, n_tiles)
    with_next_norm = g_next is not None
    gain_spec = pl.BlockSpec((1, D_MODEL), lambda i, j: (0, 0), pipeline_mode=pl.Buffered(1))
    row_specs = [pl.BlockSpec((r, D_MODEL), lambda i, j: (i, 0)) for r in rows]
    out_shape = [jax.ShapeDtypeStruct(x.shape, F32) for x in groups]
    gains = [g.reshape(1, D_MODEL)]
    if with_next_norm:
        out_shape += [jax.ShapeDtypeStruct(x.shape, BF16) for x in groups]
        gains.insert(0, g_next.reshape(1, D_MODEL))
    return pl.pallas_call(
        functools.partial(_ffn_body, n_groups=len(groups), with_next_norm=with_next_norm),
        out_shape=out_shape,
        grid=(n_tiles, D_FF // tf),
        in_specs=(
            row_specs + [gain_spec] * len(gains)
            + [pl.BlockSpec((D_MODEL, tf), lambda i, j: (0, j)),
               pl.BlockSpec((D_MODEL, tf), lambda i, j: (0, j)),
               pl.BlockSpec((tf, D_MODEL), lambda i, j: (j, 0))]),
        out_specs=row_specs * (2 if with_next_norm else 1),
        scratch_shapes=[pltpu.VMEM((sum(rows), D_MODEL), BF16)],
        compiler_params=_params("arbitrary", "arbitrary"),
        name="ffn",
    )(*groups, *gains, wg, wu, wd)


def _proj_body(*refs, n_groups, pre_normed, norm_tiles, chunk):
    x_refs, refs = refs[:n_groups], refs[n_groups:]
    g_ref = None if pre_normed else refs[0]
    w_ref, gain_ref = refs[1 - pre_normed:3 - pre_normed]
    o_refs, xn_ref = refs[3 - pre_normed:-1], refs[-1]
    bounds = _bounds(x_refs)
    j = pl.program_id(1)

    @pl.when(j == 0)
    def _():
        for x_ref, (lo, hi) in zip(x_refs, bounds):
            x = x_ref[...]
            xn_ref[lo:hi, :] = x if pre_normed else _rms(x, g_ref[...]).astype(BF16)

    w = w_ref[...].astype(BF16)
    gain = gain_ref[...]
    normed = functools.reduce(jnp.logical_or, [j == t for t in norm_tiles])

    def finish(z):
        heads = []
        for hh in range(PROJ_TILE // HEAD_DIM):
            sl = slice(hh * HEAD_DIM, (hh + 1) * HEAD_DIM)
            zz = z[:, sl]
            scale = lax.rsqrt(jnp.mean(zz * zz, axis=-1, keepdims=True) + EPS)
            heads.append(zz * jnp.where(normed, scale, 1.0) * gain[:, sl])
        return jnp.concatenate(heads, axis=1)

    for c0, c1 in _chunk_edges(bounds[-1][1], chunk):
        z = finish(_dot(xn_ref[c0:c1, :], w))
        for o_ref, (lo, hi) in zip(o_refs, bounds):
            a, b = max(lo, c0), min(hi, c1)
            if a < b:
                o_ref[a - lo:b - lo, :] = z[a - c0:b - c0, :]


def _proj(groups, g, w, gain, norm_tiles, *, n_tiles, chunk):
    d_in, d_out = w.shape
    rows = _tile_rows(groups, n_tiles)
    pre_normed = g is None
    assert all(x.dtype == (BF16 if pre_normed else F32) for x in groups)
    once = pl.Buffered(1)
    norm_args = [] if pre_normed else [g.reshape(1, d_in)]
    return pl.pallas_call(
        functools.partial(_proj_body, n_groups=len(groups), pre_normed=pre_normed,
                          norm_tiles=tuple(norm_tiles), chunk=chunk),
        out_shape=[jax.ShapeDtypeStruct((x.shape[0], d_out), F32) for x in groups],
        grid=(n_tiles, d_out // PROJ_TILE),
        in_specs=(
            [pl.BlockSpec((r, d_in), lambda i, j: (i, 0)) for r in rows]
            + [pl.BlockSpec((1, d_in), lambda i, j: (0, 0), pipeline_mode=once)] * len(norm_args)
            + [pl.BlockSpec((d_in, PROJ_TILE), lambda i, j: (0, j)),
               pl.BlockSpec((1, PROJ_TILE), lambda i, j: (0, j))]),
        out_specs=[pl.BlockSpec((r, PROJ_TILE), lambda i, j: (i, j)) for r in rows],
        scratch_shapes=[pltpu.VMEM((sum(rows), d_in), BF16)],
        compiler_params=_params("arbitrary", "arbitrary"),
        name="proj",
    )(*groups, *norm_args, w, gain.reshape(1, d_out))


def _bias_body(idx_ref, tab_ref, o_ref):
    idx = idx_ref[...]
    tab = tab_ref[0]
    bias = jnp.full(idx.shape, NEG, F32)
    for bkt in range(N_BUCKETS):
        bias = jnp.where(idx == bkt, tab[:, bkt:bkt + 1], bias)
    o_ref[0] = bias


def _bias_lookup(idx, tab, *, n_col_tiles):
    n_rows, n_cols = idx.shape
    n_g, tab_rows, _ = tab.shape
    tc = n_cols // n_col_tiles
    assert tc * n_col_tiles == n_cols and tc % LANES == 0 and tab_rows in (1, n_rows)
    return pl.pallas_call(
        _bias_body,
        out_shape=jax.ShapeDtypeStruct((n_g, n_rows, n_cols), F32),
        grid=(n_g, n_col_tiles),
        in_specs=[pl.BlockSpec((n_rows, tc), lambda g, j: (0, j)),
                  pl.BlockSpec((1, tab_rows, N_BUCKETS), lambda g, j: (g, 0, 0))],
        out_specs=pl.BlockSpec((1, n_rows, tc), lambda g, j: (g, 0, j)),
        compiler_params=_params("parallel", "parallel"),
        name="bias_lookup",
    )(idx, tab)


SLABS = DILATED[1][1]
PIECE = BLK // SLABS


def _prompt_bias_index():
    step = np.arange(BLK)[:, None] + BLK - np.arange(2 * BLK)[None, :]
    perm = (SLABS * np.arange(PIECE)[None, :] + np.arange(SLABS)[:, None]).reshape(BLK)
    maps = []
    for win, dil in DILATED:
        n_steps = win // dil + 1
        band = (step >= 0) & (step < n_steps)
        bucket = _rel_bucket(np.clip(step, 0, n_steps - 1) * dil)
        idx = np.where(band, bucket, -1)
        if dil == 1:
            idx = idx[np.ix_(perm, np.concatenate([perm, BLK + perm]))]
        maps.append(idx)
    return np.stack(maps).astype(np.int32)


def _block_pieces(dil, r, n):
    if dil == 1:
        return [(s, slice(n * PIECE, (n + 1) * PIECE), PIECE) for s in range(SLABS)]
    if dil == SLABS:
        return [(r, slice(n * BLK, (n + 1) * BLK), BLK)]
    sub = dil // SLABS
    s, t = r % SLABS, r // SLABS
    return [(s, pl.ds(t + sub * BLK * n, BLK, stride=sub), BLK)]


def _attn_body(bias_s, q_ref, k_ref, v_ref, o_ref, kh_ref, vh_ref, slab_s, acc_s, m_s, l_s, *,
               seq, group_size):
    for part, ref in enumerate((q_ref, k_ref, v_ref)):
        for s in range(SLABS):
            slab_s[part, s] = ref[pl.ds(s, seq // SLABS, stride=SLABS), :]
    q_s, k_s, v_s = ([slab_s.at[part, s] for s in range(SLABS)] for part in range(3))

    head_rows = pl.ds(pl.program_id(1), seq, stride=H_A)
    kh_ref[0, head_rows, :] = k_ref[...]
    vh_ref[0, head_rows, :] = v_ref[...]

    def load(slabs, pieces):
        parts = [slabs[s][idx, :] for s, idx, _ in pieces]
        return parts[0] if len(parts) == 1 else jnp.concatenate(parts, axis=0)

    def gather(slabs, blocks):
        return jnp.stack([load(slabs, pieces) for pieces in blocks]).astype(BF16)

    def qk(q, k):
        return jnp.einsum("gid,gjd->gij", q, k, preferred_element_type=F32)

    def pv(p, v):
        return jnp.einsum("gij,gjd->gid", p.astype(BF16), v, preferred_element_type=F32)

    def bias_of(dis, cols):
        return jnp.stack([bias_s[0, di, :, cols] for di in dis])

    def group(dis, curs, prevs):
        q = gather(q_s, curs)
        s_c = qk(q, gather(k_s, curs)) * SCALE + bias_of(dis, slice(BLK, None))
        if prevs is None:
            m = jnp.max(s_c, axis=-1, keepdims=True)
            p_c = jnp.exp(s_c - m)
            den = jnp.sum(p_c, axis=-1, keepdims=True)
            o = pv(p_c, gather(v_s, curs))
        else:
            s_p = qk(q, gather(k_s, prevs)) * SCALE + bias_of(dis, slice(0, BLK))
            m = jnp.max(jnp.maximum(s_c, s_p), axis=-1, keepdims=True)
            p_c = jnp.exp(s_c - m)
            p_p = jnp.exp(s_p - m)
            den = jnp.sum(p_c + p_p, axis=-1, keepdims=True)
            o = pv(p_c, gather(v_s, curs)) + pv(p_p, gather(v_s, prevs))
        for g, (di, pieces) in enumerate(zip(dis, curs)):
            m_g = jnp.broadcast_to(m[g], (BLK, HEAD_DIM))
            l_g = jnp.broadcast_to(den[g], (BLK, HEAD_DIM))
            lo = 0
            for s, idx, n_rows in pieces:
                acc_s[di, s, idx, :] = o[g, lo:lo + n_rows]
                m_s[di, s, idx, :] = m_g[lo:lo + n_rows]
                l_s[di, s, idx, :] = l_g[lo:lo + n_rows]
                lo += n_rows

    first, later = [], []
    for di, (win, dil) in enumerate(DILATED):
        for r in range(dil):
            first.append((di, dil, r, 0))
            later += [(di, dil, r, n) for n in range(1, seq // (dil * BLK))]
    groups = []
    for blocks, has_prev in ((first, False), (later, True)):
        for lo in range(0, len(blocks), group_size):
            part = blocks[lo:lo + group_size]
            groups.append(([di for di, _, _, _ in part],
                           [_block_pieces(dil, r, n) for _, dil, r, n in part],
                           [_block_pieces(dil, r, n - 1) for _, dil, r, n in part] if has_prev else None))
    for args in groups:
        group(*args)

    rows_per = 512
    per_slab = seq // SLABS // rows_per

    def merge(c, carry):
        s = c // per_slab
        r0 = pl.multiple_of((c % per_slab) * rows_per, rows_per)
        rc = pl.ds(r0, rows_per)
        ms = [m_s[di, s, rc, :] for di in range(len(DILATED))]
        mx = jnp.maximum(jnp.maximum(ms[0], ms[1]), ms[2])
        num = 0.0
        den = 0.0
        for di in range(len(DILATED)):
            wgt = jnp.exp(ms[di] - mx)
            num = num + wgt * acc_s[di, s, rc, :]
            den = den + wgt * l_s[di, s, rc, :]
        o_ref[pl.ds(SLABS * r0 + s, rows_per, stride=SLABS), :] = num / den
        return carry

    lax.fori_loop(0, SLABS * per_slab, merge, 0)


def _sgu_body(u_ref, v_ref, w_ref, bt_ref, o_ref, *, n_chunks):
    tri = (lax.broadcasted_iota(jnp.int32, (CHUNK, CHUNK), 0)
           >= lax.broadcasted_iota(jnp.int32, (CHUNK, CHUNK), 1))
    for g in range(H_B):
        w = jnp.where(tri, w_ref[g], 0.0).astype(BF16)
        bias = bt_ref[:, g:g + 1]
        cols = slice(g * C_B, (g + 1) * C_B)
        for c in range(n_chunks):
            rws = slice(c * CHUNK, (c + 1) * CHUNK)
            mixed = _dot(w, v_ref[rws, cols].astype(BF16)) + bias
            o_ref[rws, cols] = u_ref[rws, cols] * mixed


def _mem_body(q_ref, k_ref, v_ref, o_ref):
    for hh in range(H_M):
        sl = slice(hh * HEAD_DIM, (hh + 1) * HEAD_DIM)
        s = _dot_nt(q_ref[0, :, sl].astype(BF16), k_ref[0, :, sl].astype(BF16)) * SCALE
        m = jnp.max(s, axis=-1, keepdims=True)
        p = jnp.exp(s - m)
        den = jnp.sum(p, axis=-1, keepdims=True)
        o_ref[:, sl] = _dot(p.astype(BF16), v_ref[0, :, sl].astype(BF16)) / den


def _mems_body(q_ref, k_ref, v_ref, o_ref):
    for b in range(q_ref.shape[0]):
        s = _dot_nt(q_ref[b].astype(BF16), k_ref[b].astype(BF16)) * SCALE
        row = lax.broadcasted_iota(jnp.int32, s.shape, 0)
        col = lax.broadcasted_iota(jnp.int32, s.shape, 1)
        s = jnp.where(row % H_M == col % H_M, s, NEG)
        m = jnp.max(s, axis=-1, keepdims=True)
        p = jnp.exp(s - m)
        den = jnp.sum(p, axis=-1, keepdims=True)
        o_ref[b] = _dot(p.astype(BF16), v_ref[b].astype(BF16)) / den


def _outproj_body(*refs, n_groups):
    src_refs = [refs[4 * k:4 * k + 3] for k in range(n_groups)]
    h_refs = [refs[4 * k + 3] for k in range(n_groups)]
    g_ref, w_ref = refs[4 * n_groups:4 * n_groups + 2]
    o_refs = refs[4 * n_groups + 2:5 * n_groups + 2]
    cat_ref = refs[-1]
    bounds = _bounds(h_refs)

    @pl.when(pl.program_id(1) == 0)
    def _():
        for srcs, (r0, r1) in zip(src_refs, bounds):
            lo = 0
            for src in srcs:
                hi = lo + src.shape[1]
                x = src[...]
                cat_ref[r0:r1, lo:hi] = x if x.dtype == BF16 else _rms(x, g_ref[:, lo:hi]).astype(BF16)
                lo = hi

    res = _dot(cat_ref[...], w_ref[...].astype(BF16))
    for h_ref, o_ref, (r0, r1) in zip(h_refs, o_refs, bounds):
        o_ref[...] = h_ref[...] + res[r0:r1]


def _outproj(groups, g, w, *, n_tiles, tn):
    mix = W_A + W_B + W_M
    rows = _tile_rows([grp[3] for grp in groups], n_tiles)
    in_specs = []
    for r in rows:
        in_specs += [pl.BlockSpec((r, wd), lambda i, j: (i, 0)) for wd in (W_A, W_B, W_M)]
        in_specs += [pl.BlockSpec((r, tn), lambda i, j: (i, j))]
    in_specs += [pl.BlockSpec((1, mix), lambda i, j: (0, 0), pipeline_mode=pl.Buffered(1)),
                 pl.BlockSpec((mix, tn), lambda i, j: (0, j))]
    return pl.pallas_call(
        functools.partial(_outproj_body, n_groups=len(groups)),
        out_shape=[jax.ShapeDtypeStruct(grp[3].shape, F32) for grp in groups],
        grid=(n_tiles, D_MODEL // tn),
        in_specs=in_specs,
        out_specs=[pl.BlockSpec((r, tn), lambda i, j: (i, j)) for r in rows],
        scratch_shapes=[pltpu.VMEM((sum(rows), mix), BF16)],
        compiler_params=_params("arbitrary", "arbitrary"),
        name="outproj",
    )(*[a for grp in groups for a in grp], g.reshape(1, mix), w)


def _sample_bias_index(n_new, n_past):
    (win1, dil1), (tail, dil4), (win16, dil16) = DILATED
    assert dil1 == 1 and win1 == BLK and n_past == win16 and n_past % dil16 == 0
    assert n_past >= tail and n_new <= dil4 and n_new * H_A <= LANES
    n16 = n_past // dil16
    t = np.arange(n_new)[:, None]

    def per_head(index):
        same = np.eye(H_A, dtype=bool)[None, :, None, :]
        full = np.where(same, index[:, None, :, None], -1)
        return full.reshape(n_new * H_A, index.shape[1] * H_A)

    dist = BLK + t - np.arange(BLK)[None, :]
    d1 = np.where(dist <= win1, _rel_bucket(dist), -1)
    dist = tail + t - np.arange(tail)[None, :]
    d4 = np.where((dist % dil4 == 0) & (dist <= tail), _rel_bucket(dist), -1)
    m, tk = np.divmod(np.arange(n16 * n_new)[None, :], n_new)
    dist = n_past + t - dil16 * m - tk
    d16 = np.where(tk == t, _rel_bucket(np.maximum(dist, 0)), -1)
    s = np.arange(LANES // H_A)[None, :]
    dist = t - s
    new = [np.where((s < n_new) & (dist >= 0) & (dist % dil == 0),
                    _rel_bucket(np.maximum(dist, 0)), -1) for _, dil in DILATED]
    pieces = [d1, d4, d16] + new
    return np.concatenate([per_head(p) for p in pieces], axis=1).astype(np.int32)


def _sattn_body(bias_ref, q_ref, kn_ref, vn_ref, kt_ref, k16_ref, vt_ref, v16_ref, o_ref):
    def flat(ref):
        x = ref[0]
        return x.reshape(-1, HEAD_DIM).astype(BF16)

    q = q_ref[0].astype(BF16)
    kt, vt = flat(kt_ref), flat(vt_ref)
    k16, v16 = flat(k16_ref), flat(v16_ref)
    def new_rows(ref):
        x = ref[0]
        return jnp.concatenate([x, jnp.zeros((LANES - x.shape[0], HEAD_DIM), F32)], axis=0).astype(BF16)

    kn, vn = new_rows(kn_ref), new_rows(vn_ref)
    n_t, n_1 = kt.shape[0], BLK * H_A

    raw_t = _dot_nt(q, kt)
    raw_n = _dot_nt(q, kn)
    raws = [raw_t[:, n_t - n_1:], raw_t, _dot_nt(q, k16)] + [raw_n] * len(DILATED)
    logits = []
    lo = 0
    for raw in raws:
        hi = lo + raw.shape[1]
        logits.append(raw * SCALE + bias_ref[:, lo:hi])
        lo = hi

    m = functools.reduce(jnp.maximum, [jnp.max(x, axis=-1, keepdims=True) for x in logits])
    probs = [jnp.exp(x - m) for x in logits]
    den = functools.reduce(lambda a, b: a + b, [jnp.sum(p, axis=-1, keepdims=True) for p in probs])
    p_new = functools.reduce(lambda a, b: a + b, probs[3:])
    o = (_dot(probs[0].astype(BF16), vt[n_t - n_1:, :]) + _dot(probs[1].astype(BF16), vt)
         + _dot(probs[2].astype(BF16), v16) + _dot(p_new.astype(BF16), vn))
    o_ref[0] = o / den


N_PROMPT_IN, N_SAMPLE_IN, N_PROMPT_OUT = 4, 8, 3


def _attention_body(*refs, seq, group_size):
    prompt_in, refs = refs[:N_PROMPT_IN], refs[N_PROMPT_IN:]
    sample_in, refs = refs[:N_SAMPLE_IN], refs[N_SAMPLE_IN:]
    prompt_out, sample_out, scratch = refs[:N_PROMPT_OUT], refs[N_PROMPT_OUT], refs[N_PROMPT_OUT + 1:]
    _sattn_body(*sample_in, sample_out)
    _attn_body(*prompt_in, *prompt_out, *scratch, seq=seq, group_size=group_size)


def _attention(z, qa, ka, va, cache_k, cache_v, rel_bias, *, batch, seq):
    n = batch * seq
    dec_batch, n_new = qa.shape[:2]
    n_past = cache_k.shape[1]
    assert all(seq % (dil * BLK) == 0 for _, dil in DILATED) and dec_batch == batch * H_A

    idx = _prompt_bias_index()
    n_d = idx.shape[0]
    bias_p = _bias_lookup(jnp.asarray(idx.reshape(n_d * BLK, 2 * BLK)), rel_bias.T[:, None, :],
                          n_col_tiles=1)
    bias_p = bias_p.reshape(H_A, n_d, BLK, 2 * BLK)
    prompt_in_specs = [pl.BlockSpec((1, n_d, BLK, 2 * BLK), lambda b, h: (h, 0, 0, 0))]
    prompt_in_specs += [pl.BlockSpec((seq, HEAD_DIM), lambda b, h, part=part: (b, part * H_A + h))
                        for part in range(3)]
    by_head = pl.BlockSpec((1, seq * H_A, HEAD_DIM), lambda b, h: (b, 0, 0), pipeline_mode=pl.Buffered(1))
    by_head_shape = jax.ShapeDtypeStruct((batch, seq * H_A, HEAD_DIM), F32)
    stat = pltpu.VMEM((n_d, SLABS, seq // SLABS, HEAD_DIM), F32)

    tail = DILATED[1][0]
    dil16 = DILATED[2][1]
    n16 = n_past // dil16
    n_rows = n_new * H_A
    bias_s = _bias_lookup(jnp.asarray(_sample_bias_index(n_new, n_past)),
                          jnp.tile(rel_bias.T, (n_new, 1))[None], n_col_tiles=5)[0]

    def rows(x):
        return x.reshape(dec_batch, n_rows, HEAD_DIM)

    def sample(b, h):
        return b * H_A + h

    k_res = cache_k.reshape(dec_batch, n16, dil16, H_A, HEAD_DIM)
    v_res = cache_v.reshape(dec_batch, n16, dil16, H_A, HEAD_DIM)
    q_spec = pl.BlockSpec((1, n_rows, HEAD_DIM), lambda b, h: (sample(b, h), 0, 0))
    tail_spec = pl.BlockSpec((1, tail, H_A, HEAD_DIM),
                             lambda b, h: (sample(b, h), n_past // tail - 1, 0, 0))
    res_spec = pl.BlockSpec((1, n16, n_new, H_A, HEAD_DIM), lambda b, h: (sample(b, h), 0, 0, 0, 0))
    sample_in_specs = [pl.BlockSpec(bias_s.shape, lambda b, h: (0, 0), pipeline_mode=pl.Buffered(1)),
                       q_spec, q_spec, q_spec, tail_spec, res_spec, tail_spec, res_spec]
    assert len(prompt_in_specs) == N_PROMPT_IN and len(sample_in_specs) == N_SAMPLE_IN

    oa_p, kh, vh, oa_s = pl.pallas_call(
        functools.partial(_attention_body, seq=seq, group_size=32),
        out_shape=[jax.ShapeDtypeStruct((n, W_A), F32), by_head_shape, by_head_shape,
                   jax.ShapeDtypeStruct((dec_batch, n_rows, HEAD_DIM), F32)],
        grid=(batch, H_A),
        in_specs=prompt_in_specs + sample_in_specs,
        out_specs=[pl.BlockSpec((seq, HEAD_DIM), lambda b, h: (b, h)), by_head, by_head, q_spec],
        scratch_shapes=[stat, stat, stat, stat],
        compiler_params=_params("arbitrary", "arbitrary"),
        name="attention",
    )(bias_p, z, z, z,
      bias_s, rows(qa), rows(ka), rows(va), cache_k, k_res, cache_v, v_res)
    heads = (batch, seq, H_A, HEAD_DIM)
    return oa_p, kh.reshape(heads), vh.reshape(heads), oa_s.reshape(dec_batch, n_new, W_A)


def _sgus_body(u_ref, v_ref, w_ref, b_ref, o_ref, *, n_new):
    t_idx = lax.broadcasted_iota(jnp.int32, (1, n_new, W_B), 1)
    mixed = 0.0
    for s in range(n_new):
        w_s = jnp.where(t_idx >= s, w_ref[s][None], 0.0).astype(BF16).astype(F32)
        mixed = mixed + w_s * v_ref[:, s:s + 1, :].astype(BF16).astype(F32)
    o_ref[...] = u_ref[...] * (mixed + b_ref[...][None])


def _side_body(u_ref, v_ref, w_ref, bt_ref, gb_ref, qm_ref, mk_ref, mv_ref, gm_ref,
               us_ref, vs_ref, wl_ref, bl_ref, qs_ref, ck_ref, cv_ref,
               obn_ref, omn_ref, obs_ref, oms_ref, ob_s, om_s, *, n_chunks, n_new):
    _mems_body(qs_ref, ck_ref, cv_ref, oms_ref)
    _sgus_body(us_ref, vs_ref, wl_ref, bl_ref, obs_ref, n_new=n_new)
    _mem_body(qm_ref, mk_ref, mv_ref, om_s)
    omn_ref[0] = _rms(om_s[...], gm_ref[...]).astype(BF16)
    _sgu_body(u_ref, v_ref, w_ref, bt_ref, ob_s, n_chunks=n_chunks)
    obn_ref[...] = _rms(ob_s[...], gb_ref[...]).astype(BF16)


def _side_mixers(z, mkv, u_s, v_s, qm_s, cache_k, cache_v, w_s, b_s, g_b, g_m, *, batch, seq, n_steps):
    dec_batch, n_new, _ = u_s.shape
    tm = batch * seq // n_steps
    tb = dec_batch // n_steps
    per_batch = seq // tm
    assert tm * per_batch == seq and tb * n_steps == dec_batch and tm % CHUNK == 0
    n_rows = n_new * H_M
    w_l = jnp.repeat(jnp.transpose(w_s[:, :n_new, :n_new], (2, 1, 0)), C_B, axis=-1)
    b_l = jnp.repeat(b_s[:, :n_new].T, C_B, axis=-1)
    z3 = z.reshape(batch, seq, -1)
    mkv3 = mkv.reshape(batch, N_MEM, 2 * W_M)
    once = pl.Buffered(1)
    sgu_s_spec = pl.BlockSpec((tb, n_new, W_B), lambda i: (i, 0, 0))
    qs_spec = pl.BlockSpec((tb, n_rows, HEAD_DIM), lambda i: (i, 0, 0))
    cache_spec = pl.BlockSpec((tb, N_MEM * H_M, HEAD_DIM), lambda i: (i, 0, 0))
    om_spec = pl.BlockSpec((1, tm, W_M), lambda i: (i // per_batch, i % per_batch, 0))
    gain_b = pl.BlockSpec((1, W_B), lambda i: (0, 0), pipeline_mode=once)
    gain_m = pl.BlockSpec((1, W_M), lambda i: (0, 0), pipeline_mode=once)
    obn, omn, ob_s, om_s = pl.pallas_call(
        functools.partial(_side_body, n_chunks=tm // CHUNK, n_new=n_new),
        out_shape=[jax.ShapeDtypeStruct((batch * seq, W_B), BF16),
                   jax.ShapeDtypeStruct((batch, seq, W_M), BF16),
                   jax.ShapeDtypeStruct(u_s.shape, F32),
                   jax.ShapeDtypeStruct((dec_batch, n_rows, HEAD_DIM), F32)],
        grid=(n_steps,),
        in_specs=[
            pl.BlockSpec((tm, W_B), lambda i: (i, COL_U // W_B)),
            pl.BlockSpec((tm, W_B), lambda i: (i, COL_V // W_B)),
            pl.BlockSpec((H_B, CHUNK, CHUNK), lambda i: (0, 0, 0), pipeline_mode=once),
            pl.BlockSpec((CHUNK, H_B), lambda i: (0, 0), pipeline_mode=once),
            gain_b,
            pl.BlockSpec((1, tm, W_M), lambda i: (i // per_batch, i % per_batch, COL_QM // W_M)),
            pl.BlockSpec((1, N_MEM, W_M), lambda i: (i // per_batch, 0, 0)),
            pl.BlockSpec((1, N_MEM, W_M), lambda i: (i // per_batch, 0, 1)),
            gain_m,
            sgu_s_spec, sgu_s_spec,
            pl.BlockSpec(w_l.shape, lambda i: (0, 0, 0), pipeline_mode=once),
            pl.BlockSpec(b_l.shape, lambda i: (0, 0), pipeline_mode=once),
            qs_spec, cache_spec, cache_spec,
        ],
        out_specs=[pl.BlockSpec((tm, W_B), lambda i: (i, 0)), om_spec, sgu_s_spec, qs_spec],
        scratch_shapes=[pltpu.VMEM((tm, W_B), F32), pltpu.VMEM((tm, W_M), F32)],
        compiler_params=_params("parallel"),
        name="side_mixers",
    )(z, z, w_s, b_s.T, g_b.reshape(1, W_B), z3, mkv3, mkv3, g_m.reshape(1, W_M), u_s, v_s, w_l, b_l,
      qm_s.reshape(dec_batch, n_rows, HEAD_DIM),
      cache_k.reshape(dec_batch, N_MEM * H_M, HEAD_DIM), cache_v.reshape(dec_batch, N_MEM * H_M, HEAD_DIM))
    return obn, omn.reshape(batch * seq, W_M), ob_s, om_s.reshape(dec_batch, n_new, W_M)


COL_KA, COL_VA, COL_U, COL_V, COL_QM = W_A, 2 * W_A, 3 * W_A, 3 * W_A + W_B, 3 * W_A + 2 * W_B


def _mixer_inputs(normed_groups, w_in, g_qa, g_ka, g_sgu, g_qm, *, n_tiles):
    gain = jnp.concatenate([jnp.tile(g_qa, H_A), jnp.tile(g_ka, H_A), jnp.ones((W_A + W_B,), F32),
                            g_sgu.reshape(W_B), jnp.tile(g_qm, H_M)])
    normed = [c // PROJ_TILE for lo, hi in ((0, COL_VA), (COL_V, COL_QM + W_M))
              for c in range(lo, hi, PROJ_TILE)]
    return _proj(normed_groups, None, w_in, gain, normed, n_tiles=n_tiles, chunk=256)


def kernel(x_prompt, x_sample, cache_win_k, cache_win_v, cache_mem_k, cache_mem_v, mem_prompt, rel_bias, g_ffn1, w1_gate, w1_up, w1_down, g_mix, w_in, g_qa, g_ka, g_sgu, w_sgu, b_sgu, g_qm, g_mem, w_mem_kv, g_km, g_mix_out, w_out, g_ffn2, w2_gate, w2_up, w2_down):
    batch, seq, _ = x_prompt.shape
    dec_batch, n_new, _ = x_sample.shape
    depth = g_ffn1.shape[0]
    n_p = batch * seq
    n_s = dec_batch * n_new

    y_p = x_prompt.reshape(n_p, D_MODEL)
    y_s = x_sample.reshape(n_s, D_MODEL)
    outs = [[] for _ in range(7)]
    for l in range(depth):
        h_p, h_s, hn_p, hn_s = _ffn([y_p, y_s], g_ffn1[l], w1_gate[l], w1_up[l], w1_down[l],
                                    g_next=g_mix[l], n_tiles=8, tf=256)
        z_p, z_s = _mixer_inputs([hn_p, hn_s], w_in[l], g_qa[l], g_ka[l], g_sgu[l], g_qm[l], n_tiles=4)

        zq, zk, zv, zu, zg, zm = (z_s[:, lo:hi] for lo, hi in zip(
            (0, COL_KA, COL_VA, COL_U, COL_V, COL_QM), (COL_KA, COL_VA, COL_U, COL_V, COL_QM, z_s.shape[1])))
        heads = (dec_batch, n_new, H_A, HEAD_DIM)
        oa_p, k_p, v_p, oa_s = _attention(z_p, zq.reshape(heads), zk.reshape(heads), zv.reshape(heads),
                                          cache_win_k[l], cache_win_v[l], rel_bias, batch=batch, seq=seq)

        mem = mem_prompt.reshape(batch * N_MEM, D_MODEL)
        gain_kv = jnp.concatenate([jnp.tile(g_km[l], H_M), jnp.ones((W_M,), F32)])
        mkv, = _proj([mem], g_mem[l], w_mem_kv[l], gain_kv, range(W_M // PROJ_TILE),
                     n_tiles=2, chunk=512)
        ob_p, om_p, ob_s, om_s = _side_mixers(
            z_p, mkv, zu.reshape(dec_batch, n_new, W_B), zg.reshape(dec_batch, n_new, W_B),
            zm.reshape(dec_batch, n_new, H_M, HEAD_DIM), cache_mem_k[l], cache_mem_v[l],
            w_sgu[l], b_sgu[l], g_mix_out[l][W_A:W_A + W_B], g_mix_out[l][W_A + W_B:],
            batch=batch, seq=seq, n_steps=8)
        keep = min(DILATED[-1][0], seq)
        outs[0].append(k_p[:, seq - keep:])
        outs[1].append(v_p[:, seq - keep:])
        outs[2].append(mkv[:, :W_M].reshape(batch, N_MEM, H_M, HEAD_DIM))
        outs[3].append(mkv[:, W_M:].reshape(batch, N_MEM, H_M, HEAD_DIM))
        outs[4].append(zk.reshape(heads))
        outs[5].append(zv.reshape(heads))
        outs[6].append(zg.reshape(dec_batch, n_new, H_B, C_B))

        h_p, h_s = _outproj([(oa_p, ob_p, om_p, h_p),
                             (oa_s.reshape(n_s, W_A), ob_s.reshape(n_s, W_B), om_s.reshape(n_s, W_M), h_s)],
                            g_mix_out[l], w_out[l], n_tiles=4, tn=256)
        y_p, y_s = _ffn([h_p, h_s], g_ffn2[l], w2_gate[l], w2_up[l], w2_down[l], n_tiles=8, tf=256)

    return (y_p.reshape(batch, seq, D_MODEL), y_s.reshape(dec_batch, n_new, D_MODEL),
            *[jnp.stack(o) for o in outs])
```

```python
import functools

import numpy as np
import jax
import jax.numpy as jnp
from jax import lax
from jax.experimental import pallas as pl
from jax.experimental.pallas import tpu as pltpu

D_MODEL = 2048
HEAD_DIM = 128
H_A = 8
H_B = 4
C_B = 128
H_M = 4
W_A = H_A * HEAD_DIM
W_B = H_B * C_B
W_M = H_M * HEAD_DIM
D_FF = 5632
DILATED = ((128, 1), (512, 4), (2048, 16))
BLK = 128
CHUNK = 128
N_MEM = 256
N_BUCKETS = 32
REL_MAX_DIST = 2048
EPS = 1e-6
SCALE = HEAD_DIM ** -0.5
NEG = -1e30

LANES = 128
BF16_ROWS = 16
PROJ_TILE = 4 * HEAD_DIM
VMEM_LIMIT = 60 * 1024 * 1024

F32 = jnp.float32
BF16 = jnp.bfloat16


def _rel_bucket(dist):
    dist = np.asarray(dist, np.int64)
    max_exact = N_BUCKETS // 2
    large = max_exact + (np.log(np.maximum(dist, 1) / max_exact) / np.log(REL_MAX_DIST / max_exact)
                         * (N_BUCKETS - max_exact)).astype(np.int64)
    large = np.minimum(large, N_BUCKETS - 1)
    return np.where(dist < max_exact, dist, large).astype(np.int32)


def _rms(t, g):
    return t * lax.rsqrt(jnp.mean(t * t, axis=-1, keepdims=True) + EPS) * g


def _dot(a, b):
    return jnp.dot(a, b, preferred_element_type=F32)


def _dot_nt(a, b):
    return lax.dot_general(a, b, (((1,), (1,)), ((), ())), preferred_element_type=F32)


def _params(*sem):
    return pltpu.CompilerParams(dimension_semantics=sem, vmem_limit_bytes=VMEM_LIMIT)


def _tile_rows(groups, n_tiles):
    rows = [x.shape[0] // n_tiles for x in groups]
    assert all(r * n_tiles == x.shape[0] and r % BF16_ROWS == 0 for r, x in zip(rows, groups))
    return rows


def _bounds(refs):
    out, lo = [], 0
    for ref in refs:
        out.append((lo, lo + ref.shape[0]))
        lo += ref.shape[0]
    return out


def _chunk_edges(total, chunk):
    edges = list(range(0, total, chunk)) + [total]
    if len(edges) > 2 and edges[-1] - edges[-2] < chunk // 2:
        del edges[-2]
    return list(zip(edges[:-1], edges[1:]))


def _ffn_body(*refs, n_groups, with_next_norm):
    x_refs, refs = refs[:n_groups], refs[n_groups:]
    gn_ref = refs[0] if with_next_norm else None
    g_ref, wg_ref, wu_ref, wd_ref = refs[with_next_norm:with_next_norm + 4]
    refs = refs[with_next_norm + 4:]
    o_refs, on_refs, xn_ref = refs[:n_groups], refs[n_groups:-1], refs[-1]
    bounds = _bounds(x_refs)
    j = pl.program_id(1)

    @pl.when(j == 0)
    def _():
        for x_ref, o_ref, (lo, hi) in zip(x_refs, o_refs, bounds):
            x = x_ref[...]
            xn_ref[lo:hi, :] = _rms(x, g_ref[...]).astype(BF16)
            o_ref[...] = x

    xn = xn_ref[...]
    gate = _dot(xn, wg_ref[...].astype(BF16))
    up = _dot(xn, wu_ref[...].astype(BF16))
    act = (gate * jax.nn.sigmoid(gate)) * up * 0.5
    res = _dot(act.astype(BF16), wd_ref[...].astype(BF16))
    for o_ref, (lo, hi) in zip(o_refs, bounds):
        o_ref[...] += res[lo:hi]

    if with_next_norm:
        @pl.when(j == pl.num_programs(1) - 1)
        def _():
            for o_ref, on_ref in zip(o_refs, on_refs):
                on_ref[...] = _rms(o_ref[...], gn_ref[...]).astype(BF16)


def _ffn(groups, g, wg, wu, wd, *, g_next=None, n_tiles, tf):
    rows = _tile_rows(groups, n_tiles)
    with_next_norm = g_next is not None
    gain_spec = pl.BlockSpec((1, D_MODEL), lambda i, j: (0, 0), pipeline_mode=pl.Buffered(1))
    row_specs = [pl.BlockSpec((r, D_MODEL), lambda i, j: (i, 0)) for r in rows]
    out_shape = [jax.ShapeDtypeStruct(x.shape, F32) for x in groups]
    gains = [g.reshape(1, D_MODEL)]
    if with_next_norm:
        out_shape += [jax.ShapeDtypeStruct(x.shape, BF16) for x in groups]
        gains.insert(0, g_next.reshape(1, D_MODEL))
    return pl.pallas_call(
        functools.partial(_ffn_body, n_groups=len(groups), with_next_norm=with_next_norm),
        out_shape=out_shape,
        grid=(n_tiles, D_FF // tf),
        in_specs=(
            row_specs + [gain_spec] * len(gains)
            + [pl.BlockSpec((D_MODEL, tf), lambda i, j: (0, j)),
               pl.BlockSpec((D_MODEL, tf), lambda i, j: (0, j)),
               pl.BlockSpec((tf, D_MODEL), lambda i, j: (j, 0))]),
        out_specs=row_specs * (2 if with_next_norm else 1),
        scratch_shapes=[pltpu.VMEM((sum(rows), D_MODEL), BF16)],
        compiler_params=_params("arbitrary", "arbitrary"),
        name="ffn",
    )(*groups, *gains, wg, wu, wd)


def _proj_body(*refs, n_groups, pre_normed, norm_tiles, chunk):
    x_refs, refs = refs[:n_groups], refs[n_groups:]
    g_ref = None if pre_normed else refs[0]
    w_ref, gain_ref = refs[1 - pre_normed:3 - pre_normed]
    o_refs, xn_ref = refs[3 - pre_normed:-1], refs[-1]
    bounds = _bounds(x_refs)
    j = pl.program_id(1)

    @pl.when(j == 0)
    def _():
        for x_ref, (lo, hi) in zip(x_refs, bounds):
            x = x_ref[...]
            xn_ref[lo:hi, :] = x if pre_normed else _rms(x, g_ref[...]).astype(BF16)

    w = w_ref[...].astype(BF16)
    gain = gain_ref[...]
    normed = functools.reduce(jnp.logical_or, [j == t for t in norm_tiles])

    def finish(z):
        heads = []
        for hh in range(PROJ_TILE // HEAD_DIM):
            sl = slice(hh * HEAD_DIM, (hh + 1) * HEAD_DIM)
            zz = z[:, sl]
            scale = lax.rsqrt(jnp.mean(zz * zz, axis=-1, keepdims=True) + EPS)
            heads.append(zz * jnp.where(normed, scale, 1.0) * gain[:, sl])
        return jnp.concatenate(heads, axis=1)

    for c0, c1 in _chunk_edges(bounds[-1][1], chunk):
        z = finish(_dot(xn_ref[c0:c1, :], w))
        for o_ref, (lo, hi) in zip(o_refs, bounds):
            a, b = max(lo, c0), min(hi, c1)
            if a < b:
                o_ref[a - lo:b - lo, :] = z[a - c0:b - c0, :]


def _proj(groups, g, w, gain, norm_tiles, *, n_tiles, chunk):
    d_in, d_out = w.shape
    rows = _tile_rows(groups, n_tiles)
    pre_normed = g is None
    assert all(x.dtype == (BF16 if pre_normed else F32) for x in groups)
    once = pl.Buffered(1)
    norm_args = [] if pre_normed else [g.reshape(1, d_in)]
    return pl.pallas_call(
        functools.partial(_proj_body, n_groups=len(groups), pre_normed=pre_normed,
                          norm_tiles=tuple(norm_tiles), chunk=chunk),
        out_shape=[jax.ShapeDtypeStruct((x.shape[0], d_out), F32) for x in groups],
        grid=(n_tiles, d_out // PROJ_TILE),
        in_specs=(
            [pl.BlockSpec((r, d_in), lambda i, j: (i, 0)) for r in rows]
            + [pl.BlockSpec((1, d_in), lambda i, j: (0, 0), pipeline_mode=once)] * len(norm_args)
            + [pl.BlockSpec((d_in, PROJ_TILE), lambda i, j: (0, j)),
               pl.BlockSpec((1, PROJ_TILE), lambda i, j: (0, j))]),
        out_specs=[pl.BlockSpec((r, PROJ_TILE), lambda i, j: (i, j)) for r in rows],
        scratch_shapes=[pltpu.VMEM((sum(rows), d_in), BF16)],
        compiler_params=_params("arbitrary", "arbitrary"),
        name="proj",
    )(*groups, *norm_args, w, gain.reshape(1, d_out))


def _bias_body(idx_ref, tab_ref, o_ref):
    idx = idx_ref[...]
    tab = tab_ref[0]
    bias = jnp.full(idx.shape, NEG, F32)
    for bkt in range(N_BUCKETS):
        bias = jnp.where(idx == bkt, tab[:, bkt:bkt + 1], bias)
    o_ref[0] = bias


def _bias_lookup(idx, tab, *, n_col_tiles):
    n_rows, n_cols = idx.shape
    n_g, tab_rows, _ = tab.shape
    tc = n_cols // n_col_tiles
    assert tc * n_col_tiles == n_cols and tc % LANES == 0 and tab_rows in (1, n_rows)
    return pl.pallas_call(
        _bias_body,
        out_shape=jax.ShapeDtypeStruct((n_g, n_rows, n_cols), F32),
        grid=(n_g, n_col_tiles),
        in_specs=[pl.BlockSpec((n_rows, tc), lambda g, j: (0, j)),
                  pl.BlockSpec((1, tab_rows, N_BUCKETS), lambda g, j: (g, 0, 0))],
        out_specs=pl.BlockSpec((1, n_rows, tc), lambda g, j: (g, 0, j)),
        compiler_params=_params("parallel", "parallel"),
        name="bias_lookup",
    )(idx, tab)


SLABS = DILATED[1][1]
PIECE = BLK // SLABS


def _prompt_bias_index():
    step = np.arange(BLK)[:, None] + BLK - np.arange(2 * BLK)[None, :]
    perm = (SLABS * np.arange(PIECE)[None, :] + np.arange(SLABS)[:, None]).reshape(BLK)
    maps = []
    for win, dil in DILATED:
        n_steps = win // dil + 1
        band = (step >= 0) & (step < n_steps)
        bucket = _rel_bucket(np.clip(step, 0, n_steps - 1) * dil)
        idx = np.where(band, bucket, -1)
        if dil == 1:
            idx = idx[np.ix_(perm, np.concatenate([perm, BLK + perm]))]
        maps.append(idx)
    return np.stack(maps).astype(np.int32)


def _block_pieces(dil, r, n):
    if dil == 1:
        return [(s, slice(n * PIECE, (n + 1) * PIECE), PIECE) for s in range(SLABS)]
    if dil == SLABS:
        return [(r, slice(n * BLK, (n + 1) * BLK), BLK)]
    sub = dil // SLABS
    s, t = r % SLABS, r // SLABS
    return [(s, pl.ds(t + sub * BLK * n, BLK, stride=sub), BLK)]


def _attn_body(bias_s, q_ref, k_ref, v_ref, o_ref, kh_ref, vh_ref, slab_s, acc_s, m_s, l_s, *,
               seq, group_size):
    for part, ref in enumerate((q_ref, k_ref, v_ref)):
        for s in range(SLABS):
            slab_s[part, s] = ref[pl.ds(s, seq // SLABS, stride=SLABS), :]
    q_s, k_s, v_s = ([slab_s.at[part, s] for s in range(SLABS)] for part in range(3))

    head_rows = pl.ds(pl.program_id(1), seq, stride=H_A)
    kh_ref[0, head_rows, :] = k_ref[...]
    vh_ref[0, head_rows, :] = v_ref[...]

    def load(slabs, pieces):
        parts = [slabs[s][idx, :] for s, idx, _ in pieces]
        return parts[0] if len(parts) == 1 else jnp.concatenate(parts, axis=0)

    def gather(slabs, blocks):
        return jnp.stack([load(slabs, pieces) for pieces in blocks]).astype(BF16)

    def qk(q, k):
        return jnp.einsum("gid,gjd->gij", q, k, preferred_element_type=F32)

    def pv(p, v):
        return jnp.einsum("gij,gjd->gid", p.astype(BF16), v, preferred_element_type=F32)

    def bias_of(dis, cols):
        return jnp.stack([bias_s[0, di, :, cols] for di in dis])

    def group(dis, curs, prevs):
        q = gather(q_s, curs)
        s_c = qk(q, gather(k_s, curs)) * SCALE + bias_of(dis, slice(BLK, None))
        if prevs is None:
            m = jnp.max(s_c, axis=-1, keepdims=True)
            p_c = jnp.exp(s_c - m)
            den = jnp.sum(p_c, axis=-1, keepdims=True)
            o = pv(p_c, gather(v_s, curs))
        else:
            s_p = qk(q, gather(k_s, prevs)) * SCALE + bias_of(dis, slice(0, BLK))
            m = jnp.max(jnp.maximum(s_c, s_p), axis=-1, keepdims=True)
            p_c = jnp.exp(s_c - m)
            p_p = jnp.exp(s_p - m)
            den = jnp.sum(p_c + p_p, axis=-1, keepdims=True)
            o = pv(p_c, gather(v_s, curs)) + pv(p_p, gather(v_s, prevs))
        for g, (di, pieces) in enumerate(zip(dis, curs)):
            m_g = jnp.broadcast_to(m[g], (BLK, HEAD_DIM))
            l_g = jnp.broadcast_to(den[g], (BLK, HEAD_DIM))
            lo = 0
            for s, idx, n_rows in pieces:
                acc_s[di, s, idx, :] = o[g, lo:lo + n_rows]
                m_s[di, s, idx, :] = m_g[lo:lo + n_rows]
                l_s[di, s, idx, :] = l_g[lo:lo + n_rows]
                lo += n_rows

    first, later = [], []
    for di, (win, dil) in enumerate(DILATED):
        for r in range(dil):
            first.append((di, dil, r, 0))
            later += [(di, dil, r, n) for n in range(1, seq // (dil * BLK))]
    groups = []
    for blocks, has_prev in ((first, False), (later, True)):
        for lo in range(0, len(blocks), group_size):
            part = blocks[lo:lo + group_size]
            groups.append(([di for di, _, _, _ in part],
                           [_block_pieces(dil, r, n) for _, dil, r, n in part],
                           [_block_pieces(dil, r, n - 1) for _, dil, r, n in part] if has_prev else None))
    for args in groups:
        group(*args)

    rows_per = 256
    per_slab = seq // SLABS // rows_per

    def merge(c, carry):
        s = c // per_slab
        r0 = pl.multiple_of((c % per_slab) * rows_per, rows_per)
        rc = pl.ds(r0, rows_per)
        ms = [m_s[di, s, rc, :] for di in range(len(DILATED))]
        mx = jnp.maximum(jnp.maximum(ms[0], ms[1]), ms[2])
        num = 0.0
        den = 0.0
        for di in range(len(DILATED)):
            wgt = jnp.exp(ms[di] - mx)
            num = num + wgt * acc_s[di, s, rc, :]
            den = den + wgt * l_s[di, s, rc, :]
        o_ref[pl.ds(SLABS * r0 + s, rows_per, stride=SLABS), :] = num / den
        return carry

    lax.fori_loop(0, SLABS * per_slab, merge, 0)


def _sgu_body(u_ref, v_ref, w_ref, bt_ref, o_ref, *, n_chunks):
    tri = (lax.broadcasted_iota(jnp.int32, (CHUNK, CHUNK), 0)
           >= lax.broadcasted_iota(jnp.int32, (CHUNK, CHUNK), 1))
    for g in range(H_B):
        w = jnp.where(tri, w_ref[g], 0.0).astype(BF16)
        bias = bt_ref[:, g:g + 1]
        cols = slice(g * C_B, (g + 1) * C_B)
        for c in range(n_chunks):
            rws = slice(c * CHUNK, (c + 1) * CHUNK)
            mixed = _dot(w, v_ref[rws, cols].astype(BF16)) + bias
            o_ref[rws, cols] = u_ref[rws, cols] * mixed


def _mem_body(q_ref, k_ref, v_ref, o_ref):
    for hh in range(H_M):
        sl = slice(hh * HEAD_DIM, (hh + 1) * HEAD_DIM)
        s = _dot_nt(q_ref[0, :, sl].astype(BF16), k_ref[0, :, sl].astype(BF16)) * SCALE
        m = jnp.max(s, axis=-1, keepdims=True)
        p = jnp.exp(s - m)
        den = jnp.sum(p, axis=-1, keepdims=True)
        o_ref[:, sl] = _dot(p.astype(BF16), v_ref[0, :, sl].astype(BF16)) / den


def _mems_body(q_ref, k_ref, v_ref, o_ref):
    for b in range(q_ref.shape[0]):
        s = _dot_nt(q_ref[b].astype(BF16), k_ref[b].astype(BF16)) * SCALE
        row = lax.broadcasted_iota(jnp.int32, s.shape, 0)
        col = lax.broadcasted_iota(jnp.int32, s.shape, 1)
        s = jnp.where(row % H_M == col % H_M, s, NEG)
        m = jnp.max(s, axis=-1, keepdims=True)
        p = jnp.exp(s - m)
        den = jnp.sum(p, axis=-1, keepdims=True)
        o_ref[b] = _dot(p.astype(BF16), v_ref[b].astype(BF16)) / den


def _outproj_body(*refs, n_groups):
    src_refs = [refs[4 * k:4 * k + 3] for k in range(n_groups)]
    h_refs = [refs[4 * k + 3] for k in range(n_groups)]
    g_ref, w_ref = refs[4 * n_groups:4 * n_groups + 2]
    o_refs = refs[4 * n_groups + 2:5 * n_groups + 2]
    cat_ref = refs[-1]
    bounds = _bounds(h_refs)

    @pl.when(pl.program_id(1) == 0)
    def _():
        for srcs, (r0, r1) in zip(src_refs, bounds):
            lo = 0
            for src in srcs:
                hi = lo + src.shape[1]
                x = src[...]
                cat_ref[r0:r1, lo:hi] = x if x.dtype == BF16 else _rms(x, g_ref[:, lo:hi]).astype(BF16)
                lo = hi

    res = _dot(cat_ref[...], w_ref[...].astype(BF16))
    for h_ref, o_ref, (r0, r1) in zip(h_refs, o_refs, bounds):
        o_ref[...] = h_ref[...] + res[r0:r1]


def _outproj(groups, g, w, *, n_tiles, tn):
    mix = W_A + W_B + W_M
    rows = _tile_rows([grp[3] for grp in groups], n_tiles)
    in_specs = []
    for r in rows:
        in_specs += [pl.BlockSpec((r, wd), lambda i, j: (i, 0)) for wd in (W_A, W_B, W_M)]
        in_specs += [pl.BlockSpec((r, tn), lambda i, j: (i, j))]
    in_specs += [pl.BlockSpec((1, mix), lambda i, j: (0, 0), pipeline_mode=pl.Buffered(1)),
                 pl.BlockSpec((mix, tn), lambda i, j: (0, j))]
    return pl.pallas_call(
        functools.partial(_outproj_body, n_groups=len(groups)),
        out_shape=[jax.ShapeDtypeStruct(grp[3].shape, F32) for grp in groups],
        grid=(n_tiles, D_MODEL // tn),
        in_specs=in_specs,
        out_specs=[pl.BlockSpec((r, tn), lambda i, j: (i, j)) for r in rows],
        scratch_shapes=[pltpu.VMEM((sum(rows), mix), BF16)],
        compiler_params=_params("arbitrary", "arbitrary"),
        name="outproj",
    )(*[a for grp in groups for a in grp], g.reshape(1, mix), w)


def _sample_bias_index(n_new, n_past):
    (win1, dil1), (tail, dil4), (win16, dil16) = DILATED
    assert dil1 == 1 and win1 == BLK and n_past == win16 and n_past % dil16 == 0
    assert n_past >= tail and n_new <= dil4 and n_new * H_A <= LANES
    n16 = n_past // dil16
    t = np.arange(n_new)[:, None]

    def per_head(index):
        same = np.eye(H_A, dtype=bool)[None, :, None, :]
        full = np.where(same, index[:, None, :, None], -1)
        return full.reshape(n_new * H_A, index.shape[1] * H_A)

    dist = BLK + t - np.arange(BLK)[None, :]
    d1 = np.where(dist <= win1, _rel_bucket(dist), -1)
    dist = tail + t - np.arange(tail)[None, :]
    d4 = np.where((dist % dil4 == 0) & (dist <= tail), _rel_bucket(dist), -1)
    m, tk = np.divmod(np.arange(n16 * n_new)[None, :], n_new)
    dist = n_past + t - dil16 * m - tk
    d16 = np.where(tk == t, _rel_bucket(np.maximum(dist, 0)), -1)
    s = np.arange(LANES // H_A)[None, :]
    dist = t - s
    new = [np.where((s < n_new) & (dist >= 0) & (dist % dil == 0),
                    _rel_bucket(np.maximum(dist, 0)), -1) for _, dil in DILATED]
    pieces = [d1, d4, d16] + new
    return np.concatenate([per_head(p) for p in pieces], axis=1).astype(np.int32)


def _sattn_body(bias_ref, q_ref, kn_ref, vn_ref, kt_ref, k16_ref, vt_ref, v16_ref, o_ref):
    def flat(ref):
        x = ref[0]
        return x.reshape(-1, HEAD_DIM).astype(BF16)

    q = q_ref[0].astype(BF16)
    kt, vt = flat(kt_ref), flat(vt_ref)
    k16, v16 = flat(k16_ref), flat(v16_ref)
    def new_rows(ref):
        x = ref[0]
        return jnp.concatenate([x, jnp.zeros((LANES - x.shape[0], HEAD_DIM), F32)], axis=0).astype(BF16)

    kn, vn = new_rows(kn_ref), new_rows(vn_ref)
    n_t, n_1 = kt.shape[0], BLK * H_A

    raw_t = _dot_nt(q, kt)
    raw_n = _dot_nt(q, kn)
    raws = [raw_t[:, n_t - n_1:], raw_t, _dot_nt(q, k16)] + [raw_n] * len(DILATED)
    logits = []
    lo = 0
    for raw in raws:
        hi = lo + raw.shape[1]
        logits.append(raw * SCALE + bias_ref[:, lo:hi])
        lo = hi

    m = functools.reduce(jnp.maximum, [jnp.max(x, axis=-1, keepdims=True) for x in logits])
    probs = [jnp.exp(x - m) for x in logits]
    den = functools.reduce(lambda a, b: a + b, [jnp.sum(p, axis=-1, keepdims=True) for p in probs])
    p_new = functools.reduce(lambda a, b: a + b, probs[3:])
    o = (_dot(probs[0].astype(BF16), vt[n_t - n_1:, :]) + _dot(probs[1].astype(BF16), vt)
         + _dot(probs[2].astype(BF16), v16) + _dot(p_new.astype(BF16), vn))
    o_ref[0] = o / den


N_PROMPT_IN, N_SAMPLE_IN, N_PROMPT_OUT = 4, 8, 3


def _attention_body(*refs, seq, group_size):
    prompt_in, refs = refs[:N_PROMPT_IN], refs[N_PROMPT_IN:]
    sample_in, refs = refs[:N_SAMPLE_IN], refs[N_SAMPLE_IN:]
    prompt_out, sample_out, scratch = refs[:N_PROMPT_OUT], refs[N_PROMPT_OUT], refs[N_PROMPT_OUT + 1:]
    _sattn_body(*sample_in, sample_out)
    _attn_body(*prompt_in, *prompt_out, *scratch, seq=seq, group_size=group_size)


def _attention(z, qa, ka, va, cache_k, cache_v, rel_bias, *, batch, seq):
    n = batch * seq
    dec_batch, n_new = qa.shape[:2]
    n_past = cache_k.shape[1]
    assert all(seq % (dil * BLK) == 0 for _, dil in DILATED) and dec_batch == batch * H_A

    idx = _prompt_bias_index()
    n_d = idx.shape[0]
    bias_p = _bias_lookup(jnp.asarray(idx.reshape(n_d * BLK, 2 * BLK)), rel_bias.T[:, None, :],
                          n_col_tiles=1)
    bias_p = bias_p.reshape(H_A, n_d, BLK, 2 * BLK)
    prompt_in_specs = [pl.BlockSpec((1, n_d, BLK, 2 * BLK), lambda b, h: (h, 0, 0, 0))]
    prompt_in_specs += [pl.BlockSpec((seq, HEAD_DIM), lambda b, h, part=part: (b, part * H_A + h))
                        for part in range(3)]
    by_head = pl.BlockSpec((1, seq * H_A, HEAD_DIM), lambda b, h: (b, 0, 0), pipeline_mode=pl.Buffered(1))
    by_head_shape = jax.ShapeDtypeStruct((batch, seq * H_A, HEAD_DIM), F32)
    stat = pltpu.VMEM((n_d, SLABS, seq // SLABS, HEAD_DIM), F32)

    tail = DILATED[1][0]
    dil16 = DILATED[2][1]
    n16 = n_past // dil16
    n_rows = n_new * H_A
    bias_s = _bias_lookup(jnp.asarray(_sample_bias_index(n_new, n_past)),
                          jnp.tile(rel_bias.T, (n_new, 1))[None], n_col_tiles=5)[0]

    def rows(x):
        return x.reshape(dec_batch, n_rows, HEAD_DIM)

    def sample(b, h):
        return b * H_A + h

    k_res = cache_k.reshape(dec_batch, n16, dil16, H_A, HEAD_DIM)
    v_res = cache_v.reshape(dec_batch, n16, dil16, H_A, HEAD_DIM)
    q_spec = pl.BlockSpec((1, n_rows, HEAD_DIM), lambda b, h: (sample(b, h), 0, 0))
    tail_spec = pl.BlockSpec((1, tail, H_A, HEAD_DIM),
                             lambda b, h: (sample(b, h), n_past // tail - 1, 0, 0))
    res_spec = pl.BlockSpec((1, n16, n_new, H_A, HEAD_DIM), lambda b, h: (sample(b, h), 0, 0, 0, 0))
    sample_in_specs = [pl.BlockSpec(bias_s.shape, lambda b, h: (0, 0), pipeline_mode=pl.Buffered(1)),
                       q_spec, q_spec, q_spec, tail_spec, res_spec, tail_spec, res_spec]
    assert len(prompt_in_specs) == N_PROMPT_IN and len(sample_in_specs) == N_SAMPLE_IN

    oa_p, kh, vh, oa_s = pl.pallas_call(
        functools.partial(_attention_body, seq=seq, group_size=32),
        out_shape=[jax.ShapeDtypeStruct((n, W_A), F32), by_head_shape, by_head_shape,
                   jax.ShapeDtypeStruct((dec_batch, n_rows, HEAD_DIM), F32)],
        grid=(batch, H_A),
        in_specs=prompt_in_specs + sample_in_specs,
        out_specs=[pl.BlockSpec((seq, HEAD_DIM), lambda b, h: (b, h)), by_head, by_head, q_spec],
        scratch_shapes=[stat, stat, stat, stat],
        compiler_params=_params("arbitrary", "arbitrary"),
        name="attention",
    )(bias_p, z, z, z,
      bias_s, rows(qa), rows(ka), rows(va), cache_k, k_res, cache_v, v_res)
    heads = (batch, seq, H_A, HEAD_DIM)
    return oa_p, kh.reshape(heads), vh.reshape(heads), oa_s.reshape(dec_batch, n_new, W_A)


def _sgus_body(u_ref, v_ref, w_ref, b_ref, o_ref, *, n_new):
    t_idx = lax.broadcasted_iota(jnp.int32, (1, n_new, W_B), 1)
    mixed = 0.0
    for s in range(n_new):
        w_s = jnp.where(t_idx >= s, w_ref[s][None], 0.0).astype(BF16).astype(F32)
        mixed = mixed + w_s * v_ref[:, s:s + 1, :].astype(BF16).astype(F32)
    o_ref[...] = u_ref[...] * (mixed + b_ref[...][None])


def _side_body(u_ref, v_ref, w_ref, bt_ref, gb_ref, qm_ref, mk_ref, mv_ref, gm_ref,
               us_ref, vs_ref, wl_ref, bl_ref, qs_ref, ck_ref, cv_ref,
               obn_ref, omn_ref, obs_ref, oms_ref, ob_s, om_s, *, n_chunks, n_new):
    _mems_body(qs_ref, ck_ref, cv_ref, oms_ref)
    _sgus_body(us_ref, vs_ref, wl_ref, bl_ref, obs_ref, n_new=n_new)
    _mem_body(qm_ref, mk_ref, mv_ref, om_s)
    omn_ref[0] = _rms(om_s[...], gm_ref[...]).astype(BF16)
    _sgu_body(u_ref, v_ref, w_ref, bt_ref, ob_s, n_chunks=n_chunks)
    obn_ref[...] = _rms(ob_s[...], gb_ref[...]).astype(BF16)


def _side_mixers(z, mkv, u_s, v_s, qm_s, cache_k, cache_v, w_s, b_s, g_b, g_m, *, batch, seq, n_steps):
    dec_batch, n_new, _ = u_s.shape
    tm = batch * seq // n_steps
    tb = dec_batch // n_steps
    per_batch = seq // tm
    assert tm * per_batch == seq and tb * n_steps == dec_batch and tm % CHUNK == 0
    n_rows = n_new * H_M
    w_l = jnp.repeat(jnp.transpose(w_s[:, :n_new, :n_new], (2, 1, 0)), C_B, axis=-1)
    b_l = jnp.repeat(b_s[:, :n_new].T, C_B, axis=-1)
    z3 = z.reshape(batch, seq, -1)
    mkv3 = mkv.reshape(batch, N_MEM, 2 * W_M)
    once = pl.Buffered(1)
    sgu_s_spec = pl.BlockSpec((tb, n_new, W_B), lambda i: (i, 0, 0))
    qs_spec = pl.BlockSpec((tb, n_rows, HEAD_DIM), lambda i: (i, 0, 0))
    cache_spec = pl.BlockSpec((tb, N_MEM * H_M, HEAD_DIM), lambda i: (i, 0, 0))
    om_spec = pl.BlockSpec((1, tm, W_M), lambda i: (i // per_batch, i % per_batch, 0))
    gain_b = pl.BlockSpec((1, W_B), lambda i: (0, 0), pipeline_mode=once)
    gain_m = pl.BlockSpec((1, W_M), lambda i: (0, 0), pipeline_mode=once)
    obn, omn, ob_s, om_s = pl.pallas_call(
        functools.partial(_side_body, n_chunks=tm // CHUNK, n_new=n_new),
        out_shape=[jax.ShapeDtypeStruct((batch * seq, W_B), BF16),
                   jax.ShapeDtypeStruct((batch, seq, W_M), BF16),
                   jax.ShapeDtypeStruct(u_s.shape, F32),
                   jax.ShapeDtypeStruct((dec_batch, n_rows, HEAD_DIM), F32)],
        grid=(n_steps,),
        in_specs=[
            pl.BlockSpec((tm, W_B), lambda i: (i, COL_U // W_B)),
            pl.BlockSpec((tm, W_B), lambda i: (i, COL_V // W_B)),
            pl.BlockSpec((H_B, CHUNK, CHUNK), lambda i: (0, 0, 0), pipeline_mode=once),
            pl.BlockSpec((CHUNK, H_B), lambda i: (0, 0), pipeline_mode=once),
            gain_b,
            pl.BlockSpec((1, tm, W_M), lambda i: (i // per_batch, i % per_batch, COL_QM // W_M)),
            pl.BlockSpec((1, N_MEM, W_M), lambda i: (i // per_batch, 0, 0)),
            pl.BlockSpec((1, N_MEM, W_M), lambda i: (i // per_batch, 0, 1)),
            gain_m,
            sgu_s_spec, sgu_s_spec,
            pl.BlockSpec(w_l.shape, lambda i: (0, 0, 0), pipeline_mode=once),
            pl.BlockSpec(b_l.shape, lambda i: (0, 0), pipeline_mode=once),
            qs_spec, cache_spec, cache_spec,
        ],
        out_specs=[pl.BlockSpec((tm, W_B), lambda i: (i, 0)), om_spec, sgu_s_spec, qs_spec],
        scratch_shapes=[pltpu.VMEM((tm, W_B), F32), pltpu.VMEM((tm, W_M), F32)],
        compiler_params=_params("parallel"),
        name="side_mixers",
    )(z, z, w_s, b_s.T, g_b.reshape(1, W_B), z3, mkv3, mkv3, g_m.reshape(1, W_M), u_s, v_s, w_l, b_l,
      qm_s.reshape(dec_batch, n_rows, HEAD_DIM),
      cache_k.reshape(dec_batch, N_MEM * H_M, HEAD_DIM), cache_v.reshape(dec_batch, N_MEM * H_M, HEAD_DIM))
    return obn, omn.reshape(batch * seq, W_M), ob_s, om_s.reshape(dec_batch, n_new, W_M)


COL_KA, COL_VA, COL_U, COL_V, COL_QM = W_A, 2 * W_A, 3 * W_A, 3 * W_A + W_B, 3 * W_A + 2 * W_B


def _mixer_inputs(normed_groups, w_in, g_qa, g_ka, g_sgu, g_qm, *, n_tiles):
    gain = jnp.concatenate([jnp.tile(g_qa, H_A), jnp.tile(g_ka, H_A), jnp.ones((W_A + W_B,), F32),
                            g_sgu.reshape(W_B), jnp.tile(g_qm, H_M)])
    normed = [c // PROJ_TILE for lo, hi in ((0, COL_VA), (COL_V, COL_QM + W_M))
              for c in range(lo, hi, PROJ_TILE)]
    return _proj(normed_groups, None, w_in, gain, normed, n_tiles=n_tiles, chunk=256)


def kernel(x_prompt, x_sample, cache_win_k, cache_win_v, cache_mem_k, cache_mem_v, mem_prompt, rel_bias, g_ffn1, w1_gate, w1_up, w1_down, g_mix, w_in, g_qa, g_ka, g_sgu, w_sgu, b_sgu, g_qm, g_mem, w_mem_kv, g_km, g_mix_out, w_out, g_ffn2, w2_gate, w2_up, w2_down):
    batch, seq, _ = x_prompt.shape
    dec_batch, n_new, _ = x_sample.shape
    depth = g_ffn1.shape[0]
    n_p = batch * seq
    n_s = dec_batch * n_new

    y_p = x_prompt.reshape(n_p, D_MODEL)
    y_s = x_sample.reshape(n_s, D_MODEL)
    outs = [[] for _ in range(7)]
    for l in range(depth):
        h_p, h_s, hn_p, hn_s = _ffn([y_p, y_s], g_ffn1[l], w1_gate[l], w1_up[l], w1_down[l],
                                    g_next=g_mix[l], n_tiles=8, tf=256)
        z_p, z_s = _mixer_inputs([hn_p, hn_s], w_in[l], g_qa[l], g_ka[l], g_sgu[l], g_qm[l], n_tiles=4)

        zq, zk, zv, zu, zg, zm = (z_s[:, lo:hi] for lo, hi in zip(
            (0, COL_KA, COL_VA, COL_U, COL_V, COL_QM), (COL_KA, COL_VA, COL_U, COL_V, COL_QM, z_s.shape[1])))
        heads = (dec_batch, n_new, H_A, HEAD_DIM)
        oa_p, k_p, v_p, oa_s = _attention(z_p, zq.reshape(heads), zk.reshape(heads), zv.reshape(heads),
                                          cache_win_k[l], cache_win_v[l], rel_bias, batch=batch, seq=seq)

        mem = mem_prompt.reshape(batch * N_MEM, D_MODEL)
        gain_kv = jnp.concatenate([jnp.tile(g_km[l], H_M), jnp.ones((W_M,), F32)])
        mkv, = _proj([mem], g_mem[l], w_mem_kv[l], gain_kv, range(W_M // PROJ_TILE),
                     n_tiles=2, chunk=512)
        ob_p, om_p, ob_s, om_s = _side_mixers(
            z_p, mkv, zu.reshape(dec_batch, n_new, W_B), zg.reshape(dec_batch, n_new, W_B),
            zm.reshape(dec_batch, n_new, H_M, HEAD_DIM), cache_mem_k[l], cache_mem_v[l],
            w_sgu[l], b_sgu[l], g_mix_out[l][W_A:W_A + W_B], g_mix_out[l][W_A + W_B:],
            batch=batch, seq=seq, n_steps=8)
        keep = min(DILATED[-1][0], seq)
        outs[0].append(k_p[:, seq - keep:])
        outs[1].append(v_p[:, seq - keep:])
        outs[2].append(mkv[:, :W_M].reshape(batch, N_MEM, H_M, HEAD_DIM))
        outs[3].append(mkv[:, W_M:].reshape(batch, N_MEM, H_M, HEAD_DIM))
        outs[4].append(zk.reshape(heads))
        outs[5].append(zv.reshape(heads))
        outs[6].append(zg.reshape(dec_batch, n_new, H_B, C_B))

        h_p, h_s = _outproj([(oa_p, ob_p, om_p, h_p),
                             (oa_s.reshape(n_s, W_A), ob_s.reshape(n_s, W_B), om_s.reshape(n_s, W_M), h_s)],
                            g_mix_out[l], w_out[l], n_tiles=4, tn=256)
        y_p, y_s = _ffn([h_p, h_s], g_ffn2[l], w2_gate[l], w2_up[l], w2_down[l], n_tiles=8, tf=256)

    return (y_p.reshape(batch, seq, D_MODEL), y_s.reshape(dec_batch, n_new, D_MODEL),
            *[jnp.stack(o) for o in outs])
```

```python
import functools

import numpy as np
import jax
import jax.numpy as jnp
from jax import lax
from jax.experimental import pallas as pl
from jax.experimental.pallas import tpu as pltpu

D_MODEL = 2048
HEAD_DIM = 128
H_A = 8
H_B = 4
C_B = 128
H_M = 4
W_A = H_A * HEAD_DIM
W_B = H_B * C_B
W_M = H_M * HEAD_DIM
D_FF = 5632
DILATED = ((128, 1), (512, 4), (2048, 16))
BLK = 128
CHUNK = 128
N_MEM = 256
N_BUCKETS = 32
REL_MAX_DIST = 2048
EPS = 1e-6
SCALE = HEAD_DIM ** -0.5
NEG = -1e30

LANES = 128
BF16_ROWS = 16
PROJ_TILE = 4 * HEAD_DIM
VMEM_LIMIT = 60 * 1024 * 1024

F32 = jnp.float32
BF16 = jnp.bfloat16


def _rel_bucket(dist):
    dist = np.asarray(dist, np.int64)
    max_exact = N_BUCKETS // 2
    large = max_exact + (np.log(np.maximum(dist, 1) / max_exact) / np.log(REL_MAX_DIST / max_exact)
                         * (N_BUCKETS - max_exact)).astype(np.int64)
    large = np.minimum(large, N_BUCKETS - 1)
    return np.where(dist < max_exact, dist, large).astype(np.int32)


def _rms(t, g):
    return t * lax.rsqrt(jnp.mean(t * t, axis=-1, keepdims=True) + EPS) * g


def _dot(a, b):
    return jnp.dot(a, b, preferred_element_type=F32)


def _dot_nt(a, b):
    return lax.dot_general(a, b, (((1,), (1,)), ((), ())), preferred_element_type=F32)


def _params(*sem):
    return pltpu.CompilerParams(dimension_semantics=sem, vmem_limit_bytes=VMEM_LIMIT)


def _tile_rows(groups, n_tiles):
    rows = [x.shape[0] // n_tiles for x in groups]
    assert all(r * n_tiles == x.shape[0] and r % BF16_ROWS == 0 for r, x in zip(rows, groups))
    return rows


def _bounds(refs):
    out, lo = [], 0
    for ref in refs:
        out.append((lo, lo + ref.shape[0]))
        lo += ref.shape[0]
    return out


def _chunk_edges(total, chunk):
    edges = list(range(0, total, chunk)) + [total]
    if len(edges) > 2 and edges[-1] - edges[-2] < chunk // 2:
        del edges[-2]
    return list(zip(edges[:-1], edges[1:]))


def _ffn_body(*refs, n_groups, with_next_norm):
    x_refs, refs = refs[:n_groups], refs[n_groups:]
    gn_ref = refs[0] if with_next_norm else None
    g_ref, wg_ref, wu_ref, wd_ref = refs[with_next_norm:with_next_norm + 4]
    refs = refs[with_next_norm + 4:]
    o_refs, on_refs, xn_ref = refs[:n_groups], refs[n_groups:-1], refs[-1]
    bounds = _bounds(x_refs)
    j = pl.program_id(1)

    @pl.when(j == 0)
    def _():
        for x_ref, o_ref, (lo, hi) in zip(x_refs, o_refs, bounds):
            x = x_ref[...]
            xn_ref[lo:hi, :] = _rms(x, g_ref[...]).astype(BF16)
            o_ref[...] = x

    xn = xn_ref[...]
    gate = _dot(xn, wg_ref[...].astype(BF16))
    up = _dot(xn, wu_ref[...].astype(BF16))
    act = (gate * jax.nn.sigmoid(gate)) * up * 0.5
    res = _dot(act.astype(BF16), wd_ref[...].astype(BF16))
    for o_ref, (lo, hi) in zip(o_refs, bounds):
        o_ref[...] += res[lo:hi]

    if with_next_norm:
        @pl.when(j == pl.num_programs(1) - 1)
        def _():
            for o_ref, on_ref in zip(o_refs, on_refs):
                on_ref[...] = _rms(o_ref[...], gn_ref[...]).astype(BF16)


def _ffn(groups, g, wg, wu, wd, *, g_next=None, n_tiles, tf):
    rows = _tile_rows(groups, n_tiles)
    with_next_norm = g_next is not None
    gain_spec = pl.BlockSpec((1, D_MODEL), lambda i, j: (0, 0), pipeline_mode=pl.Buffered(1))
    row_specs = [pl.BlockSpec((r, D_MODEL), lambda i, j: (i, 0)) for r in rows]
    out_shape = [jax.ShapeDtypeStruct(x.shape, F32) for x in groups]
    gains = [g.reshape(1, D_MODEL)]
    if with_next_norm:
        out_shape += [jax.ShapeDtypeStruct(x.shape, BF16) for x in groups]
        gains.insert(0, g_next.reshape(1, D_MODEL))
    return pl.pallas_call(
        functools.partial(_ffn_body, n_groups=len(groups), with_next_norm=with_next_norm),
        out_shape=out_shape,
        grid=(n_tiles, D_FF // tf),
        in_specs=(
            row_specs + [gain_spec] * len(gains)
            + [pl.BlockSpec((D_MODEL, tf), lambda i, j: (0, j)),
               pl.BlockSpec((D_MODEL, tf), lambda i, j: (0, j)),
               pl.BlockSpec((tf, D_MODEL), lambda i, j: (j, 0))]),
        out_specs=row_specs * (2 if with_next_norm else 1),
        scratch_shapes=[pltpu.VMEM((sum(rows), D_MODEL), BF16)],
        compiler_params=_params("arbitrary", "arbitrary"),
        name="ffn",
    )(*groups, *gains, wg, wu, wd)


def _proj_body(*refs, n_groups, norm_tiles, chunk):
    x_refs = refs[:n_groups]
    w_ref, gain_ref = refs[n_groups:n_groups + 2]
    o_refs, xn_ref = refs[n_groups + 2:-1], refs[-1]
    bounds = _bounds(x_refs)
    j = pl.program_id(1)

    @pl.when(j == 0)
    def _():
        for x_ref, (lo, hi) in zip(x_refs, bounds):
            xn_ref[lo:hi, :] = x_ref[...]

    w = w_ref[...].astype(BF16)
    gain = gain_ref[...]
    normed = functools.reduce(jnp.logical_or, [j == t for t in norm_tiles])

    def finish(z):
        heads = []
        for hh in range(PROJ_TILE // HEAD_DIM):
            sl = slice(hh * HEAD_DIM, (hh + 1) * HEAD_DIM)
            zz = z[:, sl]
            scale = lax.rsqrt(jnp.mean(zz * zz, axis=-1, keepdims=True) + EPS)
            heads.append(zz * jnp.where(normed, scale, 1.0) * gain[:, sl])
        return jnp.concatenate(heads, axis=1)

    for c0, c1 in _chunk_edges(bounds[-1][1], chunk):
        z = finish(_dot(xn_ref[c0:c1, :], w))
        for o_ref, (lo, hi) in zip(o_refs, bounds):
            a, b = max(lo, c0), min(hi, c1)
            if a < b:
                o_ref[a - lo:b - lo, :] = z[a - c0:b - c0, :]


def _proj(groups, w, gain, norm_tiles, *, n_tiles, chunk):
    d_in, d_out = w.shape
    rows = _tile_rows(groups, n_tiles)
    assert all(x.dtype == BF16 for x in groups)
    return pl.pallas_call(
        functools.partial(_proj_body, n_groups=len(groups), norm_tiles=tuple(norm_tiles), chunk=chunk),
        out_shape=[jax.ShapeDtypeStruct((x.shape[0], d_out), F32) for x in groups],
        grid=(n_tiles, d_out // PROJ_TILE),
        in_specs=(
            [pl.BlockSpec((r, d_in), lambda i, j: (i, 0)) for r in rows]
            + [pl.BlockSpec((d_in, PROJ_TILE), lambda i, j: (0, j)),
               pl.BlockSpec((1, PROJ_TILE), lambda i, j: (0, j))]),
        out_specs=[pl.BlockSpec((r, PROJ_TILE), lambda i, j: (i, j)) for r in rows],
        scratch_shapes=[pltpu.VMEM((sum(rows), d_in), BF16)],
        compiler_params=_params("arbitrary", "arbitrary"),
        name="proj",
    )(*groups, w, gain.reshape(1, d_out))


def _bias_body(idx_ref, tab_ref, o_ref):
    idx = idx_ref[...]
    tab = tab_ref[0]
    bias = jnp.full(idx.shape, NEG, F32)
    for bkt in range(N_BUCKETS):
        bias = jnp.where(idx == bkt, tab[:, bkt:bkt + 1], bias)
    o_ref[0] = bias


def _bias_lookup(idx, tab, *, n_col_tiles):
    n_rows, n_cols = idx.shape
    n_g, tab_rows, _ = tab.shape
    tc = n_cols // n_col_tiles
    assert tc * n_col_tiles == n_cols and tc % LANES == 0 and tab_rows in (1, n_rows)
    return pl.pallas_call(
        _bias_body,
        out_shape=jax.ShapeDtypeStruct((n_g, n_rows, n_cols), F32),
        grid=(n_g, n_col_tiles),
        in_specs=[pl.BlockSpec((n_rows, tc), lambda g, j: (0, j)),
                  pl.BlockSpec((1, tab_rows, N_BUCKETS), lambda g, j: (g, 0, 0))],
        out_specs=pl.BlockSpec((1, n_rows, tc), lambda g, j: (g, 0, j)),
        compiler_params=_params("parallel", "parallel"),
        name="bias_lookup",
    )(idx, tab)


SLABS = DILATED[1][1]
PIECE = BLK // SLABS


def _prompt_bias_index():
    step = np.arange(BLK)[:, None] + BLK - np.arange(2 * BLK)[None, :]
    perm = (SLABS * np.arange(PIECE)[None, :] + np.arange(SLABS)[:, None]).reshape(BLK)
    maps = []
    for win, dil in DILATED:
        n_steps = win // dil + 1
        band = (step >= 0) & (step < n_steps)
        bucket = _rel_bucket(np.clip(step, 0, n_steps - 1) * dil)
        idx = np.where(band, bucket, -1)
        if dil == 1:
            idx = idx[np.ix_(perm, np.concatenate([perm, BLK + perm]))]
        maps.append(idx)
    return np.stack(maps).astype(np.int32)


def _block_pieces(dil, r, n):
    if dil == 1:
        return [(s, slice(n * PIECE, (n + 1) * PIECE), PIECE) for s in range(SLABS)]
    if dil == SLABS:
        return [(r, slice(n * BLK, (n + 1) * BLK), BLK)]
    sub = dil // SLABS
    s, t = r % SLABS, r // SLABS
    return [(s, pl.ds(t + sub * BLK * n, BLK, stride=sub), BLK)]


def _attn_body(bias_s, q_ref, k_ref, v_ref, o_ref, kh_ref, vh_ref, slab_s, acc_s, m_s, l_s, *,
               seq, group_size):
    for part, ref in enumerate((q_ref, k_ref, v_ref)):
        for s in range(SLABS):
            slab_s[part, s] = ref[pl.ds(s, seq // SLABS, stride=SLABS), :]
    q_s, k_s, v_s = ([slab_s.at[part, s] for s in range(SLABS)] for part in range(3))

    head_rows = pl.ds(pl.program_id(1), seq, stride=H_A)
    kh_ref[0, head_rows, :] = k_ref[...]
    vh_ref[0, head_rows, :] = v_ref[...]

    def load(slabs, pieces):
        parts = [slabs[s][idx, :] for s, idx, _ in pieces]
        return parts[0] if len(parts) == 1 else jnp.concatenate(parts, axis=0)

    def gather(slabs, blocks):
        return jnp.stack([load(slabs, pieces) for pieces in blocks]).astype(BF16)

    def qk(q, k):
        return jnp.einsum("gid,gjd->gij", q, k, preferred_element_type=F32)

    def pv(p, v):
        return jnp.einsum("gij,gjd->gid", p.astype(BF16), v, preferred_element_type=F32)

    def bias_of(dis, cols):
        return jnp.stack([bias_s[0, di, :, cols] for di in dis])

    def group(dis, curs, prevs):
        q = gather(q_s, curs)
        s_c = qk(q, gather(k_s, curs)) * SCALE + bias_of(dis, slice(BLK, None))
        if prevs is None:
            m = jnp.max(s_c, axis=-1, keepdims=True)
            p_c = jnp.exp(s_c - m)
            den = jnp.sum(p_c, axis=-1, keepdims=True)
            o = pv(p_c, gather(v_s, curs))
        else:
            s_p = qk(q, gather(k_s, prevs)) * SCALE + bias_of(dis, slice(0, BLK))
            m = jnp.max(jnp.maximum(s_c, s_p), axis=-1, keepdims=True)
            p_c = jnp.exp(s_c - m)
            p_p = jnp.exp(s_p - m)
            den = jnp.sum(p_c + p_p, axis=-1, keepdims=True)
            o = pv(p_c, gather(v_s, curs)) + pv(p_p, gather(v_s, prevs))
        for g, (di, pieces) in enumerate(zip(dis, curs)):
            m_g = jnp.broadcast_to(m[g], (BLK, HEAD_DIM))
            l_g = jnp.broadcast_to(den[g], (BLK, HEAD_DIM))
            lo = 0
            for s, idx, n_rows in pieces:
                acc_s[di, s, idx, :] = o[g, lo:lo + n_rows]
                m_s[di, s, idx, :] = m_g[lo:lo + n_rows]
                l_s[di, s, idx, :] = l_g[lo:lo + n_rows]
                lo += n_rows

    first, later = [], []
    for di, (win, dil) in enumerate(DILATED):
        for r in range(dil):
            first.append((di, dil, r, 0))
            later += [(di, dil, r, n) for n in range(1, seq // (dil * BLK))]
    groups = []
    for blocks, has_prev in ((first, False), (later, True)):
        for lo in range(0, len(blocks), group_size):
            part = blocks[lo:lo + group_size]
            groups.append(([di for di, _, _, _ in part],
                           [_block_pieces(dil, r, n) for _, dil, r, n in part],
                           [_block_pieces(dil, r, n - 1) for _, dil, r, n in part] if has_prev else None))
    for args in groups:
        group(*args)

    rows_per = 256
    per_slab = seq // SLABS // rows_per

    def merge(c, carry):
        s = c // per_slab
        r0 = pl.multiple_of((c % per_slab) * rows_per, rows_per)
        rc = pl.ds(r0, rows_per)
        ms = [m_s[di, s, rc, :] for di in range(len(DILATED))]
        mx = jnp.maximum(jnp.maximum(ms[0], ms[1]), ms[2])
        num = 0.0
        den = 0.0
        for di in range(len(DILATED)):
            wgt = jnp.exp(ms[di] - mx)
            num = num + wgt * acc_s[di, s, rc, :]
            den = den + wgt * l_s[di, s, rc, :]
        o_ref[pl.ds(SLABS * r0 + s, rows_per, stride=SLABS), :] = num / den
        return carry

    lax.fori_loop(0, SLABS * per_slab, merge, 0)


def _sgu_body(u_ref, v_ref, w_ref, bt_ref, o_ref, *, n_chunks):
    tri = (lax.broadcasted_iota(jnp.int32, (CHUNK, CHUNK), 0)
           >= lax.broadcasted_iota(jnp.int32, (CHUNK, CHUNK), 1))
    for g in range(H_B):
        w = jnp.where(tri, w_ref[g], 0.0).astype(BF16)
        bias = bt_ref[:, g:g + 1]
        cols = slice(g * C_B, (g + 1) * C_B)
        for c in range(n_chunks):
            rws = slice(c * CHUNK, (c + 1) * CHUNK)
            mixed = _dot(w, v_ref[rws, cols].astype(BF16)) + bias
            o_ref[rws, cols] = u_ref[rws, cols] * mixed


def _mem_kv_body(mem_ref, g_ref, w_ref, gain_ref, kv_ref):
    z = _dot(_rms(mem_ref[...], g_ref[...]).astype(BF16), w_ref[...].astype(BF16))
    for hh in range(H_M):
        sl = slice(hh * HEAD_DIM, (hh + 1) * HEAD_DIM)
        kv_ref[:, sl] = _rms(z[:, sl], gain_ref[:, sl])
    kv_ref[:, W_M:] = z[:, W_M:]


def _mem_body(q_ref, kv_ref, o_ref):
    for hh in range(H_M):
        k_sl = slice(hh * HEAD_DIM, (hh + 1) * HEAD_DIM)
        v_sl = slice(W_M + hh * HEAD_DIM, W_M + (hh + 1) * HEAD_DIM)
        s = _dot_nt(q_ref[0, :, k_sl].astype(BF16), kv_ref[:, k_sl].astype(BF16)) * SCALE
        m = jnp.max(s, axis=-1, keepdims=True)
        p = jnp.exp(s - m)
        den = jnp.sum(p, axis=-1, keepdims=True)
        o_ref[:, k_sl] = _dot(p.astype(BF16), kv_ref[:, v_sl].astype(BF16)) / den


def _mems_body(q_ref, k_ref, v_ref, o_ref):
    for b in range(q_ref.shape[0]):
        s = _dot_nt(q_ref[b].astype(BF16), k_ref[b].astype(BF16)) * SCALE
        row = lax.broadcasted_iota(jnp.int32, s.shape, 0)
        col = lax.broadcasted_iota(jnp.int32, s.shape, 1)
        s = jnp.where(row % H_M == col % H_M, s, NEG)
        m = jnp.max(s, axis=-1, keepdims=True)
        p = jnp.exp(s - m)
        den = jnp.sum(p, axis=-1, keepdims=True)
        o_ref[b] = _dot(p.astype(BF16), v_ref[b].astype(BF16)) / den


def _outproj_body(*refs, n_groups):
    src_refs = [refs[4 * k:4 * k + 3] for k in range(n_groups)]
    h_refs = [refs[4 * k + 3] for k in range(n_groups)]
    g_ref, w_ref = refs[4 * n_groups:4 * n_groups + 2]
    o_refs = refs[4 * n_groups + 2:5 * n_groups + 2]
    cat_ref = refs[-1]
    bounds = _bounds(h_refs)

    @pl.when(pl.program_id(1) == 0)
    def _():
        for srcs, (r0, r1) in zip(src_refs, bounds):
            lo = 0
            for src in srcs:
                hi = lo + src.shape[1]
                x = src[...]
                cat_ref[r0:r1, lo:hi] = x if x.dtype == BF16 else _rms(x, g_ref[:, lo:hi]).astype(BF16)
                lo = hi

    res = _dot(cat_ref[...], w_ref[...].astype(BF16))
    for h_ref, o_ref, (r0, r1) in zip(h_refs, o_refs, bounds):
        o_ref[...] = h_ref[...] + res[r0:r1]


def _outproj(groups, g, w, *, n_tiles, tn):
    mix = W_A + W_B + W_M
    rows = _tile_rows([grp[3] for grp in groups], n_tiles)
    in_specs = []
    for r in rows:
        in_specs += [pl.BlockSpec((r, wd), lambda i, j: (i, 0)) for wd in (W_A, W_B, W_M)]
        in_specs += [pl.BlockSpec((r, tn), lambda i, j: (i, j))]
    in_specs += [pl.BlockSpec((1, mix), lambda i, j: (0, 0), pipeline_mode=pl.Buffered(1)),
                 pl.BlockSpec((mix, tn), lambda i, j: (0, j))]
    return pl.pallas_call(
        functools.partial(_outproj_body, n_groups=len(groups)),
        out_shape=[jax.ShapeDtypeStruct(grp[3].shape, F32) for grp in groups],
        grid=(n_tiles, D_MODEL // tn),
        in_specs=in_specs,
        out_specs=[pl.BlockSpec((r, tn), lambda i, j: (i, j)) for r in rows],
        scratch_shapes=[pltpu.VMEM((sum(rows), mix), BF16)],
        compiler_params=_params("arbitrary", "arbitrary"),
        name="outproj",
    )(*[a for grp in groups for a in grp], g.reshape(1, mix), w)


def _sample_bias_index(n_new, n_past):
    (win1, dil1), (tail, dil4), (win16, dil16) = DILATED
    assert dil1 == 1 and win1 == BLK and n_past == win16 and n_past % dil16 == 0
    assert n_past >= tail and n_new <= dil4 and n_new * H_A <= LANES
    n16 = n_past // dil16
    t = np.arange(n_new)[:, None]

    def per_head(index):
        same = np.eye(H_A, dtype=bool)[None, :, None, :]
        full = np.where(same, index[:, None, :, None], -1)
        return full.reshape(n_new * H_A, index.shape[1] * H_A)

    dist = BLK + t - np.arange(BLK)[None, :]
    d1 = np.where(dist <= win1, _rel_bucket(dist), -1)
    dist = tail + t - np.arange(tail)[None, :]
    d4 = np.where((dist % dil4 == 0) & (dist <= tail), _rel_bucket(dist), -1)
    m, tk = np.divmod(np.arange(n16 * n_new)[None, :], n_new)
    dist = n_past + t - dil16 * m - tk
    d16 = np.where(tk == t, _rel_bucket(np.maximum(dist, 0)), -1)
    s = np.arange(LANES // H_A)[None, :]
    dist = t - s
    new = [np.where((s < n_new) & (dist >= 0) & (dist % dil == 0),
                    _rel_bucket(np.maximum(dist, 0)), -1) for _, dil in DILATED]
    pieces = [d1, d4, d16] + new
    return np.concatenate([per_head(p) for p in pieces], axis=1).astype(np.int32)


def _sattn_body(bias_ref, q_ref, kn_ref, vn_ref, kt_ref, k16_ref, vt_ref, v16_ref, o_ref):
    def flat(ref):
        x = ref[0]
        return x.reshape(-1, HEAD_DIM).astype(BF16)

    q = q_ref[0].astype(BF16)
    kt, vt = flat(kt_ref), flat(vt_ref)
    k16, v16 = flat(k16_ref), flat(v16_ref)
    def new_rows(ref):
        x = ref[0]
        return jnp.concatenate([x, jnp.zeros((LANES - x.shape[0], HEAD_DIM), F32)], axis=0).astype(BF16)

    kn, vn = new_rows(kn_ref), new_rows(vn_ref)
    n_t, n_1 = kt.shape[0], BLK * H_A

    raw_t = _dot_nt(q, kt)
    raw_n = _dot_nt(q, kn)
    raws = [raw_t[:, n_t - n_1:], raw_t, _dot_nt(q, k16)] + [raw_n] * len(DILATED)
    logits = []
    lo = 0
    for raw in raws:
        hi = lo + raw.shape[1]
        logits.append(raw * SCALE + bias_ref[:, lo:hi])
        lo = hi

    m = functools.reduce(jnp.maximum, [jnp.max(x, axis=-1, keepdims=True) for x in logits])
    probs = [jnp.exp(x - m) for x in logits]
    den = functools.reduce(lambda a, b: a + b, [jnp.sum(p, axis=-1, keepdims=True) for p in probs])
    p_new = functools.reduce(lambda a, b: a + b, probs[3:])
    o = (_dot(probs[0].astype(BF16), vt[n_t - n_1:, :]) + _dot(probs[1].astype(BF16), vt)
         + _dot(probs[2].astype(BF16), v16) + _dot(p_new.astype(BF16), vn))
    o_ref[0] = o / den


N_PROMPT_IN, N_SAMPLE_IN, N_PROMPT_OUT = 4, 8, 3


def _attention_body(*refs, seq, group_size):
    prompt_in, refs = refs[:N_PROMPT_IN], refs[N_PROMPT_IN:]
    sample_in, refs = refs[:N_SAMPLE_IN], refs[N_SAMPLE_IN:]
    prompt_out, sample_out, scratch = refs[:N_PROMPT_OUT], refs[N_PROMPT_OUT], refs[N_PROMPT_OUT + 1:]
    _sattn_body(*sample_in, sample_out)
    _attn_body(*prompt_in, *prompt_out, *scratch, seq=seq, group_size=group_size)


def _attention(z, qa, ka, va, cache_k, cache_v, rel_bias, *, batch, seq):
    n = batch * seq
    dec_batch, n_new = qa.shape[:2]
    n_past = cache_k.shape[1]
    assert all(seq % (dil * BLK) == 0 for _, dil in DILATED) and dec_batch == batch * H_A

    idx = _prompt_bias_index()
    n_d = idx.shape[0]
    bias_p = _bias_lookup(jnp.asarray(idx.reshape(n_d * BLK, 2 * BLK)), rel_bias.T[:, None, :],
                          n_col_tiles=1)
    bias_p = bias_p.reshape(H_A, n_d, BLK, 2 * BLK)
    prompt_in_specs = [pl.BlockSpec((1, n_d, BLK, 2 * BLK), lambda b, h: (h, 0, 0, 0))]
    prompt_in_specs += [pl.BlockSpec((seq, HEAD_DIM), lambda b, h, part=part: (b, part * H_A + h))
                        for part in range(3)]
    by_head = pl.BlockSpec((1, seq * H_A, HEAD_DIM), lambda b, h: (b, 0, 0), pipeline_mode=pl.Buffered(1))
    by_head_shape = jax.ShapeDtypeStruct((batch, seq * H_A, HEAD_DIM), F32)
    stat = pltpu.VMEM((n_d, SLABS, seq // SLABS, HEAD_DIM), F32)

    tail = DILATED[1][0]
    dil16 = DILATED[2][1]
    n16 = n_past // dil16
    n_rows = n_new * H_A
    bias_s = _bias_lookup(jnp.asarray(_sample_bias_index(n_new, n_past)),
                          jnp.tile(rel_bias.T, (n_new, 1))[None], n_col_tiles=5)[0]

    def rows(x):
        return x.reshape(dec_batch, n_rows, HEAD_DIM)

    def sample(b, h):
        return b * H_A + h

    k_res = cache_k.reshape(dec_batch, n16, dil16, H_A, HEAD_DIM)
    v_res = cache_v.reshape(dec_batch, n16, dil16, H_A, HEAD_DIM)
    q_spec = pl.BlockSpec((1, n_rows, HEAD_DIM), lambda b, h: (sample(b, h), 0, 0))
    tail_spec = pl.BlockSpec((1, tail, H_A, HEAD_DIM),
                             lambda b, h: (sample(b, h), n_past // tail - 1, 0, 0))
    res_spec = pl.BlockSpec((1, n16, n_new, H_A, HEAD_DIM), lambda b, h: (sample(b, h), 0, 0, 0, 0))
    sample_in_specs = [pl.BlockSpec(bias_s.shape, lambda b, h: (0, 0), pipeline_mode=pl.Buffered(1)),
                       q_spec, q_spec, q_spec, tail_spec, res_spec, tail_spec, res_spec]
    assert len(prompt_in_specs) == N_PROMPT_IN and len(sample_in_specs) == N_SAMPLE_IN

    oa_p, kh, vh, oa_s = pl.pallas_call(
        functools.partial(_attention_body, seq=seq, group_size=32),
        out_shape=[jax.ShapeDtypeStruct((n, W_A), F32), by_head_shape, by_head_shape,
                   jax.ShapeDtypeStruct((dec_batch, n_rows, HEAD_DIM), F32)],
        grid=(batch, H_A),
        in_specs=prompt_in_specs + sample_in_specs,
        out_specs=[pl.BlockSpec((seq, HEAD_DIM), lambda b, h: (b, h)), by_head, by_head, q_spec],
        scratch_shapes=[stat, stat, stat, stat],
        compiler_params=_params("arbitrary", "arbitrary"),
        name="attention",
    )(bias_p, z, z, z,
      bias_s, rows(qa), rows(ka), rows(va), cache_k, k_res, cache_v, v_res)
    heads = (batch, seq, H_A, HEAD_DIM)
    return oa_p, kh.reshape(heads), vh.reshape(heads), oa_s.reshape(dec_batch, n_new, W_A)


def _sgus_body(u_ref, v_ref, w_ref, b_ref, o_ref, *, n_new):
    t_idx = lax.broadcasted_iota(jnp.int32, (1, n_new, W_B), 1)
    mixed = 0.0
    for s in range(n_new):
        w_s = jnp.where(t_idx >= s, w_ref[s][None], 0.0).astype(BF16).astype(F32)
        mixed = mixed + w_s * v_ref[:, s:s + 1, :].astype(BF16).astype(F32)
    o_ref[...] = u_ref[...] * (mixed + b_ref[...][None])


def _side_body(u_ref, v_ref, w_ref, bt_ref, gb_ref, qm_ref, gm_ref,
               mem_ref, gmem_ref, wkv_ref, gkv_ref,
               us_ref, vs_ref, wl_ref, bl_ref, qs_ref, ck_ref, cv_ref,
               obn_ref, omn_ref, mkv_ref, obs_ref, oms_ref, ob_s, om_s, *, n_chunks, n_new, per_batch):
    _mems_body(qs_ref, ck_ref, cv_ref, oms_ref)
    _sgus_body(us_ref, vs_ref, wl_ref, bl_ref, obs_ref, n_new=n_new)

    @pl.when(pl.program_id(0) % per_batch == 0)
    def _():
        _mem_kv_body(mem_ref, gmem_ref, wkv_ref, gkv_ref, mkv_ref)

    _mem_body(qm_ref, mkv_ref, om_s)
    omn_ref[0] = _rms(om_s[...], gm_ref[...]).astype(BF16)
    _sgu_body(u_ref, v_ref, w_ref, bt_ref, ob_s, n_chunks=n_chunks)
    obn_ref[...] = _rms(ob_s[...], gb_ref[...]).astype(BF16)


def _side_mixers(z, mem, g_mem, w_kv, g_km, u_s, v_s, qm_s, cache_k, cache_v, w_s, b_s, g_b, g_m, *,
                 batch, seq, n_steps):
    dec_batch, n_new, _ = u_s.shape
    tm = batch * seq // n_steps
    tb = dec_batch // n_steps
    per_batch = seq // tm
    assert tm * per_batch == seq and tb * n_steps == dec_batch and tm % CHUNK == 0
    n_rows = n_new * H_M
    w_l = jnp.repeat(jnp.transpose(w_s[:, :n_new, :n_new], (2, 1, 0)), C_B, axis=-1)
    b_l = jnp.repeat(b_s[:, :n_new].T, C_B, axis=-1)
    z3 = z.reshape(batch, seq, -1)
    gain_kv = jnp.concatenate([jnp.tile(g_km, H_M), jnp.ones((W_M,), F32)]).reshape(1, 2 * W_M)
    once = pl.Buffered(1)
    sgu_s_spec = pl.BlockSpec((tb, n_new, W_B), lambda i: (i, 0, 0))
    qs_spec = pl.BlockSpec((tb, n_rows, HEAD_DIM), lambda i: (i, 0, 0))
    cache_spec = pl.BlockSpec((tb, N_MEM * H_M, HEAD_DIM), lambda i: (i, 0, 0))
    om_spec = pl.BlockSpec((1, tm, W_M), lambda i: (i // per_batch, i % per_batch, 0))
    gain_b = pl.BlockSpec((1, W_B), lambda i: (0, 0), pipeline_mode=once)
    gain_m = pl.BlockSpec((1, W_M), lambda i: (0, 0), pipeline_mode=once)
    obn, omn, mkv, ob_s, om_s = pl.pallas_call(
        functools.partial(_side_body, n_chunks=tm // CHUNK, n_new=n_new, per_batch=per_batch),
        out_shape=[jax.ShapeDtypeStruct((batch * seq, W_B), BF16),
                   jax.ShapeDtypeStruct((batch, seq, W_M), BF16),
                   jax.ShapeDtypeStruct((batch * N_MEM, 2 * W_M), F32),
                   jax.ShapeDtypeStruct(u_s.shape, F32),
                   jax.ShapeDtypeStruct((dec_batch, n_rows, HEAD_DIM), F32)],
        grid=(n_steps,),
        in_specs=[
            pl.BlockSpec((tm, W_B), lambda i: (i, COL_U // W_B)),
            pl.BlockSpec((tm, W_B), lambda i: (i, COL_V // W_B)),
            pl.BlockSpec((H_B, CHUNK, CHUNK), lambda i: (0, 0, 0), pipeline_mode=once),
            pl.BlockSpec((CHUNK, H_B), lambda i: (0, 0), pipeline_mode=once),
            gain_b,
            pl.BlockSpec((1, tm, W_M), lambda i: (i // per_batch, i % per_batch, COL_QM // W_M)),
            gain_m,
            pl.BlockSpec((N_MEM, D_MODEL), lambda i: (i // per_batch, 0)),
            pl.BlockSpec((1, D_MODEL), lambda i: (0, 0), pipeline_mode=once),
            pl.BlockSpec((D_MODEL, 2 * W_M), lambda i: (0, 0), pipeline_mode=once),
            pl.BlockSpec((1, 2 * W_M), lambda i: (0, 0), pipeline_mode=once),
            sgu_s_spec, sgu_s_spec,
            pl.BlockSpec(w_l.shape, lambda i: (0, 0, 0), pipeline_mode=once),
            pl.BlockSpec(b_l.shape, lambda i: (0, 0), pipeline_mode=once),
            qs_spec, cache_spec, cache_spec,
        ],
        out_specs=[pl.BlockSpec((tm, W_B), lambda i: (i, 0)), om_spec,
                   pl.BlockSpec((N_MEM, 2 * W_M), lambda i: (i // per_batch, 0)),
                   sgu_s_spec, qs_spec],
        scratch_shapes=[pltpu.VMEM((tm, W_B), F32), pltpu.VMEM((tm, W_M), F32)],
        compiler_params=_params("arbitrary"),
        name="side_mixers",
    )(z, z, w_s, b_s.T, g_b.reshape(1, W_B), z3, g_m.reshape(1, W_M),
      mem, g_mem.reshape(1, D_MODEL), w_kv, gain_kv, u_s, v_s, w_l, b_l,
      qm_s.reshape(dec_batch, n_rows, HEAD_DIM),
      cache_k.reshape(dec_batch, N_MEM * H_M, HEAD_DIM), cache_v.reshape(dec_batch, N_MEM * H_M, HEAD_DIM))
    return obn, omn.reshape(batch * seq, W_M), mkv, ob_s, om_s.reshape(dec_batch, n_new, W_M)


COL_KA, COL_VA, COL_U, COL_V, COL_QM = W_A, 2 * W_A, 3 * W_A, 3 * W_A + W_B, 3 * W_A + 2 * W_B


def _mixer_inputs(normed_groups, w_in, g_qa, g_ka, g_sgu, g_qm, *, n_tiles):
    gain = jnp.concatenate([jnp.tile(g_qa, H_A), jnp.tile(g_ka, H_A), jnp.ones((W_A + W_B,), F32),
                            g_sgu.reshape(W_B), jnp.tile(g_qm, H_M)])
    normed = [c // PROJ_TILE for lo, hi in ((0, COL_VA), (COL_V, COL_QM + W_M))
              for c in range(lo, hi, PROJ_TILE)]
    return _proj(normed_groups, w_in, gain, normed, n_tiles=n_tiles, chunk=256)


def kernel(x_prompt, x_sample, cache_win_k, cache_win_v, cache_mem_k, cache_mem_v, mem_prompt, rel_bias, g_ffn1, w1_gate, w1_up, w1_down, g_mix, w_in, g_qa, g_ka, g_sgu, w_sgu, b_sgu, g_qm, g_mem, w_mem_kv, g_km, g_mix_out, w_out, g_ffn2, w2_gate, w2_up, w2_down):
    batch, seq, _ = x_prompt.shape
    dec_batch, n_new, _ = x_sample.shape
    depth = g_ffn1.shape[0]
    n_p = batch * seq
    n_s = dec_batch * n_new

    y_p = x_prompt.reshape(n_p, D_MODEL)
    y_s = x_sample.reshape(n_s, D_MODEL)
    outs = [[] for _ in range(7)]
    for l in range(depth):
        h_p, h_s, hn_p, hn_s = _ffn([y_p, y_s], g_ffn1[l], w1_gate[l], w1_up[l], w1_down[l],
                                    g_next=g_mix[l], n_tiles=8, tf=256)
        z_p, z_s = _mixer_inputs([hn_p, hn_s], w_in[l], g_qa[l], g_ka[l], g_sgu[l], g_qm[l], n_tiles=4)

        zq, zk, zv, zu, zg, zm = (z_s[:, lo:hi] for lo, hi in zip(
            (0, COL_KA, COL_VA, COL_U, COL_V, COL_QM), (COL_KA, COL_VA, COL_U, COL_V, COL_QM, z_s.shape[1])))
        heads = (dec_batch, n_new, H_A, HEAD_DIM)
        oa_p, k_p, v_p, oa_s = _attention(z_p, zq.reshape(heads), zk.reshape(heads), zv.reshape(heads),
                                          cache_win_k[l], cache_win_v[l], rel_bias, batch=batch, seq=seq)

        ob_p, om_p, mkv, ob_s, om_s = _side_mixers(
            z_p, mem_prompt.reshape(batch * N_MEM, D_MODEL), g_mem[l], w_mem_kv[l], g_km[l],
            zu.reshape(dec_batch, n_new, W_B), zg.reshape(dec_batch, n_new, W_B),
            zm.reshape(dec_batch, n_new, H_M, HEAD_DIM), cache_mem_k[l], cache_mem_v[l],
            w_sgu[l], b_sgu[l], g_mix_out[l][W_A:W_A + W_B], g_mix_out[l][W_A + W_B:],
            batch=batch, seq=seq, n_steps=8)
        keep = min(DILATED[-1][0], seq)
        outs[0].append(k_p[:, seq - keep:])
        outs[1].append(v_p[:, seq - keep:])
        outs[2].append(mkv[:, :W_M].reshape(batch, N_MEM, H_M, HEAD_DIM))
        outs[3].append(mkv[:, W_M:].reshape(batch, N_MEM, H_M, HEAD_DIM))
        outs[4].append(zk.reshape(heads))
        outs[5].append(zv.reshape(heads))
        outs[6].append(zg.reshape(dec_batch, n_new, H_B, C_B))

        h_p, h_s = _outproj([(oa_p, ob_p, om_p, h_p),
                             (oa_s.reshape(n_s, W_A), ob_s.reshape(n_s, W_B), om_s.reshape(n_s, W_M), h_s)],
                            g_mix_out[l], w_out[l], n_tiles=4, tn=256)
        y_p, y_s = _ffn([h_p, h_s], g_ffn2[l], w2_gate[l], w2_up[l], w2_down[l], n_tiles=8, tf=256)

    return (y_p.reshape(batch, seq, D_MODEL), y_s.reshape(dec_batch, n_new, D_MODEL),
            *[jnp.stack(o) for o in outs])
```

```python
import functools

import numpy as np
import jax
import jax.numpy as jnp
from jax import lax
from jax.experimental import pallas as pl
from jax.experimental.pallas import tpu as pltpu

D_MODEL = 2048
HEAD_DIM = 128
H_A = 8
H_B = 4
C_B = 128
H_M = 4
W_A = H_A * HEAD_DIM
W_B = H_B * C_B
W_M = H_M * HEAD_DIM
D_FF = 5632
DILATED = ((128, 1), (512, 4), (2048, 16))
BLK = 128
CHUNK = 128
N_MEM = 256
N_BUCKETS = 32
REL_MAX_DIST = 2048
EPS = 1e-6
SCALE = HEAD_DIM ** -0.5
NEG = -1e30

LANES = 128
BF16_ROWS = 16
PROJ_TILE = 4 * HEAD_DIM
VMEM_LIMIT = 60 * 1024 * 1024

F32 = jnp.float32
BF16 = jnp.bfloat16


def _rel_bucket(dist):
    dist = np.asarray(dist, np.int64)
    max_exact = N_BUCKETS // 2
    large = max_exact + (np.log(np.maximum(dist, 1) / max_exact) / np.log(REL_MAX_DIST / max_exact)
                         * (N_BUCKETS - max_exact)).astype(np.int64)
    large = np.minimum(large, N_BUCKETS - 1)
    return np.where(dist < max_exact, dist, large).astype(np.int32)


def _rms(t, g):
    return t * lax.rsqrt(jnp.mean(t * t, axis=-1, keepdims=True) + EPS) * g


def _dot(a, b):
    return jnp.dot(a, b, preferred_element_type=F32)


def _dot_nt(a, b):
    return lax.dot_general(a, b, (((1,), (1,)), ((), ())), preferred_element_type=F32)


def _params(*sem):
    return pltpu.CompilerParams(dimension_semantics=sem, vmem_limit_bytes=VMEM_LIMIT)


def _tile_rows(groups, n_tiles):
    rows = [x.shape[0] // n_tiles for x in groups]
    assert all(r * n_tiles == x.shape[0] and r % BF16_ROWS == 0 for r, x in zip(rows, groups))
    return rows


def _bounds(refs):
    out, lo = [], 0
    for ref in refs:
        out.append((lo, lo + ref.shape[0]))
        lo += ref.shape[0]
    return out


def _chunk_edges(total, chunk):
    edges = list(range(0, total, chunk)) + [total]
    if len(edges) > 2 and edges[-1] - edges[-2] < chunk // 2:
        del edges[-2]
    return list(zip(edges[:-1], edges[1:]))


def _ffn_body(*refs, n_groups, with_next_norm):
    x_refs, refs = refs[:n_groups], refs[n_groups:]
    gn_ref = refs[0] if with_next_norm else None
    g_ref, wg_ref, wu_ref, wd_ref = refs[with_next_norm:with_next_norm + 4]
    refs = refs[with_next_norm + 4:]
    o_refs, on_refs, xn_ref = refs[:n_groups], refs[n_groups:-1], refs[-1]
    bounds = _bounds(x_refs)
    j = pl.program_id(1)

    @pl.when(j == 0)
    def _():
        for x_ref, o_ref, (lo, hi) in zip(x_refs, o_refs, bounds):
            x = x_ref[...]
            xn_ref[lo:hi, :] = _rms(x, g_ref[...]).astype(BF16)
            o_ref[...] = x

    xn = xn_ref[...]
    gate = _dot(xn, wg_ref[...].astype(BF16))
    up = _dot(xn, wu_ref[...].astype(BF16))
    act = (gate * jax.nn.sigmoid(gate)) * up * 0.5
    res = _dot(act.astype(BF16), wd_ref[...].astype(BF16))
    for o_ref, (lo, hi) in zip(o_refs, bounds):
        o_ref[...] += res[lo:hi]

    if with_next_norm:
        @pl.when(j == pl.num_programs(1) - 1)
        def _():
            for o_ref, on_ref in zip(o_refs, on_refs):
                on_ref[...] = _rms(o_ref[...], gn_ref[...]).astype(BF16)


def _ffn(groups, g, wg, wu, wd, *, g_next=None, n_tiles, tf):
    rows = _tile_rows(groups, n_tiles)
    with_next_norm = g_next is not None
    gain_spec = pl.BlockSpec((1, D_MODEL), lambda i, j: (0, 0), pipeline_mode=pl.Buffered(1))
    row_specs = [pl.BlockSpec((r, D_MODEL), lambda i, j: (i, 0)) for r in rows]
    out_shape = [jax.ShapeDtypeStruct(x.shape, F32) for x in groups]
    gains = [g.reshape(1, D_MODEL)]
    if with_next_norm:
        out_shape += [jax.ShapeDtypeStruct(x.shape, BF16) for x in groups]
        gains.insert(0, g_next.reshape(1, D_MODEL))
    return pl.pallas_call(
        functools.partial(_ffn_body, n_groups=len(groups), with_next_norm=with_next_norm),
        out_shape=out_shape,
        grid=(n_tiles, D_FF // tf),
        in_specs=(
            row_specs + [gain_spec] * len(gains)
            + [pl.BlockSpec((D_MODEL, tf), lambda i, j: (0, j)),
               pl.BlockSpec((D_MODEL, tf), lambda i, j: (0, j)),
               pl.BlockSpec((tf, D_MODEL), lambda i, j: (j, 0))]),
        out_specs=row_specs * (2 if with_next_norm else 1),
        scratch_shapes=[pltpu.VMEM((sum(rows), D_MODEL), BF16)],
        compiler_params=_params("arbitrary", "arbitrary"),
        name="ffn",
    )(*groups, *gains, wg, wu, wd)


def _proj_body(*refs, n_groups, norm_tiles, chunk):
    x_refs = refs[:n_groups]
    w_ref, gain_ref = refs[n_groups:n_groups + 2]
    o_refs, xn_ref = refs[n_groups + 2:-1], refs[-1]
    bounds = _bounds(x_refs)
    j = pl.program_id(1)

    @pl.when(j == 0)
    def _():
        for x_ref, (lo, hi) in zip(x_refs, bounds):
            xn_ref[lo:hi, :] = x_ref[...]

    w = w_ref[...].astype(BF16)
    gain = gain_ref[...]
    normed = functools.reduce(jnp.logical_or, [j == t for t in norm_tiles])

    def finish(z):
        heads = []
        for hh in range(PROJ_TILE // HEAD_DIM):
            sl = slice(hh * HEAD_DIM, (hh + 1) * HEAD_DIM)
            zz = z[:, sl]
            scale = lax.rsqrt(jnp.mean(zz * zz, axis=-1, keepdims=True) + EPS)
            heads.append(zz * jnp.where(normed, scale, 1.0) * gain[:, sl])
        return jnp.concatenate(heads, axis=1)

    for c0, c1 in _chunk_edges(bounds[-1][1], chunk):
        z = finish(_dot(xn_ref[c0:c1, :], w))
        for o_ref, (lo, hi) in zip(o_refs, bounds):
            a, b = max(lo, c0), min(hi, c1)
            if a < b:
                o_ref[a - lo:b - lo, :] = z[a - c0:b - c0, :]


def _proj(groups, w, gain, norm_tiles, *, n_tiles, chunk):
    d_in, d_out = w.shape
    rows = _tile_rows(groups, n_tiles)
    assert all(x.dtype == BF16 for x in groups)
    return pl.pallas_call(
        functools.partial(_proj_body, n_groups=len(groups), norm_tiles=tuple(norm_tiles), chunk=chunk),
        out_shape=[jax.ShapeDtypeStruct((x.shape[0], d_out), F32) for x in groups],
        grid=(n_tiles, d_out // PROJ_TILE),
        in_specs=(
            [pl.BlockSpec((r, d_in), lambda i, j: (i, 0)) for r in rows]
            + [pl.BlockSpec((d_in, PROJ_TILE), lambda i, j: (0, j)),
               pl.BlockSpec((1, PROJ_TILE), lambda i, j: (0, j))]),
        out_specs=[pl.BlockSpec((r, PROJ_TILE), lambda i, j: (i, j)) for r in rows],
        scratch_shapes=[pltpu.VMEM((sum(rows), d_in), BF16)],
        compiler_params=_params("arbitrary", "arbitrary"),
        name="proj",
    )(*groups, w, gain.reshape(1, d_out))


def _bias_body(idx_ref, tab_ref, o_ref):
    idx = idx_ref[...]
    tab = tab_ref[0]
    bias = jnp.full(idx.shape, NEG, F32)
    for bkt in range(N_BUCKETS):
        bias = jnp.where(idx == bkt, tab[:, bkt:bkt + 1], bias)
    o_ref[0] = bias


def _bias_lookup(idx, tab, *, n_col_tiles):
    n_rows, n_cols = idx.shape
    n_g, tab_rows, _ = tab.shape
    tc = n_cols // n_col_tiles
    assert tc * n_col_tiles == n_cols and tc % LANES == 0 and tab_rows in (1, n_rows)
    return pl.pallas_call(
        _bias_body,
        out_shape=jax.ShapeDtypeStruct((n_g, n_rows, n_cols), F32),
        grid=(n_g, n_col_tiles),
        in_specs=[pl.BlockSpec((n_rows, tc), lambda g, j: (0, j)),
                  pl.BlockSpec((1, tab_rows, N_BUCKETS), lambda g, j: (g, 0, 0))],
        out_specs=pl.BlockSpec((1, n_rows, tc), lambda g, j: (g, 0, j)),
        compiler_params=_params("parallel", "parallel"),
        name="bias_lookup",
    )(idx, tab)


SLABS = DILATED[1][1]
PIECE = BLK // SLABS


def _prompt_bias_index():
    step = np.arange(BLK)[:, None] + BLK - np.arange(2 * BLK)[None, :]
    perm = (SLABS * np.arange(PIECE)[None, :] + np.arange(SLABS)[:, None]).reshape(BLK)
    maps = []
    for win, dil in DILATED:
        n_steps = win // dil + 1
        band = (step >= 0) & (step < n_steps)
        bucket = _rel_bucket(np.clip(step, 0, n_steps - 1) * dil)
        idx = np.where(band, bucket, -1)
        if dil == 1:
            idx = idx[np.ix_(perm, np.concatenate([perm, BLK + perm]))]
        maps.append(idx)
    return np.stack(maps).astype(np.int32)


def _block_pieces(dil, r, n):
    if dil == 1:
        return [(s, slice(n * PIECE, (n + 1) * PIECE), PIECE) for s in range(SLABS)]
    if dil == SLABS:
        return [(r, slice(n * BLK, (n + 1) * BLK), BLK)]
    sub = dil // SLABS
    s, t = r % SLABS, r // SLABS
    return [(s, pl.ds(t + sub * BLK * n, BLK, stride=sub), BLK)]


def _attn_body(bias_s, q_ref, k_ref, v_ref, o_ref, kh_ref, vh_ref, slab_s, acc_s, m_s, l_s, *,
               seq, group_size):
    for part, ref in enumerate((q_ref, k_ref, v_ref)):
        for s in range(SLABS):
            slab_s[part, s] = ref[pl.ds(s, seq // SLABS, stride=SLABS), :]
    q_s, k_s, v_s = ([slab_s.at[part, s] for s in range(SLABS)] for part in range(3))

    head_rows = pl.ds(pl.program_id(1), seq, stride=H_A)
    kh_ref[0, head_rows, :] = k_ref[...]
    vh_ref[0, head_rows, :] = v_ref[...]

    def load(slabs, pieces):
        parts = [slabs[s][idx, :] for s, idx, _ in pieces]
        return parts[0] if len(parts) == 1 else jnp.concatenate(parts, axis=0)

    def gather(slabs, blocks):
        return jnp.stack([load(slabs, pieces) for pieces in blocks]).astype(BF16)

    def qk(q, k):
        return jnp.einsum("gid,gjd->gij", q, k, preferred_element_type=F32)

    def pv(p, v):
        return jnp.einsum("gij,gjd->gid", p.astype(BF16), v, preferred_element_type=F32)

    def bias_of(dis, cols):
        return jnp.stack([bias_s[0, di, :, cols] for di in dis])

    def group(dis, curs, prevs):
        q = gather(q_s, curs)
        s_c = qk(q, gather(k_s, curs)) * SCALE + bias_of(dis, slice(BLK, None))
        if prevs is None:
            m = jnp.max(s_c, axis=-1, keepdims=True)
            p_c = jnp.exp(s_c - m)
            den = jnp.sum(p_c, axis=-1, keepdims=True)
            o = pv(p_c, gather(v_s, curs))
        else:
            s_p = qk(q, gather(k_s, prevs)) * SCALE + bias_of(dis, slice(0, BLK))
            m = jnp.max(jnp.maximum(s_c, s_p), axis=-1, keepdims=True)
            p_c = jnp.exp(s_c - m)
            p_p = jnp.exp(s_p - m)
            den = jnp.sum(p_c + p_p, axis=-1, keepdims=True)
            o = pv(p_c, gather(v_s, curs)) + pv(p_p, gather(v_s, prevs))
        for g, (di, pieces) in enumerate(zip(dis, curs)):
            m_g = jnp.broadcast_to(m[g], (BLK, HEAD_DIM))
            l_g = jnp.broadcast_to(den[g], (BLK, HEAD_DIM))
            lo = 0
            for s, idx, n_rows in pieces:
                acc_s[di, s, idx, :] = o[g, lo:lo + n_rows]
                m_s[di, s, idx, :] = m_g[lo:lo + n_rows]
                l_s[di, s, idx, :] = l_g[lo:lo + n_rows]
                lo += n_rows

    first, later = [], []
    for di, (win, dil) in enumerate(DILATED):
        for r in range(dil):
            first.append((di, dil, r, 0))
            later += [(di, dil, r, n) for n in range(1, seq // (dil * BLK))]
    groups = []
    for blocks, has_prev in ((first, False), (later, True)):
        for lo in range(0, len(blocks), group_size):
            part = blocks[lo:lo + group_size]
            groups.append(([di for di, _, _, _ in part],
                           [_block_pieces(dil, r, n) for _, dil, r, n in part],
                           [_block_pieces(dil, r, n - 1) for _, dil, r, n in part] if has_prev else None))
    for args in groups:
        group(*args)

    rows_per = 256
    per_slab = seq // SLABS // rows_per

    def merge(c, carry):
        s = c // per_slab
        r0 = pl.multiple_of((c % per_slab) * rows_per, rows_per)
        rc = pl.ds(r0, rows_per)
        ms = [m_s[di, s, rc, :] for di in range(len(DILATED))]
        mx = jnp.maximum(jnp.maximum(ms[0], ms[1]), ms[2])
        num = 0.0
        den = 0.0
        for di in range(len(DILATED)):
            wgt = jnp.exp(ms[di] - mx)
            num = num + wgt * acc_s[di, s, rc, :]
            den = den + wgt * l_s[di, s, rc, :]
        o_ref[pl.ds(SLABS * r0 + s, rows_per, stride=SLABS), :] = num / den
        return carry

    lax.fori_loop(0, SLABS * per_slab, merge, 0)


def _sgu_body(u_ref, v_ref, w_ref, bt_ref, o_ref, *, n_chunks):
    tri = (lax.broadcasted_iota(jnp.int32, (CHUNK, CHUNK), 0)
           >= lax.broadcasted_iota(jnp.int32, (CHUNK, CHUNK), 1))
    for g in range(H_B):
        w = jnp.where(tri, w_ref[g], 0.0).astype(BF16)
        bias = bt_ref[:, g:g + 1]
        cols = slice(g * C_B, (g + 1) * C_B)
        for c in range(n_chunks):
            rws = slice(c * CHUNK, (c + 1) * CHUNK)
            mixed = _dot(w, v_ref[rws, cols].astype(BF16)) + bias
            o_ref[rws, cols] = u_ref[rws, cols] * mixed


def _mem_kv_body(mem_ref, g_ref, w_ref, gain_ref, kv_ref):
    z = _dot(_rms(mem_ref[...], g_ref[...]).astype(BF16), w_ref[...].astype(BF16))
    for hh in range(H_M):
        sl = slice(hh * HEAD_DIM, (hh + 1) * HEAD_DIM)
        kv_ref[:, sl] = _rms(z[:, sl], gain_ref[:, sl])
    kv_ref[:, W_M:] = z[:, W_M:]


def _mem_body(q_ref, kv_ref, o_ref):
    for hh in range(H_M):
        k_sl = slice(hh * HEAD_DIM, (hh + 1) * HEAD_DIM)
        v_sl = slice(W_M + hh * HEAD_DIM, W_M + (hh + 1) * HEAD_DIM)
        s = _dot_nt(q_ref[0, :, k_sl].astype(BF16), kv_ref[:, k_sl].astype(BF16)) * SCALE
        m = jnp.max(s, axis=-1, keepdims=True)
        p = jnp.exp(s - m)
        den = jnp.sum(p, axis=-1, keepdims=True)
        o_ref[:, k_sl] = _dot(p.astype(BF16), kv_ref[:, v_sl].astype(BF16)) / den


def _mems_body(q_ref, k_ref, v_ref, o_ref):
    for b in range(q_ref.shape[0]):
        s = _dot_nt(q_ref[b].astype(BF16), k_ref[b].astype(BF16)) * SCALE
        row = lax.broadcasted_iota(jnp.int32, s.shape, 0)
        col = lax.broadcasted_iota(jnp.int32, s.shape, 1)
        s = jnp.where(row % H_M == col % H_M, s, NEG)
        m = jnp.max(s, axis=-1, keepdims=True)
        p = jnp.exp(s - m)
        den = jnp.sum(p, axis=-1, keepdims=True)
        o_ref[b] = _dot(p.astype(BF16), v_ref[b].astype(BF16)) / den


def _outproj_body(*refs, n_groups):
    src_refs = [refs[4 * k:4 * k + 3] for k in range(n_groups)]
    h_refs = [refs[4 * k + 3] for k in range(n_groups)]
    g_ref, w_ref = refs[4 * n_groups:4 * n_groups + 2]
    o_refs = refs[4 * n_groups + 2:5 * n_groups + 2]
    cat_ref = refs[-1]
    bounds = _bounds(h_refs)

    @pl.when(pl.program_id(1) == 0)
    def _():
        for srcs, (r0, r1) in zip(src_refs, bounds):
            lo = 0
            for src in srcs:
                hi = lo + src.shape[1]
                x = src[...]
                cat_ref[r0:r1, lo:hi] = x if x.dtype == BF16 else _rms(x, g_ref[:, lo:hi]).astype(BF16)
                lo = hi

    res = _dot(cat_ref[...], w_ref[...].astype(BF16))
    for h_ref, o_ref, (r0, r1) in zip(h_refs, o_refs, bounds):
        o_ref[...] = h_ref[...] + res[r0:r1]


def _outproj(groups, g, w, *, n_tiles, tn):
    mix = W_A + W_B + W_M
    rows = _tile_rows([grp[3] for grp in groups], n_tiles)
    in_specs = []
    for r in rows:
        in_specs += [pl.BlockSpec((r, wd), lambda i, j: (i, 0)) for wd in (W_A, W_B, W_M)]
        in_specs += [pl.BlockSpec((r, tn), lambda i, j: (i, j))]
    in_specs += [pl.BlockSpec((1, mix), lambda i, j: (0, 0), pipeline_mode=pl.Buffered(1)),
                 pl.BlockSpec((mix, tn), lambda i, j: (0, j))]
    return pl.pallas_call(
        functools.partial(_outproj_body, n_groups=len(groups)),
        out_shape=[jax.ShapeDtypeStruct(grp[3].shape, F32) for grp in groups],
        grid=(n_tiles, D_MODEL // tn),
        in_specs=in_specs,
        out_specs=[pl.BlockSpec((r, tn), lambda i, j: (i, j)) for r in rows],
        scratch_shapes=[pltpu.VMEM((sum(rows), mix), BF16)],
        compiler_params=_params("arbitrary", "arbitrary"),
        name="outproj",
    )(*[a for grp in groups for a in grp], g.reshape(1, mix), w)


def _sample_bias_index(n_new, n_past):
    (win1, dil1), (tail, dil4), (win16, dil16) = DILATED
    assert dil1 == 1 and win1 == BLK and n_past == win16 and n_past % dil16 == 0
    assert n_past >= tail and tail % dil16 == 0 and n_new <= dil4 and n_new * H_A <= LANES
    n16 = (n_past - tail) // dil16
    t = np.arange(n_new)[:, None]

    def per_head(index):
        same = np.eye(H_A, dtype=bool)[None, :, None, :]
        full = np.where(same, index[:, None, :, None], -1)
        return full.reshape(n_new * H_A, index.shape[1] * H_A)

    dist = BLK + t - np.arange(BLK)[None, :]
    d1 = np.where(dist <= win1, _rel_bucket(dist), -1)
    dist = tail + t - np.arange(tail)[None, :]
    d4 = np.where((dist % dil4 == 0) & (dist <= tail), _rel_bucket(dist), -1)
    dist = tail + t - np.arange(tail)[None, :]
    d16_tail = np.where(dist % dil16 == 0, _rel_bucket(dist), -1)
    m, tk = np.divmod(np.arange(n16 * n_new)[None, :], n_new)
    dist = n_past + t - dil16 * m - tk
    d16 = np.where(tk == t, _rel_bucket(np.maximum(dist, 0)), -1)
    s = np.arange(LANES // H_A)[None, :]
    dist = t - s
    new = [np.where((s < n_new) & (dist >= 0) & (dist % dil == 0),
                    _rel_bucket(np.maximum(dist, 0)), -1) for _, dil in DILATED]
    pieces = [d1, d4, d16_tail, d16] + new
    return np.concatenate([per_head(p) for p in pieces], axis=1).astype(np.int32)


def _sattn_body(bias_ref, q_ref, kn_ref, vn_ref, kt_ref, k16_ref, vt_ref, v16_ref, o_ref):
    def flat(ref):
        x = ref[0]
        return x.reshape(-1, HEAD_DIM).astype(BF16)

    q = q_ref[0].astype(BF16)
    kt, vt = flat(kt_ref), flat(vt_ref)
    k16, v16 = flat(k16_ref), flat(v16_ref)
    def new_rows(ref):
        x = ref[0]
        return jnp.concatenate([x, jnp.zeros((LANES - x.shape[0], HEAD_DIM), F32)], axis=0).astype(BF16)

    kn, vn = new_rows(kn_ref), new_rows(vn_ref)
    n_t, n_1 = kt.shape[0], BLK * H_A

    raw_t = _dot_nt(q, kt)
    raw_n = _dot_nt(q, kn)
    raws = [raw_t[:, n_t - n_1:], raw_t, raw_t, _dot_nt(q, k16)] + [raw_n] * len(DILATED)
    logits = []
    lo = 0
    for raw in raws:
        hi = lo + raw.shape[1]
        logits.append(raw * SCALE + bias_ref[:, lo:hi])
        lo = hi

    m = functools.reduce(jnp.maximum, [jnp.max(x, axis=-1, keepdims=True) for x in logits])
    probs = [jnp.exp(x - m) for x in logits]
    den = functools.reduce(lambda a, b: a + b, [jnp.sum(p, axis=-1, keepdims=True) for p in probs])
    p_new = functools.reduce(lambda a, b: a + b, probs[4:])
    o = (_dot(probs[0].astype(BF16), vt[n_t - n_1:, :]) + _dot(probs[1].astype(BF16), vt)
         + _dot(probs[2].astype(BF16), vt) + _dot(probs[3].astype(BF16), v16)
         + _dot(p_new.astype(BF16), vn))
    o_ref[0] = o / den


N_PROMPT_IN, N_SAMPLE_IN, N_PROMPT_OUT = 4, 8, 3


def _attention_body(*refs, seq, group_size):
    prompt_in, refs = refs[:N_PROMPT_IN], refs[N_PROMPT_IN:]
    sample_in, refs = refs[:N_SAMPLE_IN], refs[N_SAMPLE_IN:]
    prompt_out, sample_out, scratch = refs[:N_PROMPT_OUT], refs[N_PROMPT_OUT], refs[N_PROMPT_OUT + 1:]
    _sattn_body(*sample_in, sample_out)
    _attn_body(*prompt_in, *prompt_out, *scratch, seq=seq, group_size=group_size)


def _attention(z, qa, ka, va, cache_k, cache_v, rel_bias, *, batch, seq):
    n = batch * seq
    dec_batch, n_new = qa.shape[:2]
    n_past = cache_k.shape[1]
    assert all(seq % (dil * BLK) == 0 for _, dil in DILATED) and dec_batch == batch * H_A

    idx = _prompt_bias_index()
    n_d = idx.shape[0]
    bias_p = _bias_lookup(jnp.asarray(idx.reshape(n_d * BLK, 2 * BLK)), rel_bias.T[:, None, :],
                          n_col_tiles=1)
    bias_p = bias_p.reshape(H_A, n_d, BLK, 2 * BLK)
    prompt_in_specs = [pl.BlockSpec((1, n_d, BLK, 2 * BLK), lambda b, h: (h, 0, 0, 0))]
    prompt_in_specs += [pl.BlockSpec((seq, HEAD_DIM), lambda b, h, part=part: (b, part * H_A + h))
                        for part in range(3)]
    by_head = pl.BlockSpec((1, seq * H_A, HEAD_DIM), lambda b, h: (b, 0, 0), pipeline_mode=pl.Buffered(1))
    by_head_shape = jax.ShapeDtypeStruct((batch, seq * H_A, HEAD_DIM), F32)
    stat = pltpu.VMEM((n_d, SLABS, seq // SLABS, HEAD_DIM), F32)

    tail = DILATED[1][0]
    dil16 = DILATED[2][1]
    n16 = n_past // dil16
    n16_head = (n_past - tail) // dil16
    n_rows = n_new * H_A
    bias_s = _bias_lookup(jnp.asarray(_sample_bias_index(n_new, n_past)),
                          jnp.tile(rel_bias.T, (n_new, 1))[None], n_col_tiles=3)[0]

    def rows(x):
        return x.reshape(dec_batch, n_rows, HEAD_DIM)

    def sample(b, h):
        return b * H_A + h

    k_res = cache_k.reshape(dec_batch, n16, dil16, H_A, HEAD_DIM)
    v_res = cache_v.reshape(dec_batch, n16, dil16, H_A, HEAD_DIM)
    q_spec = pl.BlockSpec((1, n_rows, HEAD_DIM), lambda b, h: (sample(b, h), 0, 0))
    tail_spec = pl.BlockSpec((1, tail, H_A, HEAD_DIM),
                             lambda b, h: (sample(b, h), n_past // tail - 1, 0, 0))
    res_spec = pl.BlockSpec((1, n16_head, n_new, H_A, HEAD_DIM), lambda b, h: (sample(b, h), 0, 0, 0, 0))
    sample_in_specs = [pl.BlockSpec(bias_s.shape, lambda b, h: (0, 0), pipeline_mode=pl.Buffered(1)),
                       q_spec, q_spec, q_spec, tail_spec, res_spec, tail_spec, res_spec]
    assert len(prompt_in_specs) == N_PROMPT_IN and len(sample_in_specs) == N_SAMPLE_IN

    oa_p, kh, vh, oa_s = pl.pallas_call(
        functools.partial(_attention_body, seq=seq, group_size=32),
        out_shape=[jax.ShapeDtypeStruct((n, W_A), F32), by_head_shape, by_head_shape,
                   jax.ShapeDtypeStruct((dec_batch, n_rows, HEAD_DIM), F32)],
        grid=(batch, H_A),
        in_specs=prompt_in_specs + sample_in_specs,
        out_specs=[pl.BlockSpec((seq, HEAD_DIM), lambda b, h: (b, h)), by_head, by_head, q_spec],
        scratch_shapes=[stat, stat, stat, stat],
        compiler_params=_params("arbitrary", "arbitrary"),
        name="attention",
    )(bias_p, z, z, z,
      bias_s, rows(qa), rows(ka), rows(va), cache_k, k_res, cache_v, v_res)
    heads = (batch, seq, H_A, HEAD_DIM)
    return oa_p, kh.reshape(heads), vh.reshape(heads), oa_s.reshape(dec_batch, n_new, W_A)


def _sgus_body(u_ref, v_ref, w_ref, b_ref, o_ref, *, n_new):
    t_idx = lax.broadcasted_iota(jnp.int32, (1, n_new, W_B), 1)
    mixed = 0.0
    for s in range(n_new):
        w_s = jnp.where(t_idx >= s, w_ref[s][None], 0.0).astype(BF16).astype(F32)
        mixed = mixed + w_s * v_ref[:, s:s + 1, :].astype(BF16).astype(F32)
    o_ref[...] = u_ref[...] * (mixed + b_ref[...][None])


def _side_body(u_ref, v_ref, w_ref, bt_ref, gb_ref, qm_ref, gm_ref,
               mem_ref, gmem_ref, wkv_ref, gkv_ref,
               us_ref, vs_ref, wl_ref, bl_ref, qs_ref, ck_ref, cv_ref,
               obn_ref, omn_ref, mkv_ref, obs_ref, oms_ref, ob_s, om_s, *, n_chunks, n_new, per_batch):
    _mems_body(qs_ref, ck_ref, cv_ref, oms_ref)
    _sgus_body(us_ref, vs_ref, wl_ref, bl_ref, obs_ref, n_new=n_new)

    @pl.when(pl.program_id(0) % per_batch == 0)
    def _():
        _mem_kv_body(mem_ref, gmem_ref, wkv_ref, gkv_ref, mkv_ref)

    _mem_body(qm_ref, mkv_ref, om_s)
    omn_ref[0] = _rms(om_s[...], gm_ref[...]).astype(BF16)
    _sgu_body(u_ref, v_ref, w_ref, bt_ref, ob_s, n_chunks=n_chunks)
    obn_ref[...] = _rms(ob_s[...], gb_ref[...]).astype(BF16)


def _side_mixers(z, mem, g_mem, w_kv, g_km, u_s, v_s, qm_s, cache_k, cache_v, w_s, b_s, g_b, g_m, *,
                 batch, seq, n_steps):
    dec_batch, n_new, _ = u_s.shape
    tm = batch * seq // n_steps
    tb = dec_batch // n_steps
    per_batch = seq // tm
    assert tm * per_batch == seq and tb * n_steps == dec_batch and tm % CHUNK == 0
    n_rows = n_new * H_M
    w_l = jnp.repeat(jnp.transpose(w_s[:, :n_new, :n_new], (2, 1, 0)), C_B, axis=-1)
    b_l = jnp.repeat(b_s[:, :n_new].T, C_B, axis=-1)
    z3 = z.reshape(batch, seq, -1)
    gain_kv = jnp.concatenate([jnp.tile(g_km, H_M), jnp.ones((W_M,), F32)]).reshape(1, 2 * W_M)
    once = pl.Buffered(1)
    sgu_s_spec = pl.BlockSpec((tb, n_new, W_B), lambda i: (i, 0, 0))
    qs_spec = pl.BlockSpec((tb, n_rows, HEAD_DIM), lambda i: (i, 0, 0))
    cache_spec = pl.BlockSpec((tb, N_MEM * H_M, HEAD_DIM), lambda i: (i, 0, 0))
    om_spec = pl.BlockSpec((1, tm, W_M), lambda i: (i // per_batch, i % per_batch, 0))
    gain_b = pl.BlockSpec((1, W_B), lambda i: (0, 0), pipeline_mode=once)
    gain_m = pl.BlockSpec((1, W_M), lambda i: (0, 0), pipeline_mode=once)
    obn, omn, mkv, ob_s, om_s = pl.pallas_call(
        functools.partial(_side_body, n_chunks=tm // CHUNK, n_new=n_new, per_batch=per_batch),
        out_shape=[jax.ShapeDtypeStruct((batch * seq, W_B), BF16),
                   jax.ShapeDtypeStruct((batch, seq, W_M), BF16),
                   jax.ShapeDtypeStruct((batch * N_MEM, 2 * W_M), F32),
                   jax.ShapeDtypeStruct(u_s.shape, F32),
                   jax.ShapeDtypeStruct((dec_batch, n_rows, HEAD_DIM), F32)],
        grid=(n_steps,),
        in_specs=[
            pl.BlockSpec((tm, W_B), lambda i: (i, COL_U // W_B)),
            pl.BlockSpec((tm, W_B), lambda i: (i, COL_V // W_B)),
            pl.BlockSpec((H_B, CHUNK, CHUNK), lambda i: (0, 0, 0), pipeline_mode=once),
            pl.BlockSpec((CHUNK, H_B), lambda i: (0, 0), pipeline_mode=once),
            gain_b,
            pl.BlockSpec((1, tm, W_M), lambda i: (i // per_batch, i % per_batch, COL_QM // W_M)),
            gain_m,
            pl.BlockSpec((N_MEM, D_MODEL), lambda i: (i // per_batch, 0)),
            pl.BlockSpec((1, D_MODEL), lambda i: (0, 0), pipeline_mode=once),
            pl.BlockSpec((D_MODEL, 2 * W_M), lambda i: (0, 0), pipeline_mode=once),
            pl.BlockSpec((1, 2 * W_M), lambda i: (0, 0), pipeline_mode=once),
            sgu_s_spec, sgu_s_spec,
            pl.BlockSpec(w_l.shape, lambda i: (0, 0, 0), pipeline_mode=once),
            pl.BlockSpec(b_l.shape, lambda i: (0, 0), pipeline_mode=once),
            qs_spec, cache_spec, cache_spec,
        ],
        out_specs=[pl.BlockSpec((tm, W_B), lambda i: (i, 0)), om_spec,
                   pl.BlockSpec((N_MEM, 2 * W_M), lambda i: (i // per_batch, 0)),
                   sgu_s_spec, qs_spec],
        scratch_shapes=[pltpu.VMEM((tm, W_B), F32), pltpu.VMEM((tm, W_M), F32)],
        compiler_params=_params("arbitrary"),
        name="side_mixers",
    )(z, z, w_s, b_s.T, g_b.reshape(1, W_B), z3, g_m.reshape(1, W_M),
      mem, g_mem.reshape(1, D_MODEL), w_kv, gain_kv, u_s, v_s, w_l, b_l,
      qm_s.reshape(dec_batch, n_rows, HEAD_DIM),
      cache_k.reshape(dec_batch, N_MEM * H_M, HEAD_DIM), cache_v.reshape(dec_batch, N_MEM * H_M, HEAD_DIM))
    return obn, omn.reshape(batch * seq, W_M), mkv, ob_s, om_s.reshape(dec_batch, n_new, W_M)


COL_KA, COL_VA, COL_U, COL_V, COL_QM = W_A, 2 * W_A, 3 * W_A, 3 * W_A + W_B, 3 * W_A + 2 * W_B


def _mixer_inputs(normed_groups, w_in, g_qa, g_ka, g_sgu, g_qm, *, n_tiles):
    gain = jnp.concatenate([jnp.tile(g_qa, H_A), jnp.tile(g_ka, H_A), jnp.ones((W_A + W_B,), F32),
                            g_sgu.reshape(W_B), jnp.tile(g_qm, H_M)])
    normed = [c // PROJ_TILE for lo, hi in ((0, COL_VA), (COL_V, COL_QM + W_M))
              for c in range(lo, hi, PROJ_TILE)]
    return _proj(normed_groups, w_in, gain, normed, n_tiles=n_tiles, chunk=256)


def kernel(x_prompt, x_sample, cache_win_k, cache_win_v, cache_mem_k, cache_mem_v, mem_prompt, rel_bias, g_ffn1, w1_gate, w1_up, w1_down, g_mix, w_in, g_qa, g_ka, g_sgu, w_sgu, b_sgu, g_qm, g_mem, w_mem_kv, g_km, g_mix_out, w_out, g_ffn2, w2_gate, w2_up, w2_down):
    batch, seq, _ = x_prompt.shape
    dec_batch, n_new, _ = x_sample.shape
    depth = g_ffn1.shape[0]
    n_p = batch * seq
    n_s = dec_batch * n_new

    y_p = x_prompt.reshape(n_p, D_MODEL)
    y_s = x_sample.reshape(n_s, D_MODEL)
    outs = [[] for _ in range(7)]
    for l in range(depth):
        h_p, h_s, hn_p, hn_s = _ffn([y_p, y_s], g_ffn1[l], w1_gate[l], w1_up[l], w1_down[l],
                                    g_next=g_mix[l], n_tiles=8, tf=256)
        z_p, z_s = _mixer_inputs([hn_p, hn_s], w_in[l], g_qa[l], g_ka[l], g_sgu[l], g_qm[l], n_tiles=4)

        zq, zk, zv, zu, zg, zm = (z_s[:, lo:hi] for lo, hi in zip(
            (0, COL_KA, COL_VA, COL_U, COL_V, COL_QM), (COL_KA, COL_VA, COL_U, COL_V, COL_QM, z_s.shape[1])))
        heads = (dec_batch, n_new, H_A, HEAD_DIM)
        oa_p, k_p, v_p, oa_s = _attention(z_p, zq.reshape(heads), zk.reshape(heads), zv.reshape(heads),
                                          cache_win_k[l], cache_win_v[l], rel_bias, batch=batch, seq=seq)

        ob_p, om_p, mkv, ob_s, om_s = _side_mixers(
            z_p, mem_prompt.reshape(batch * N_MEM, D_MODEL), g_mem[l], w_mem_kv[l], g_km[l],
            zu.reshape(dec_batch, n_new, W_B), zg.reshape(dec_batch, n_new, W_B),
            zm.reshape(dec_batch, n_new, H_M, HEAD_DIM), cache_mem_k[l], cache_mem_v[l],
            w_sgu[l], b_sgu[l], g_mix_out[l][W_A:W_A + W_B], g_mix_out[l][W_A + W_B:],
            batch=batch, seq=seq, n_steps=8)
        keep = min(DILATED[-1][0], seq)
        outs[0].append(k_p[:, seq - keep:])
        outs[1].append(v_p[:, seq - keep:])
        outs[2].append(mkv[:, :W_M].reshape(batch, N_MEM, H_M, HEAD_DIM))
        outs[3].append(mkv[:, W_M:].reshape(batch, N_MEM, H_M, HEAD_DIM))
        outs[4].append(zk.reshape(heads))
        outs[5].append(zv.reshape(heads))
        outs[6].append(zg.reshape(dec_batch, n_new, H_B, C_B))

        h_p, h_s = _outproj([(oa_p, ob_p, om_p, h_p),
                             (oa_s.reshape(n_s, W_A), ob_s.reshape(n_s, W_B), om_s.reshape(n_s, W_M), h_s)],
                            g_mix_out[l], w_out[l], n_tiles=4, tn=256)
        y_p, y_s = _ffn([h_p, h_s], g_ffn2[l], w2_gate[l], w2_up[l], w2_down[l], n_tiles=8, tf=256)

    return (y_p.reshape(batch, seq, D_MODEL), y_s.reshape(dec_batch, n_new, D_MODEL),
            *[jnp.stack(o) for o in outs])
```

```python
import functools

import numpy as np
import jax
import jax.numpy as jnp
from jax import lax
from jax.experimental import pallas as pl
from jax.experimental.pallas import tpu as pltpu

D_MODEL = 2048
HEAD_DIM = 128
H_A = 8
H_B = 4
C_B = 128
H_M = 4
W_A = H_A * HEAD_DIM
W_B = H_B * C_B
W_M = H_M * HEAD_DIM
D_FF = 5632
DILATED = ((128, 1), (512, 4), (2048, 16))
BLK = 128
CHUNK = 128
N_MEM = 256
N_BUCKETS = 32
REL_MAX_DIST = 2048
EPS = 1e-6
SCALE = HEAD_DIM ** -0.5
NEG = -1e30

LANES = 128
BF16_ROWS = 16
PROJ_TILE = 4 * HEAD_DIM
VMEM_LIMIT = 60 * 1024 * 1024

F32 = jnp.float32
BF16 = jnp.bfloat16


def _rel_bucket(dist):
    dist = np.asarray(dist, np.int64)
    max_exact = N_BUCKETS // 2
    large = max_exact + (np.log(np.maximum(dist, 1) / max_exact) / np.log(REL_MAX_DIST / max_exact)
                         * (N_BUCKETS - max_exact)).astype(np.int64)
    large = np.minimum(large, N_BUCKETS - 1)
    return np.where(dist < max_exact, dist, large).astype(np.int32)


def _rms(t, g):
    return t * lax.rsqrt(jnp.mean(t * t, axis=-1, keepdims=True) + EPS) * g


def _dot(a, b):
    return jnp.dot(a, b, preferred_element_type=F32)


def _dot_nt(a, b):
    return lax.dot_general(a, b, (((1,), (1,)), ((), ())), preferred_element_type=F32)


def _params(*sem):
    return pltpu.CompilerParams(dimension_semantics=sem, vmem_limit_bytes=VMEM_LIMIT)


def _tile_rows(groups, n_tiles):
    rows = [x.shape[0] // n_tiles for x in groups]
    assert all(r * n_tiles == x.shape[0] and r % BF16_ROWS == 0 for r, x in zip(rows, groups))
    return rows


def _bounds(refs):
    out, lo = [], 0
    for ref in refs:
        out.append((lo, lo + ref.shape[0]))
        lo += ref.shape[0]
    return out


def _chunk_edges(total, chunk):
    edges = list(range(0, total, chunk)) + [total]
    if len(edges) > 2 and edges[-1] - edges[-2] < chunk // 2:
        del edges[-2]
    return list(zip(edges[:-1], edges[1:]))


def _ffn_body(*refs, n_groups, with_next_norm):
    x_refs, refs = refs[:n_groups], refs[n_groups:]
    gn_ref = refs[0] if with_next_norm else None
    g_ref, wg_ref, wu_ref, wd_ref = refs[with_next_norm:with_next_norm + 4]
    refs = refs[with_next_norm + 4:]
    o_refs, on_refs, xn_ref = refs[:n_groups], refs[n_groups:-1], refs[-1]
    bounds = _bounds(x_refs)
    j = pl.program_id(1)

    @pl.when(j == 0)
    def _():
        for x_ref, o_ref, (lo, hi) in zip(x_refs, o_refs, bounds):
            x = x_ref[...]
            xn_ref[lo:hi, :] = _rms(x, g_ref[...]).astype(BF16)
            o_ref[...] = x

    xn = xn_ref[...]
    gate = _dot(xn, wg_ref[...].astype(BF16))
    up = _dot(xn, wu_ref[...].astype(BF16))
    act = (gate * jax.nn.sigmoid(gate)) * up * 0.5
    res = _dot(act.astype(BF16), wd_ref[...].astype(BF16))
    for o_ref, (lo, hi) in zip(o_refs, bounds):
        o_ref[...] += res[lo:hi]

    if with_next_norm:
        @pl.when(j == pl.num_programs(1) - 1)
        def _():
            for o_ref, on_ref in zip(o_refs, on_refs):
                on_ref[...] = _rms(o_ref[...], gn_ref[...]).astype(BF16)


def _ffn(groups, g, wg, wu, wd, *, g_next=None, n_tiles, tf):
    rows = _tile_rows(groups, n_tiles)
    with_next_norm = g_next is not None
    gain_spec = pl.BlockSpec((1, D_MODEL), lambda i, j: (0, 0), pipeline_mode=pl.Buffered(1))
    row_specs = [pl.BlockSpec((r, D_MODEL), lambda i, j: (i, 0)) for r in rows]
    out_shape = [jax.ShapeDtypeStruct(x.shape, F32) for x in groups]
    gains = [g.reshape(1, D_MODEL)]
    if with_next_norm:
        out_shape += [jax.ShapeDtypeStruct(x.shape, BF16) for x in groups]
        gains.insert(0, g_next.reshape(1, D_MODEL))
    return pl.pallas_call(
        functools.partial(_ffn_body, n_groups=len(groups), with_next_norm=with_next_norm),
        out_shape=out_shape,
        grid=(n_tiles, D_FF // tf),
        in_specs=(
            row_specs + [gain_spec] * len(gains)
            + [pl.BlockSpec((D_MODEL, tf), lambda i, j: (0, j)),
               pl.BlockSpec((D_MODEL, tf), lambda i, j: (0, j)),
               pl.BlockSpec((tf, D_MODEL), lambda i, j: (j, 0))]),
        out_specs=row_specs * (2 if with_next_norm else 1),
        scratch_shapes=[pltpu.VMEM((sum(rows), D_MODEL), BF16)],
        compiler_params=_params("arbitrary", "arbitrary"),
        name="ffn",
    )(*groups, *gains, wg, wu, wd)


def _proj_body(*refs, n_groups, norm_tiles, chunk):
    x_refs = refs[:n_groups]
    w_ref, gain_ref = refs[n_groups:n_groups + 2]
    o_refs, xn_ref = refs[n_groups + 2:-1], refs[-1]
    bounds = _bounds(x_refs)
    j = pl.program_id(1)

    @pl.when(j == 0)
    def _():
        for x_ref, (lo, hi) in zip(x_refs, bounds):
            xn_ref[lo:hi, :] = x_ref[...]

    w = w_ref[...].astype(BF16)
    gain = gain_ref[...]
    normed = functools.reduce(jnp.logical_or, [j == t for t in norm_tiles])

    def finish(z):
        heads = []
        for hh in range(PROJ_TILE // HEAD_DIM):
            sl = slice(hh * HEAD_DIM, (hh + 1) * HEAD_DIM)
            zz = z[:, sl]
            scale = lax.rsqrt(jnp.mean(zz * zz, axis=-1, keepdims=True) + EPS)
            heads.append(zz * jnp.where(normed, scale, 1.0) * gain[:, sl])
        return jnp.concatenate(heads, axis=1)

    for c0, c1 in _chunk_edges(bounds[-1][1], chunk):
        z = finish(_dot(xn_ref[c0:c1, :], w))
        for o_ref, (lo, hi) in zip(o_refs, bounds):
            a, b = max(lo, c0), min(hi, c1)
            if a < b:
                o_ref[a - lo:b - lo, :] = z[a - c0:b - c0, :]


def _proj(groups, w, gain, norm_tiles, *, n_tiles, chunk):
    d_in, d_out = w.shape
    rows = _tile_rows(groups, n_tiles)
    assert all(x.dtype == BF16 for x in groups)
    return pl.pallas_call(
        functools.partial(_proj_body, n_groups=len(groups), norm_tiles=tuple(norm_tiles), chunk=chunk),
        out_shape=[jax.ShapeDtypeStruct((x.shape[0], d_out), F32) for x in groups],
        grid=(n_tiles, d_out // PROJ_TILE),
        in_specs=(
            [pl.BlockSpec((r, d_in), lambda i, j: (i, 0)) for r in rows]
            + [pl.BlockSpec((d_in, PROJ_TILE), lambda i, j: (0, j)),
               pl.BlockSpec((1, PROJ_TILE), lambda i, j: (0, j))]),
        out_specs=[pl.BlockSpec((r, PROJ_TILE), lambda i, j: (i, j)) for r in rows],
        scratch_shapes=[pltpu.VMEM((sum(rows), d_in), BF16)],
        compiler_params=_params("arbitrary", "arbitrary"),
        name="proj",
    )(*groups, w, gain.reshape(1, d_out))


def _bias_body(idx_ref, tab_ref, o_ref):
    idx = idx_ref[...]
    tab = tab_ref[0]
    bias = jnp.full(idx.shape, NEG, F32)
    for bkt in range(N_BUCKETS):
        bias = jnp.where(idx == bkt, tab[:, bkt:bkt + 1], bias)
    o_ref[0] = bias


def _bias_lookup(idx, tab, *, n_col_tiles):
    n_rows, n_cols = idx.shape
    n_g, tab_rows, _ = tab.shape
    tc = n_cols // n_col_tiles
    assert tc * n_col_tiles == n_cols and tc % LANES == 0 and tab_rows in (1, n_rows)
    return pl.pallas_call(
        _bias_body,
        out_shape=jax.ShapeDtypeStruct((n_g, n_rows, n_cols), F32),
        grid=(n_g, n_col_tiles),
        in_specs=[pl.BlockSpec((n_rows, tc), lambda g, j: (0, j)),
                  pl.BlockSpec((1, tab_rows, N_BUCKETS), lambda g, j: (g, 0, 0))],
        out_specs=pl.BlockSpec((1, n_rows, tc), lambda g, j: (g, 0, j)),
        compiler_params=_params("parallel", "parallel"),
        name="bias_lookup",
    )(idx, tab)


SLABS = DILATED[1][1]
PIECE = BLK // SLABS


def _prompt_bias_index():
    step = np.arange(BLK)[:, None] + BLK - np.arange(2 * BLK)[None, :]
    perm = (SLABS * np.arange(PIECE)[None, :] + np.arange(SLABS)[:, None]).reshape(BLK)
    maps = []
    for win, dil in DILATED:
        n_steps = win // dil + 1
        band = (step >= 0) & (step < n_steps)
        bucket = _rel_bucket(np.clip(step, 0, n_steps - 1) * dil)
        idx = np.where(band, bucket, -1)
        if dil == 1:
            idx = idx[np.ix_(perm, np.concatenate([perm, BLK + perm]))]
        maps.append(idx)
    return np.stack(maps).astype(np.int32)


def _block_pieces(dil, r, n):
    if dil == 1:
        return [(s, slice(n * PIECE, (n + 1) * PIECE), PIECE) for s in range(SLABS)]
    if dil == SLABS:
        return [(r, slice(n * BLK, (n + 1) * BLK), BLK)]
    sub = dil // SLABS
    s, t = r % SLABS, r // SLABS
    return [(s, pl.ds(t + sub * BLK * n, BLK, stride=sub), BLK)]


def _attn_body(bias_s, q_ref, k_ref, v_ref, o_ref, kh_ref, vh_ref, slab_s, acc_s, m_s, l_s, *,
               seq, group_size):
    for part, ref in enumerate((q_ref, k_ref, v_ref)):
        for s in range(SLABS):
            slab_s[part, s] = ref[pl.ds(s, seq // SLABS, stride=SLABS), :]
    q_s, k_s, v_s = ([slab_s.at[part, s] for s in range(SLABS)] for part in range(3))

    def copy_by_head(c, n_chunks):
        n_rows = seq // n_chunks
        src = pl.ds(c * n_rows, n_rows)
        dst = pl.ds(pl.program_id(1) + H_A * c * n_rows, n_rows, stride=H_A)
        kh_ref[0, dst, :] = k_ref[src, :]
        vh_ref[0, dst, :] = v_ref[src, :]

    def load(slabs, pieces):
        parts = [slabs[s][idx, :] for s, idx, _ in pieces]
        return parts[0] if len(parts) == 1 else jnp.concatenate(parts, axis=0)

    def gather(slabs, blocks):
        return jnp.stack([load(slabs, pieces) for pieces in blocks]).astype(BF16)

    def qk(q, k):
        return jnp.einsum("gid,gjd->gij", q, k, preferred_element_type=F32)

    def pv(p, v):
        return jnp.einsum("gij,gjd->gid", p.astype(BF16), v, preferred_element_type=F32)

    def bias_of(dis, cols):
        return jnp.stack([bias_s[0, di, :, cols] for di in dis])

    def group(dis, curs, prevs):
        q = gather(q_s, curs)
        s_c = qk(q, gather(k_s, curs)) * SCALE + bias_of(dis, slice(BLK, None))
        if prevs is None:
            m = jnp.max(s_c, axis=-1, keepdims=True)
            p_c = jnp.exp(s_c - m)
            den = jnp.sum(p_c, axis=-1, keepdims=True)
            o = pv(p_c, gather(v_s, curs))
        else:
            s_p = qk(q, gather(k_s, prevs)) * SCALE + bias_of(dis, slice(0, BLK))
            m = jnp.max(jnp.maximum(s_c, s_p), axis=-1, keepdims=True)
            p_c = jnp.exp(s_c - m)
            p_p = jnp.exp(s_p - m)
            den = jnp.sum(p_c + p_p, axis=-1, keepdims=True)
            o = pv(p_c, gather(v_s, curs)) + pv(p_p, gather(v_s, prevs))
        for g, (di, pieces) in enumerate(zip(dis, curs)):
            m_g = jnp.broadcast_to(m[g], (BLK, HEAD_DIM))
            l_g = jnp.broadcast_to(den[g], (BLK, HEAD_DIM))
            lo = 0
            for s, idx, n_rows in pieces:
                acc_s[di, s, idx, :] = o[g, lo:lo + n_rows]
                m_s[di, s, idx, :] = m_g[lo:lo + n_rows]
                l_s[di, s, idx, :] = l_g[lo:lo + n_rows]
                lo += n_rows

    first, later = [], []
    for di, (win, dil) in enumerate(DILATED):
        for r in range(dil):
            first.append((di, dil, r, 0))
            later += [(di, dil, r, n) for n in range(1, seq // (dil * BLK))]
    groups = []
    for blocks, has_prev in ((first, False), (later, True)):
        for lo in range(0, len(blocks), group_size):
            part = blocks[lo:lo + group_size]
            groups.append(([di for di, _, _, _ in part],
                           [_block_pieces(dil, r, n) for _, dil, r, n in part],
                           [_block_pieces(dil, r, n - 1) for _, dil, r, n in part] if has_prev else None))
    n_chunks = 1 << (len(groups).bit_length() - 1)
    for c, args in enumerate(groups):
        group(*args)
        if c < n_chunks:
            copy_by_head(c, n_chunks)

    rows_per = 256
    per_slab = seq // SLABS // rows_per

    def merge(c, carry):
        s = c // per_slab
        r0 = pl.multiple_of((c % per_slab) * rows_per, rows_per)
        rc = pl.ds(r0, rows_per)
        ms = [m_s[di, s, rc, :] for di in range(len(DILATED))]
        mx = jnp.maximum(jnp.maximum(ms[0], ms[1]), ms[2])
        num = 0.0
        den = 0.0
        for di in range(len(DILATED)):
            wgt = jnp.exp(ms[di] - mx)
            num = num + wgt * acc_s[di, s, rc, :]
            den = den + wgt * l_s[di, s, rc, :]
        o_ref[pl.ds(SLABS * r0 + s, rows_per, stride=SLABS), :] = num / den
        return carry

    lax.fori_loop(0, SLABS * per_slab, merge, 0)


def _sgu_body(u_ref, v_ref, w_ref, bt_ref, o_ref, *, n_chunks):
    tri = (lax.broadcasted_iota(jnp.int32, (CHUNK, CHUNK), 0)
           >= lax.broadcasted_iota(jnp.int32, (CHUNK, CHUNK), 1))
    for g in range(H_B):
        w = jnp.where(tri, w_ref[g], 0.0).astype(BF16)
        bias = bt_ref[:, g:g + 1]
        cols = slice(g * C_B, (g + 1) * C_B)
        for c in range(n_chunks):
            rws = slice(c * CHUNK, (c + 1) * CHUNK)
            mixed = _dot(w, v_ref[rws, cols].astype(BF16)) + bias
            o_ref[rws, cols] = u_ref[rws, cols] * mixed


def _mem_kv_body(mem_ref, g_ref, w_ref, gain_ref, kv_ref):
    z = _dot(_rms(mem_ref[...], g_ref[...]).astype(BF16), w_ref[...].astype(BF16))
    for hh in range(H_M):
        sl = slice(hh * HEAD_DIM, (hh + 1) * HEAD_DIM)
        kv_ref[:, sl] = _rms(z[:, sl], gain_ref[:, sl])
    kv_ref[:, W_M:] = z[:, W_M:]


def _mem_body(q_ref, kv_ref, o_ref):
    for hh in range(H_M):
        k_sl = slice(hh * HEAD_DIM, (hh + 1) * HEAD_DIM)
        v_sl = slice(W_M + hh * HEAD_DIM, W_M + (hh + 1) * HEAD_DIM)
        s = _dot_nt(q_ref[0, :, k_sl].astype(BF16), kv_ref[:, k_sl].astype(BF16)) * SCALE
        m = jnp.max(s, axis=-1, keepdims=True)
        p = jnp.exp(s - m)
        den = jnp.sum(p, axis=-1, keepdims=True)
        o_ref[:, k_sl] = _dot(p.astype(BF16), kv_ref[:, v_sl].astype(BF16)) / den


def _mems_body(q_ref, k_ref, v_ref, o_ref):
    for b in range(q_ref.shape[0]):
        s = _dot_nt(q_ref[b].astype(BF16), k_ref[b].astype(BF16)) * SCALE
        row = lax.broadcasted_iota(jnp.int32, s.shape, 0)
        col = lax.broadcasted_iota(jnp.int32, s.shape, 1)
        s = jnp.where(row % H_M == col % H_M, s, NEG)
        m = jnp.max(s, axis=-1, keepdims=True)
        p = jnp.exp(s - m)
        den = jnp.sum(p, axis=-1, keepdims=True)
        o_ref[b] = _dot(p.astype(BF16), v_ref[b].astype(BF16)) / den


def _outproj_body(*refs, n_groups):
    src_refs = [refs[4 * k:4 * k + 3] for k in range(n_groups)]
    h_refs = [refs[4 * k + 3] for k in range(n_groups)]
    g_ref, w_ref = refs[4 * n_groups:4 * n_groups + 2]
    o_refs = refs[4 * n_groups + 2:5 * n_groups + 2]
    cat_ref = refs[-1]
    bounds = _bounds(h_refs)

    @pl.when(pl.program_id(1) == 0)
    def _():
        for srcs, (r0, r1) in zip(src_refs, bounds):
            lo = 0
            for src in srcs:
                hi = lo + src.shape[1]
                x = src[...]
                cat_ref[r0:r1, lo:hi] = x if x.dtype == BF16 else _rms(x, g_ref[:, lo:hi]).astype(BF16)
                lo = hi

    res = _dot(cat_ref[...], w_ref[...].astype(BF16))
    for h_ref, o_ref, (r0, r1) in zip(h_refs, o_refs, bounds):
        o_ref[...] = h_ref[...] + res[r0:r1]


def _outproj(groups, g, w, *, n_tiles, tn):
    mix = W_A + W_B + W_M
    rows = _tile_rows([grp[3] for grp in groups], n_tiles)
    in_specs = []
    for r in rows:
        in_specs += [pl.BlockSpec((r, wd), lambda i, j: (i, 0)) for wd in (W_A, W_B, W_M)]
        in_specs += [pl.BlockSpec((r, tn), lambda i, j: (i, j))]
    in_specs += [pl.BlockSpec((1, mix), lambda i, j: (0, 0), pipeline_mode=pl.Buffered(1)),
                 pl.BlockSpec((mix, tn), lambda i, j: (0, j))]
    return pl.pallas_call(
        functools.partial(_outproj_body, n_groups=len(groups)),
        out_shape=[jax.ShapeDtypeStruct(grp[3].shape, F32) for grp in groups],
        grid=(n_tiles, D_MODEL // tn),
        in_specs=in_specs,
        out_specs=[pl.BlockSpec((r, tn), lambda i, j: (i, j)) for r in rows],
        scratch_shapes=[pltpu.VMEM((sum(rows), mix), BF16)],
        compiler_params=_params("arbitrary", "arbitrary"),
        name="outproj",
    )(*[a for grp in groups for a in grp], g.reshape(1, mix), w)


def _sample_bias_index(n_new, n_past):
    (win1, dil1), (tail, dil4), (win16, dil16) = DILATED
    assert dil1 == 1 and win1 == BLK and n_past == win16 and n_past % dil16 == 0
    assert n_past >= tail and n_new <= dil4 and n_new * H_A <= LANES
    n16 = n_past // dil16
    t = np.arange(n_new)[:, None]

    def per_head(index):
        same = np.eye(H_A, dtype=bool)[None, :, None, :]
        full = np.where(same, index[:, None, :, None], -1)
        return full.reshape(n_new * H_A, index.shape[1] * H_A)

    dist = BLK + t - np.arange(BLK)[None, :]
    d1 = np.where(dist <= win1, _rel_bucket(dist), -1)
    dist = tail + t - np.arange(tail)[None, :]
    d4 = np.where((dist % dil4 == 0) & (dist <= tail), _rel_bucket(dist), -1)
    m, tk =np.divmod(np.arange(n16 * n_new)[None, :], n_new)
    dist = n_past + t - dil16 * m - tk
    d16 = np.where(tk == t, _rel_bucket(np.maximum(dist, 0)), -1)
    s = np.arange(LANES // H_A)[None, :]
    dist = t - s
    new = [np.where((s < n_new) & (dist >= 0) & (dist % dil == 0),
                    _rel_bucket(np.maximum(dist, 0)), -1) for _, dil in DILATED]
    pieces = [d1, d4, d16] + new
    return np.concatenate([per_head(p) for p in pieces], axis=1).astype(np.int32)


def _sattn_body(bias_ref, q_ref, kn_ref, vn_ref, kt_ref, k16_ref, vt_ref, v16_ref, o_ref):
    def flat(ref):
        x = ref[0]
        return x.reshape(-1, HEAD_DIM).astype(BF16)

    q = q_ref[0].astype(BF16)
    kt, vt = flat(kt_ref), flat(vt_ref)
    k16, v16 = flat(k16_ref), flat(v16_ref)
    def new_rows(ref):
        x = ref[0]
        return jnp.concatenate([x, jnp.zeros((LANES - x.shape[0], HEAD_DIM), F32)], axis=0).astype(BF16)

    kn, vn = new_rows(kn_ref), new_rows(vn_ref)
    n_t, n_1 = kt.shape[0], BLK * H_A

    raw_t = _dot_nt(q, kt)
    raw_n = _dot_nt(q, kn)
    raws = [raw_t[:, n_t - n_1:], raw_t, _dot_nt(q, k16)] + [raw_n] * len(DILATED)
    logits = []
    lo = 0
    for raw in raws:
        hi = lo + raw.shape[1]
        logits.append(raw * SCALE + bias_ref[:, lo:hi])
        lo = hi

    m = functools.reduce(jnp.maximum, [jnp.max(x, axis=-1, keepdims=True) for x in logits])
    probs = [jnp.exp(x - m) for x in logits]
    den = functools.reduce(lambda a, b: a + b, [jnp.sum(p, axis=-1, keepdims=True) for p in probs])
    p_new = functools.reduce(lambda a, b: a + b, probs[3:])
    o = (_dot(probs[0].astype(BF16), vt[n_t - n_1:, :]) + _dot(probs[1].astype(BF16), vt)
         + _dot(probs[2].astype(BF16), v16) + _dot(p_new.astype(BF16), vn))
    o_ref[0] = o / den


N_PROMPT_IN, N_SAMPLE_IN, N_PROMPT_OUT = 4, 8, 3


def _attention_body(*refs, seq, group_size):
    prompt_in, refs = refs[:N_PROMPT_IN], refs[N_PROMPT_IN:]
    sample_in, refs = refs[:N_SAMPLE_IN], refs[N_SAMPLE_IN:]
    prompt_out, sample_out, scratch = refs[:N_PROMPT_OUT], refs[N_PROMPT_OUT], refs[N_PROMPT_OUT + 1:]
    _sattn_body(*sample_in, sample_out)
    _attn_body(*prompt_in, *prompt_out, *scratch, seq=seq, group_size=group_size)


def _attention(z, qa, ka, va, cache_k, cache_v, rel_bias, *, batch, seq):
    n = batch * seq
    dec_batch, n_new = qa.shape[:2]
    n_past = cache_k.shape[1]
    assert all(seq % (dil * BLK) == 0 for _, dil in DILATED) and dec_batch == batch * H_A

    idx = _prompt_bias_index()
    n_d = idx.shape[0]
    bias_p = _bias_lookup(jnp.asarray(idx.reshape(n_d * BLK, 2 * BLK)), rel_bias.T[:, None, :],
                          n_col_tiles=1)
    bias_p = bias_p.reshape(H_A, n_d, BLK, 2 * BLK)
    prompt_in_specs = [pl.BlockSpec((1, n_d, BLK, 2 * BLK), lambda b, h: (h, 0, 0, 0))]
    prompt_in_specs += [pl.BlockSpec((seq, HEAD_DIM), lambda b, h, part=part: (b, part * H_A + h))
                        for part in range(3)]
    by_head = pl.BlockSpec((1, seq * H_A, HEAD_DIM), lambda b, h: (b, 0, 0), pipeline_mode=pl.Buffered(1))
    by_head_shape = jax.ShapeDtypeStruct((batch, seq * H_A, HEAD_DIM), F32)
    stat = pltpu.VMEM((n_d, SLABS, seq // SLABS, HEAD_DIM), F32)

    tail = DILATED[1][0]
    dil16 = DILATED[2][1]
    n16 = n_past // dil16
    n_rows = n_new * H_A
    bias_s = _bias_lookup(jnp.asarray(_sample_bias_index(n_new, n_past)),
                          jnp.tile(rel_bias.T, (n_new, 1))[None], n_col_tiles=5)[0]

    def rows(x):
        return x.reshape(dec_batch, n_rows, HEAD_DIM)

    def sample(b, h):
        return b * H_A + h

    k_res = cache_k.reshape(dec_batch, n16, dil16, H_A, HEAD_DIM)
    v_res = cache_v.reshape(dec_batch, n16, dil16, H_A, HEAD_DIM)
    q_spec = pl.BlockSpec((1, n_rows, HEAD_DIM), lambda b, h: (sample(b, h), 0, 0))
    tail_spec = pl.BlockSpec((1, tail, H_A, HEAD_DIM),
                             lambda b, h: (sample(b, h), n_past // tail - 1, 0, 0))
    res_spec = pl.BlockSpec((1, n16, n_new, H_A, HEAD_DIM), lambda b, h: (sample(b, h), 0, 0, 0, 0))
    sample_in_specs = [pl.BlockSpec(bias_s.shape, lambda b, h: (0, 0), pipeline_mode=pl.Buffered(1)),
                       q_spec, q_spec, q_spec, tail_spec, res_spec, tail_spec, res_spec]
    assert len(prompt_in_specs) == N_PROMPT_IN and len(sample_in_specs) == N_SAMPLE_IN

    oa_p, kh, vh, oa_s = pl.pallas_call(
        functools.partial(_attention_body, seq=seq, group_size=32),
        out_shape=[jax.ShapeDtypeStruct((n, W_A), F32), by_head_shape, by_head_shape,
                   jax.ShapeDtypeStruct((dec_batch, n_rows, HEAD_DIM), F32)],
        grid=(batch, H_A),
        in_specs=prompt_in_specs + sample_in_specs,
        out_specs=[pl.BlockSpec((seq, HEAD_DIM), lambda b, h: (b, h)), by_head, by_head, q_spec],
        scratch_shapes=[stat, stat, stat, stat],
        compiler_params=_params("arbitrary", "arbitrary"),
        name="attention",
    )(bias_p, z, z, z,
      bias_s, rows(qa), rows(ka), rows(va), cache_k, k_res, cache_v, v_res)
    heads = (batch, seq, H_A, HEAD_DIM)
    return oa_p, kh.reshape(heads), vh.reshape(heads), oa_s.reshape(dec_batch, n_new, W_A)


def _sgus_body(u_ref, v_ref, w_ref, b_ref, o_ref, *, n_new):
    t_idx = lax.broadcasted_iota(jnp.int32, (1, n_new, W_B), 1)
    mixed = 0.0
    for s in range(n_new):
        w_s = jnp.where(t_idx >= s, w_ref[s][None], 0.0).astype(BF16).astype(F32)
        mixed = mixed + w_s * v_ref[:, s:s + 1, :].astype(BF16).astype(F32)
    o_ref[...] = u_ref[...] * (mixed + b_ref[...][None])


def _side_body(u_ref, v_ref, w_ref, bt_ref, gb_ref, qm_ref, gm_ref,
               mem_ref, gmem_ref, wkv_ref, gkv_ref,
               us_ref, vs_ref, wl_ref, bl_ref, qs_ref, ck_ref, cv_ref,
               obn_ref, omn_ref, mkv_ref, obs_ref, oms_ref, ob_s, om_s, *, n_chunks, n_new, per_batch):
    _mems_body(qs_ref, ck_ref, cv_ref, oms_ref)
    _sgus_body(us_ref, vs_ref, wl_ref, bl_ref, obs_ref, n_new=n_new)

    @pl.when(pl.program_id(0) % per_batch == 0)
    def _():
        _mem_kv_body(mem_ref, gmem_ref, wkv_ref, gkv_ref, mkv_ref)

    _mem_body(qm_ref, mkv_ref, om_s)
    omn_ref[0] = _rms(om_s[...], gm_ref[...]).astype(BF16)
    _sgu_body(u_ref, v_ref, w_ref, bt_ref, ob_s, n_chunks=n_chunks)
    obn_ref[...] = _rms(ob_s[...], gb_ref[...]).astype(BF16)


def _side_mixers(z, mem, g_mem, w_kv, g_km, u_s, v_s, qm_s, cache_k, cache_v, w_s, b_s, g_b, g_m, *,
                 batch, seq, n_steps):
    dec_batch, n_new, _ = u_s.shape
    tm = batch * seq // n_steps
    tb = dec_batch // n_steps
    per_batch = seq // tm
    assert tm * per_batch == seq and tb * n_steps == dec_batch and tm % CHUNK == 0
    n_rows = n_new * H_M
    w_l = jnp.repeat(jnp.transpose(w_s[:, :n_new, :n_new], (2, 1, 0)), C_B, axis=-1)
    b_l = jnp.repeat(b_s[:, :n_new].T, C_B, axis=-1)
    z3 = z.reshape(batch, seq, -1)
    gain_kv = jnp.concatenate([jnp.tile(g_km, H_M), jnp.ones((W_M,), F32)]).reshape(1, 2 * W_M)
    once = pl.Buffered(1)
    sgu_s_spec = pl.BlockSpec((tb, n_new, W_B), lambda i: (i, 0, 0))
    qs_spec = pl.BlockSpec((tb, n_rows, HEAD_DIM), lambda i: (i, 0, 0))
    cache_spec = pl.BlockSpec((tb, N_MEM * H_M, HEAD_DIM), lambda i: (i, 0, 0))
    om_spec = pl.BlockSpec((1, tm, W_M), lambda i: (i // per_batch, i % per_batch, 0))
    gain_b = pl.BlockSpec((1, W_B), lambda i: (0, 0), pipeline_mode=once)
    gain_m = pl.BlockSpec((1, W_M), lambda i: (0, 0), pipeline_mode=once)
    obn, omn, mkv, ob_s, om_s = pl.pallas_call(
        functools.partial(_side_body, n_chunks=tm // CHUNK, n_new=n_new, per_batch=per_batch),
        out_shape=[jax.ShapeDtypeStruct((batch * seq, W_B), BF16),
                   jax.ShapeDtypeStruct((batch, seq, W_M), BF16),
                   jax.ShapeDtypeStruct((batch * N_MEM, 2 * W_M), F32),
                   jax.ShapeDtypeStruct(u_s.shape, F32),
                   jax.ShapeDtypeStruct((dec_batch, n_rows, HEAD_DIM), F32)],
        grid=(n_steps,),
        in_specs=[
            pl.BlockSpec((tm, W_B), lambda i: (i, COL_U // W_B)),
            pl.BlockSpec((tm, W_B), lambda i: (i, COL_V // W_B)),
            pl.BlockSpec((H_B, CHUNK, CHUNK), lambda i: (0, 0, 0), pipeline_mode=once),
            pl.BlockSpec((CHUNK, H_B), lambda i: (0, 0), pipeline_mode=once),
            gain_b,
            pl.BlockSpec((1, tm, W_M), lambda i: (i // per_batch, i % per_batch, COL_QM // W_M)),
            gain_m,
            pl.BlockSpec((N_MEM, D_MODEL), lambda i: (i // per_batch, 0)),
            pl.BlockSpec((1, D_MODEL), lambda i: (0, 0), pipeline_mode=once),
            pl.BlockSpec((D_MODEL, 2 * W_M), lambda i: (0, 0), pipeline_mode=once),
            pl.BlockSpec((1, 2 * W_M), lambda i: (0, 0), pipeline_mode=once),
            sgu_s_spec, sgu_s_spec,
            pl.BlockSpec(w_l.shape, lambda i: (0, 0, 0), pipeline_mode=once),
            pl.BlockSpec(b_l.shape, lambda i: (0, 0), pipeline_mode=once),
            qs_spec, cache_spec, cache_spec,
        ],
        out_specs=[pl.BlockSpec((tm, W_B), lambda i: (i, 0)), om_spec,
                   pl.BlockSpec((N_MEM, 2 * W_M), lambda i: (i // per_batch, 0)),
                   sgu_s_spec, qs_spec],
        scratch_shapes=[pltpu.VMEM((tm, W_B), F32), pltpu.VMEM((tm, W_M), F32)],
        compiler_params=_params("arbitrary"),
        name="side_mixers",
    )(z, z, w_s, b_s.T, g_b.reshape(1, W_B), z3, g_m.reshape(1, W_M),
      mem, g_mem.reshape(1, D_MODEL), w_kv, gain_kv, u_s, v_s, w_l, b_l,
      qm_s.reshape(dec_batch, n_rows, HEAD_DIM),
      cache_k.reshape(dec_batch, N_MEM * H_M, HEAD_DIM), cache_v.reshape(dec_batch, N_MEM * H_M, HEAD_DIM))
    return obn, omn.reshape(batch * seq, W_M), mkv, ob_s, om_s.reshape(dec_batch, n_new, W_M)


COL_KA, COL_VA, COL_U, COL_V, COL_QM = W_A, 2 * W_A, 3 * W_A, 3 * W_A + W_B, 3 * W_A + 2 * W_B


def _mixer_inputs(normed_groups, w_in, g_qa, g_ka, g_sgu, g_qm, *, n_tiles):
    gain = jnp.concatenate([jnp.tile(g_qa, H_A), jnp.tile(g_ka, H_A), jnp.ones((W_A + W_B,), F32),
                            g_sgu.reshape(W_B), jnp.tile(g_qm, H_M)])
    normed = [c // PROJ_TILE for lo, hi in ((0, COL_VA), (COL_V, COL_QM + W_M))
              for c in range(lo, hi, PROJ_TILE)]
    return _proj(normed_groups, w_in, gain, normed, n_tiles=n_tiles, chunk=256)


def kernel(x_prompt, x_sample, cache_win_k, cache_win_v, cache_mem_k, cache_mem_v, mem_prompt, rel_bias, g_ffn1, w1_gate, w1_up, w1_down, g_mix, w_in, g_qa, g_ka, g_sgu, w_sgu, b_sgu, g_qm, g_mem, w_mem_kv, g_km, g_mix_out, w_out, g_ffn2, w2_gate, w2_up, w2_down):
    batch, seq, _ = x_prompt.shape
    dec_batch, n_new, _ = x_sample.shape
    depth = g_ffn1.shape[0]
    n_p = batch * seq
    n_s = dec_batch * n_new

    y_p = x_prompt.reshape(n_p, D_MODEL)
    y_s = x_sample.reshape(n_s, D_MODEL)
    outs = [[] for _ in range(7)]
    for l in range(depth):
        h_p, h_s, hn_p, hn_s = _ffn([y_p, y_s], g_ffn1[l], w1_gate[l], w1_up[l], w1_down[l],
                                    g_next=g_mix[l], n_tiles=8, tf=256)
        z_p, z_s = _mixer_inputs([hn_p, hn_s], w_in[l], g_qa[l], g_ka[l], g_sgu[l], g_qm[l], n_tiles=4)

        zq, zk, zv, zu, zg, zm = (z_s[:, lo:hi] for lo, hi in zip(
            (0, COL_KA, COL_VA, COL_U, COL_V, COL_QM), (COL_KA, COL_VA, COL_U, COL_V, COL_QM, z_s.shape[1])))
        heads = (dec_batch, n_new, H_A, HEAD_DIM)
        oa_p, k_p, v_p, oa_s = _attention(z_p, zq.reshape(heads), zk.reshape(heads), zv.reshape(heads),
                                          cache_win_k[l], cache_win_v[l], rel_bias, batch=batch, seq=seq)

        ob_p, om_p, mkv, ob_s, om_s = _side_mixers(
            z_p, mem_prompt.reshape(batch * N_MEM, D_MODEL), g_mem[l], w_mem_kv[l], g_km[l],
            zu.reshape(dec_batch, n_new, W_B), zg.reshape(dec_batch, n_new, W_B),
            zm.reshape(dec_batch, n_new, H_M, HEAD_DIM), cache_mem_k[l], cache_mem_v[l],
            w_sgu[l], b_sgu[l], g_mix_out[l][W_A:W_A + W_B], g_mix_out[l][W_A + W_B:],
            batch=batch, seq=seq, n_steps=8)
        keep = min(DILATED[-1][0], seq)
        outs[0].append(k_p[:, seq - keep:])
        outs[1].append(v_p[:, seq - keep:])
        outs[2].append(mkv[:, :W_M].reshape(batch, N_MEM, H_M, HEAD_DIM))
        outs[3].append(mkv[:, W_M:].reshape(batch, N_MEM, H_M, HEAD_DIM))
        outs[4].append(zk.reshape(heads))
        outs[5].append(zv.reshape(heads))
        outs[6].append(zg.reshape(dec_batch, n_new, H_B, C_B))

        h_p, h_s = _outproj([(oa_p, ob_p, om_p, h_p),
                             (oa_s.reshape(n_s, W_A), ob_s.reshape(n_s, W_B), om_s.reshape(n_s, W_M), h_s)],
                            g_mix_out[l], w_out[l], n_tiles=4, tn=256)
        y_p, y_s = _ffn([h_p, h_s], g_ffn2[l], w2_gate[l], w2_up[l], w2_down[l], n_tiles=8, tf=256)

    return (y_p.reshape(batch, seq, D_MODEL), y_s.reshape(dec_batch, n_new, D_MODEL),
            *[jnp.stack(o) for o in outs])
```

```python
import functools

import numpy as np
import jax
import jax.numpy as jnp
from jax import lax
from jax.experimental import pallas as pl
from jax.experimental.pallas import tpu as pltpu

D_MODEL = 2048
HEAD_DIM = 128
H_A = 8
H_B = 4
C_B = 128
H_M = 4
W_A = H_A * HEAD_DIM
W_B = H_B * C_B
W_M = H_M * HEAD_DIM
D_FF = 5632
DILATED = ((128, 1), (512, 4), (2048, 16))
BLK = 128
CHUNK = 128
N_MEM = 256
N_BUCKETS = 32
REL_MAX_DIST = 2048
EPS = 1e-6
SCALE = HEAD_DIM ** -0.5
NEG = -1e30

LANES = 128
BF16_ROWS = 16
PROJ_TILE = 4 * HEAD_DIM
VMEM_LIMIT = 60 * 1024 * 1024

F32 = jnp.float32
BF16 = jnp.bfloat16


def _rel_bucket(dist):
    dist = np.asarray(dist, np.int64)
    max_exact = N_BUCKETS // 2
    large = max_exact + (np.log(np.maximum(dist, 1) / max_exact) / np.log(REL_MAX_DIST / max_exact)
                         * (N_BUCKETS - max_exact)).astype(np.int64)
    large = np.minimum(large, N_BUCKETS - 1)
    return np.where(dist < max_exact, dist, large).astype(np.int32)


def _rms(t, g):
    return t * lax.rsqrt(jnp.mean(t * t, axis=-1, keepdims=True) + EPS) * g


def _dot(a, b):
    return jnp.dot(a, b, preferred_element_type=F32)


def _dot_nt(a, b):
    return lax.dot_general(a, b, (((1,), (1,)), ((), ())), preferred_element_type=F32)


def _params(*sem):
    return pltpu.CompilerParams(dimension_semantics=sem, vmem_limit_bytes=VMEM_LIMIT)


def _tile_rows(groups, n_tiles):
    rows = [x.shape[0] // n_tiles for x in groups]
    assert all(r * n_tiles == x.shape[0] and r % BF16_ROWS == 0 for r, x in zip(rows, groups))
    return rows


def _bounds(refs):
    out, lo = [], 0
    for ref in refs:
        out.append((lo, lo + ref.shape[0]))
        lo += ref.shape[0]
    return out


def _chunk_edges(total, chunk):
    edges = list(range(0, total, chunk)) + [total]
    if len(edges) > 2 and edges[-1] - edges[-2] < chunk // 2:
        del edges[-2]
    return list(zip(edges[:-1], edges[1:]))


def _ffn_body(*refs, n_groups, with_next_norm):
    x_refs, refs = refs[:n_groups], refs[n_groups:]
    gn_ref = refs[0] if with_next_norm else None
    g_ref, wg_ref, wu_ref, wd_ref = refs[with_next_norm:with_next_norm + 4]
    refs = refs[with_next_norm + 4:]
    o_refs, on_refs, xn_ref = refs[:n_groups], refs[n_groups:-1], refs[-1]
    bounds = _bounds(x_refs)
    j = pl.program_id(1)

    @pl.when(j == 0)
    def _():
        for x_ref, o_ref, (lo, hi) in zip(x_refs, o_refs, bounds):
            x = x_ref[...]
            xn_ref[lo:hi, :] = _rms(x, g_ref[...]).astype(BF16)
            o_ref[...] = x

    xn = xn_ref[...]
    gate = _dot(xn, wg_ref[...].astype(BF16))
    up = _dot(xn, wu_ref[...].astype(BF16))
    act = (gate * jax.nn.sigmoid(gate)) * up * 0.5
    res = _dot(act.astype(BF16), wd_ref[...].astype(BF16))
    for o_ref, (lo, hi) in zip(o_refs, bounds):
        o_ref[...] += res[lo:hi]

    if with_next_norm:
        @pl.when(j == pl.num_programs(1) - 1)
        def _():
            for o_ref, on_ref in zip(o_refs, on_refs):
                on_ref[...] = _rms(o_ref[...], gn_ref[...]).astype(BF16)


def _ffn(groups, g, wg, wu, wd, *, g_next=None, n_tiles, tf):
    rows = _tile_rows(groups, n_tiles)
    with_next_norm = g_next is not None
    gain_spec = pl.BlockSpec((1, D_MODEL), lambda i, j: (0, 0), pipeline_mode=pl.Buffered(1))
    row_specs = [pl.BlockSpec((r, D_MODEL), lambda i, j: (i, 0)) for r in rows]
    out_shape = [jax.ShapeDtypeStruct(x.shape, F32) for x in groups]
    gains = [g.reshape(1, D_MODEL)]
    if with_next_norm:
        out_shape += [jax.ShapeDtypeStruct(x.shape, BF16) for x in groups]
        gains.insert(0, g_next.reshape(1, D_MODEL))
    return pl.pallas_call(
        functools.partial(_ffn_body, n_groups=len(groups), with_next_norm=with_next_norm),
        out_shape=out_shape,
        grid=(n_tiles, D_FF // tf),
        in_specs=(
            row_specs + [gain_spec] * len(gains)
            + [pl.BlockSpec((D_MODEL, tf), lambda i, j: (0, j)),
               pl.BlockSpec((D_MODEL, tf), lambda i, j: (0, j)),
               pl.BlockSpec((tf, D_MODEL), lambda i, j: (j, 0))]),
        out_specs=row_specs * (2 if with_next_norm else 1),
        scratch_shapes=[pltpu.VMEM((sum(rows), D_MODEL), BF16)],
        compiler_params=_params("arbitrary", "arbitrary"),
        name="ffn",
    )(*groups, *gains, wg, wu, wd)


def _proj_body(*refs, n_groups, norm_tiles, chunk):
    x_refs = refs[:n_groups]
    w_ref, gain_ref = refs[n_groups:n_groups + 2]
    o_refs, xn_ref = refs[n_groups + 2:-1], refs[-1]
    bounds = _bounds(x_refs)
    j = pl.program_id(1)

    @pl.when(j == 0)
    def _():
        for x_ref, (lo, hi) in zip(x_refs, bounds):
            xn_ref[lo:hi, :] = x_ref[...]

    w = w_ref[...].astype(BF16)
    gain = gain_ref[...]
    normed = functools.reduce(jnp.logical_or, [j == t for t in norm_tiles])

    def finish(z):
        heads = []
        for hh in range(PROJ_TILE // HEAD_DIM):
            sl = slice(hh * HEAD_DIM, (hh + 1) * HEAD_DIM)
            zz = z[:, sl]
            scale = lax.rsqrt(jnp.mean(zz * zz, axis=-1, keepdims=True) + EPS)
            heads.append(zz * jnp.where(normed, scale, 1.0) * gain[:, sl])
        return jnp.concatenate(heads, axis=1)

    for c0, c1 in _chunk_edges(bounds[-1][1], chunk):
        z = finish(_dot(xn_ref[c0:c1, :], w))
        for o_ref, (lo, hi) in zip(o_refs, bounds):
            a, b = max(lo, c0), min(hi, c1)
            if a < b:
                o_ref[a - lo:b - lo, :] = z[a - c0:b - c0, :]


def _proj(groups, w, gain, norm_tiles, *, n_tiles, chunk):
    d_in, d_out = w.shape
    rows = _tile_rows(groups, n_tiles)
    assert all(x.dtype == BF16 for x in groups)
    return pl.pallas_call(
        functools.partial(_proj_body, n_groups=len(groups), norm_tiles=tuple(norm_tiles), chunk=chunk),
        out_shape=[jax.ShapeDtypeStruct((x.shape[0], d_out), F32) for x in groups],
        grid=(n_tiles, d_out // PROJ_TILE),
        in_specs=(
            [pl.BlockSpec((r, d_in), lambda i, j: (i, 0)) for r in rows]
            + [pl.BlockSpec((d_in, PROJ_TILE), lambda i, j: (0, j)),
               pl.BlockSpec((1, PROJ_TILE), lambda i, j: (0, j))]),
        out_specs=[pl.BlockSpec((r, PROJ_TILE), lambda i, j: (i, j)) for r in rows],
        scratch_shapes=[pltpu.VMEM((sum(rows), d_in), BF16)],
        compiler_params=_params("arbitrary", "arbitrary"),
        name="proj",
    )(*groups, w, gain.reshape(1, d_out))


def _bias_body(idx_ref, tab_ref, o_ref):
    idx = idx_ref[...]
    tab = tab_ref[0]
    bias = jnp.full(idx.shape, NEG, F32)
    for bkt in range(N_BUCKETS):
        bias = jnp.where(idx == bkt, tab[:, bkt:bkt + 1], bias)
    o_ref[0] = bias


def _bias_lookup(idx, tab, *, n_col_tiles):
    n_rows, n_cols = idx.shape
    n_g, tab_rows, _ = tab.shape
    tc = n_cols // n_col_tiles
    assert tc * n_col_tiles == n_cols and tc % LANES == 0 and tab_rows in (1, n_rows)
    return pl.pallas_call(
        _bias_body,
        out_shape=jax.ShapeDtypeStruct((n_g, n_rows, n_cols), F32),
        grid=(n_g, n_col_tiles),
        in_specs=[pl.BlockSpec((n_rows, tc), lambda g, j: (0, j)),
                  pl.BlockSpec((1, tab_rows, N_BUCKETS), lambda g, j: (g, 0, 0))],
        out_specs=pl.BlockSpec((1, n_rows, tc), lambda g, j: (g, 0, j)),
        compiler_params=_params("parallel", "parallel"),
        name="bias_lookup",
    )(idx, tab)


SLABS = DILATED[1][1]
PIECE = BLK // SLABS


def _prompt_bias_index():
    step = np.arange(BLK)[:, None] + BLK - np.arange(2 * BLK)[None, :]
    perm = (SLABS * np.arange(PIECE)[None, :] + np.arange(SLABS)[:, None]).reshape(BLK)
    maps = []
    for win, dil in DILATED:
        n_steps = win // dil + 1
        band = (step >= 0) & (step < n_steps)
        bucket = _rel_bucket(np.clip(step, 0, n_steps - 1) * dil)
        idx = np.where(band, bucket, -1)
        if dil == 1:
            idx = idx[np.ix_(perm, np.concatenate([perm, BLK + perm]))]
        maps.append(idx)
    return np.stack(maps).astype(np.int32)


def _block_pieces(dil, r, n):
    if dil == 1:
        return [(s, slice(n * PIECE, (n + 1) * PIECE), PIECE) for s in range(SLABS)]
    if dil == SLABS:
        return [(r, slice(n * BLK, (n + 1) * BLK), BLK)]
    sub = dil // SLABS
    s, t = r % SLABS, r // SLABS
    return [(s, pl.ds(t + sub * BLK * n, BLK, stride=sub), BLK)]


def _attn_body(bias_s, q_ref, k_ref, v_ref, o_ref, kh_ref, vh_ref, slab_s, acc_s, m_s, l_s, *,
               seq, group_size):
    for part, ref in enumerate((q_ref, k_ref, v_ref)):
        for s in range(SLABS):
            slab_s[part, s] = ref[pl.ds(s, seq // SLABS, stride=SLABS), :]
    q_s, k_s, v_s = ([slab_s.at[part, s] for s in range(SLABS)] for part in range(3))

    def copy_by_head(c, n_chunks):
        n_rows = seq // n_chunks
        src = pl.ds(c * n_rows, n_rows)
        dst = pl.ds(pl.program_id(1) + H_A * c * n_rows, n_rows, stride=H_A)
        kh_ref[0, dst, :] = k_ref[src, :]
        vh_ref[0, dst, :] = v_ref[src, :]

    def load(slabs, pieces):
        parts = [slabs[s][idx, :] for s, idx, _ in pieces]
        return parts[0] if len(parts) == 1 else jnp.concatenate(parts, axis=0)

    def gather(slabs, blocks):
        return jnp.stack([load(slabs, pieces) for pieces in blocks]).astype(BF16)

    def qk(q, k):
        return jnp.einsum("gid,gjd->gij", q, k, preferred_element_type=F32)

    def pv(p, v):
        return jnp.einsum("gij,gjd->gid", p.astype(BF16), v, preferred_element_type=F32)

    def bias_of(dis, cols):
        return jnp.stack([bias_s[0, di, :, cols] for di in dis])

    def group(dis, curs, prevs):
        q = gather(q_s, curs)
        s_c = qk(q, gather(k_s, curs)) * SCALE + bias_of(dis, slice(BLK, None))
        if prevs is None:
            m = jnp.max(s_c, axis=-1, keepdims=True)
            p_c = jnp.exp(s_c - m)
            den = jnp.sum(p_c, axis=-1, keepdims=True)
            o = pv(p_c, gather(v_s, curs))
        else:
            s_p = qk(q, gather(k_s, prevs)) * SCALE + bias_of(dis, slice(0, BLK))
            m = jnp.max(jnp.maximum(s_c, s_p), axis=-1, keepdims=True)
            p_c = jnp.exp(s_c - m)
            p_p = jnp.exp(s_p - m)
            den = jnp.sum(p_c + p_p, axis=-1, keepdims=True)
            o = pv(p_c, gather(v_s, curs)) + pv(p_p, gather(v_s, prevs))
        for g, (di, pieces) in enumerate(zip(dis, curs)):
            m_g = jnp.broadcast_to(m[g], (BLK, HEAD_DIM))
            l_g = jnp.broadcast_to(den[g], (BLK, HEAD_DIM))
            lo = 0
            for s, idx, n_rows in pieces:
                acc_s[di, s, idx, :] = o[g, lo:lo + n_rows]
                m_s[di, s, idx, :] = m_g[lo:lo + n_rows]
                l_s[di, s, idx, :] = l_g[lo:lo + n_rows]
                lo += n_rows

    first, later = [], []
    for di, (win, dil) in enumerate(DILATED):
        for r in range(dil):
            first.append((di, dil, r, 0))
            later += [(di, dil, r, n) for n in range(1, seq // (dil * BLK))]
    groups = []
    for blocks, has_prev in ((first, False), (later, True)):
        for lo in range(0, len(blocks), group_size):
            part = blocks[lo:lo + group_size]
            groups.append(([di for di, _, _, _ in part],
                           [_block_pieces(dil, r, n) for _, dil, r, n in part],
                           [_block_pieces(dil, r, n - 1) for _, dil, r, n in part] if has_prev else None))
    n_chunks = 1 << (len(groups).bit_length() - 1)
    for c, args in enumerate(groups):
        group(*args)
        if c < n_chunks:
            copy_by_head(c, n_chunks)

    rows_per = 256
    per_slab = seq // SLABS // rows_per

    def merge(c, carry):
        s = c // per_slab
        r0 = pl.multiple_of((c % per_slab) * rows_per, rows_per)
        rc = pl.ds(r0, rows_per)
        ms = [m_s[di, s, rc, :] for di in range(len(DILATED))]
        mx = jnp.maximum(jnp.maximum(ms[0], ms[1]), ms[2])
        num = 0.0
        den = 0.0
        for di in range(len(DILATED)):
            wgt = jnp.exp(ms[di] - mx)
            num = num + wgt * acc_s[di, s, rc, :]
            den = den + wgt * l_s[di, s, rc, :]
        o_ref[pl.ds(SLABS * r0 + s, rows_per, stride=SLABS), :] = num / den
        return carry

    lax.fori_loop(0, SLABS * per_slab, merge, 0)


def _sgu_body(u_ref, v_ref, w_ref, bt_ref, o_ref, *, n_chunks):
    tri = (lax.broadcasted_iota(jnp.int32, (CHUNK, CHUNK), 0)
           >= lax.broadcasted_iota(jnp.int32, (CHUNK, CHUNK), 1))
    for g in range(H_B):
        w = jnp.where(tri, w_ref[g], 0.0).astype(BF16)
        bias = bt_ref[:, g:g + 1]
        cols = slice(g * C_B, (g + 1) * C_B)
        for c in range(n_chunks):
            rws = slice(c * CHUNK, (c + 1) * CHUNK)
            mixed = _dot(w, v_ref[rws, cols].astype(BF16)) + bias
            o_ref[rws, cols] = u_ref[rws, cols] * mixed


def _mem_kv_body(mem_ref, g_ref, w_ref, gain_ref, kv_ref):
    z = _dot(_rms(mem_ref[...], g_ref[...]).astype(BF16), w_ref[...].astype(BF16))
    for hh in range(H_M):
        sl = slice(hh * HEAD_DIM, (hh + 1) * HEAD_DIM)
        kv_ref[:, sl] = _rms(z[:, sl], gain_ref[:, sl])
    kv_ref[:, W_M:] = z[:, W_M:]


def _mem_body(q_ref, kv_ref, o_ref):
    for hh in range(H_M):
        k_sl = slice(hh * HEAD_DIM, (hh + 1) * HEAD_DIM)
        v_sl = slice(W_M + hh * HEAD_DIM, W_M + (hh + 1) * HEAD_DIM)
        s = _dot_nt(q_ref[0, :, k_sl].astype(BF16), kv_ref[:, k_sl].astype(BF16)) * SCALE
        m = jnp.max(s, axis=-1, keepdims=True)
        p = jnp.exp(s - m)
        den = jnp.sum(p, axis=-1, keepdims=True)
        o_ref[:, k_sl] = _dot(p.astype(BF16), kv_ref[:, v_sl].astype(BF16)) / den


def _mems_body(q_ref, k_ref, v_ref, o_ref):
    for b in range(q_ref.shape[0]):
        s = _dot_nt(q_ref[b].astype(BF16), k_ref[b].astype(BF16)) * SCALE
        row = lax.broadcasted_iota(jnp.int32, s.shape, 0)
        col = lax.broadcasted_iota(jnp.int32, s.shape, 1)
        s = jnp.where(row % H_M == col % H_M, s, NEG)
        m = jnp.max(s, axis=-1, keepdims=True)
        p = jnp.exp(s - m)
        den = jnp.sum(p, axis=-1, keepdims=True)
        o_ref[b] = _dot(p.astype(BF16), v_ref[b].astype(BF16)) / den


def _outproj_body(*refs, n_groups):
    src_refs = [refs[4 * k:4 * k + 3] for k in range(n_groups)]
    h_refs = [refs[4 * k + 3] for k in range(n_groups)]
    g_ref, w_ref = refs[4 * n_groups:4 * n_groups + 2]
    o_refs = refs[4 * n_groups + 2:5 * n_groups + 2]
    cat_ref = refs[-1]
    bounds = _bounds(h_refs)

    @pl.when(pl.program_id(1) == 0)
    def _():
        for srcs, (r0, r1) in zip(src_refs, bounds):
            lo = 0
            for src in srcs:
                hi = lo + src.shape[1]
                x = src[...]
                cat_ref[r0:r1, lo:hi] = x if x.dtype == BF16 else _rms(x, g_ref[:, lo:hi]).astype(BF16)
                lo = hi

    res = _dot(cat_ref[...], w_ref[...].astype(BF16))
    for h_ref, o_ref, (r0, r1) in zip(h_refs, o_refs, bounds):
        o_ref[...] = h_ref[...] + res[r0:r1]


def _outproj(groups, g, w, *, n_tiles, tn):
    mix = W_A + W_B + W_M
    rows = _tile_rows([grp[3] for grp in groups], n_tiles)
    in_specs = []
    for r in rows:
        in_specs += [pl.BlockSpec((r, wd), lambda i, j: (i, 0)) for wd in (W_A, W_B, W_M)]
        in_specs += [pl.BlockSpec((r, tn), lambda i, j: (i, j))]
    in_specs += [pl.BlockSpec((1, mix), lambda i, j: (0, 0), pipeline_mode=pl.Buffered(1)),
                 pl.BlockSpec((mix, tn), lambda i, j: (0, j))]
    return pl.pallas_call(
        functools.partial(_outproj_body, n_groups=len(groups)),
        out_shape=[jax.ShapeDtypeStruct(grp[3].shape, F32) for grp in groups],
        grid=(n_tiles, D_MODEL // tn),
        in_specs=in_specs,
        out_specs=[pl.BlockSpec((r, tn), lambda i, j: (i, j)) for r in rows],
        scratch_shapes=[pltpu.VMEM((sum(rows), mix), BF16)],
        compiler_params=_params("arbitrary", "arbitrary"),
        name="outproj",
    )(*[a for grp in groups for a in grp], g.reshape(1, mix), w)


def _sample_bias_index(n_new, n_past):
    (win1, dil1), (tail, dil4), (win16, dil16) = DILATED
    assert dil1 == 1 and win1 == BLK and n_past == win16 and n_past % dil16 == 0
    assert n_past >= tail and n_new <= dil4 and n_new * H_A <= LANES
    n16 = n_past // dil16
    t = np.arange(n_new)[:, None]

    def per_head(index):
        same = np.eye(H_A, dtype=bool)[None, :, None, :]
        full = np.where(same, index[:, None, :, None], -1)
        return full.reshape(n_new * H_A, index.shape[1] * H_A)

    dist = BLK + t - np.arange(BLK)[None, :]
    d1 = np.where(dist <= win1, _rel_bucket(dist), -1)
    dist = tail + t - np.arange(tail)[None, :]
    d4 = np.where((dist % dil4 == 0) & (dist <= tail), _rel_bucket(dist), -1)
    m, tk =np.divmod(np.arange(n16 * n_new)[None, :], n_new)
    dist = n_past + t - dil16 * m - tk
    d16 = np.where(tk == t, _rel_bucket(np.maximum(dist, 0)), -1)
    s = np.arange(LANES // H_A)[None, :]
    dist = t - s
    new = [np.where((s < n_new) & (dist >= 0) & (dist % dil == 0),
                    _rel_bucket(np.maximum(dist, 0)), -1) for _, dil in DILATED]
    pieces = [d1, d4, d16] + new
    return np.concatenate([per_head(p) for p in pieces], axis=1).astype(np.int32)


def _sattn_body(bias_ref, q_ref, kn_ref, vn_ref, kt_ref, k16_ref, vt_ref, v16_ref, o_ref):
    def flat(ref):
        x = ref[0]
        return x.reshape(-1, HEAD_DIM).astype(BF16)

    q = q_ref[0].astype(BF16)
    kt, vt = flat(kt_ref), flat(vt_ref)
    k16, v16 = flat(k16_ref), flat(v16_ref)
    def new_rows(ref):
        x = ref[0]
        return jnp.concatenate([x, jnp.zeros((LANES - x.shape[0], HEAD_DIM), F32)], axis=0).astype(BF16)

    kn, vn = new_rows(kn_ref), new_rows(vn_ref)
    n_t, n_1 = kt.shape[0], BLK * H_A

    raw_t = _dot_nt(q, kt)
    raw_n = _dot_nt(q, kn)
    raws = [raw_t[:, n_t - n_1:], raw_t, _dot_nt(q, k16)] + [raw_n] * len(DILATED)
    logits = []
    lo = 0
    for raw in raws:
        hi = lo + raw.shape[1]
        logits.append(raw * SCALE + bias_ref[:, lo:hi])
        lo = hi

    m = functools.reduce(jnp.maximum, [jnp.max(x, axis=-1, keepdims=True) for x in logits])
    probs = [jnp.exp(x - m) for x in logits]
    den = functools.reduce(lambda a, b: a + b, [jnp.sum(p, axis=-1, keepdims=True) for p in probs])
    p_new = functools.reduce(lambda a, b: a + b, probs[3:])
    o = (_dot(probs[0].astype(BF16), vt[n_t - n_1:, :]) + _dot(probs[1].astype(BF16), vt)
         + _dot(probs[2].astype(BF16), v16) + _dot(p_new.astype(BF16), vn))
    o_ref[0] = o / den


N_PROMPT_IN, N_SAMPLE_IN, N_PROMPT_OUT = 4, 8, 3


def _attention_body(*refs, seq, group_size):
    prompt_in, refs = refs[:N_PROMPT_IN], refs[N_PROMPT_IN:]
    sample_in, refs = refs[:N_SAMPLE_IN], refs[N_SAMPLE_IN:]
    prompt_out, sample_out, scratch = refs[:N_PROMPT_OUT], refs[N_PROMPT_OUT], refs[N_PROMPT_OUT + 1:]
    _sattn_body(*sample_in, sample_out)
    _attn_body(*prompt_in, *prompt_out, *scratch, seq=seq, group_size=group_size)


def _attention(z, qa, ka, va, cache_k, cache_v, rel_bias, *, batch, seq):
    n = batch * seq
    dec_batch, n_new = qa.shape[:2]
    n_past = cache_k.shape[1]
    assert all(seq % (dil * BLK) == 0 for _, dil in DILATED) and dec_batch == batch * H_A

    idx = _prompt_bias_index()
    n_d = idx.shape[0]
    bias_p = _bias_lookup(jnp.asarray(idx.reshape(n_d * BLK, 2 * BLK)), rel_bias.T[:, None, :],
                          n_col_tiles=1)
    bias_p = bias_p.reshape(H_A, n_d, BLK, 2 * BLK)
    prompt_in_specs = [pl.BlockSpec((1, n_d, BLK, 2 * BLK), lambda b, h: (h, 0, 0, 0))]
    prompt_in_specs += [pl.BlockSpec((seq, HEAD_DIM), lambda b, h, part=part: (b, part * H_A + h))
                        for part in range(3)]
    by_head = pl.BlockSpec((1, seq * H_A, HEAD_DIM), lambda b, h: (b, 0, 0), pipeline_mode=pl.Buffered(1))
    by_head_shape = jax.ShapeDtypeStruct((batch, seq * H_A, HEAD_DIM), F32)
    stat = pltpu.VMEM((n_d, SLABS, seq // SLABS, HEAD_DIM), F32)

    tail = DILATED[1][0]
    dil16 = DILATED[2][1]
    n16 = n_past // dil16
    n_rows = n_new * H_A
    bias_s = _bias_lookup(jnp.asarray(_sample_bias_index(n_new, n_past)),
                          jnp.tile(rel_bias.T, (n_new, 1))[None], n_col_tiles=5)[0]

    def rows(x):
        return x.reshape(dec_batch, n_rows, HEAD_DIM)

    def sample(b, h):
        return b * H_A + h

    k_res = cache_k.reshape(dec_batch, n16, dil16, H_A, HEAD_DIM)
    v_res = cache_v.reshape(dec_batch, n16, dil16, H_A, HEAD_DIM)
    q_spec = pl.BlockSpec((1, n_rows, HEAD_DIM), lambda b, h: (sample(b, h), 0, 0))
    tail_spec = pl.BlockSpec((1, tail, H_A, HEAD_DIM),
                             lambda b, h: (sample(b, h), n_past // tail - 1, 0, 0))
    res_spec = pl.BlockSpec((1, n16, n_new, H_A, HEAD_DIM), lambda b, h: (sample(b, h), 0, 0, 0, 0))
    sample_in_specs = [pl.BlockSpec(bias_s.shape, lambda b, h: (0, 0), pipeline_mode=pl.Buffered(1)),
                       q_spec, q_spec, q_spec, tail_spec, res_spec, tail_spec, res_spec]
    assert len(prompt_in_specs) == N_PROMPT_IN and len(sample_in_specs) == N_SAMPLE_IN

    oa_p, kh, vh, oa_s = pl.pallas_call(
        functools.partial(_attention_body, seq=seq, group_size=16),
        out_shape=[jax.ShapeDtypeStruct((n, W_A), F32), by_head_shape, by_head_shape,
                   jax.ShapeDtypeStruct((dec_batch, n_rows, HEAD_DIM), F32)],
        grid=(batch, H_A),
        in_specs=prompt_in_specs + sample_in_specs,
        out_specs=[pl.BlockSpec((seq, HEAD_DIM), lambda b, h: (b, h)), by_head, by_head, q_spec],
        scratch_shapes=[stat, stat, stat, stat],
        compiler_params=_params("arbitrary", "arbitrary"),
        name="attention",
    )(bias_p, z, z, z,
      bias_s, rows(qa), rows(ka), rows(va), cache_k, k_res, cache_v, v_res)
    heads = (batch, seq, H_A, HEAD_DIM)
    return oa_p, kh.reshape(heads), vh.reshape(heads), oa_s.reshape(dec_batch, n_new, W_A)


def _sgus_body(u_ref, v_ref, w_ref, b_ref, o_ref, *, n_new):
    t_idx = lax.broadcasted_iota(jnp.int32, (1, n_new, W_B), 1)
    mixed = 0.0
    for s in range(n_new):
        w_s = jnp.where(t_idx >= s, w_ref[s][None], 0.0).astype(BF16).astype(F32)
        mixed = mixed + w_s * v_ref[:, s:s + 1, :].astype(BF16).astype(F32)
    o_ref[...] = u_ref[...] * (mixed + b_ref[...][None])


def _side_body(u_ref, v_ref, w_ref, bt_ref, gb_ref, qm_ref, gm_ref,
               mem_ref, gmem_ref, wkv_ref, gkv_ref,
               us_ref, vs_ref, wl_ref, bl_ref, qs_ref, ck_ref, cv_ref,
               obn_ref, omn_ref, mkv_ref, obs_ref, oms_ref, ob_s, om_s, *, n_chunks, n_new, per_batch):
    _mems_body(qs_ref, ck_ref, cv_ref, oms_ref)
    _sgus_body(us_ref, vs_ref, wl_ref, bl_ref, obs_ref, n_new=n_new)

    @pl.when(pl.program_id(0) % per_batch == 0)
    def _():
        _mem_kv_body(mem_ref, gmem_ref, wkv_ref, gkv_ref, mkv_ref)

    _mem_body(qm_ref, mkv_ref, om_s)
    omn_ref[0] = _rms(om_s[...], gm_ref[...]).astype(BF16)
    _sgu_body(u_ref, v_ref, w_ref, bt_ref, ob_s, n_chunks=n_chunks)
    obn_ref[...] = _rms(ob_s[...], gb_ref[...]).astype(BF16)


def _side_mixers(z, mem, g_mem, w_kv, g_km, u_s, v_s, qm_s, cache_k, cache_v, w_s, b_s, g_b, g_m, *,
                 batch, seq, n_steps):
    dec_batch, n_new, _ = u_s.shape
    tm = batch * seq // n_steps
    tb = dec_batch // n_steps
    per_batch = seq // tm
    assert tm * per_batch == seq and tb * n_steps == dec_batch and tm % CHUNK == 0
    n_rows = n_new * H_M
    w_l = jnp.repeat(jnp.transpose(w_s[:, :n_new, :n_new], (2, 1, 0)), C_B, axis=-1)
    b_l = jnp.repeat(b_s[:, :n_new].T, C_B, axis=-1)
    z3 = z.reshape(batch, seq, -1)
    gain_kv = jnp.concatenate([jnp.tile(g_km, H_M), jnp.ones((W_M,), F32)]).reshape(1, 2 * W_M)
    once = pl.Buffered(1)
    sgu_s_spec = pl.BlockSpec((tb, n_new, W_B), lambda i: (i, 0, 0))
    qs_spec = pl.BlockSpec((tb, n_rows, HEAD_DIM), lambda i: (i, 0, 0))
    cache_spec = pl.BlockSpec((tb, N_MEM * H_M, HEAD_DIM), lambda i: (i, 0, 0))
    om_spec = pl.BlockSpec((1, tm, W_M), lambda i: (i // per_batch, i % per_batch, 0))
    gain_b = pl.BlockSpec((1, W_B), lambda i: (0, 0), pipeline_mode=once)
    gain_m = pl.BlockSpec((1, W_M), lambda i: (0, 0), pipeline_mode=once)
    obn, omn, mkv, ob_s, om_s = pl.pallas_call(
        functools.partial(_side_body, n_chunks=tm // CHUNK, n_new=n_new, per_batch=per_batch),
        out_shape=[jax.ShapeDtypeStruct((batch * seq, W_B), BF16),
                   jax.ShapeDtypeStruct((batch, seq, W_M), BF16),
                   jax.ShapeDtypeStruct((batch * N_MEM, 2 * W_M), F32),
                   jax.ShapeDtypeStruct(u_s.shape, F32),
                   jax.ShapeDtypeStruct((dec_batch, n_rows, HEAD_DIM), F32)],
        grid=(n_steps,),
        in_specs=[
            pl.BlockSpec((tm, W_B), lambda i: (i, COL_U // W_B)),
            pl.BlockSpec((tm, W_B), lambda i: (i, COL_V // W_B)),
            pl.BlockSpec((H_B, CHUNK, CHUNK), lambda i: (0, 0, 0), pipeline_mode=once),
            pl.BlockSpec((CHUNK, H_B), lambda i: (0, 0), pipeline_mode=once),
            gain_b,
            pl.BlockSpec((1, tm, W_M), lambda i: (i // per_batch, i % per_batch, COL_QM // W_M)),
            gain_m,
            pl.BlockSpec((N_MEM, D_MODEL), lambda i: (i // per_batch, 0)),
            pl.BlockSpec((1, D_MODEL), lambda i: (0, 0), pipeline_mode=once),
            pl.BlockSpec((D_MODEL, 2 * W_M), lambda i: (0, 0), pipeline_mode=once),
            pl.BlockSpec((1, 2 * W_M), lambda i: (0, 0), pipeline_mode=once),
            sgu_s_spec, sgu_s_spec,
            pl.BlockSpec(w_l.shape, lambda i: (0, 0, 0), pipeline_mode=once),
            pl.BlockSpec(b_l.shape, lambda i: (0, 0), pipeline_mode=once),
            qs_spec, cache_spec, cache_spec,
        ],
        out_specs=[pl.BlockSpec((tm, W_B), lambda i: (i, 0)), om_spec,
                   pl.BlockSpec((N_MEM, 2 * W_M), lambda i: (i // per_batch, 0)),
                   sgu_s_spec, qs_spec],
        scratch_shapes=[pltpu.VMEM((tm, W_B), F32), pltpu.VMEM((tm, W_M), F32)],
        compiler_params=_params("arbitrary"),
        name="side_mixers",
    )(z, z, w_s, b_s.T, g_b.reshape(1, W_B), z3, g_m.reshape(1, W_M),
      mem, g_mem.reshape(1, D_MODEL), w_kv, gain_kv, u_s, v_s, w_l, b_l,
      qm_s.reshape(dec_batch, n_rows, HEAD_DIM),
      cache_k.reshape(dec_batch, N_MEM * H_M, HEAD_DIM), cache_v.reshape(dec_batch, N_MEM * H_M, HEAD_DIM))
    return obn, omn.reshape(batch * seq, W_M), mkv, ob_s, om_s.reshape(dec_batch, n_new, W_M)


COL_KA, COL_VA, COL_U, COL_V, COL_QM = W_A, 2 * W_A, 3 * W_A, 3 * W_A + W_B, 3 * W_A + 2 * W_B


def _mixer_inputs(normed_groups, w_in, g_qa, g_ka, g_sgu, g_qm, *, n_tiles):
    gain = jnp.concatenate([jnp.tile(g_qa, H_A), jnp.tile(g_ka, H_A), jnp.ones((W_A + W_B,), F32),
                            g_sgu.reshape(W_B), jnp.tile(g_qm, H_M)])
    normed = [c // PROJ_TILE for lo, hi in ((0, COL_VA), (COL_V, COL_QM + W_M))
              for c in range(lo, hi, PROJ_TILE)]
    return _proj(normed_groups, w_in, gain, normed, n_tiles=n_tiles, chunk=256)


def kernel(x_prompt, x_sample, cache_win_k, cache_win_v, cache_mem_k, cache_mem_v, mem_prompt, rel_bias, g_ffn1, w1_gate, w1_up, w1_down, g_mix, w_in, g_qa, g_ka, g_sgu, w_sgu, b_sgu, g_qm, g_mem, w_mem_kv, g_km, g_mix_out, w_out, g_ffn2, w2_gate, w2_up, w2_down):
    batch, seq, _ = x_prompt.shape
    dec_batch, n_new, _ = x_sample.shape
    depth = g_ffn1.shape[0]
    n_p = batch * seq
    n_s = dec_batch * n_new

    y_p = x_prompt.reshape(n_p, D_MODEL)
    y_s = x_sample.reshape(n_s, D_MODEL)
    outs = [[] for _ in range(7)]
    for l in range(depth):
        h_p, h_s, hn_p, hn_s = _ffn([y_p, y_s], g_ffn1[l], w1_gate[l], w1_up[l], w1_down[l],
                                    g_next=g_mix[l], n_tiles=8, tf=256)
        z_p, z_s = _mixer_inputs([hn_p, hn_s], w_in[l], g_qa[l], g_ka[l], g_sgu[l], g_qm[l], n_tiles=4)

        zq, zk, zv, zu, zg, zm = (z_s[:, lo:hi] for lo, hi in zip(
            (0, COL_KA, COL_VA, COL_U, COL_V, COL_QM), (COL_KA, COL_VA, COL_U, COL_V, COL_QM, z_s.shape[1])))
        heads = (dec_batch, n_new, H_A, HEAD_DIM)
        oa_p, k_p, v_p, oa_s = _attention(z_p, zq.reshape(heads), zk.reshape(heads), zv.reshape(heads),
                                          cache_win_k[l], cache_win_v[l], rel_bias, batch=batch, seq=seq)

        ob_p, om_p, mkv, ob_s, om_s = _side_mixers(
            z_p, mem_prompt.reshape(batch * N_MEM, D_MODEL), g_mem[l], w_mem_kv[l], g_km[l],
            zu.reshape(dec_batch, n_new, W_B), zg.reshape(dec_batch, n_new, W_B),
            zm.reshape(dec_batch, n_new, H_M, HEAD_DIM), cache_mem_k[l], cache_mem_v[l],
            w_sgu[l], b_sgu[l], g_mix_out[l][W_A:W_A + W_B], g_mix_out[l][W_A + W_B:],
            batch=batch, seq=seq, n_steps=8)
        keep = min(DILATED[-1][0], seq)
        outs[0].append(k_p[:, seq - keep:])
        outs[1].append(v_p[:, seq - keep:])
        outs[2].append(mkv[:, :W_M].reshape(batch, N_MEM, H_M, HEAD_DIM))
        outs[3].append(mkv[:, W_M:].reshape(batch, N_MEM, H_M, HEAD_DIM))
        outs[4].append(zk.reshape(heads))
        outs[5].append(zv.reshape(heads))
        outs[6].append(zg.reshape(dec_batch, n_new, H_B, C_B))

        h_p, h_s = _outproj([(oa_p, ob_p, om_p, h_p),
                             (oa_s.reshape(n_s, W_A), ob_s.reshape(n_s, W_B), om_s.reshape(n_s, W_M), h_s)],
                            g_mix_out[l], w_out[l], n_tiles=4, tn=256)
        y_p, y_s = _ffn([h_p, h_s], g_ffn2[l], w2_gate[l], w2_up[l], w2_down[l], n_tiles=8, tf=256)

    return (y_p.reshape(batch, seq, D_MODEL), y_s.reshape(dec_batch, n_new, D_MODEL),
            *[jnp.stack(o) for o in outs])
```

```python
import functools

import numpy as np
import jax
import jax.numpy as jnp
from jax import lax
from jax.experimental import pallas as pl
from jax.experimental.pallas import tpu as pltpu

D_MODEL = 2048
HEAD_DIM = 128
H_A = 8
H_B = 4
C_B = 128
H_M = 4
W_A = H_A * HEAD_DIM
W_B = H_B * C_B
W_M = H_M * HEAD_DIM
D_FF = 5632
DILATED = ((128, 1), (512, 4), (2048, 16))
BLK = 128
CHUNK = 128
N_MEM = 256
N_BUCKETS = 32
REL_MAX_DIST = 2048
EPS = 1e-6
SCALE = HEAD_DIM ** -0.5
NEG = -1e30

LANES = 128
BF16_ROWS = 16
PROJ_TILE = 4 * HEAD_DIM
VMEM_LIMIT = 60 * 1024 * 1024

F32 = jnp.float32
BF16 = jnp.bfloat16


def _rel_bucket(dist):
    dist = np.asarray(dist, np.int64)
    max_exact = N_BUCKETS // 2
    large = max_exact + (np.log(np.maximum(dist, 1) / max_exact) / np.log(REL_MAX_DIST / max_exact)
                         * (N_BUCKETS - max_exact)).astype(np.int64)
    large = np.minimum(large, N_BUCKETS - 1)
    return np.where(dist < max_exact, dist, large).astype(np.int32)


def _rms(t, g):
    return t * lax.rsqrt(jnp.mean(t * t, axis=-1, keepdims=True) + EPS) * g


def _dot(a, b):
    return jnp.dot(a, b, preferred_element_type=F32)


def _dot_nt(a, b):
    return lax.dot_general(a, b, (((1,), (1,)), ((), ())), preferred_element_type=F32)


def _params(*sem):
    return pltpu.CompilerParams(dimension_semantics=sem, vmem_limit_bytes=VMEM_LIMIT)


def _tile_rows(groups, n_tiles):
    rows = [x.shape[0] // n_tiles for x in groups]
    assert all(r * n_tiles == x.shape[0] and r % BF16_ROWS == 0 for r, x in zip(rows, groups))
    return rows


def _bounds(refs):
    out, lo = [], 0
    for ref in refs:
        out.append((lo, lo + ref.shape[0]))
        lo += ref.shape[0]
    return out


def _chunk_edges(total, chunk):
    edges = list(range(0, total, chunk)) + [total]
    if len(edges) > 2 and edges[-1] - edges[-2] < chunk // 2:
        del edges[-2]
    return list(zip(edges[:-1], edges[1:]))


def _ffn_body(*refs, n_groups, with_next_norm):
    x_refs, refs = refs[:n_groups], refs[n_groups:]
    gn_ref = refs[0] if with_next_norm else None
    g_ref, wg_ref, wu_ref, wd_ref = refs[with_next_norm:with_next_norm + 4]
    refs = refs[with_next_norm + 4:]
    o_refs, on_refs, xn_ref = refs[:n_groups], refs[n_groups:-1], refs[-1]
    bounds = _bounds(x_refs)
    j = pl.program_id(1)

    @pl.when(j == 0)
    def _():
        for x_ref, o_ref, (lo, hi) in zip(x_refs, o_refs, bounds):
            x = x_ref[...]
            xn_ref[lo:hi, :] = _rms(x, g_ref[...]).astype(BF16)
            o_ref[...] = x

    xn = xn_ref[...]
    gate = _dot(xn, wg_ref[...].astype(BF16))
    up = _dot(xn, wu_ref[...].astype(BF16))
    act = (gate * jax.nn.sigmoid(gate)) * up * 0.5
    res = _dot(act.astype(BF16), wd_ref[...].astype(BF16))
    for o_ref, (lo, hi) in zip(o_refs, bounds):
        o_ref[...] += res[lo:hi]

    if with_next_norm:
        @pl.when(j == pl.num_programs(1) - 1)
        def _():
            for o_ref, on_ref in zip(o_refs, on_refs):
                on_ref[...] = _rms(o_ref[...], gn_ref[...]).astype(BF16)


def _ffn(groups, g, wg, wu, wd, *, g_next=None, n_tiles, tf):
    rows = _tile_rows(groups, n_tiles)
    with_next_norm = g_next is not None
    gain_spec = pl.BlockSpec((1, D_MODEL), lambda i, j: (0, 0), pipeline_mode=pl.Buffered(1))
    row_specs = [pl.BlockSpec((r, D_MODEL), lambda i, j: (i, 0)) for r in rows]
    out_shape = [jax.ShapeDtypeStruct(x.shape, F32) for x in groups]
    gains = [g.reshape(1, D_MODEL)]
    if with_next_norm:
        out_shape += [jax.ShapeDtypeStruct(x.shape, BF16) for x in groups]
        gains.insert(0, g_next.reshape(1, D_MODEL))
    return pl.pallas_call(
        functools.partial(_ffn_body, n_groups=len(groups), with_next_norm=with_next_norm),
        out_shape=out_shape,
        grid=(n_tiles, D_FF // tf),
        in_specs=(
            row_specs + [gain_spec] * len(gains)
            + [pl.BlockSpec((D_MODEL, tf), lambda i, j: (0, j)),
               pl.BlockSpec((D_MODEL, tf), lambda i, j: (0, j)),
               pl.BlockSpec((tf, D_MODEL), lambda i, j: (j, 0))]),
        out_specs=row_specs * (2 if with_next_norm else 1),
        scratch_shapes=[pltpu.VMEM((sum(rows), D_MODEL), BF16)],
        compiler_params=_params("arbitrary", "arbitrary"),
        name="ffn",
    )(*groups, *gains, wg, wu, wd)


def _proj_body(*refs, n_groups, norm_tiles, chunk):
    x_refs = refs[:n_groups]
    w_ref, gain_ref = refs[n_groups:n_groups + 2]
    o_refs, xn_ref = refs[n_groups + 2:-1], refs[-1]
    bounds = _bounds(x_refs)
    j = pl.program_id(1)

    @pl.when(j == 0)
    def _():
        for x_ref, (lo, hi) in zip(x_refs, bounds):
            xn_ref[lo:hi, :] = x_ref[...]

    w = w_ref[...].astype(BF16)
    gain = gain_ref[...]
    normed = functools.reduce(jnp.logical_or, [j == t for t in norm_tiles])

    def finish(z):
        heads = []
        for hh in range(PROJ_TILE // HEAD_DIM):
            sl = slice(hh * HEAD_DIM, (hh + 1) * HEAD_DIM)
            zz = z[:, sl]
            scale = lax.rsqrt(jnp.mean(zz * zz, axis=-1, keepdims=True) + EPS)
            heads.append(zz * jnp.where(normed, scale, 1.0) * gain[:, sl])
        return jnp.concatenate(heads, axis=1)

    for c0, c1 in _chunk_edges(bounds[-1][1], chunk):
        z = finish(_dot(xn_ref[c0:c1, :], w))
        for o_ref, (lo, hi) in zip(o_refs, bounds):
            a, b = max(lo, c0), min(hi, c1)
            if a < b:
                o_ref[a - lo:b - lo, :] = z[a - c0:b - c0, :]


def _proj(groups, w, gain, norm_tiles, *, n_tiles, chunk):
    d_in, d_out = w.shape
    rows = _tile_rows(groups, n_tiles)
    assert all(x.dtype == BF16 for x in groups)
    return pl.pallas_call(
        functools.partial(_proj_body, n_groups=len(groups), norm_tiles=tuple(norm_tiles), chunk=chunk),
        out_shape=[jax.ShapeDtypeStruct((x.shape[0], d_out), F32) for x in groups],
        grid=(n_tiles, d_out // PROJ_TILE),
        in_specs=(
            [pl.BlockSpec((r, d_in), lambda i, j: (i, 0)) for r in rows]
            + [pl.BlockSpec((d_in, PROJ_TILE), lambda i, j: (0, j)),
               pl.BlockSpec((1, PROJ_TILE), lambda i, j: (0, j))]),
        out_specs=[pl.BlockSpec((r, PROJ_TILE), lambda i, j: (i, j)) for r in rows],
        scratch_shapes=[pltpu.VMEM((sum(rows), d_in), BF16)],
        compiler_params=_params("arbitrary", "arbitrary"),
        name="proj",
    )(*groups, w, gain.reshape(1, d_out))


def _bias_body(idx_ref, tab_ref, o_ref):
    idx = idx_ref[...]
    tab = tab_ref[0]
    bias = jnp.full(idx.shape, NEG, F32)
    for bkt in range(N_BUCKETS):
        bias = jnp.where(idx == bkt, tab[:, bkt:bkt + 1], bias)
    o_ref[0] = bias


def _bias_lookup(idx, tab, *, n_col_tiles):
    n_rows, n_cols = idx.shape
    n_g, tab_rows, _ = tab.shape
    tc = n_cols // n_col_tiles
    assert tc * n_col_tiles == n_cols and tc % LANES == 0 and tab_rows in (1, n_rows)
    return pl.pallas_call(
        _bias_body,
        out_shape=jax.ShapeDtypeStruct((n_g, n_rows, n_cols), F32),
        grid=(n_g, n_col_tiles),
        in_specs=[pl.BlockSpec((n_rows, tc), lambda g, j: (0, j)),
                  pl.BlockSpec((1, tab_rows, N_BUCKETS), lambda g, j: (g, 0, 0))],
        out_specs=pl.BlockSpec((1, n_rows, tc), lambda g, j: (g, 0, j)),
        compiler_params=_params("parallel", "parallel"),
        name="bias_lookup",
    )(idx, tab)


SLABS = DILATED[1][1]
PIECE = BLK // SLABS


def _prompt_bias_index():
    step = np.arange(BLK)[:, None] + BLK - np.arange(2 * BLK)[None, :]
    perm = (SLABS * np.arange(PIECE)[None, :] + np.arange(SLABS)[:, None]).reshape(BLK)
    maps = []
    for win, dil in DILATED:
        n_steps = win // dil + 1
        band = (step >= 0) & (step < n_steps)
        bucket = _rel_bucket(np.clip(step, 0, n_steps - 1) * dil)
        idx = np.where(band, bucket, -1)
        if dil == 1:
            idx = idx[np.ix_(perm, np.concatenate([perm, BLK + perm]))]
        maps.append(idx)
    return np.stack(maps).astype(np.int32)


def _block_pieces(dil, r, n):
    if dil == 1:
        return [(s, slice(n * PIECE, (n + 1) * PIECE), PIECE) for s in range(SLABS)]
    if dil == SLABS:
        return [(r, slice(n * BLK, (n + 1) * BLK), BLK)]
    sub = dil // SLABS
    s, t = r % SLABS, r // SLABS
    return [(s, pl.ds(t + sub * BLK * n, BLK, stride=sub), BLK)]


def _attn_body(bias_s, q_ref, k_ref, v_ref, o_ref, kh_ref, vh_ref, slab_s, acc_s, m_s, l_s, *,
               seq, group_size):
    for part, ref in enumerate((q_ref, k_ref, v_ref)):
        for s in range(SLABS):
            slab_s[part, s] = ref[pl.ds(s, seq // SLABS, stride=SLABS), :]
    q_s, k_s, v_s = ([slab_s.at[part, s] for s in range(SLABS)] for part in range(3))

    def copy_by_head(c, n_chunks):
        n_rows = seq // n_chunks
        src = pl.ds(c * n_rows, n_rows)
        dst = pl.ds(pl.program_id(1) + H_A * c * n_rows, n_rows, stride=H_A)
        kh_ref[0, dst, :] = k_ref[src, :]
        vh_ref[0, dst, :] = v_ref[src, :]

    def load(slabs, pieces):
        parts = [slabs[s][idx, :] for s, idx, _ in pieces]
        return parts[0] if len(parts) == 1 else jnp.concatenate(parts, axis=0)

    def gather(slabs, blocks):
        return jnp.stack([load(slabs, pieces) for pieces in blocks]).astype(BF16)

    def qk(q, k):
        return jnp.einsum("gid,gjd->gij", q, k, preferred_element_type=F32)

    def pv(p, v):
        return jnp.einsum("gij,gjd->gid", p.astype(BF16), v, preferred_element_type=F32)

    def bias_of(dis, cols):
        return jnp.stack([bias_s[0, di, :, cols] for di in dis])

    def group(dis, curs, prevs):
        q = gather(q_s, curs)
        s_c = qk(q, gather(k_s, curs)) * SCALE + bias_of(dis, slice(BLK, None))
        if prevs is None:
            m = jnp.max(s_c, axis=-1, keepdims=True)
            p_c = jnp.exp(s_c - m)
            den = jnp.sum(p_c, axis=-1, keepdims=True)
            o = pv(p_c, gather(v_s, curs))
        else:
            s_p = qk(q, gather(k_s, prevs)) * SCALE + bias_of(dis, slice(0, BLK))
            m = jnp.max(jnp.maximum(s_c, s_p), axis=-1, keepdims=True)
            p_c = jnp.exp(s_c - m)
            p_p = jnp.exp(s_p - m)
            den = jnp.sum(p_c + p_p, axis=-1, keepdims=True)
            o = pv(p_c, gather(v_s, curs)) + pv(p_p, gather(v_s, prevs))
        for g, (di, pieces) in enumerate(zip(dis, curs)):
            m_g = jnp.broadcast_to(m[g], (BLK, HEAD_DIM))
            l_g = jnp.broadcast_to(den[g], (BLK, HEAD_DIM))
            lo = 0
            for s, idx, n_rows in pieces:
                acc_s[di, s, idx, :] = o[g, lo:lo + n_rows]
                m_s[di, s, idx, :] = m_g[lo:lo + n_rows]
                l_s[di, s, idx, :] = l_g[lo:lo + n_rows]
                lo += n_rows

    first, later = [], []
    for di, (win, dil) in enumerate(DILATED):
        for r in range(dil):
            first.append((di, dil, r, 0))
            later += [(di, dil, r, n) for n in range(1, seq // (dil * BLK))]
    groups = []
    for blocks, has_prev in ((first, False), (later, True)):
        for lo in range(0, len(blocks), group_size):
            part = blocks[lo:lo + group_size]
            groups.append(([di for di, _, _, _ in part],
                           [_block_pieces(dil, r, n) for _, dil, r, n in part],
                           [_block_pieces(dil, r, n - 1) for _, dil, r, n in part] if has_prev else None))
    n_chunks = 1 << (len(groups).bit_length() - 1)
    for c, args in enumerate(groups):
        group(*args)
        if c < n_chunks:
            copy_by_head(c, n_chunks)

    rows_per = 512
    per_slab = seq // SLABS // rows_per

    def merge(c, carry):
        s = c // per_slab
        r0 = pl.multiple_of((c % per_slab) * rows_per, rows_per)
        rc = pl.ds(r0, rows_per)
        ms = [m_s[di, s, rc, :] for di in range(len(DILATED))]
        mx = jnp.maximum(jnp.maximum(ms[0], ms[1]), ms[2])
        num = 0.0
        den = 0.0
        for di in range(len(DILATED)):
            wgt = jnp.exp(ms[di] - mx)
            num = num + wgt * acc_s[di, s, rc, :]
            den = den + wgt * l_s[di, s, rc, :]
        o_ref[pl.ds(SLABS * r0 + s, rows_per, stride=SLABS), :] = num / den
        return carry

    lax.fori_loop(0, SLABS * per_slab, merge, 0)


def _sgu_body(u_ref, v_ref, w_ref, bt_ref, o_ref, *, n_chunks):
    tri = (lax.broadcasted_iota(jnp.int32, (CHUNK, CHUNK), 0)
           >= lax.broadcasted_iota(jnp.int32, (CHUNK, CHUNK), 1))
    for g in range(H_B):
        w = jnp.where(tri, w_ref[g], 0.0).astype(BF16)
        bias = bt_ref[:, g:g + 1]
        cols = slice(g * C_B, (g + 1) * C_B)
        for c in range(n_chunks):
            rws = slice(c * CHUNK, (c + 1) * CHUNK)
            mixed = _dot(w, v_ref[rws, cols].astype(BF16)) + bias
            o_ref[rws, cols] = u_ref[rws, cols] * mixed


def _mem_kv_body(mem_ref, g_ref, w_ref, gain_ref, kv_ref):
    z = _dot(_rms(mem_ref[...], g_ref[...]).astype(BF16), w_ref[...].astype(BF16))
    for hh in range(H_M):
        sl = slice(hh * HEAD_DIM, (hh + 1) * HEAD_DIM)
        kv_ref[:, sl] = _rms(z[:, sl], gain_ref[:, sl])
    kv_ref[:, W_M:] = z[:, W_M:]


def _mem_body(q_ref, kv_ref, o_ref):
    for hh in range(H_M):
        k_sl = slice(hh * HEAD_DIM, (hh + 1) * HEAD_DIM)
        v_sl = slice(W_M + hh * HEAD_DIM, W_M + (hh + 1) * HEAD_DIM)
        s = _dot_nt(q_ref[0, :, k_sl].astype(BF16), kv_ref[:, k_sl].astype(BF16)) * SCALE
        m = jnp.max(s, axis=-1, keepdims=True)
        p = jnp.exp(s - m)
        den = jnp.sum(p, axis=-1, keepdims=True)
        o_ref[:, k_sl] = _dot(p.astype(BF16), kv_ref[:, v_sl].astype(BF16)) / den


def _mems_body(q_ref, k_ref, v_ref, o_ref):
    for b in range(q_ref.shape[0]):
        s = _dot_nt(q_ref[b].astype(BF16), k_ref[b].astype(BF16)) * SCALE
        row = lax.broadcasted_iota(jnp.int32, s.shape, 0)
        col = lax.broadcasted_iota(jnp.int32, s.shape, 1)
        s = jnp.where(row % H_M == col % H_M, s, NEG)
        m = jnp.max(s, axis=-1, keepdims=True)
        p = jnp.exp(s - m)
        den = jnp.sum(p, axis=-1, keepdims=True)
        o_ref[b] = _dot(p.astype(BF16), v_ref[b].astype(BF16)) / den


def _outproj_body(*refs, n_groups, chunk):
    src_refs = [refs[4 * k:4 * k + 3] for k in range(n_groups)]
    h_refs = [refs[4 * k + 3] for k in range(n_groups)]
    g_ref, w_ref = refs[4 * n_groups:4 * n_groups + 2]
    o_refs = refs[4 * n_groups + 2:5 * n_groups + 2]
    cat_ref = refs[-1]
    bounds = _bounds(h_refs)

    @pl.when(pl.program_id(1) == 0)
    def _():
        for srcs, (r0, r1) in zip(src_refs, bounds):
            lo = 0
            for src in srcs:
                hi = lo + src.shape[1]
                x = src[...]
                cat_ref[r0:r1, lo:hi] = x if x.dtype == BF16 else _rms(x, g_ref[:, lo:hi]).astype(BF16)
                lo = hi

    w = w_ref[...].astype(BF16)
    for c0, c1 in _chunk_edges(bounds[-1][1], chunk):
        res = _dot(cat_ref[c0:c1, :], w)
        for h_ref, o_ref, (r0, r1) in zip(h_refs, o_refs, bounds):
            a, b = max(r0, c0), min(r1, c1)
            if a < b:
                o_ref[a - r0:b - r0, :] = h_ref[a - r0:b - r0, :] + res[a - c0:b - c0, :]


def _outproj(groups, g, w, *, n_tiles, tn, chunk):
    mix = W_A + W_B + W_M
    rows = _tile_rows([grp[3] for grp in groups], n_tiles)
    in_specs = []
    for r in rows:
        in_specs += [pl.BlockSpec((r, wd), lambda i, j: (i, 0)) for wd in (W_A, W_B, W_M)]
        in_specs += [pl.BlockSpec((r, tn), lambda i, j: (i, j))]
    in_specs += [pl.BlockSpec((1, mix), lambda i, j: (0, 0), pipeline_mode=pl.Buffered(1)),
                 pl.BlockSpec((mix, tn), lambda i, j: (0, j))]
    return pl.pallas_call(
        functools.partial(_outproj_body, n_groups=len(groups), chunk=chunk),
        out_shape=[jax.ShapeDtypeStruct(grp[3].shape, F32) for grp in groups],
        grid=(n_tiles, D_MODEL // tn),
        in_specs=in_specs,
        out_specs=[pl.BlockSpec((r, tn), lambda i, j: (i, j)) for r in rows],
        scratch_shapes=[pltpu.VMEM((sum(rows), mix), BF16)],
        compiler_params=_params("arbitrary", "arbitrary"),
        name="outproj",
    )(*[a for grp in groups for a in grp], g.reshape(1, mix), w)


def _sample_bias_index(n_new, n_past):
    (win1, dil1), (tail, dil4), (win16, dil16) = DILATED
    assert dil1 == 1 and win1 == BLK and n_past == win16 and n_past % dil16 == 0
    assert n_past >= tail and n_new <= dil4 and n_new * H_A <= LANES
    n16 = n_past // dil16
    t = np.arange(n_new)[:, None]

    def per_head(index):
        same = np.eye(H_A, dtype=bool)[None, :, None, :]
        full = np.where(same, index[:, None, :, None], -1)
        return full.reshape(n_new * H_A, index.shape[1] * H_A)

    dist = BLK + t - np.arange(BLK)[None, :]
    d1 = np.where(dist <= win1, _rel_bucket(dist), -1)
    dist = tail + t - np.arange(tail)[None, :]
    d4 = np.where((dist % dil4 == 0) & (dist <= tail), _rel_bucket(dist), -1)
    m, tk =np.divmod(np.arange(n16 * n_new)[None, :], n_new)
    dist = n_past + t - dil16 * m - tk
    d16 = np.where(tk == t, _rel_bucket(np.maximum(dist, 0)), -1)
    s = np.arange(LANES // H_A)[None, :]
    dist = t - s
    new = [np.where((s < n_new) & (dist >= 0) & (dist % dil == 0),
                    _rel_bucket(np.maximum(dist, 0)), -1) for _, dil in DILATED]
    pieces = [d1, d4, d16] + new
    return np.concatenate([per_head(p) for p in pieces], axis=1).astype(np.int32)


def _sattn_body(bias_ref, q_ref, kn_ref, vn_ref, kt_ref, k16_ref, vt_ref, v16_ref, o_ref):
    def flat(ref):
        x = ref[0]
        return x.reshape(-1, HEAD_DIM).astype(BF16)

    q = q_ref[0].astype(BF16)
    kt, vt = flat(kt_ref), flat(vt_ref)
    k16, v16 = flat(k16_ref), flat(v16_ref)
    def new_rows(ref):
        x = ref[0]
        return jnp.concatenate([x, jnp.zeros((LANES - x.shape[0], HEAD_DIM), F32)], axis=0).astype(BF16)

    kn, vn = new_rows(kn_ref), new_rows(vn_ref)
    n_t, n_1 = kt.shape[0], BLK * H_A

    raw_t = _dot_nt(q, kt)
    raw_n = _dot_nt(q, kn)
    raws = [raw_t[:, n_t - n_1:], raw_t, _dot_nt(q, k16)] + [raw_n] * len(DILATED)
    logits = []
    lo = 0
    for raw in raws:
        hi = lo + raw.shape[1]
        logits.append(raw * SCALE + bias_ref[:, lo:hi])
        lo = hi

    m = functools.reduce(jnp.maximum, [jnp.max(x, axis=-1, keepdims=True) for x in logits])
    probs = [jnp.exp(x - m) for x in logits]
    den = functools.reduce(lambda a, b: a + b, [jnp.sum(p, axis=-1, keepdims=True) for p in probs])
    p_new = functools.reduce(lambda a, b: a + b, probs[3:])
    o = (_dot(probs[0].astype(BF16), vt[n_t - n_1:, :]) + _dot(probs[1].astype(BF16), vt)
         + _dot(probs[2].astype(BF16), v16) + _dot(p_new.astype(BF16), vn))
    o_ref[0] = o / den


N_PROMPT_IN, N_SAMPLE_IN, N_PROMPT_OUT = 4, 8, 3


def _attention_body(*refs, seq, group_size):
    prompt_in, refs = refs[:N_PROMPT_IN], refs[N_PROMPT_IN:]
    sample_in, refs = refs[:N_SAMPLE_IN], refs[N_SAMPLE_IN:]
    prompt_out, sample_out, scratch = refs[:N_PROMPT_OUT], refs[N_PROMPT_OUT], refs[N_PROMPT_OUT + 1:]
    _sattn_body(*sample_in, sample_out)
    _attn_body(*prompt_in, *prompt_out, *scratch, seq=seq, group_size=group_size)


def _attention(z, qa, ka, va, cache_k, cache_v, rel_bias, *, batch, seq):
    n = batch * seq
    dec_batch, n_new = qa.shape[:2]
    n_past = cache_k.shape[1]
    assert all(seq % (dil * BLK) == 0 for _, dil in DILATED) and dec_batch == batch * H_A

    idx = _prompt_bias_index()
    n_d = idx.shape[0]
    bias_p = _bias_lookup(jnp.asarray(idx.reshape(n_d * BLK, 2 * BLK)), rel_bias.T[:, None, :],
                          n_col_tiles=1)
    bias_p = bias_p.reshape(H_A, n_d, BLK, 2 * BLK)
    prompt_in_specs = [pl.BlockSpec((1, n_d, BLK, 2 * BLK), lambda b, h: (h, 0, 0, 0))]
    prompt_in_specs += [pl.BlockSpec((seq, HEAD_DIM), lambda b, h, part=part: (b, part * H_A + h))
                        for part in range(3)]
    by_head = pl.BlockSpec((1, seq * H_A, HEAD_DIM), lambda b, h: (b, 0, 0), pipeline_mode=pl.Buffered(1))
    by_head_shape = jax.ShapeDtypeStruct((batch, seq * H_A, HEAD_DIM), F32)
    stat = pltpu.VMEM((n_d, SLABS, seq // SLABS, HEAD_DIM), F32)

    tail = DILATED[1][0]
    dil16 = DILATED[2][1]
    n16 = n_past // dil16
    n_rows = n_new * H_A
    bias_s = _bias_lookup(jnp.asarray(_sample_bias_index(n_new, n_past)),
                          jnp.tile(rel_bias.T, (n_new, 1))[None], n_col_tiles=5)[0]

    def rows(x):
        return x.reshape(dec_batch, n_rows, HEAD_DIM)

    def sample(b, h):
        return b * H_A + h

    k_res = cache_k.reshape(dec_batch, n16, dil16, H_A, HEAD_DIM)
    v_res = cache_v.reshape(dec_batch, n16, dil16, H_A, HEAD_DIM)
    q_spec = pl.BlockSpec((1, n_rows, HEAD_DIM), lambda b, h: (sample(b, h), 0, 0))
    tail_spec = pl.BlockSpec((1, tail, H_A, HEAD_DIM),
                             lambda b, h: (sample(b, h), n_past // tail - 1, 0, 0))
    res_spec = pl.BlockSpec((1, n16, n_new, H_A, HEAD_DIM), lambda b, h: (sample(b, h), 0, 0, 0, 0))
    sample_in_specs = [pl.BlockSpec(bias_s.shape, lambda b, h: (0, 0), pipeline_mode=pl.Buffered(1)),
                       q_spec, q_spec, q_spec, tail_spec, res_spec, tail_spec, res_spec]
    assert len(prompt_in_specs) == N_PROMPT_IN and len(sample_in_specs) == N_SAMPLE_IN

    oa_p, kh, vh, oa_s = pl.pallas_call(
        functools.partial(_attention_body, seq=seq, group_size=16),
        out_shape=[jax.ShapeDtypeStruct((n, W_A), F32), by_head_shape, by_head_shape,
                   jax.ShapeDtypeStruct((dec_batch, n_rows, HEAD_DIM), F32)],
        grid=(batch, H_A),
        in_specs=prompt_in_specs + sample_in_specs,
        out_specs=[pl.BlockSpec((seq, HEAD_DIM), lambda b, h: (b, h)), by_head, by_head, q_spec],
        scratch_shapes=[stat, stat, stat, stat],
        compiler_params=_params("arbitrary", "arbitrary"),
        name="attention",
    )(bias_p, z, z, z,
      bias_s, rows(qa), rows(ka), rows(va), cache_k, k_res, cache_v, v_res)
    heads = (batch, seq, H_A, HEAD_DIM)
    return oa_p, kh.reshape(heads), vh.reshape(heads), oa_s.reshape(dec_batch, n_new, W_A)


def _sgus_body(u_ref, v_ref, w_ref, b_ref, o_ref, *, n_new):
    t_idx = lax.broadcasted_iota(jnp.int32, (1, n_new, W_B), 1)
    mixed = 0.0
    for s in range(n_new):
        w_s = jnp.where(t_idx >= s, w_ref[s][None], 0.0).astype(BF16).astype(F32)
        mixed = mixed + w_s * v_ref[:, s:s + 1, :].astype(BF16).astype(F32)
    o_ref[...] = u_ref[...] * (mixed + b_ref[...][None])


def _side_body(u_ref, v_ref, w_ref, bt_ref, gb_ref, qm_ref, gm_ref,
               mem_ref, gmem_ref, wkv_ref, gkv_ref,
               us_ref, vs_ref, wl_ref, bl_ref, qs_ref, ck_ref, cv_ref,
               obn_ref, omn_ref, mkv_ref, obs_ref, oms_ref, ob_s, om_s, *, n_chunks, n_new, per_batch):
    _mems_body(qs_ref, ck_ref, cv_ref, oms_ref)
    _sgus_body(us_ref, vs_ref, wl_ref, bl_ref, obs_ref, n_new=n_new)

    @pl.when(pl.program_id(0) % per_batch == 0)
    def _():
        _mem_kv_body(mem_ref, gmem_ref, wkv_ref, gkv_ref, mkv_ref)

    _mem_body(qm_ref, mkv_ref, om_s)
    omn_ref[0] = _rms(om_s[...], gm_ref[...]).astype(BF16)
    _sgu_body(u_ref, v_ref, w_ref, bt_ref, ob_s, n_chunks=n_chunks)
    obn_ref[...] = _rms(ob_s[...], gb_ref[...]).astype(BF16)


def _side_mixers(z, mem, g_mem, w_kv, g_km, u_s, v_s, qm_s, cache_k, cache_v, w_s, b_s, g_b, g_m, *,
                 batch, seq, n_steps):
    dec_batch, n_new, _ = u_s.shape
    tm = batch * seq // n_steps
    tb = dec_batch // n_steps
    per_batch = seq // tm
    assert tm * per_batch == seq and tb * n_steps == dec_batch and tm % CHUNK == 0
    n_rows = n_new * H_M
    w_l = jnp.repeat(jnp.transpose(w_s[:, :n_new, :n_new], (2, 1, 0)), C_B, axis=-1)
    b_l = jnp.repeat(b_s[:, :n_new].T, C_B, axis=-1)
    z3 = z.reshape(batch, seq, -1)
    gain_kv = jnp.concatenate([jnp.tile(g_km, H_M), jnp.ones((W_M,), F32)]).reshape(1, 2 * W_M)
    once = pl.Buffered(1)
    sgu_s_spec = pl.BlockSpec((tb, n_new, W_B), lambda i: (i, 0, 0))
    qs_spec = pl.BlockSpec((tb, n_rows, HEAD_DIM), lambda i: (i, 0, 0))
    cache_spec = pl.BlockSpec((tb, N_MEM * H_M, HEAD_DIM), lambda i: (i, 0, 0))
    om_spec = pl.BlockSpec((1, tm, W_M), lambda i: (i // per_batch, i % per_batch, 0))
    gain_b = pl.BlockSpec((1, W_B), lambda i: (0, 0), pipeline_mode=once)
    gain_m = pl.BlockSpec((1, W_M), lambda i: (0, 0), pipeline_mode=once)
    obn, omn, mkv, ob_s, om_s = pl.pallas_call(
        functools.partial(_side_body, n_chunks=tm // CHUNK, n_new=n_new, per_batch=per_batch),
        out_shape=[jax.ShapeDtypeStruct((batch * seq, W_B), BF16),
                   jax.ShapeDtypeStruct((batch, seq, W_M), BF16),
                   jax.ShapeDtypeStruct((batch * N_MEM, 2 * W_M), F32),
                   jax.ShapeDtypeStruct(u_s.shape, F32),
                   jax.ShapeDtypeStruct((dec_batch, n_rows, HEAD_DIM), F32)],
        grid=(n_steps,),
        in_specs=[
            pl.BlockSpec((tm, W_B), lambda i: (i, COL_U // W_B)),
            pl.BlockSpec((tm, W_B), lambda i: (i, COL_V // W_B)),
            pl.BlockSpec((H_B, CHUNK, CHUNK), lambda i: (0, 0, 0), pipeline_mode=once),
            pl.BlockSpec((CHUNK, H_B), lambda i: (0, 0), pipeline_mode=once),
            gain_b,
            pl.BlockSpec((1, tm, W_M), lambda i: (i // per_batch, i % per_batch, COL_QM // W_M)),
            gain_m,
            pl.BlockSpec((N_MEM, D_MODEL), lambda i: (i // per_batch, 0)),
            pl.BlockSpec((1, D_MODEL), lambda i: (0, 0), pipeline_mode=once),
            pl.BlockSpec((D_MODEL, 2 * W_M), lambda i: (0, 0), pipeline_mode=once),
            pl.BlockSpec((1, 2 * W_M), lambda i: (0, 0), pipeline_mode=once),
            sgu_s_spec, sgu_s_spec,
            pl.BlockSpec(w_l.shape, lambda i: (0, 0, 0), pipeline_mode=once),
            pl.BlockSpec(b_l.shape, lambda i: (0, 0), pipeline_mode=once),
            qs_spec, cache_spec, cache_spec,
        ],
        out_specs=[pl.BlockSpec((tm, W_B), lambda i: (i, 0)), om_spec,
                   pl.BlockSpec((N_MEM, 2 * W_M), lambda i: (i // per_batch, 0)),
                   sgu_s_spec, qs_spec],
        scratch_shapes=[pltpu.VMEM((tm, W_B), F32), pltpu.VMEM((tm, W_M), F32)],
        compiler_params=_params("arbitrary"),
        name="side_mixers",
    )(z, z, w_s, b_s.T, g_b.reshape(1, W_B), z3, g_m.reshape(1, W_M),
      mem, g_mem.reshape(1, D_MODEL), w_kv, gain_kv, u_s, v_s, w_l, b_l,
      qm_s.reshape(dec_batch, n_rows, HEAD_DIM),
      cache_k.reshape(dec_batch, N_MEM * H_M, HEAD_DIM), cache_v.reshape(dec_batch, N_MEM * H_M, HEAD_DIM))
    return obn, omn.reshape(batch * seq, W_M), mkv, ob_s, om_s.reshape(dec_batch, n_new, W_M)


COL_KA, COL_VA, COL_U, COL_V, COL_QM = W_A, 2 * W_A, 3 * W_A, 3 * W_A + W_B, 3 * W_A + 2 * W_B


def _mixer_inputs(normed_groups, w_in, g_qa, g_ka, g_sgu, g_qm, *, n_tiles):
    gain = jnp.concatenate([jnp.tile(g_qa, H_A), jnp.tile(g_ka, H_A), jnp.ones((W_A + W_B,), F32),
                            g_sgu.reshape(W_B), jnp.tile(g_qm, H_M)])
    normed = [c // PROJ_TILE for lo, hi in ((0, COL_VA), (COL_V, COL_QM + W_M))
              for c in range(lo, hi, PROJ_TILE)]
    return _proj(normed_groups, w_in, gain, normed, n_tiles=n_tiles, chunk=256)


def kernel(x_prompt, x_sample, cache_win_k, cache_win_v, cache_mem_k, cache_mem_v, mem_prompt, rel_bias, g_ffn1, w1_gate, w1_up, w1_down, g_mix, w_in, g_qa, g_ka, g_sgu, w_sgu, b_sgu, g_qm, g_mem, w_mem_kv, g_km, g_mix_out, w_out, g_ffn2, w2_gate, w2_up, w2_down):
    batch, seq, _ = x_prompt.shape
    dec_batch, n_new, _ = x_sample.shape
    depth = g_ffn1.shape[0]
    n_p = batch * seq
    n_s = dec_batch * n_new

    y_p = x_prompt.reshape(n_p, D_MODEL)
    y_s = x_sample.reshape(n_s, D_MODEL)
    outs = [[] for _ in range(7)]
    for l in range(depth):
        h_p, h_s, hn_p, hn_s = _ffn([y_p, y_s], g_ffn1[l], w1_gate[l], w1_up[l], w1_down[l],
                                    g_next=g_mix[l], n_tiles=8, tf=256)
        z_p, z_s = _mixer_inputs([hn_p, hn_s], w_in[l], g_qa[l], g_ka[l], g_sgu[l], g_qm[l], n_tiles=4)

        zq, zk, zv, zu, zg, zm = (z_s[:, lo:hi] for lo, hi in zip(
            (0, COL_KA, COL_VA, COL_U, COL_V, COL_QM), (COL_KA, COL_VA, COL_U, COL_V, COL_QM, z_s.shape[1])))
        heads = (dec_batch, n_new, H_A, HEAD_DIM)
        oa_p, k_p, v_p, oa_s = _attention(z_p, zq.reshape(heads), zk.reshape(heads), zv.reshape(heads),
                                          cache_win_k[l], cache_win_v[l], rel_bias, batch=batch, seq=seq)

        ob_p, om_p, mkv, ob_s, om_s = _side_mixers(
            z_p, mem_prompt.reshape(batch * N_MEM, D_MODEL), g_mem[l], w_mem_kv[l], g_km[l],
            zu.reshape(dec_batch, n_new, W_B), zg.reshape(dec_batch, n_new, W_B),
            zm.reshape(dec_batch, n_new, H_M, HEAD_DIM), cache_mem_k[l], cache_mem_v[l],
            w_sgu[l], b_sgu[l], g_mix_out[l][W_A:W_A + W_B], g_mix_out[l][W_A + W_B:],
            batch=batch, seq=seq, n_steps=8)
        keep = min(DILATED[-1][0], seq)
        outs[0].append(k_p[:, seq - keep:])
        outs[1].append(v_p[:, seq - keep:])
        outs[2].append(mkv[:, :W_M].reshape(batch, N_MEM, H_M, HEAD_DIM))
        outs[3].append(mkv[:, W_M:].reshape(batch, N_MEM, H_M, HEAD_DIM))
        outs[4].append(zk.reshape(heads))
        outs[5].append(zv.reshape(heads))
        outs[6].append(zg.reshape(dec_batch, n_new, H_B, C_B))

        h_p, h_s = _outproj([(oa_p, ob_p, om_p, h_p),
                             (oa_s.reshape(n_s, W_A), ob_s.reshape(n_s, W_B), om_s.reshape(n_s, W_M), h_s)],
                            g_mix_out[l], w_out[l], n_tiles=4, tn=256, chunk=512)
        y_p, y_s = _ffn([h_p, h_s], g_ffn2[l], w2_gate[l], w2_up[l], w2_down[l], n_tiles=8, tf=256)

    return (y_p.reshape(batch, seq, D_MODEL), y_s.reshape(dec_batch, n_new, D_MODEL),
            *[jnp.stack(o) for o in outs])
```

```python
import functools

import numpy as np
import jax
import jax.numpy as jnp
from jax import lax
from jax.experimental import pallas as pl
from jax.experimental.pallas import tpu as pltpu

D_MODEL = 2048
HEAD_DIM = 128
H_A = 8
H_B = 4
C_B = 128
H_M = 4
W_A = H_A * HEAD_DIM
W_B = H_B * C_B
W_M = H_M * HEAD_DIM
D_FF = 5632
DILATED = ((128, 1), (512, 4), (2048, 16))
BLK = 128
CHUNK = 128
N_MEM = 256
N_BUCKETS = 32
REL_MAX_DIST = 2048
EPS = 1e-6
SCALE = HEAD_DIM ** -0.5
NEG = -1e30

LANES = 128
BF16_ROWS = 16
PROJ_TILE = 4 * HEAD_DIM
VMEM_LIMIT = 60 * 1024 * 1024

F32 = jnp.float32
BF16 = jnp.bfloat16


def _rel_bucket(dist):
    dist = np.asarray(dist, np.int64)
    max_exact = N_BUCKETS // 2
    large = max_exact + (np.log(np.maximum(dist, 1) / max_exact) / np.log(REL_MAX_DIST / max_exact)
                         * (N_BUCKETS - max_exact)).astype(np.int64)
    large = np.minimum(large, N_BUCKETS - 1)
    return np.where(dist < max_exact, dist, large).astype(np.int32)


def _rms(t, g):
    return t * lax.rsqrt(jnp.mean(t * t, axis=-1, keepdims=True) + EPS) * g


def _dot(a, b):
    return jnp.dot(a, b, preferred_element_type=F32)


def _dot_nt(a, b):
    return lax.dot_general(a, b, (((1,), (1,)), ((), ())), preferred_element_type=F32)


def _params(*sem):
    return pltpu.CompilerParams(dimension_semantics=sem, vmem_limit_bytes=VMEM_LIMIT)


def _tile_rows(groups, n_tiles):
    rows = [x.shape[0] // n_tiles for x in groups]
    assert all(r * n_tiles == x.shape[0] and r % BF16_ROWS == 0 for r, x in zip(rows, groups))
    return rows


def _bounds(refs):
    out, lo = [], 0
    for ref in refs:
        out.append((lo, lo + ref.shape[0]))
        lo += ref.shape[0]
    return out


def _chunk_edges(total, chunk):
    edges = list(range(0, total, chunk)) + [total]
    if len(edges) > 2 and edges[-1] - edges[-2] < chunk // 2:
        del edges[-2]
    return list(zip(edges[:-1], edges[1:]))


def _ffn_body(*refs, n_groups, with_next_norm):
    x_refs, refs = refs[:n_groups], refs[n_groups:]
    gn_ref = refs[0] if with_next_norm else None
    g_ref, wg_ref, wu_ref, wd_ref = refs[with_next_norm:with_next_norm + 4]
    refs = refs[with_next_norm + 4:]
    o_refs, on_refs, xn_ref = refs[:n_groups], refs[n_groups:-1], refs[-1]
    bounds = _bounds(x_refs)
    j = pl.program_id(1)

    @pl.when(j == 0)
    def _():
        for x_ref, o_ref, (lo, hi) in zip(x_refs, o_refs, bounds):
            x = x_ref[...]
            xn_ref[lo:hi, :] = _rms(x, g_ref[...]).astype(BF16)
            o_ref[...] = x

    xn = xn_ref[...]
    gate = _dot(xn, wg_ref[...].astype(BF16))
    up = _dot(xn, wu_ref[...].astype(BF16))
    act = (gate * jax.nn.sigmoid(gate)) * up * 0.5
    res = _dot(act.astype(BF16), wd_ref[...].astype(BF16))
    for o_ref, (lo, hi) in zip(o_refs, bounds):
        o_ref[...] += res[lo:hi]

    if with_next_norm:
        @pl.when(j == pl.num_programs(1) - 1)
        def _():
            for o_ref, on_ref in zip(o_refs, on_refs):
                on_ref[...] = _rms(o_ref[...], gn_ref[...]).astype(BF16)


def _ffn(groups, g, wg, wu, wd, *, g_next=None, n_tiles, tf):
    rows = _tile_rows(groups, n_tiles)
    with_next_norm = g_next is not None
    gain_spec = pl.BlockSpec((1, D_MODEL), lambda i, j: (0, 0), pipeline_mode=pl.Buffered(1))
    row_specs = [pl.BlockSpec((r, D_MODEL), lambda i, j: (i, 0)) for r in rows]
    out_shape = [jax.ShapeDtypeStruct(x.shape, F32) for x in groups]
    gains = [g.reshape(1, D_MODEL)]
    if with_next_norm:
        out_shape += [jax.ShapeDtypeStruct(x.shape, BF16) for x in groups]
        gains.insert(0, g_next.reshape(1, D_MODEL))
    return pl.pallas_call(
        functools.partial(_ffn_body, n_groups=len(groups), with_next_norm=with_next_norm),
        out_shape=out_shape,
        grid=(n_tiles, D_FF // tf),
        in_specs=(
            row_specs + [gain_spec] * len(gains)
            + [pl.BlockSpec((D_MODEL, tf), lambda i, j: (0, j)),
               pl.BlockSpec((D_MODEL, tf), lambda i, j: (0, j)),
               pl.BlockSpec((tf, D_MODEL), lambda i, j: (j, 0))]),
        out_specs=row_specs * (2 if with_next_norm else 1),
        scratch_shapes=[pltpu.VMEM((sum(rows), D_MODEL), BF16)],
        compiler_params=_params("arbitrary", "arbitrary"),
        name="ffn",
    )(*groups, *gains, wg, wu, wd)


def _proj_body(*refs, n_groups, norm_tiles, chunk):
    x_refs = refs[:n_groups]
    w_ref, gain_ref = refs[n_groups:n_groups + 2]
    o_refs, xn_ref = refs[n_groups + 2:-1], refs[-1]
    bounds = _bounds(x_refs)
    j = pl.program_id(1)

    @pl.when(j == 0)
    def _():
        for x_ref, (lo, hi) in zip(x_refs, bounds):
            xn_ref[lo:hi, :] = x_ref[...]

    w = w_ref[...].astype(BF16)
    gain = gain_ref[...]
    normed = functools.reduce(jnp.logical_or, [j == t for t in norm_tiles])

    def finish(z):
        heads = []
        for hh in range(PROJ_TILE // HEAD_DIM):
            sl = slice(hh * HEAD_DIM, (hh + 1) * HEAD_DIM)
            zz = z[:, sl]
            scale = lax.rsqrt(jnp.mean(zz * zz, axis=-1, keepdims=True) + EPS)
            heads.append(zz * jnp.where(normed, scale, 1.0) * gain[:, sl])
        return jnp.concatenate(heads, axis=1)

    for c0, c1 in _chunk_edges(bounds[-1][1], chunk):
        z = finish(_dot(xn_ref[c0:c1, :], w))
        for o_ref, (lo, hi) in zip(o_refs, bounds):
            a, b = max(lo, c0), min(hi, c1)
            if a < b:
                o_ref[a - lo:b - lo, :] = z[a - c0:b - c0, :]


def _proj(groups, w, gain, norm_tiles, *, n_tiles, chunk):
    d_in, d_out = w.shape
    rows = _tile_rows(groups, n_tiles)
    assert all(x.dtype == BF16 for x in groups)
    return pl.pallas_call(
        functools.partial(_proj_body, n_groups=len(groups), norm_tiles=tuple(norm_tiles), chunk=chunk),
        out_shape=[jax.ShapeDtypeStruct((x.shape[0], d_out), F32) for x in groups],
        grid=(n_tiles, d_out // PROJ_TILE),
        in_specs=(
            [pl.BlockSpec((r, d_in), lambda i, j: (i, 0)) for r in rows]
            + [pl.BlockSpec((d_in, PROJ_TILE), lambda i, j: (0, j)),
               pl.BlockSpec((1, PROJ_TILE), lambda i, j: (0, j))]),
        out_specs=[pl.BlockSpec((r, PROJ_TILE), lambda i, j: (i, j)) for r in rows],
        scratch_shapes=[pltpu.VMEM((sum(rows), d_in), BF16)],
        compiler_params=_params("arbitrary", "arbitrary"),
        name="proj",
    )(*groups, w, gain.reshape(1, d_out))


def _bias_body(idx_ref, tab_ref, o_ref):
    idx = idx_ref[...]
    tab = tab_ref[0]
    bias = jnp.full(idx.shape, NEG, F32)
    for bkt in range(N_BUCKETS):
        bias = jnp.where(idx == bkt, tab[:, bkt:bkt + 1], bias)
    o_ref[0] = bias


def _bias_lookup(idx, tab, *, n_col_tiles):
    n_rows, n_cols = idx.shape
    n_g, tab_rows, _ = tab.shape
    tc = n_cols // n_col_tiles
    assert tc * n_col_tiles == n_cols and tc % LANES == 0 and tab_rows in (1, n_rows)
    return pl.pallas_call(
        _bias_body,
        out_shape=jax.ShapeDtypeStruct((n_g, n_rows, n_cols), F32),
        grid=(n_g, n_col_tiles),
        in_specs=[pl.BlockSpec((n_rows, tc), lambda g, j: (0, j)),
                  pl.BlockSpec((1, tab_rows, N_BUCKETS), lambda g, j: (g, 0, 0))],
        out_specs=pl.BlockSpec((1, n_rows, tc), lambda g, j: (g, 0, j)),
        compiler_params=_params("parallel", "parallel"),
        name="bias_lookup",
    )(idx, tab)


SLABS = DILATED[1][1]
PIECE = BLK // SLABS


def _prompt_bias_index():
    step = np.arange(BLK)[:, None] + BLK - np.arange(2 * BLK)[None, :]
    perm = (SLABS * np.arange(PIECE)[None, :] + np.arange(SLABS)[:, None]).reshape(BLK)
    maps = []
    for win, dil in DILATED:
        n_steps = win // dil + 1
        band = (step >= 0) & (step < n_steps)
        bucket = _rel_bucket(np.clip(step, 0, n_steps - 1) * dil)
        idx = np.where(band, bucket, -1)
        if dil == 1:
            idx = idx[np.ix_(perm, np.concatenate([perm, BLK + perm]))]
        maps.append(idx)
    return np.stack(maps).astype(np.int32)


def _block_pieces(dil, r, n):
    if dil == 1:
        return [(s, slice(n * PIECE, (n + 1) * PIECE), PIECE) for s in range(SLABS)]
    if dil == SLABS:
        return [(r, slice(n * BLK, (n + 1) * BLK), BLK)]
    sub = dil // SLABS
    s, t = r % SLABS, r // SLABS
    return [(s, pl.ds(t + sub * BLK * n, BLK, stride=sub), BLK)]


def _attn_body(bias_s, q_ref, k_ref, v_ref, o_ref, kh_ref, vh_ref, slab_s, acc_s, m_s, l_s, *,
               seq, group_size):
    for part, ref in enumerate((q_ref, k_ref, v_ref)):
        for s in range(SLABS):
            slab_s[part, s] = ref[pl.ds(s, seq // SLABS, stride=SLABS), :]
    q_s, k_s, v_s = ([slab_s.at[part, s] for s in range(SLABS)] for part in range(3))

    def copy_by_head(c, n_chunks):
        n_rows = seq // n_chunks
        src = pl.ds(c * n_rows, n_rows)
        dst = pl.ds(pl.program_id(1) + H_A * c * n_rows, n_rows, stride=H_A)
        kh_ref[0, dst, :] = k_ref[src, :]
        vh_ref[0, dst, :] = v_ref[src, :]

    def load(slabs, pieces):
        parts = [slabs[s][idx, :] for s, idx, _ in pieces]
        return parts[0] if len(parts) == 1 else jnp.concatenate(parts, axis=0)

    def gather(slabs, blocks):
        return jnp.stack([load(slabs, pieces) for pieces in blocks]).astype(BF16)

    def qk(q, k):
        return jnp.einsum("gid,gjd->gij", q, k, preferred_element_type=F32)

    def pv(p, v):
        return jnp.einsum("gij,gjd->gid", p.astype(BF16), v, preferred_element_type=F32)

    def bias_of(dis, cols):
        return jnp.stack([bias_s[0, di, :, cols] for di in dis])

    def group(dis, curs, prevs):
        q = gather(q_s, curs)
        s_c = qk(q, gather(k_s, curs)) * SCALE + bias_of(dis, slice(BLK, None))
        if prevs is None:
            m = jnp.max(s_c, axis=-1, keepdims=True)
            p_c = jnp.exp(s_c - m)
            den = jnp.sum(p_c, axis=-1, keepdims=True)
            o = pv(p_c, gather(v_s, curs))
        else:
            s_p = qk(q, gather(k_s, prevs)) * SCALE + bias_of(dis, slice(0, BLK))
            m = jnp.max(jnp.maximum(s_c, s_p), axis=-1, keepdims=True)
            p_c = jnp.exp(s_c - m)
            p_p = jnp.exp(s_p - m)
            den = jnp.sum(p_c + p_p, axis=-1, keepdims=True)
            o = pv(p_c, gather(v_s, curs)) + pv(p_p, gather(v_s, prevs))
        for g, (di, pieces) in enumerate(zip(dis, curs)):
            m_g = jnp.broadcast_to(m[g], (BLK, HEAD_DIM))
            l_g = jnp.broadcast_to(den[g], (BLK, HEAD_DIM))
            lo = 0
            for s, idx, n_rows in pieces:
                acc_s[di, s, idx, :] = o[g, lo:lo + n_rows]
                m_s[di, s, idx, :] = m_g[lo:lo + n_rows]
                l_s[di, s, idx, :] = l_g[lo:lo + n_rows]
                lo += n_rows

    first, later = [], []
    for di, (win, dil) in enumerate(DILATED):
        for r in range(dil):
            first.append((di, dil, r, 0))
            later += [(di, dil, r, n) for n in range(1, seq // (dil * BLK))]
    groups = []
    for blocks, has_prev in ((first, False), (later, True)):
        for lo in range(0, len(blocks), group_size):
            part = blocks[lo:lo + group_size]
            groups.append(([di for di, _, _, _ in part],
                           [_block_pieces(dil, r, n) for _, dil, r, n in part],
                           [_block_pieces(dil, r, n - 1) for _, dil, r, n in part] if has_prev else None))
    n_chunks = 1 << (len(groups).bit_length() - 1)
    for c, args in enumerate(groups):
        group(*args)
        if c < n_chunks:
            copy_by_head(c, n_chunks)

    rows_per = 512
    per_slab = seq // SLABS // rows_per

    def merge(c, carry):
        s = c // per_slab
        r0 = pl.multiple_of((c % per_slab) * rows_per, rows_per)
        rc = pl.ds(r0, rows_per)
        ms = [m_s[di, s, rc, :] for di in range(len(DILATED))]
        mx = jnp.maximum(jnp.maximum(ms[0], ms[1]), ms[2])
        num = 0.0
        den = 0.0
        for di in range(len(DILATED)):
            wgt = jnp.exp(ms[di] - mx)
            num = num + wgt * acc_s[di, s, rc, :]
            den = den + wgt * l_s[di, s, rc, :]
        o_ref[pl.ds(SLABS * r0 + s, rows_per, stride=SLABS), :] = num / den
        return carry

    lax.fori_loop(0, SLABS * per_slab, merge, 0)


def _sgu_body(u_ref, v_ref, w_ref, bt_ref, o_ref, *, n_chunks):
    tri = (lax.broadcasted_iota(jnp.int32, (CHUNK, CHUNK), 0)
           >= lax.broadcasted_iota(jnp.int32, (CHUNK, CHUNK), 1))
    for g in range(H_B):
        w = jnp.where(tri, w_ref[g], 0.0).astype(BF16)
        bias = bt_ref[:, g:g + 1]
        cols = slice(g * C_B, (g + 1) * C_B)
        for c in range(n_chunks):
            rws = slice(c * CHUNK, (c + 1) * CHUNK)
            mixed = _dot(w, v_ref[rws, cols].astype(BF16)) + bias
            o_ref[rws, cols] = u_ref[rws, cols] * mixed


def _mem_kv_body(mem_ref, g_ref, w_ref, gain_ref, kv_ref):
    z = _dot(_rms(mem_ref[...], g_ref[...]).astype(BF16), w_ref[...].astype(BF16))
    for hh in range(H_M):
        sl = slice(hh * HEAD_DIM, (hh + 1) * HEAD_DIM)
        kv_ref[:, sl] = _rms(z[:, sl], gain_ref[:, sl])
    kv_ref[:, W_M:] = z[:, W_M:]


def _mem_body(q_ref, kv_ref, o_ref):
    for hh in range(H_M):
        k_sl = slice(hh * HEAD_DIM, (hh + 1) * HEAD_DIM)
        v_sl = slice(W_M + hh * HEAD_DIM, W_M + (hh + 1) * HEAD_DIM)
        s = _dot_nt(q_ref[0, :, k_sl].astype(BF16), kv_ref[:, k_sl].astype(BF16)) * SCALE
        m = jnp.max(s, axis=-1, keepdims=True)
        p = jnp.exp(s - m)
        den = jnp.sum(p, axis=-1, keepdims=True)
        o_ref[:, k_sl] = _dot(p.astype(BF16), kv_ref[:, v_sl].astype(BF16)) / den


def _mems_body(q_ref, k_ref, v_ref, o_ref):
    for b in range(q_ref.shape[0]):
        s = _dot_nt(q_ref[b].astype(BF16), k_ref[b].astype(BF16)) * SCALE
        row = lax.broadcasted_iota(jnp.int32, s.shape, 0)
        col = lax.broadcasted_iota(jnp.int32, s.shape, 1)
        s = jnp.where(row % H_M == col % H_M, s, NEG)
        m = jnp.max(s, axis=-1, keepdims=True)
        p = jnp.exp(s - m)
        den = jnp.sum(p, axis=-1, keepdims=True)
        o_ref[b] = _dot(p.astype(BF16), v_ref[b].astype(BF16)) / den


def _outproj_body(*refs, n_groups):
    src_refs = [refs[4 * k:4 * k + 3] for k in range(n_groups)]
    h_refs = [refs[4 * k + 3] for k in range(n_groups)]
    g_ref, w_ref = refs[4 * n_groups:4 * n_groups + 2]
    o_refs = refs[4 * n_groups + 2:5 * n_groups + 2]
    cat_ref = refs[-1]
    bounds = _bounds(h_refs)

    @pl.when(pl.program_id(1) == 0)
    def _():
        for srcs, (r0, r1) in zip(src_refs, bounds):
            lo = 0
            for src in srcs:
                hi = lo + src.shape[1]
                x = src[...]
                cat_ref[r0:r1, lo:hi] = x if x.dtype == BF16 else _rms(x, g_ref[:, lo:hi]).astype(BF16)
                lo = hi

    res = _dot(cat_ref[...], w_ref[...].astype(BF16))
    for h_ref, o_ref, (r0, r1) in zip(h_refs, o_refs, bounds):
        o_ref[...] = h_ref[...] + res[r0:r1]


def _outproj(groups, g, w, *, n_tiles, tn):
    mix = W_A + W_B + W_M
    rows = _tile_rows([grp[3] for grp in groups], n_tiles)
    in_specs = []
    for r in rows:
        in_specs += [pl.BlockSpec((r, wd), lambda i, j: (i, 0)) for wd in (W_A, W_B, W_M)]
        in_specs += [pl.BlockSpec((r, tn), lambda i, j: (i, j))]
    in_specs += [pl.BlockSpec((1, mix), lambda i, j: (0, 0), pipeline_mode=pl.Buffered(1)),
                 pl.BlockSpec((mix, tn), lambda i, j: (0, j))]
    return pl.pallas_call(
        functools.partial(_outproj_body, n_groups=len(groups)),
        out_shape=[jax.ShapeDtypeStruct(grp[3].shape, F32) for grp in groups],
        grid=(n_tiles, D_MODEL // tn),
        in_specs=in_specs,
        out_specs=[pl.BlockSpec((r, tn), lambda i, j: (i, j)) for r in rows],
        scratch_shapes=[pltpu.VMEM((sum(rows), mix), BF16)],
        compiler_params=_params("arbitrary", "arbitrary"),
        name="outproj",
    )(*[a for grp in groups for a in grp], g.reshape(1, mix), w)


def _sample_bias_index(n_new, n_past):
    (win1, dil1), (tail, dil4), (win16, dil16) = DILATED
    assert dil1 == 1 and win1 == BLK and n_past == win16 and n_past % dil16 == 0
    assert n_past >= tail and n_new <= dil4 and n_new * H_A <= LANES
    n16 = n_past // dil16
    t = np.arange(n_new)[:, None]

    def per_head(index):
        same = np.eye(H_A, dtype=bool)[None, :, None, :]
        full = np.where(same, index[:, None, :, None], -1)
        return full.reshape(n_new * H_A, index.shape[1] * H_A)

    dist = BLK + t - np.arange(BLK)[None, :]
    d1 = np.where(dist <= win1, _rel_bucket(dist), -1)
    dist = tail + t - np.arange(tail)[None, :]
    d4 = np.where((dist % dil4 == 0) & (dist <= tail), _rel_bucket(dist), -1)
    m, tk =np.divmod(np.arange(n16 * n_new)[None, :], n_new)
    dist = n_past + t - dil16 * m - tk
    d16 = np.where(tk == t, _rel_bucket(np.maximum(dist, 0)), -1)
    s = np.arange(LANES // H_A)[None, :]
    dist = t - s
    new = [np.where((s < n_new) & (dist >= 0) & (dist % dil == 0),
                    _rel_bucket(np.maximum(dist, 0)), -1) for _, dil in DILATED]
    pieces = [d1, d4, d16] + new
    return np.concatenate([per_head(p) for p in pieces], axis=1).astype(np.int32)


def _sattn_body(bias_ref, q_ref, kn_ref, vn_ref, kt_ref, k16_ref, vt_ref, v16_ref, o_ref):
    def flat(ref):
        x = ref[0]
        return x.reshape(-1, HEAD_DIM).astype(BF16)

    q = q_ref[0].astype(BF16)
    kt, vt = flat(kt_ref), flat(vt_ref)
    k16, v16 = flat(k16_ref), flat(v16_ref)
    def new_rows(ref):
        x = ref[0]
        return jnp.concatenate([x, jnp.zeros((LANES - x.shape[0], HEAD_DIM), F32)], axis=0).astype(BF16)

    kn, vn = new_rows(kn_ref), new_rows(vn_ref)
    n_t, n_1 = kt.shape[0], BLK * H_A

    raw_t = _dot_nt(q, kt)
    raw_n = _dot_nt(q, kn)
    raws = [raw_t[:, n_t - n_1:], raw_t, _dot_nt(q, k16)] + [raw_n] * len(DILATED)
    logits = []
    lo = 0
    for raw in raws:
        hi = lo + raw.shape[1]
        logits.append(raw * SCALE + bias_ref[:, lo:hi])
        lo = hi

    m = functools.reduce(jnp.maximum, [jnp.max(x, axis=-1, keepdims=True) for x in logits])
    probs = [jnp.exp(x - m) for x in logits]
    den = functools.reduce(lambda a, b: a + b, [jnp.sum(p, axis=-1, keepdims=True) for p in probs])
    p_new = functools.reduce(lambda a, b: a + b, probs[3:])
    o = (_dot(probs[0].astype(BF16), vt[n_t - n_1:, :]) + _dot(probs[1].astype(BF16), vt)
         + _dot(probs[2].astype(BF16), v16) + _dot(p_new.astype(BF16), vn))
    o_ref[0] = o / den


N_PROMPT_IN, N_SAMPLE_IN, N_PROMPT_OUT = 4, 8, 3


def _attention_body(*refs, seq, group_size):
    prompt_in, refs = refs[:N_PROMPT_IN], refs[N_PROMPT_IN:]
    sample_in, refs = refs[:N_SAMPLE_IN], refs[N_SAMPLE_IN:]
    prompt_out, sample_out, scratch = refs[:N_PROMPT_OUT], refs[N_PROMPT_OUT], refs[N_PROMPT_OUT + 1:]
    _sattn_body(*sample_in, sample_out)
    _attn_body(*prompt_in, *prompt_out, *scratch, seq=seq, group_size=group_size)


def _attention(z, qa, ka, va, cache_k, cache_v, rel_bias, *, batch, seq):
    n = batch * seq
    dec_batch, n_new = qa.shape[:2]
    n_past = cache_k.shape[1]
    assert all(seq % (dil * BLK) == 0 for _, dil in DILATED) and dec_batch == batch * H_A

    idx = _prompt_bias_index()
    n_d = idx.shape[0]
    bias_p = _bias_lookup(jnp.asarray(idx.reshape(n_d * BLK, 2 * BLK)), rel_bias.T[:, None, :],
                          n_col_tiles=1)
    bias_p = bias_p.reshape(H_A, n_d, BLK, 2 * BLK)
    prompt_in_specs = [pl.BlockSpec((1, n_d, BLK, 2 * BLK), lambda b, h: (h, 0, 0, 0))]
    prompt_in_specs += [pl.BlockSpec((seq, HEAD_DIM), lambda b, h, part=part: (b, part * H_A + h))
                        for part in range(3)]
    by_head = pl.BlockSpec((1, seq * H_A, HEAD_DIM), lambda b, h: (b, 0, 0), pipeline_mode=pl.Buffered(1))
    by_head_shape = jax.ShapeDtypeStruct((batch, seq * H_A, HEAD_DIM), F32)
    stat = pltpu.VMEM((n_d, SLABS, seq // SLABS, HEAD_DIM), F32)

    tail = DILATED[1][0]
    dil16 = DILATED[2][1]
    n16 = n_past // dil16
    n_rows = n_new * H_A
    bias_s = _bias_lookup(jnp.asarray(_sample_bias_index(n_new, n_past)),
                          jnp.tile(rel_bias.T, (n_new, 1))[None], n_col_tiles=5)[0]

    def rows(x):
        return x.reshape(dec_batch, n_rows, HEAD_DIM)

    def sample(b, h):
        return b * H_A + h

    k_res = cache_k.reshape(dec_batch, n16, dil16, H_A, HEAD_DIM)
    v_res = cache_v.reshape(dec_batch, n16, dil16, H_A, HEAD_DIM)
    q_spec = pl.BlockSpec((1, n_rows, HEAD_DIM), lambda b, h: (sample(b, h), 0, 0))
    tail_spec = pl.BlockSpec((1, tail, H_A, HEAD_DIM),
                             lambda b, h: (sample(b, h), n_past // tail - 1, 0, 0))
    res_spec = pl.BlockSpec((1, n16, n_new, H_A, HEAD_DIM), lambda b, h: (sample(b, h), 0, 0, 0, 0))
    sample_in_specs = [pl.BlockSpec(bias_s.shape, lambda b, h: (0, 0), pipeline_mode=pl.Buffered(1)),
                       q_spec, q_spec, q_spec, tail_spec, res_spec, tail_spec, res_spec]
    assert len(prompt_in_specs) == N_PROMPT_IN and len(sample_in_specs) == N_SAMPLE_IN

    oa_p, kh, vh, oa_s = pl.pallas_call(
        functools.partial(_attention_body, seq=seq, group_size=16),
        out_shape=[jax.ShapeDtypeStruct((n, W_A), F32), by_head_shape, by_head_shape,
                   jax.ShapeDtypeStruct((dec_batch, n_rows, HEAD_DIM), F32)],
        grid=(batch, H_A),
        in_specs=prompt_in_specs + sample_in_specs,
        out_specs=[pl.BlockSpec((seq, HEAD_DIM), lambda b, h: (b, h)), by_head, by_head, q_spec],
        scratch_shapes=[stat, stat, stat, stat],
        compiler_params=_params("arbitrary", "arbitrary"),
        name="attention",
    )(bias_p, z, z, z,
      bias_s, rows(qa), rows(ka), rows(va), cache_k, k_res, cache_v, v_res)
    heads = (batch, seq, H_A, HEAD_DIM)
    return oa_p, kh.reshape(heads), vh.reshape(heads), oa_s.reshape(dec_batch, n_new, W_A)


def _sgus_body(u_ref, v_ref, w_ref, b_ref, o_ref, *, n_new):
    t_idx = lax.broadcasted_iota(jnp.int32, (1, n_new, W_B), 1)
    mixed = 0.0
    for s in range(n_new):
        w_s = jnp.where(t_idx >= s, w_ref[s][None], 0.0).astype(BF16).astype(F32)
        mixed = mixed + w_s * v_ref[:, s:s + 1, :].astype(BF16).astype(F32)
    o_ref[...] = u_ref[...] * (mixed + b_ref[...][None])


def _side_body(u_ref, v_ref, w_ref, bt_ref, gb_ref, qm_ref, gm_ref,
               mem_ref, gmem_ref, wkv_ref, gkv_ref,
               us_ref, vs_ref, wl_ref, bl_ref, qs_ref, ck_ref, cv_ref,
               obn_ref, omn_ref, mkv_ref, obs_ref, oms_ref, ob_s, om_s, *, n_chunks, n_new, per_batch):
    _mems_body(qs_ref, ck_ref, cv_ref, oms_ref)
    _sgus_body(us_ref, vs_ref, wl_ref, bl_ref, obs_ref, n_new=n_new)

    @pl.when(pl.program_id(0) % per_batch == 0)
    def _():
        _mem_kv_body(mem_ref, gmem_ref, wkv_ref, gkv_ref, mkv_ref)

    _mem_body(qm_ref, mkv_ref, om_s)
    omn_ref[0] = _rms(om_s[...], gm_ref[...]).astype(BF16)
    _sgu_body(u_ref, v_ref, w_ref, bt_ref, ob_s, n_chunks=n_chunks)
    obn_ref[...] = _rms(ob_s[...], gb_ref[...]).astype(BF16)


def _side_mixers(z, mem, g_mem, w_kv, g_km, u_s, v_s, qm_s, cache_k, cache_v, w_s, b_s, g_b, g_m, *,
                 batch, seq, n_steps):
    dec_batch, n_new, _ = u_s.shape
    tm = batch * seq // n_steps
    tb = dec_batch // n_steps
    per_batch = seq // tm
    assert tm * per_batch == seq and tb * n_steps == dec_batch and tm % CHUNK == 0
    n_rows = n_new * H_M
    w_l = jnp.repeat(jnp.transpose(w_s[:, :n_new, :n_new], (2, 1, 0)), C_B, axis=-1)
    b_l = jnp.repeat(b_s[:, :n_new].T, C_B, axis=-1)
    z3 = z.reshape(batch, seq, -1)
    gain_kv = jnp.concatenate([jnp.tile(g_km, H_M), jnp.ones((W_M,), F32)]).reshape(1, 2 * W_M)
    once = pl.Buffered(1)
    sgu_s_spec = pl.BlockSpec((tb, n_new, W_B), lambda i: (i, 0, 0))
    qs_spec = pl.BlockSpec((tb, n_rows, HEAD_DIM), lambda i: (i, 0, 0))
    cache_spec = pl.BlockSpec((tb, N_MEM * H_M, HEAD_DIM), lambda i: (i, 0, 0))
    om_spec = pl.BlockSpec((1, tm, W_M), lambda i: (i // per_batch, i % per_batch, 0))
    gain_b = pl.BlockSpec((1, W_B), lambda i: (0, 0), pipeline_mode=once)
    gain_m = pl.BlockSpec((1, W_M), lambda i: (0, 0), pipeline_mode=once)
    obn, omn, mkv, ob_s, om_s = pl.pallas_call(
        functools.partial(_side_body, n_chunks=tm // CHUNK, n_new=n_new, per_batch=per_batch),
        out_shape=[jax.ShapeDtypeStruct((batch * seq, W_B), BF16),
                   jax.ShapeDtypeStruct((batch, seq, W_M), BF16),
                   jax.ShapeDtypeStruct((batch * N_MEM, 2 * W_M), F32),
                   jax.ShapeDtypeStruct(u_s.shape, F32),
                   jax.ShapeDtypeStruct((dec_batch, n_rows, HEAD_DIM), F32)],
        grid=(n_steps,),
        in_specs=[
            pl.BlockSpec((tm, W_B), lambda i: (i, COL_U // W_B)),
            pl.BlockSpec((tm, W_B), lambda i: (i, COL_V // W_B)),
            pl.BlockSpec((H_B, CHUNK, CHUNK), lambda i: (0, 0, 0), pipeline_mode=once),
            pl.BlockSpec((CHUNK, H_B), lambda i: (0, 0), pipeline_mode=once),
            gain_b,
            pl.BlockSpec((1, tm, W_M), lambda i: (i // per_batch, i % per_batch, COL_QM // W_M)),
            gain_m,
            pl.BlockSpec((N_MEM, D_MODEL), lambda i: (i // per_batch, 0)),
            pl.BlockSpec((1, D_MODEL), lambda i: (0, 0), pipeline_mode=once),
            pl.BlockSpec((D_MODEL, 2 * W_M), lambda i: (0, 0), pipeline_mode=once),
            pl.BlockSpec((1, 2 * W_M), lambda i: (0, 0), pipeline_mode=once),
            sgu_s_spec, sgu_s_spec,
            pl.BlockSpec(w_l.shape, lambda i: (0, 0, 0), pipeline_mode=once),
            pl.BlockSpec(b_l.shape, lambda i: (0, 0), pipeline_mode=once),
            qs_spec, cache_spec, cache_spec,
        ],
        out_specs=[pl.BlockSpec((tm, W_B), lambda i: (i, 0)), om_spec,
                   pl.BlockSpec((N_MEM, 2 * W_M), lambda i: (i // per_batch, 0)),
                   sgu_s_spec, qs_spec],
        scratch_shapes=[pltpu.VMEM((tm, W_B), F32), pltpu.VMEM((tm, W_M), F32)],
        compiler_params=_params("arbitrary"),
        name="side_mixers",
    )(z, z, w_s, b_s.T, g_b.reshape(1, W_B), z3, g_m.reshape(1, W_M),
      mem, g_mem.reshape(1, D_MODEL), w_kv, gain_kv, u_s, v_s, w_l, b_l,
      qm_s.reshape(dec_batch, n_rows, HEAD_DIM),
      cache_k.reshape(dec_batch, N_MEM * H_M, HEAD_DIM), cache_v.reshape(dec_batch, N_MEM * H_M, HEAD_DIM))
    return obn, omn.reshape(batch * seq, W_M), mkv, ob_s, om_s.reshape(dec_batch, n_new, W_M)


COL_KA, COL_VA, COL_U, COL_V, COL_QM = W_A, 2 * W_A, 3 * W_A, 3 * W_A + W_B, 3 * W_A + 2 * W_B


def _mixer_inputs(normed_groups, w_in, g_qa, g_ka, g_sgu, g_qm, *, n_tiles):
    gain = jnp.concatenate([jnp.tile(g_qa, H_A), jnp.tile(g_ka, H_A), jnp.ones((W_A + W_B,), F32),
                            g_sgu.reshape(W_B), jnp.tile(g_qm, H_M)])
    normed = [c // PROJ_TILE for lo, hi in ((0, COL_VA), (COL_V, COL_QM + W_M))
              for c in range(lo, hi, PROJ_TILE)]
    return _proj(normed_groups, w_in, gain, normed, n_tiles=n_tiles, chunk=128)


def kernel(x_prompt, x_sample, cache_win_k, cache_win_v, cache_mem_k, cache_mem_v, mem_prompt, rel_bias, g_ffn1, w1_gate, w1_up, w1_down, g_mix, w_in, g_qa, g_ka, g_sgu, w_sgu, b_sgu, g_qm, g_mem, w_mem_kv, g_km, g_mix_out, w_out, g_ffn2, w2_gate, w2_up, w2_down):
    batch, seq, _ = x_prompt.shape
    dec_batch, n_new, _ = x_sample.shape
    depth = g_ffn1.shape[0]
    n_p = batch * seq
    n_s = dec_batch * n_new

    y_p = x_prompt.reshape(n_p, D_MODEL)
    y_s = x_sample.reshape(n_s, D_MODEL)
    outs = [[] for _ in range(7)]
    for l in range(depth):
        h_p, h_s, hn_p, hn_s = _ffn([y_p, y_s], g_ffn1[l], w1_gate[l], w1_up[l], w1_down[l],
                                    g_next=g_mix[l], n_tiles=8, tf=256)
        z_p, z_s = _mixer_inputs([hn_p, hn_s], w_in[l], g_qa[l], g_ka[l], g_sgu[l], g_qm[l], n_tiles=4)

        zq, zk, zv, zu, zg, zm = (z_s[:, lo:hi] for lo, hi in zip(
            (0, COL_KA, COL_VA, COL_U, COL_V, COL_QM), (COL_KA, COL_VA, COL_U, COL_V, COL_QM, z_s.shape[1])))
        heads = (dec_batch, n_new, H_A, HEAD_DIM)
        oa_p, k_p, v_p, oa_s = _attention(z_p, zq.reshape(heads), zk.reshape(heads), zv.reshape(heads),
                                          cache_win_k[l], cache_win_v[l], rel_bias, batch=batch, seq=seq)

        ob_p, om_p, mkv, ob_s, om_s = _side_mixers(
            z_p, mem_prompt.reshape(batch * N_MEM, D_MODEL), g_mem[l], w_mem_kv[l], g_km[l],
            zu.reshape(dec_batch, n_new, W_B), zg.reshape(dec_batch, n_new, W_B),
            zm.reshape(dec_batch, n_new, H_M, HEAD_DIM), cache_mem_k[l], cache_mem_v[l],
            w_sgu[l], b_sgu[l], g_mix_out[l][W_A:W_A + W_B], g_mix_out[l][W_A + W_B:],
            batch=batch, seq=seq, n_steps=8)
        keep = min(DILATED[-1][0], seq)
        outs[0].append(k_p[:, seq - keep:])
        outs[1].append(v_p[:, seq - keep:])
        outs[2].append(mkv[:, :W_M].reshape(batch, N_MEM, H_M, HEAD_DIM))
        outs[3].append(mkv[:, W_M:].reshape(batch, N_MEM, H_M, HEAD_DIM))
        outs[4].append(zk.reshape(heads))
        outs[5].append(zv.reshape(heads))
        outs[6].append(zg.reshape(dec_batch, n_new, H_B, C_B))

        h_p, h_s = _outproj([(oa_p, ob_p, om_p, h_p),
                             (oa_s.reshape(n_s, W_A), ob_s.reshape(n_s, W_B), om_s.reshape(n_s, W_M), h_s)],
                            g_mix_out[l], w_out[l], n_tiles=4, tn=256)
        y_p, y_s = _ffn([h_p, h_s], g_ffn2[l], w2_gate[l], w2_up[l], w2_down[l], n_tiles=8, tf=256)

    return (y_p.reshape(batch, seq, D_MODEL), y_s.reshape(dec_batch, n_new, D_MODEL),
            *[jnp.stack(o) for o in outs])
```
